```python
import math
import jax, jax.numpy as jnp
from jax import lax
import numpy as np

D_MODEL = 1024
BATCH = 4
SEQ = 8192
DEPTH = 2

MLA_HEADS = 8
MLA_NOPE_DIM = 128
MLA_ROPE_DIM = 64
MLA_V_DIM = 128
MLA_Q_RANK = 384
MLA_KV_RANK = 256
ROPE_THETA = 10000.0
SWA_Q_HEADS = 16
SWA_KV_HEADS = 4
SWA_HEAD_DIM = 64
WINDOW = 128
REL_BUCKETS = 32
REL_MAX_DIST = 128
D_FF = 4 * D_MODEL
BLOCK = 128
LN_EPS = 1e-5
RMS_EPS = 1e-6

kernel_name = 'yoco_mla_swa_sink_t5_deepnorm'


def _layernorm(x, g, b):
    xf = x.astype(jnp.float32)
    mu = xf.mean(-1, keepdims=True)
    var = jnp.square(xf - mu).mean(-1, keepdims=True)
    return ((xf - mu) * lax.rsqrt(var + LN_EPS) * g + b).astype(x.dtype)


def _rmsnorm(x, g):
    xf = x.astype(jnp.float32)
    return (xf * lax.rsqrt(jnp.mean(xf * xf, -1, keepdims=True) + RMS_EPS) * g).astype(x.dtype)


def _rope(x, pos):
    half = x.shape[-1] // 2
    inv = ROPE_THETA ** (-jnp.arange(half, dtype=jnp.float32) / half)
    ang = pos.astype(jnp.float32)[:, None] * inv[None, :]
    ang = ang.reshape(ang.shape[:1] + (1,) * (x.ndim - 3) + ang.shape[1:])
    cos, sin = jnp.cos(ang), jnp.sin(ang)
    xf = x.astype(jnp.float32)
    x1, x2 = xf[..., :half], xf[..., half:]
    return jnp.concatenate([x1 * cos - x2 * sin, x1 * sin + x2 * cos], -1).astype(x.dtype)


def _t5_bucket(dist):
    n = jnp.maximum(dist, 0)
    max_exact = REL_BUCKETS // 2
    nf = jnp.maximum(n, 1).astype(jnp.float32)
    large = max_exact + (jnp.log(nf / max_exact) / math.log(REL_MAX_DIST / max_exact)
                         * (REL_BUCKETS - max_exact)).astype(jnp.int32)
    large = jnp.minimum(large, REL_BUCKETS - 1)
    return jnp.where(n < max_exact, n, large)


def _to_blocks(t, nb):
    return t.reshape((t.shape[0], nb, BLOCK) + t.shape[2:]).swapaxes(0, 1)


def _mla(x, w_in, g_q, g_kv, w_uq, w_uk, w_uv, w_o, pos):
    B, S, _ = x.shape
    nb = S // BLOCK
    h = x @ w_in
    c_q = _rmsnorm(h[..., :MLA_Q_RANK], g_q)
    c_kv = _rmsnorm(h[..., MLA_Q_RANK:MLA_Q_RANK + MLA_KV_RANK], g_kv)
    k_r = _rope(h[..., MLA_Q_RANK + MLA_KV_RANK:], pos)
    q = jnp.einsum('bsr,rhd->bshd', c_q, w_uq)
    q_n = q[..., :MLA_NOPE_DIM]
    q_r = _rope(q[..., MLA_NOPE_DIM:], pos)
    q_lat = jnp.einsum('bshn,chn->bshc', q_n, w_uk)
    scale = (MLA_NOPE_DIM + MLA_ROPE_DIM) ** -0.5
    kpos = jnp.arange(S)

    def block(args):
        ql, qr, i = args
        s = (jnp.einsum('bqhc,bkc->bhqk', ql, c_kv)
             + jnp.einsum('bqhr,bkr->bhqk', qr, k_r)).astype(jnp.float32) * scale
        qpos = i * BLOCK + jnp.arange(BLOCK)
        s = jnp.where(kpos[None, :] <= qpos[:, None], s, -jnp.inf)
        p = jax.nn.softmax(s, axis=-1).astype(x.dtype)
        return jnp.einsum('bhqk,bkc->bqhc', p, c_kv)

    o_lat = lax.map(block, (_to_blocks(q_lat, nb), _to_blocks(q_r, nb), jnp.arange(nb)))
    o_lat = o_lat.swapaxes(0, 1).reshape(B, S, MLA_HEADS, MLA_KV_RANK)
    o = jnp.einsum('bshc,chv->bshv', o_lat, w_uv).reshape(B, S, MLA_HEADS * MLA_V_DIM)
    return o @ w_o


def _shared_kv(x, w_kv):
    B, S, _ = x.shape
    nb = S // BLOCK
    kv = (x @ w_kv).reshape(B, nb, BLOCK, 2, SWA_KV_HEADS, SWA_HEAD_DIM)
    k, v = kv[:, :, :, 0], kv[:, :, :, 1]

    def band(t):
        prev = jnp.pad(t, ((0, 0), (1, 0), (0, 0), (0, 0), (0, 0)))[:, :-1]
        return jnp.concatenate([prev, t], axis=2)

    return band(k), band(v)


def _swa(x, w_q, sinks, w_o, k_band, v_band, rel_bias):
    B, S, _ = x.shape
    nb = S // BLOCK
    G = SWA_Q_HEADS // SWA_KV_HEADS
    q = (x @ w_q).reshape(B, nb, BLOCK, SWA_KV_HEADS, G, SWA_HEAD_DIM)
    s = jnp.einsum('bnqkgd,bnjkd->bnkgqj', q, k_band).astype(jnp.float32) * SWA_HEAD_DIM ** -0.5
    i = jnp.arange(BLOCK)
    j = jnp.arange(2 * BLOCK)
    dist = i[:, None] + BLOCK - j[None, :]
    bias = rel_bias[_t5_bucket(dist)].astype(jnp.float32)
    bias = bias.transpose(2, 0, 1).reshape(SWA_KV_HEADS, G, BLOCK, 2 * BLOCK)
    kpos = jnp.arange(nb)[:, None] * BLOCK - BLOCK + j[None, :]
    valid = (dist >= 0) & (dist < WINDOW)
    mask = valid[None] & (kpos >= 0)[:, None, :]
    s = jnp.where(mask[None, :, None, None], s + bias, -jnp.inf)
    sink = sinks.astype(jnp.float32).reshape(SWA_KV_HEADS, G)[..., None]
    m = jnp.maximum(s.max(-1), sink)
    p = jnp.exp(s - m[..., None])
    denom = p.sum(-1) + jnp.exp(sink - m)
    p = (p / denom[..., None]).astype(x.dtype)
    o = jnp.einsum('bnkgqj,bnjkd->bnqkgd', p, v_band).reshape(B, S, SWA_Q_HEADS * SWA_HEAD_DIM)
    return o @ w_o


def _mlp(x, w_up, w_down):
    h = jax.nn.relu(x @ w_up)
    return (h * h) @ w_down


def setup_inputs(seed: int = 0) -> dict:
    key = jax.random.key(seed)
    ks = jax.random.split(key, 20)
    n_a = DEPTH // 2
    n_b = DEPTH - n_a
    beta = (8 * DEPTH) ** -0.25
    f32 = jnp.float32

    def nrm(k, shape, fan_in, gain=1.0):
        return jax.random.normal(k, shape, f32) * (gain * fan_in ** -0.5)

    in_w = MLA_Q_RANK + MLA_KV_RANK + MLA_ROPE_DIM
    kv_k = nrm(ks[8], (D_MODEL, 1, SWA_KV_HEADS * SWA_HEAD_DIM), D_MODEL)
    kv_v = nrm(ks[9], (D_MODEL, 1, SWA_KV_HEADS * SWA_HEAD_DIM), D_MODEL, beta)
    return {
        'x': jax.random.normal(ks[0], (BATCH, SEQ, D_MODEL), f32),
        'mla_w_in': nrm(ks[1], (n_a, D_MODEL, in_w), D_MODEL),
        'mla_g_q': 1.0 + 0.05 * jax.random.normal(ks[2], (n_a, MLA_Q_RANK), f32),
        'mla_g_kv': 1.0 + 0.05 * jax.random.normal(ks[3], (n_a, MLA_KV_RANK), f32),
        'mla_w_uq': nrm(ks[4], (n_a, MLA_Q_RANK, MLA_HEADS, MLA_NOPE_DIM + MLA_ROPE_DIM), MLA_Q_RANK),
        'mla_w_uk': nrm(ks[5], (n_a, MLA_KV_RANK, MLA_HEADS, MLA_NOPE_DIM), MLA_KV_RANK),
        'mla_w_uv': nrm(ks[6], (n_a, MLA_KV_RANK, MLA_HEADS, MLA_V_DIM), MLA_KV_RANK, beta),
        'mla_w_o': nrm(ks[7], (n_a, MLA_HEADS * MLA_V_DIM, D_MODEL), MLA_HEADS * MLA_V_DIM, beta),
        'kv_w_shared': jnp.concatenate([kv_k, kv_v], axis=1).reshape(D_MODEL, 2 * SWA_KV_HEADS * SWA_HEAD_DIM),
        'swa_w_q': nrm(ks[10], (n_b, D_MODEL, SWA_Q_HEADS * SWA_HEAD_DIM), D_MODEL),
        'swa_sinks': 0.5 * jax.random.normal(ks[11], (n_b, SWA_Q_HEADS), f32),
        'swa_w_o': nrm(ks[12], (n_b, SWA_Q_HEADS * SWA_HEAD_DIM, D_MODEL), SWA_Q_HEADS * SWA_HEAD_DIM, beta),
        'rel_bias': 0.5 * jax.random.normal(ks[13], (REL_BUCKETS, SWA_Q_HEADS), f32),
        'mlp_w_up': nrm(ks[14], (DEPTH, D_MODEL, D_FF), D_MODEL),
        'mlp_w_down': nrm(ks[15], (DEPTH, D_FF, D_MODEL), D_FF, beta),
        'ln_mix_g': 1.0 + 0.05 * jax.random.normal(ks[16], (DEPTH, D_MODEL), f32),
        'ln_mix_b': 0.02 * jax.random.normal(ks[17], (DEPTH, D_MODEL), f32),
        'ln_mlp_g': 1.0 + 0.05 * jax.random.normal(ks[18], (DEPTH, D_MODEL), f32),
        'ln_mlp_b': 0.02 * jax.random.normal(ks[19], (DEPTH, D_MODEL), f32),
    }


def reference(x, mla_w_in, mla_g_q, mla_g_kv, mla_w_uq, mla_w_uk, mla_w_uv, mla_w_o,
              kv_w_shared, swa_w_q, swa_sinks, swa_w_o, rel_bias,
              mlp_w_up, mlp_w_down, ln_mix_g, ln_mix_b, ln_mlp_g, ln_mlp_b):
    alpha = (2 * DEPTH) ** 0.25
    n_a = DEPTH // 2
    pos = jnp.arange(x.shape[1])
    k_band = v_band = None
    for l in range(DEPTH):
        if l < n_a:
            y = _mla(x, mla_w_in[l], mla_g_q[l], mla_g_kv[l], mla_w_uq[l], mla_w_uk[l],
                     mla_w_uv[l], mla_w_o[l], pos)
        else:
            if l == n_a:
                k_band, v_band = _shared_kv(x, kv_w_shared)
            b = l - n_a
            y = _swa(x, swa_w_q[b], swa_sinks[b], swa_w_o[b], k_band, v_band, rel_bias)
        x = _layernorm(alpha * x + y, ln_mix_g[l], ln_mix_b[l])
        x = _layernorm(alpha * x + _mlp(x, mlp_w_up[l], mlp_w_down[l]), ln_mlp_g[l], ln_mlp_b[l])
    return x
```

```python
import functools
import math

import numpy as np
import jax
import jax.numpy as jnp
from jax import lax
from jax.experimental import pallas as pl
from jax.experimental.pallas import tpu as pltpu

_F32 = jnp.float32
_BF16 = jnp.bfloat16

_LN_EPS = 1e-5
_RMS_EPS = 1e-6
_ROPE_THETA = 10000.0
_SWA_BLOCK = 128
_REL_BUCKETS = 32
_REL_MAX_DIST = 128
_LOG2E = math.log2(math.e)

_V7X_VMEM_BYTES = 64 * 1024 * 1024
_VMEM_LIMIT_BYTES = _V7X_VMEM_BYTES - 8 * 1024 * 1024
_LANES = 128

_TOKEN_TILE = 512
_ATTN_TILE = 512
_FF_CHUNK = 1024

_NT = (((1,), (1,)), ((), ()))
_TN = (((0,), (0,)), ((), ()))


def _const_spec(shape):
    nd = len(shape)
    return pl.BlockSpec(shape, lambda *_: (0,) * nd, pipeline_mode=pl.Buffered(1))


def _layernorm(v, g, b):
    mu = jnp.mean(v, axis=-1, keepdims=True)
    d = v - mu
    var = jnp.mean(d * d, axis=-1, keepdims=True)
    return d * lax.rsqrt(var + _LN_EPS) * g + b


def _rmsnorm(v, g):
    return v * lax.rsqrt(jnp.mean(v * v, axis=-1, keepdims=True) + _RMS_EPS) * g


def _mlp_ln(x1, alpha, wup_ref, wdown_ref, g_ref, b_ref):
    x1b = x1.astype(_BF16)
    d_ff = wup_ref.shape[1]
    acc = None
    for c in range(d_ff // _FF_CHUNK):
        lo = c * _FF_CHUNK
        hmid = jnp.dot(x1b, wup_ref[:, lo:lo + _FF_CHUNK], preferred_element_type=_F32)
        hmid = jnp.maximum(hmid, 0.0)
        hmid = (hmid * hmid).astype(_BF16)
        part = jnp.dot(hmid, wdown_ref[lo:lo + _FF_CHUNK, :], preferred_element_type=_F32)
        acc = part if acc is None else acc + part
    return _layernorm(alpha * x1 + acc, g_ref[...], b_ref[...])


def _mla_proj_kernel(x_ref, win_ref, gq_ref, gkv_ref, wqT_ref, wuk_ref, wvT_ref,
                     cs_ref, cosT_ref, sinT_ref, qT_ref, k_ref, vT_ref,
                     *, q_rank, kv_rank, heads, nope, rope, q_scale):
    xb = x_ref[...].astype(_BF16)
    h = jnp.dot(xb, win_ref[...], preferred_element_type=_F32)
    cq = _rmsnorm(h[:, :q_rank], gq_ref[...]).astype(_BF16)
    ckv = _rmsnorm(h[:, q_rank:q_rank + kv_rank], gkv_ref[...]).astype(_BF16)
    t = h[:, q_rank + kv_rank:] * cs_ref[...]
    kr = (t + pltpu.roll(t, rope, 1))[:, :rope].astype(_BF16)

    half = rope // 2
    hd = nope + rope
    qT = lax.dot_general(wqT_ref[...], cq, _NT, preferred_element_type=_F32)
    cosT = cosT_ref[...]
    sinT = sinT_ref[...]
    for hh in range(heads):
        base = hh * hd
        qT_ref[hh, 0:nope, :] = (qT[base:base + nope] * q_scale).astype(_BF16)
        x1 = qT[base + nope:base + nope + half]
        x2 = qT[base + nope + half:base + hd]
        qT_ref[hh, nope:nope + half, :] = ((x1 * cosT - x2 * sinT) * q_scale).astype(_BF16)
        qT_ref[hh, nope + half:hd, :] = ((x1 * sinT + x2 * cosT) * q_scale).astype(_BF16)

    kn = jnp.dot(ckv, wuk_ref[...], preferred_element_type=_F32)
    for hh in range(heads):
        k_ref[hh, :, 0:nope] = kn[:, hh * nope:(hh + 1) * nope].astype(_BF16)
        k_ref[hh, :, nope:hd] = kr

    vdim = wvT_ref.shape[0] // heads
    vT = lax.dot_general(wvT_ref[...], ckv, _NT, preferred_element_type=_F32)
    for hh in range(heads):
        vT_ref[hh] = vT[hh * vdim:(hh + 1) * vdim].astype(_BF16)


def _softmax_update(s, vT_chunk, m_scr, l_scr, acc_scr):
    m_prev = m_scr[...]
    m_new = jnp.maximum(m_prev, jnp.max(s, axis=0, keepdims=True))
    alpha = jnp.exp2(m_prev - m_new)
    p = jnp.exp2(s - m_new)
    l_scr[...] = alpha * l_scr[...] + jnp.sum(p, axis=0, keepdims=True)
    acc_scr[...] = alpha * acc_scr[...] + jnp.dot(
        vT_chunk, p.astype(_BF16), preferred_element_type=_F32)
    m_scr[...] = m_new


def _mla_attn_kernel(qT_ref, k_ref, vT_ref, o_ref, m_scr, l_scr, acc_scr):
    n_q, _, tq = qT_ref.shape
    tk = vT_ref.shape[2]

    def scores(qi, kj):
        k = k_ref[pl.ds(pl.multiple_of(kj * tk, tk), tk), :]
        return jnp.dot(k, qT_ref[qi], preferred_element_type=_F32)

    def masked_step(qi, kj):
        s = scores(qi, kj)
        kpos = kj * tk + lax.broadcasted_iota(jnp.int32, (tk, tq), 0)
        qpos = qi * tq + lax.broadcasted_iota(jnp.int32, (tk, tq), 1)
        s = jnp.where(kpos <= qpos, s, -jnp.inf)
        _softmax_update(s, vT_ref[kj], m_scr, l_scr, acc_scr)

    def q_body(qi, carry):
        m_scr[...] = jnp.full(m_scr.shape, -jnp.inf, _F32)
        l_scr[...] = jnp.zeros(l_scr.shape, _F32)
        acc_scr[...] = jnp.zeros(acc_scr.shape, _F32)

        def pair_body(pi, c):
            for u in range(2):
                kj = 2 * pi + u
                _softmax_update(scores(qi, kj), vT_ref[kj], m_scr, l_scr, acc_scr)
            return c

        n_pairs = qi // 2
        lax.fori_loop(0, n_pairs, pair_body, 0)

        c0 = 2 * n_pairs
        masked_step(qi, c0)

        @pl.when(c0 < qi)
        def _():
            masked_step(qi, c0 + 1)

        inv_l = 1.0 / l_scr[...]
        o = (acc_scr[...] * inv_l).T
        o_ref[pl.ds(pl.multiple_of(qi * tq, tq), tq), :] = o.astype(o_ref.dtype)
        return carry

    lax.fori_loop(0, n_q, q_body, 0)


def _outproj_mlp_kernel(o_ref, x_ref, wo_ref, g1_ref, b1_ref, wup_ref, wdown_ref,
                        g2_ref, b2_ref, out_ref, *, alpha):
    y = jnp.dot(o_ref[...], wo_ref[...], preferred_element_type=_F32)
    x1 = _layernorm(alpha * x_ref[...] + y, g1_ref[...], b1_ref[...])
    out_ref[...] = _mlp_ln(x1, alpha, wup_ref, wdown_ref, g2_ref, b2_ref)


def _swa_mlp_kernel(x_ref, wqT_ref, wk_ref, wvT_ref, wo_ref, bucketT_ref, relb_ref, sink_ref,
                    g1_ref, b1_ref, wup_ref, wdown_ref, g2_ref, b2_ref, out_ref,
                    kprev_scr, vTprev_scr, biasT_scr,
                    *, alpha, q_heads, kv_heads, head_dim, q_scale):
    blk = _SWA_BLOCK
    group = q_heads // kv_heads
    tm = x_ref.shape[0]
    n_blk = tm // blk
    first_tile = pl.program_id(1) == 0

    @pl.when((pl.program_id(0) == 0) & first_tile)
    def _():
        bkt = bucketT_ref[...]
        for hq in range(q_heads):
            bias = jnp.full(bkt.shape, -jnp.inf, _F32)
            for bb in range(_REL_BUCKETS):
                bias = jnp.where(bkt == bb, relb_ref[bb, hq] * _LOG2E, bias)
            g, gi = divmod(hq, group)
            biasT_scr[g, :, gi * blk:(gi + 1) * blk] = bias

    @pl.when(first_tile)
    def _():
        kprev_scr[...] = jnp.zeros(kprev_scr.shape, kprev_scr.dtype)
        vTprev_scr[...] = jnp.zeros(vTprev_scr.shape, vTprev_scr.dtype)

    x = x_ref[...]
    xb = x.astype(_BF16)
    qT = (lax.dot_general(wqT_ref[...], xb, _NT, preferred_element_type=_F32)
          * q_scale).astype(_BF16)
    k = jnp.dot(xb, wk_ref[...], preferred_element_type=_F32).astype(_BF16)
    vT = lax.dot_general(wvT_ref[...], xb, _NT, preferred_element_type=_F32).astype(_BF16)

    first_pen = jnp.where(first_tile, -jnp.inf, 0.0).astype(_F32)
    zeros_q = jnp.zeros((head_dim, group * blk), _BF16)
    heads_per_lane_tile = _LANES // head_dim

    oT_blocks = []
    for n in range(n_blk):
        lo, hi = n * blk, (n + 1) * blk
        k_prev = kprev_scr[...] if n == 0 else k[lo - blk:lo]
        kband = jnp.concatenate([k_prev, k[lo:hi]], axis=0)
        vT_prev = vTprev_scr[...] if n == 0 else vT[:, lo - blk:lo]
        vbandT = jnp.concatenate([vT_prev, vT[:, lo:hi]], axis=1)
        head_rows = []
        for g in range(kv_heads):
            lt = g // heads_per_lane_tile
            kpair = kband[:, lt * _LANES:(lt + 1) * _LANES]
            qg = jnp.concatenate(
                [qT[(g * group + gi) * head_dim:(g * group + gi + 1) * head_dim, lo:hi]
                 for gi in range(group)], axis=1)
            pieces = [zeros_q] * heads_per_lane_tile
            pieces[g % heads_per_lane_tile] = qg
            qpad = jnp.concatenate(pieces, axis=0)
            s = jnp.dot(kpair, qpad, preferred_element_type=_F32) + biasT_scr[g]
            if n == 0:
                s = jnp.concatenate([s[:blk] + first_pen, s[blk:]], axis=0)
            sink = jnp.concatenate(
                [jnp.full((1, blk), sink_ref[g * group + gi] * _LOG2E, _F32) for gi in range(group)],
                axis=1)
            m = jnp.maximum(jnp.max(s, axis=0, keepdims=True), sink)
            p = jnp.exp2(s - m)
            denom = jnp.sum(p, axis=0, keepdims=True) + jnp.exp2(sink - m)
            oT = jnp.dot(vbandT[g * head_dim:(g + 1) * head_dim, :], p.astype(_BF16),
                         preferred_element_type=_F32)
            oT = (oT * (1.0 / denom)).astype(_BF16)
            head_rows.extend(oT[:, gi * blk:(gi + 1) * blk] for gi in range(group))
        oT_blocks.append(jnp.concatenate(head_rows, axis=0))
    oT_tile = jnp.concatenate(oT_blocks, axis=1)

    kprev_scr[...] = k[tm - blk:tm]
    vTprev_scr[...] = vT[:, tm - blk:tm]

    y = lax.dot_general(oT_tile, wo_ref[...], _TN, preferred_element_type=_F32)
    x1 = _layernorm(alpha * x + y, g1_ref[...], b1_ref[...])
    out_ref[...] = _mlp_ln(x1, alpha, wup_ref, wdown_ref, g2_ref, b2_ref)


def _t5_bucket_table(blk):
    i = np.arange(blk)[None, :]
    j = np.arange(2 * blk)[:, None]
    dist = i + blk - j
    max_exact = _REL_BUCKETS // 2
    nf = np.maximum(dist, 1).astype(np.float32)
    large = max_exact + (np.log(nf / np.float32(max_exact))
                         / np.float32(math.log(_REL_MAX_DIST / max_exact))
                         * np.float32(_REL_BUCKETS - max_exact)).astype(np.int32)
    large = np.minimum(large, _REL_BUCKETS - 1)
    bucket = np.where(dist < max_exact, np.maximum(dist, 0), large)
    valid = (dist >= 0) & (dist < blk)
    return np.where(valid, bucket, -1).astype(np.int32)


def _rope_tables(seq, rope):
    half = rope // 2
    inv = _ROPE_THETA ** (-jnp.arange(half, dtype=_F32) / half)
    ang = jnp.arange(seq).astype(_F32)[:, None] * inv[None, :]
    cos, sin = jnp.cos(ang), jnp.sin(ang)
    cs = jnp.concatenate([cos, cos, -sin, sin], axis=1)
    return cs, cos.T, sin.T


def _row(v):
    return v.reshape(1, -1).astype(_F32)


def kernel(x, mla_w_in, mla_g_q, mla_g_kv, mla_w_uq, mla_w_uk, mla_w_uv, mla_w_o, kv_w_shared,
           swa_w_q, swa_sinks, swa_w_o, rel_bias, mlp_w_up, mlp_w_down, ln_mix_g, ln_mix_b,
           ln_mlp_g, ln_mlp_b):
    B, S, D = x.shape
    depth = mlp_w_up.shape[0]
    assert depth == 2 and mla_w_in.shape[0] == 1 and swa_w_q.shape[0] == 1
    alpha = (2 * depth) ** 0.25
    T = B * S
    tm = _TOKEN_TILE
    ta = _ATTN_TILE
    assert tm == ta and S % tm == 0

    q_rank, heads, hd = mla_w_uq.shape[1:]
    kv_rank, _, nope = mla_w_uk.shape[1:]
    vdim = mla_w_uv.shape[3]
    rope = hd - nope
    half = rope // 2
    assert 2 * rope == _LANES and mla_w_in.shape[2] == q_rank + kv_rank + rope

    w_in = mla_w_in[0]
    r0 = q_rank + kv_rank
    w_in_ext = jnp.concatenate(
        [w_in, w_in[:, r0 + half:r0 + rope], w_in[:, r0:r0 + half]], axis=1).astype(_BF16)
    wqT = mla_w_uq[0].reshape(q_rank, heads * hd).T.astype(_BF16)
    wuk = mla_w_uk[0].reshape(kv_rank, heads * nope).astype(_BF16)
    wvT = mla_w_uv[0].reshape(kv_rank, heads * vdim).T.astype(_BF16)
    cs, cosT, sinT = _rope_tables(S, rope)
    n_t = S // tm
    q_scale = hd ** -0.5 * _LOG2E

    cparams2 = pltpu.CompilerParams(dimension_semantics=("arbitrary", "arbitrary"),
                                    vmem_limit_bytes=_VMEM_LIMIT_BYTES)
    cparams1 = pltpu.CompilerParams(dimension_semantics=("arbitrary",),
                                    vmem_limit_bytes=_VMEM_LIMIT_BYTES)

    qT_all, k_all, vT_all = pl.pallas_call(
        functools.partial(_mla_proj_kernel, q_rank=q_rank, kv_rank=kv_rank, heads=heads,
                          nope=nope, rope=rope, q_scale=q_scale),
        grid=(B, n_t),
        in_specs=[
            pl.BlockSpec((None, tm, D), lambda b, i: (b, i, 0)),
            _const_spec(w_in_ext.shape),
            _const_spec((1, q_rank)),
            _const_spec((1, kv_rank)),
            _const_spec(wqT.shape),
            _const_spec(wuk.shape),
            _const_spec(wvT.shape),
            pl.BlockSpec((tm, 2 * rope), lambda b, i: (i, 0)),
            pl.BlockSpec((half, tm), lambda b, i: (0, i)),
            pl.BlockSpec((half, tm), lambda b, i: (0, i)),
        ],
        out_specs=[
            pl.BlockSpec((None, heads, None, hd, tm), lambda b, i: (b, 0, i, 0, 0)),
            pl.BlockSpec((None, heads, tm, hd), lambda b, i: (b, 0, i, 0)),
            pl.BlockSpec((None, heads, None, vdim, tm), lambda b, i: (b, 0, i, 0, 0)),
        ],
        out_shape=[
            jax.ShapeDtypeStruct((B, heads, n_t, hd, tm), _BF16),
            jax.ShapeDtypeStruct((B, heads, S, hd), _BF16),
            jax.ShapeDtypeStruct((B, heads, n_t, vdim, tm), _BF16),
        ],
        compiler_params=cparams2,
        name="mla_proj",
    )(x, w_in_ext, _row(mla_g_q[0]), _row(mla_g_kv[0]), wqT, wuk, wvT, cs, cosT, sinT)

    o = pl.pallas_call(
        _mla_attn_kernel,
        grid=(B, heads),
        in_specs=[
            pl.BlockSpec((None, None, n_t, hd, ta), lambda b, h: (b, h, 0, 0, 0)),
            pl.BlockSpec((None, None, S, hd), lambda b, h: (b, h, 0, 0)),
            pl.BlockSpec((None, None, n_t, vdim, ta), lambda b, h: (b, h, 0, 0, 0)),
        ],
        out_specs=pl.BlockSpec((None, S, vdim), lambda b, h: (b, 0, h)),
        out_shape=jax.ShapeDtypeStruct((B, S, heads * vdim), _BF16),
        scratch_shapes=[
            pltpu.VMEM((1, ta), _F32),
            pltpu.VMEM((1, ta), _F32),
            pltpu.VMEM((vdim, ta), _F32),
        ],
        compiler_params=cparams2,
        name="mla_attn",
    )(qT_all, k_all, vT_all)

    d_ff = mlp_w_up.shape[2]
    x1 = pl.pallas_call(
        functools.partial(_outproj_mlp_kernel, alpha=alpha),
        grid=(T // tm,),
        in_specs=[
            pl.BlockSpec((tm, heads * vdim), lambda i: (i, 0)),
            pl.BlockSpec((tm, D), lambda i: (i, 0)),
            _const_spec((heads * vdim, D)),
            _const_spec((1, D)),
            _const_spec((1, D)),
            _const_spec((D, d_ff)),
            _const_spec((d_ff, D)),
            _const_spec((1, D)),
            _const_spec((1, D)),
        ],
        out_specs=pl.BlockSpec((tm, D), lambda i: (i, 0)),
        out_shape=jax.ShapeDtypeStruct((T, D), _F32),
        compiler_params=cparams1,
        name="mla_out_mlp",
    )(o.reshape(T, heads * vdim), x.reshape(T, D), mla_w_o[0].astype(_BF16),
      _row(ln_mix_g[0]), _row(ln_mix_b[0]), mlp_w_up[0].astype(_BF16),
      mlp_w_down[0].astype(_BF16), _row(ln_mlp_g[0]), _row(ln_mlp_b[0]))

    q_heads = swa_sinks.shape[1]
    s_hd = swa_w_q.shape[2] // q_heads
    kv_heads = kv_w_shared.shape[1] // (2 * s_hd)
    group = q_heads // kv_heads
    blk = _SWA_BLOCK
    assert _LANES % s_hd == 0 and tm % blk == 0
    wqT1 = swa_w_q[0].T.astype(_BF16)
    wk1 = kv_w_shared[:, :kv_heads * s_hd].astype(_BF16)
    wvT1 = kv_w_shared[:, kv_heads * s_hd:].T.astype(_BF16)
    bucketT = jnp.asarray(_t5_bucket_table(blk))

    smem = pl.BlockSpec(memory_space=pltpu.SMEM)
    out = pl.pallas_call(
        functools.partial(_swa_mlp_kernel, alpha=alpha, q_heads=q_heads, kv_heads=kv_heads,
                          head_dim=s_hd, q_scale=s_hd ** -0.5 * _LOG2E),
        grid=(B, n_t),
        in_specs=[
            pl.BlockSpec((None, tm, D), lambda b, i: (b, i, 0)),
            _const_spec(wqT1.shape),
            _const_spec(wk1.shape),
            _const_spec(wvT1.shape),
            _const_spec((q_heads * s_hd, D)),
            _const_spec(bucketT.shape),
            smem,
            smem,
            _const_spec((1, D)),
            _const_spec((1, D)),
            _const_spec((D, d_ff)),
            _const_spec((d_ff, D)),
            _const_spec((1, D)),
            _const_spec((1, D)),
        ],
        out_specs=pl.BlockSpec((None, tm, D), lambda b, i: (b, i, 0)),
        out_shape=jax.ShapeDtypeStruct((B, S, D), _F32),
        scratch_shapes=[
            pltpu.VMEM((blk, kv_heads * s_hd), _BF16),
            pltpu.VMEM((kv_heads * s_hd, blk), _BF16),
            pltpu.VMEM((kv_heads, 2 * blk, group * blk), _F32),
        ],
        compiler_params=cparams2,
        name="swa_mlp",
    )(x1.reshape(B, S, D), wqT1, wk1, wvT1, swa_w_o[0].astype(_BF16), bucketT,
      rel_bias.astype(_F32), swa_sinks[0].astype(_F32),
      _row(ln_mix_g[1]), _row(ln_mix_b[1]), mlp_w_up[1].astype(_BF16),
      mlp_w_down[1].astype(_BF16), _row(ln_mlp_g[1]), _row(ln_mlp_b[1]))
    return out
```

```python
import functools
import math

import numpy as np
import jax
import jax.numpy as jnp
from jax import lax
from jax.experimental import pallas as pl
from jax.experimental.pallas import tpu as pltpu

_F32 = jnp.float32
_BF16 = jnp.bfloat16

_LN_EPS = 1e-5
_RMS_EPS = 1e-6
_ROPE_THETA = 10000.0
_SWA_BLOCK = 128
_REL_BUCKETS = 32
_REL_MAX_DIST = 128
_LOG2E = math.log2(math.e)

_V7X_VMEM_BYTES = 64 * 1024 * 1024
_VMEM_LIMIT_BYTES = _V7X_VMEM_BYTES - 8 * 1024 * 1024
_LANES = 128

_TOKEN_TILE = 512
_ATTN_TILE = 1024
_FF_CHUNK = 1024

_NT = (((1,), (1,)), ((), ()))
_TN = (((0,), (0,)), ((), ()))


def _const_spec(shape):
    nd = len(shape)
    return pl.BlockSpec(shape, lambda *_: (0,) * nd, pipeline_mode=pl.Buffered(1))


def _layernorm(v, g, b):
    mu = jnp.mean(v, axis=-1, keepdims=True)
    d = v - mu
    var = jnp.mean(d * d, axis=-1, keepdims=True)
    return d * lax.rsqrt(var + _LN_EPS) * g + b


def _rmsnorm(v, g):
    return v * lax.rsqrt(jnp.mean(v * v, axis=-1, keepdims=True) + _RMS_EPS) * g


def _mlp_ln(x1, alpha, wup_ref, wdown_ref, g_ref, b_ref):
    x1b = x1.astype(_BF16)
    d_ff = wup_ref.shape[1]
    acc = None
    for c in range(d_ff // _FF_CHUNK):
        lo = c * _FF_CHUNK
        hmid = jnp.dot(x1b, wup_ref[:, lo:lo + _FF_CHUNK], preferred_element_type=_F32)
        hmid = jnp.maximum(hmid, 0.0)
        hmid = (hmid * hmid).astype(_BF16)
        part = jnp.dot(hmid, wdown_ref[lo:lo + _FF_CHUNK, :], preferred_element_type=_F32)
        acc = part if acc is None else acc + part
    return _layernorm(alpha * x1 + acc, g_ref[...], b_ref[...])


def _mla_proj_kernel(x_ref, win_ref, gq_ref, gkv_ref, wqT_ref, wuk_ref, wvT_ref,
                     cs_ref, cosT_ref, sinT_ref, qT_ref, k_ref, vT_ref,
                     *, q_rank, kv_rank, heads, nope, rope, q_scale):
    xb = x_ref[...].astype(_BF16)
    h = jnp.dot(xb, win_ref[...], preferred_element_type=_F32)
    cq = _rmsnorm(h[:, :q_rank], gq_ref[...]).astype(_BF16)
    ckv = _rmsnorm(h[:, q_rank:q_rank + kv_rank], gkv_ref[...]).astype(_BF16)
    t = h[:, q_rank + kv_rank:] * cs_ref[...]
    kr = (t + pltpu.roll(t, rope, 1))[:, :rope].astype(_BF16)

    half = rope // 2
    hd = nope + rope
    qT = lax.dot_general(wqT_ref[...], cq, _NT, preferred_element_type=_F32)
    cosT = cosT_ref[...]
    sinT = sinT_ref[...]
    for hh in range(heads):
        base = hh * hd
        qT_ref[hh, 0:nope, :] = (qT[base:base + nope] * q_scale).astype(_BF16)
        x1 = qT[base + nope:base + nope + half]
        x2 = qT[base + nope + half:base + hd]
        qT_ref[hh, nope:nope + half, :] = ((x1 * cosT - x2 * sinT) * q_scale).astype(_BF16)
        qT_ref[hh, nope + half:hd, :] = ((x1 * sinT + x2 * cosT) * q_scale).astype(_BF16)

    kn = jnp.dot(ckv, wuk_ref[...], preferred_element_type=_F32)
    for hh in range(heads):
        k_ref[hh, :, 0:nope] = kn[:, hh * nope:(hh + 1) * nope].astype(_BF16)
        k_ref[hh, :, nope:hd] = kr

    vdim = wvT_ref.shape[0] // heads
    vT = lax.dot_general(wvT_ref[...], ckv, _NT, preferred_element_type=_F32)
    tk = vT_ref.shape[3]
    for hh in range(heads):
        for j in range(vT_ref.shape[1]):
            vT_ref[hh, j] = vT[hh * vdim:(hh + 1) * vdim, j * tk:(j + 1) * tk].astype(_BF16)


def _mla_attn_kernel(qT_ref, k_ref, vT_ref, o_ref, m_scr, l_scr, acc_scr,
                     s_a, s_b, p_a, p_b, cm_a, cm_b, al_a, al_b):
    tk, tq = s_a.shape
    tsub = qT_ref.shape[2]
    nsub = tq // tsub
    n_q = qT_ref.shape[0] // nsub
    assert tq == 2 * tk and tk == vT_ref.shape[2]

    def q_stage(qi, c, s_ref, cm_ref):
        k = k_ref[pl.ds(pl.multiple_of(c * tk, tk), tk), :]
        for j in range(nsub):
            s = jnp.dot(k, qT_ref[qi * nsub + j], preferred_element_type=_F32)
            s_ref[:, j * tsub:(j + 1) * tsub] = s
            cm_ref[:, j * tsub:(j + 1) * tsub] = jnp.max(s, axis=0, keepdims=True)

    def x_stage(s_ref, cm_ref, p_ref, al_ref, mask=None):
        s = s_ref[...]
        if mask is None:
            cm = cm_ref[...]
        else:
            s = jnp.where(mask, s, -jnp.inf)
            cm = jnp.max(s, axis=0, keepdims=True)
        m_prev = m_scr[...]
        m_new = jnp.maximum(m_prev, cm)
        alpha = jnp.exp2(m_prev - m_new)
        p = jnp.exp2(s - m_new)
        l_scr[...] = alpha * l_scr[...] + jnp.sum(p, axis=0, keepdims=True)
        p_ref[...] = p.astype(_BF16)
        al_ref[...] = alpha
        m_scr[...] = m_new

    def v_stage(c, p_ref, al_ref):
        acc_scr[...] = al_ref[...] * acc_scr[...] + jnp.dot(
            vT_ref[c], p_ref[...], preferred_element_type=_F32)

    krel = lax.broadcasted_iota(jnp.int32, (tk, tq), 0)
    qrel = lax.broadcasted_iota(jnp.int32, (tk, tq), 1)
    mask_a = krel <= qrel
    mask_b = krel + tk <= qrel

    def q_body(qi, carry):
        m_scr[...] = jnp.full(m_scr.shape, -jnp.inf, _F32)
        l_scr[...] = jnp.zeros(l_scr.shape, _F32)
        acc_scr[...] = jnp.zeros(acc_scr.shape, _F32)
        p_b[...] = jnp.zeros(p_b.shape, p_b.dtype)
        al_b[...] = jnp.ones(al_b.shape, _F32)
        q_stage(qi, 0, s_a, cm_a)

        def pair_body(u, c):
            a = 2 * u
            q_stage(qi, a + 1, s_b, cm_b)
            x_stage(s_a, cm_a, p_a, al_a)
            v_stage(jnp.maximum(a - 1, 0), p_b, al_b)
            q_stage(qi, a + 2, s_a, cm_a)
            x_stage(s_b, cm_b, p_b, al_b)
            v_stage(a, p_a, al_a)
            return c

        lax.fori_loop(0, qi, pair_body, 0)

        a = 2 * qi
        q_stage(qi, a + 1, s_b, cm_b)
        x_stage(s_a, cm_a, p_a, al_a, mask=mask_a)
        v_stage(jnp.maximum(a - 1, 0), p_b, al_b)
        x_stage(s_b, cm_b, p_b, al_b, mask=mask_b)
        v_stage(a, p_a, al_a)
        v_stage(a + 1, p_b, al_b)

        inv_l = 1.0 / l_scr[...]
        o = (acc_scr[...] * inv_l).T
        o_ref[pl.ds(pl.multiple_of(qi * tq, tq), tq), :] = o.astype(o_ref.dtype)
        return carry

    lax.fori_loop(0, n_q, q_body, 0)


def _outproj_mlp_kernel(o_ref, x_ref, wo_ref, g1_ref, b1_ref, wup_ref, wdown_ref,
                        g2_ref, b2_ref, out_ref, *, alpha):
    y = jnp.dot(o_ref[...], wo_ref[...], preferred_element_type=_F32)
    x1 = _layernorm(alpha * x_ref[...] + y, g1_ref[...], b1_ref[...])
    out_ref[...] = _mlp_ln(x1, alpha, wup_ref, wdown_ref, g2_ref, b2_ref)


def _swa_mlp_kernel(x_ref, wqT_ref, wk_ref, wvT_ref, wo_ref, bucketT_ref, relb_ref, sink_ref,
                    g1_ref, b1_ref, wup_ref, wdown_ref, g2_ref, b2_ref, out_ref,
                    kprev_scr, vTprev_scr, biasT_scr,
                    *, alpha, q_heads, kv_heads, head_dim, q_scale):
    blk = _SWA_BLOCK
    group = q_heads // kv_heads
    tm = x_ref.shape[0]
    n_blk = tm // blk
    first_tile = pl.program_id(1) == 0

    @pl.when((pl.program_id(0) == 0) & first_tile)
    def _():
        bkt = bucketT_ref[...]
        for hq in range(q_heads):
            bias = jnp.full(bkt.shape, -jnp.inf, _F32)
            for bb in range(_REL_BUCKETS):
                bias = jnp.where(bkt == bb, relb_ref[bb, hq] * _LOG2E, bias)
            g, gi = divmod(hq, group)
            biasT_scr[g, :, gi * blk:(gi + 1) * blk] = bias

    @pl.when(first_tile)
    def _():
        kprev_scr[...] = jnp.zeros(kprev_scr.shape, kprev_scr.dtype)
        vTprev_scr[...] = jnp.zeros(vTprev_scr.shape, vTprev_scr.dtype)

    x = x_ref[...]
    xb = x.astype(_BF16)
    qT = (lax.dot_general(wqT_ref[...], xb, _NT, preferred_element_type=_F32)
          * q_scale).astype(_BF16)
    k = jnp.dot(xb, wk_ref[...], preferred_element_type=_F32).astype(_BF16)
    vT = lax.dot_general(wvT_ref[...], xb, _NT, preferred_element_type=_F32).astype(_BF16)

    first_pen = jnp.where(first_tile, -jnp.inf, 0.0).astype(_F32)
    zeros_q = jnp.zeros((head_dim, group * blk), _BF16)
    heads_per_lane_tile = _LANES // head_dim

    oT_blocks = []
    for n in range(n_blk):
        lo, hi = n * blk, (n + 1) * blk
        k_prev = kprev_scr[...] if n == 0 else k[lo - blk:lo]
        kband = jnp.concatenate([k_prev, k[lo:hi]], axis=0)
        vT_prev = vTprev_scr[...] if n == 0 else vT[:, lo - blk:lo]
        vbandT = jnp.concatenate([vT_prev, vT[:, lo:hi]], axis=1)
        head_rows = []
        for g in range(kv_heads):
            lt = g // heads_per_lane_tile
            kpair = kband[:, lt * _LANES:(lt + 1) * _LANES]
            qg = jnp.concatenate(
                [qT[(g * group + gi) * head_dim:(g * group + gi + 1) * head_dim, lo:hi]
                 for gi in range(group)], axis=1)
            pieces = [zeros_q] * heads_per_lane_tile
            pieces[g % heads_per_lane_tile] = qg
            qpad = jnp.concatenate(pieces, axis=0)
            s = jnp.dot(kpair, qpad, preferred_element_type=_F32) + biasT_scr[g]
            if n == 0:
                s = jnp.concatenate([s[:blk] + first_pen, s[blk:]], axis=0)
            sink = jnp.concatenate(
                [jnp.full((1, blk), sink_ref[g * group + gi] * _LOG2E, _F32) for gi in range(group)],
                axis=1)
            m = jnp.maximum(jnp.max(s, axis=0, keepdims=True), sink)
            p = jnp.exp2(s - m)
            denom = jnp.sum(p, axis=0, keepdims=True) + jnp.exp2(sink - m)
            oT = jnp.dot(vbandT[g * head_dim:(g + 1) * head_dim, :], p.astype(_BF16),
                         preferred_element_type=_F32)
            oT = (oT * (1.0 / denom)).astype(_BF16)
            head_rows.extend(oT[:, gi * blk:(gi + 1) * blk] for gi in range(group))
        oT_blocks.append(jnp.concatenate(head_rows, axis=0))
    oT_tile = jnp.concatenate(oT_blocks, axis=1)

    kprev_scr[...] = k[tm - blk:tm]
    vTprev_scr[...] = vT[:, tm - blk:tm]

    y = lax.dot_general(oT_tile, wo_ref[...], _TN, preferred_element_type=_F32)
    x1 = _layernorm(alpha * x + y, g1_ref[...], b1_ref[...])
    out_ref[...] = _mlp_ln(x1, alpha, wup_ref, wdown_ref, g2_ref, b2_ref)


def _t5_bucket_table(blk):
    i = np.arange(blk)[None, :]
    j = np.arange(2 * blk)[:, None]
    dist = i + blk - j
    max_exact = _REL_BUCKETS // 2
    nf = np.maximum(dist, 1).astype(np.float32)
    large = max_exact + (np.log(nf / np.float32(max_exact))
                         / np.float32(math.log(_REL_MAX_DIST / max_exact))
                         * np.float32(_REL_BUCKETS - max_exact)).astype(np.int32)
    large = np.minimum(large, _REL_BUCKETS - 1)
    bucket = np.where(dist < max_exact, np.maximum(dist, 0), large)
    valid = (dist >= 0) & (dist < blk)
    return np.where(valid, bucket, -1).astype(np.int32)


def _rope_tables(seq, rope):
    half = rope // 2
    inv = _ROPE_THETA ** (-jnp.arange(half, dtype=_F32) / half)
    ang = jnp.arange(seq).astype(_F32)[:, None] * inv[None, :]
    cos, sin = jnp.cos(ang), jnp.sin(ang)
    cs = jnp.concatenate([cos, cos, -sin, sin], axis=1)
    return cs, cos.T, sin.T


def _row(v):
    return v.reshape(1, -1).astype(_F32)


def kernel(x, mla_w_in, mla_g_q, mla_g_kv, mla_w_uq, mla_w_uk, mla_w_uv, mla_w_o, kv_w_shared,
           swa_w_q, swa_sinks, swa_w_o, rel_bias, mlp_w_up, mlp_w_down, ln_mix_g, ln_mix_b,
           ln_mlp_g, ln_mlp_b):
    B, S, D = x.shape
    depth = mlp_w_up.shape[0]
    assert depth == 2 and mla_w_in.shape[0] == 1 and swa_w_q.shape[0] == 1
    alpha = (2 * depth) ** 0.25
    T = B * S
    tm = _TOKEN_TILE
    ta = _ATTN_TILE
    tk = ta // 2
    assert ta % tm == 0 and tm % tk == 0 and S % ta == 0

    q_rank, heads, hd = mla_w_uq.shape[1:]
    kv_rank, _, nope = mla_w_uk.shape[1:]
    vdim = mla_w_uv.shape[3]
    rope = hd - nope
    half = rope // 2
    assert 2 * rope == _LANES and mla_w_in.shape[2] == q_rank + kv_rank + rope

    w_in = mla_w_in[0]
    r0 = q_rank + kv_rank
    w_in_ext = jnp.concatenate(
        [w_in, w_in[:, r0 + half:r0 + rope], w_in[:, r0:r0 + half]], axis=1).astype(_BF16)
    wqT = mla_w_uq[0].reshape(q_rank, heads * hd).T.astype(_BF16)
    wuk = mla_w_uk[0].reshape(kv_rank, heads * nope).astype(_BF16)
    wvT = mla_w_uv[0].reshape(kv_rank, heads * vdim).T.astype(_BF16)
    cs, cosT, sinT = _rope_tables(S, rope)
    n_t = S // tm
    q_scale = hd ** -0.5 * _LOG2E

    cparams2 = pltpu.CompilerParams(dimension_semantics=("arbitrary", "arbitrary"),
                                    vmem_limit_bytes=_VMEM_LIMIT_BYTES)
    cparams1 = pltpu.CompilerParams(dimension_semantics=("arbitrary",),
                                    vmem_limit_bytes=_VMEM_LIMIT_BYTES)

    qT_all, k_all, vT_all = pl.pallas_call(
        functools.partial(_mla_proj_kernel, q_rank=q_rank, kv_rank=kv_rank, heads=heads,
                          nope=nope, rope=rope, q_scale=q_scale),
        grid=(B, n_t),
        in_specs=[
            pl.BlockSpec((None, tm, D), lambda b, i: (b, i, 0)),
            _const_spec(w_in_ext.shape),
            _const_spec((1, q_rank)),
            _const_spec((1, kv_rank)),
            _const_spec(wqT.shape),
            _const_spec(wuk.shape),
            _const_spec(wvT.shape),
            pl.BlockSpec((tm, 2 * rope), lambda b, i: (i, 0)),
            pl.BlockSpec((half, tm), lambda b, i: (0, i)),
            pl.BlockSpec((half, tm), lambda b, i: (0, i)),
        ],
        out_specs=[
            pl.BlockSpec((None, heads, None, hd, tm), lambda b, i: (b, 0, i, 0, 0)),
            pl.BlockSpec((None, heads, tm, hd), lambda b, i: (b, 0, i, 0)),
            pl.BlockSpec((None, heads, tm // tk, vdim, tk), lambda b, i: (b, 0, i, 0, 0)),
        ],
        out_shape=[
            jax.ShapeDtypeStruct((B, heads, n_t, hd, tm), _BF16),
            jax.ShapeDtypeStruct((B, heads, S, hd), _BF16),
            jax.ShapeDtypeStruct((B, heads, S // tk, vdim, tk), _BF16),
        ],
        compiler_params=cparams2,
        name="mla_proj",
    )(x, w_in_ext, _row(mla_g_q[0]), _row(mla_g_kv[0]), wqT, wuk, wvT, cs, cosT, sinT)

    o = pl.pallas_call(
        _mla_attn_kernel,
        grid=(B, heads),
        in_specs=[
            pl.BlockSpec((None, None, n_t, hd, tm), lambda b, h: (b, h, 0, 0, 0)),
            pl.BlockSpec((None, None, S, hd), lambda b, h: (b, h, 0, 0)),
            pl.BlockSpec((None, None, S // tk, vdim, tk), lambda b, h: (b, h, 0, 0, 0)),
        ],
        out_specs=pl.BlockSpec((None, S, vdim), lambda b, h: (b, 0, h)),
        out_shape=jax.ShapeDtypeStruct((B, S, heads * vdim), _BF16),
        scratch_shapes=[
            pltpu.VMEM((1, ta), _F32),
            pltpu.VMEM((1, ta), _F32),
            pltpu.VMEM((vdim, ta), _F32),
            pltpu.VMEM((tk, ta), _F32),
            pltpu.VMEM((tk, ta), _F32),
            pltpu.VMEM((tk, ta), _BF16),
            pltpu.VMEM((tk, ta), _BF16),
            pltpu.VMEM((1, ta), _F32),
            pltpu.VMEM((1, ta), _F32),
            pltpu.VMEM((1, ta), _F32),
            pltpu.VMEM((1, ta), _F32),
        ],
        compiler_params=cparams2,
        name="mla_attn",
    )(qT_all, k_all, vT_all)

    d_ff = mlp_w_up.shape[2]
    x1 = pl.pallas_call(
        functools.partial(_outproj_mlp_kernel, alpha=alpha),
        grid=(T // tm,),
        in_specs=[
            pl.BlockSpec((tm, heads * vdim), lambda i: (i, 0)),
            pl.BlockSpec((tm, D), lambda i: (i, 0)),
            _const_spec((heads * vdim, D)),
            _const_spec((1, D)),
            _const_spec((1, D)),
            _const_spec((D, d_ff)),
            _const_spec((d_ff, D)),
            _const_spec((1, D)),
            _const_spec((1, D)),
        ],
        out_specs=pl.BlockSpec((tm, D), lambda i: (i, 0)),
        out_shape=jax.ShapeDtypeStruct((T, D), _F32),
        compiler_params=cparams1,
        name="mla_out_mlp",
    )(o.reshape(T, heads * vdim), x.reshape(T, D), mla_w_o[0].astype(_BF16),
      _row(ln_mix_g[0]), _row(ln_mix_b[0]), mlp_w_up[0].astype(_BF16),
      mlp_w_down[0].astype(_BF16), _row(ln_mlp_g[0]), _row(ln_mlp_b[0]))

    q_heads = swa_sinks.shape[1]
    s_hd = swa_w_q.shape[2] // q_heads
    kv_heads = kv_w_shared.shape[1] // (2 * s_hd)
    group = q_heads // kv_heads
    blk = _SWA_BLOCK
    assert _LANES % s_hd == 0 and tm % blk == 0
    wqT1 = swa_w_q[0].T.astype(_BF16)
    wk1 = kv_w_shared[:, :kv_heads * s_hd].astype(_BF16)
    wvT1 = kv_w_shared[:, kv_heads * s_hd:].T.astype(_BF16)
    bucketT = jnp.asarray(_t5_bucket_table(blk))

    smem = pl.BlockSpec(memory_space=pltpu.SMEM)
    out = pl.pallas_call(
        functools.partial(_swa_mlp_kernel, alpha=alpha, q_heads=q_heads, kv_heads=kv_heads,
                          head_dim=s_hd, q_scale=s_hd ** -0.5 * _LOG2E),
        grid=(B, n_t),
        in_specs=[
            pl.BlockSpec((None, tm, D), lambda b, i: (b, i, 0)),
            _const_spec(wqT1.shape),
            _const_spec(wk1.shape),
            _const_spec(wvT1.shape),
            _const_spec((q_heads * s_hd, D)),
            _const_spec(bucketT.shape),
            smem,
            smem,
            _const_spec((1, D)),
            _const_spec((1, D)),
            _const_spec((D, d_ff)),
            _const_spec((d_ff, D)),
            _const_spec((1, D)),
            _const_spec((1, D)),
        ],
        out_specs=pl.BlockSpec((None, tm, D), lambda b, i: (b, i, 0)),
        out_shape=jax.ShapeDtypeStruct((B, S, D), _F32),
        scratch_shapes=[
            pltpu.VMEM((blk, kv_heads * s_hd), _BF16),
            pltpu.VMEM((kv_heads * s_hd, blk), _BF16),
            pltpu.VMEM((kv_heads, 2 * blk, group * blk), _F32),
        ],
        compiler_params=cparams2,
        name="swa_mlp",
    )(x1.reshape(B, S, D), wqT1, wk1, wvT1, swa_w_o[0].astype(_BF16), bucketT,
      rel_bias.astype(_F32), swa_sinks[0].astype(_F32),
      _row(ln_mix_g[1]), _row(ln_mix_b[1]), mlp_w_up[1].astype(_BF16),
      mlp_w_down[1].astype(_BF16), _row(ln_mlp_g[1]), _row(ln_mlp_b[1]))
    return out
```

```python
import functools
import math

import numpy as np
import jax
import jax.numpy as jnp
from jax import lax
from jax.experimental import pallas as pl
from jax.experimental.pallas import tpu as pltpu

_F32 = jnp.float32
_BF16 = jnp.bfloat16

_LN_EPS = 1e-5
_RMS_EPS = 1e-6
_ROPE_THETA = 10000.0
_SWA_BLOCK = 128
_REL_BUCKETS = 32
_REL_MAX_DIST = 128
_LOG2E = math.log2(math.e)

_V7X_VMEM_BYTES = 64 * 1024 * 1024
_VMEM_LIMIT_BYTES = _V7X_VMEM_BYTES - 8 * 1024 * 1024
_LANES = 128

_TOKEN_TILE = 512
_ATTN_TILE = 1024
_ATTN_ROW_BLOCK = 512
_FF_CHUNK = 1024

_NT = (((1,), (1,)), ((), ()))
_TN = (((0,), (0,)), ((), ()))


def _const_spec(shape):
    nd = len(shape)
    return pl.BlockSpec(shape, lambda *_: (0,) * nd, pipeline_mode=pl.Buffered(1))


def _layernorm(v, g, b):
    mu = jnp.mean(v, axis=-1, keepdims=True)
    d = v - mu
    var = jnp.mean(d * d, axis=-1, keepdims=True)
    return d * lax.rsqrt(var + _LN_EPS) * g + b


def _rmsnorm(v, g):
    return v * lax.rsqrt(jnp.mean(v * v, axis=-1, keepdims=True) + _RMS_EPS) * g


def _mlp_ln(x1, alpha, wup_ref, wdown_ref, g_ref, b_ref):
    x1b = x1.astype(_BF16)
    d_ff = wup_ref.shape[1]
    acc = None
    for c in range(d_ff // _FF_CHUNK):
        lo = c * _FF_CHUNK
        hmid = jnp.dot(x1b, wup_ref[:, lo:lo + _FF_CHUNK], preferred_element_type=_F32)
        hmid = jnp.maximum(hmid, 0.0)
        hmid = (hmid * hmid).astype(_BF16)
        part = jnp.dot(hmid, wdown_ref[lo:lo + _FF_CHUNK, :], preferred_element_type=_F32)
        acc = part if acc is None else acc + part
    return _layernorm(alpha * x1 + acc, g_ref[...], b_ref[...])


def _mla_proj_kernel(x_ref, win_ref, gq_ref, gkv_ref, wqT_ref, wuk_ref, wvT_ref,
                     cs_ref, cosT_ref, sinT_ref, qT_ref, k_ref, vT_ref,
                     *, q_rank, kv_rank, heads, nope, rope, q_scale):
    xb = x_ref[...].astype(_BF16)
    h = jnp.dot(xb, win_ref[...], preferred_element_type=_F32)
    cq = _rmsnorm(h[:, :q_rank], gq_ref[...]).astype(_BF16)
    ckv = _rmsnorm(h[:, q_rank:q_rank + kv_rank], gkv_ref[...]).astype(_BF16)
    t = h[:, q_rank + kv_rank:] * cs_ref[...]
    kr = (t + pltpu.roll(t, rope, 1))[:, :rope].astype(_BF16)

    half = rope // 2
    hd = nope + rope
    qT = lax.dot_general(wqT_ref[...], cq, _NT, preferred_element_type=_F32)
    cosT = cosT_ref[...]
    sinT = sinT_ref[...]
    for hh in range(heads):
        base = hh * hd
        qT_ref[hh, 0:nope, :] = (qT[base:base + nope] * q_scale).astype(_BF16)
        x1 = qT[base + nope:base + nope + half]
        x2 = qT[base + nope + half:base + hd]
        qT_ref[hh, nope:nope + half, :] = ((x1 * cosT - x2 * sinT) * q_scale).astype(_BF16)
        qT_ref[hh, nope + half:hd, :] = ((x1 * sinT + x2 * cosT) * q_scale).astype(_BF16)

    kn = jnp.dot(ckv, wuk_ref[...], preferred_element_type=_F32)
    for hh in range(heads):
        k_ref[hh, :, 0:nope] = kn[:, hh * nope:(hh + 1) * nope].astype(_BF16)
        k_ref[hh, :, nope:hd] = kr

    vdim = wvT_ref.shape[0] // heads
    vT = lax.dot_general(wvT_ref[...], ckv, _NT, preferred_element_type=_F32)
    tk = vT_ref.shape[3]
    pad = vT_ref.shape[2] - vdim
    ones_rows = (lax.broadcasted_iota(jnp.int32, (pad, tk), 0) == 0).astype(_BF16)
    for hh in range(heads):
        for j in range(vT_ref.shape[1]):
            vT_ref[hh, j, 0:vdim, :] = vT[hh * vdim:(hh + 1) * vdim, j * tk:(j + 1) * tk].astype(_BF16)
            vT_ref[hh, j, vdim:vdim + pad, :] = ones_rows


def _mla_attn_kernel(qT_ref, k_ref, vT_ref, o_ref, m_scr, acc_scr,
                     s_a, s_b, p_a, p_b, cm_a, cm_b, al_a, al_b):
    tk, tq = s_a.shape
    tsub = qT_ref.shape[2]
    nsub = tq // tsub
    n_q = qT_ref.shape[0] // nsub
    rb = _ATTN_ROW_BLOCK
    assert tq == 2 * tk and tk == vT_ref.shape[2] and tk % rb == 0

    def q_stage(qi, c, s_ref, cm_ref, col_lo=0):
        for r0 in range(0, tk, rb):
            k = k_ref[pl.ds(pl.multiple_of(c * tk + r0, rb), rb), :]
            for j in range(col_lo // tsub, nsub):
                s = jnp.dot(k, qT_ref[qi * nsub + j], preferred_element_type=_F32)
                s_ref[r0:r0 + rb, j * tsub:(j + 1) * tsub] = s
                cm = jnp.max(s, axis=0, keepdims=True)
                if r0 > 0:
                    cm = jnp.maximum(cm, cm_ref[:, j * tsub:(j + 1) * tsub])
                cm_ref[:, j * tsub:(j + 1) * tsub] = cm

    def x_stage(s_ref, cm_ref, p_ref, al_ref, mask_off=None, col_lo=0):
        cols = slice(col_lo, tq)
        s = s_ref[:, cols]
        if mask_off is None:
            cm = cm_ref[:, cols]
        else:
            krel = lax.broadcasted_iota(jnp.int32, s.shape, 0) + mask_off
            qrel = lax.broadcasted_iota(jnp.int32, s.shape, 1) + col_lo
            s = jnp.where(krel <= qrel, s, -jnp.inf)
            cm = jnp.max(s, axis=0, keepdims=True)
        m_prev = m_scr[:, cols]
        m_new = jnp.maximum(m_prev, cm)
        alpha = jnp.exp2(m_prev - m_new)
        p = jnp.exp2(s - m_new)
        p_ref[:, cols] = p.astype(_BF16)
        al_ref[:, cols] = alpha
        m_scr[:, cols] = m_new

    def v_stage(c, p_ref, al_ref, col_lo=0):
        cols = slice(col_lo, tq)
        acc_scr[:, cols] = al_ref[:, cols] * acc_scr[:, cols] + jnp.dot(
            vT_ref[c], p_ref[:, cols], preferred_element_type=_F32)

    def q_body(qi, carry):
        m_scr[...] = jnp.full(m_scr.shape, -jnp.inf, _F32)
        acc_scr[...] = jnp.zeros(acc_scr.shape, _F32)
        p_b[...] = jnp.zeros(p_b.shape, p_b.dtype)
        al_b[...] = jnp.ones(al_b.shape, _F32)
        q_stage(qi, 0, s_a, cm_a)

        def pair_body(u, c):
            a = 2 * u
            q_stage(qi, a + 1, s_b, cm_b)
            x_stage(s_a, cm_a, p_a, al_a)
            v_stage(jnp.maximum(a - 1, 0), p_b, al_b)
            q_stage(qi, a + 2, s_a, cm_a)
            x_stage(s_b, cm_b, p_b, al_b)
            v_stage(a, p_a, al_a)
            return c

        lax.fori_loop(0, qi, pair_body, 0)

        a = 2 * qi
        q_stage(qi, a + 1, s_b, cm_b, col_lo=tk)
        x_stage(s_a, cm_a, p_a, al_a, mask_off=0)
        v_stage(jnp.maximum(a - 1, 0), p_b, al_b)
        x_stage(s_b, cm_b, p_b, al_b, mask_off=tk, col_lo=tk)
        v_stage(a, p_a, al_a)
        v_stage(a + 1, p_b, al_b, col_lo=tk)

        vdim = o_ref.shape[1]
        inv_l = 1.0 / acc_scr[vdim:vdim + 1, :]
        o = (acc_scr[0:vdim, :] * inv_l).T
        o_ref[pl.ds(pl.multiple_of(qi * tq, tq), tq), :] = o.astype(o_ref.dtype)
        return carry

    lax.fori_loop(0, n_q, q_body, 0)


def _outproj_mlp_kernel(o_ref, x_ref, wo_ref, g1_ref, b1_ref, wup_ref, wdown_ref,
                        g2_ref, b2_ref, out_ref, *, alpha):
    y = jnp.dot(o_ref[...], wo_ref[...], preferred_element_type=_F32)
    x1 = _layernorm(alpha * x_ref[...] + y, g1_ref[...], b1_ref[...])
    out_ref[...] = _mlp_ln(x1, alpha, wup_ref, wdown_ref, g2_ref, b2_ref)


def _swa_mlp_kernel(x_ref, wqT_ref, wk_ref, wvT_ref, wo_ref, bucketT_ref, relb_ref, sink_ref,
                    g1_ref, b1_ref, wup_ref, wdown_ref, g2_ref, b2_ref, out_ref,
                    kprev_scr, vTprev_scr, biasT_scr,
                    *, alpha, q_heads, kv_heads, head_dim, q_scale):
    blk = _SWA_BLOCK
    group = q_heads // kv_heads
    tm = x_ref.shape[0]
    n_blk = tm // blk
    first_tile = pl.program_id(1) == 0

    @pl.when((pl.program_id(0) == 0) & first_tile)
    def _():
        bkt = bucketT_ref[...]
        for hq in range(q_heads):
            bias = jnp.full(bkt.shape, -jnp.inf, _F32)
            for bb in range(_REL_BUCKETS):
                bias = jnp.where(bkt == bb, relb_ref[bb, hq] * _LOG2E, bias)
            g, gi = divmod(hq, group)
            biasT_scr[g, :, gi * blk:(gi + 1) * blk] = bias

    @pl.when(first_tile)
    def _():
        kprev_scr[...] = jnp.zeros(kprev_scr.shape, kprev_scr.dtype)
        vTprev_scr[...] = jnp.zeros(vTprev_scr.shape, vTprev_scr.dtype)

    x = x_ref[...]
    xb = x.astype(_BF16)
    qT = (lax.dot_general(wqT_ref[...], xb, _NT, preferred_element_type=_F32)
          * q_scale).astype(_BF16)
    k = jnp.dot(xb, wk_ref[...], preferred_element_type=_F32).astype(_BF16)
    vT = lax.dot_general(wvT_ref[...], xb, _NT, preferred_element_type=_F32).astype(_BF16)

    first_pen = jnp.where(first_tile, -jnp.inf, 0.0).astype(_F32)
    zeros_q = jnp.zeros((head_dim, group * blk), _BF16)
    heads_per_lane_tile = _LANES // head_dim

    oT_blocks = []
    for n in range(n_blk):
        lo, hi = n * blk, (n + 1) * blk
        k_prev = kprev_scr[...] if n == 0 else k[lo - blk:lo]
        kband = jnp.concatenate([k_prev, k[lo:hi]], axis=0)
        vT_prev = vTprev_scr[...] if n == 0 else vT[:, lo - blk:lo]
        vbandT = jnp.concatenate([vT_prev, vT[:, lo:hi]], axis=1)
        head_rows = []
        for g in range(kv_heads):
            lt = g // heads_per_lane_tile
            kpair = kband[:, lt * _LANES:(lt + 1) * _LANES]
            qg = jnp.concatenate(
                [qT[(g * group + gi) * head_dim:(g * group + gi + 1) * head_dim, lo:hi]
                 for gi in range(group)], axis=1)
            pieces = [zeros_q] * heads_per_lane_tile
            pieces[g % heads_per_lane_tile] = qg
            qpad = jnp.concatenate(pieces, axis=0)
            s = jnp.dot(kpair, qpad, preferred_element_type=_F32) + biasT_scr[g]
            if n == 0:
                s = jnp.concatenate([s[:blk] + first_pen, s[blk:]], axis=0)
            sink = jnp.concatenate(
                [jnp.full((1, blk), sink_ref[g * group + gi] * _LOG2E, _F32) for gi in range(group)],
                axis=1)
            m = jnp.maximum(jnp.max(s, axis=0, keepdims=True), sink)
            p = jnp.exp2(s - m)
            denom = jnp.sum(p, axis=0, keepdims=True) + jnp.exp2(sink - m)
            oT = jnp.dot(vbandT[g * head_dim:(g + 1) * head_dim, :], p.astype(_BF16),
                         preferred_element_type=_F32)
            oT = (oT * (1.0 / denom)).astype(_BF16)
            head_rows.extend(oT[:, gi * blk:(gi + 1) * blk] for gi in range(group))
        oT_blocks.append(jnp.concatenate(head_rows, axis=0))
    oT_tile = jnp.concatenate(oT_blocks, axis=1)

    kprev_scr[...] = k[tm - blk:tm]
    vTprev_scr[...] = vT[:, tm - blk:tm]

    y = lax.dot_general(oT_tile, wo_ref[...], _TN, preferred_element_type=_F32)
    x1 = _layernorm(alpha * x + y, g1_ref[...], b1_ref[...])
    out_ref[...] = _mlp_ln(x1, alpha, wup_ref, wdown_ref, g2_ref, b2_ref)


def _t5_bucket_table(blk):
    i = np.arange(blk)[None, :]
    j = np.arange(2 * blk)[:, None]
    dist = i + blk - j
    max_exact = _REL_BUCKETS // 2
    nf = np.maximum(dist, 1).astype(np.float32)
    large = max_exact + (np.log(nf / np.float32(max_exact))
                         / np.float32(math.log(_REL_MAX_DIST / max_exact))
                         * np.float32(_REL_BUCKETS - max_exact)).astype(np.int32)
    large = np.minimum(large, _REL_BUCKETS - 1)
    bucket = np.where(dist < max_exact, np.maximum(dist, 0), large)
    valid = (dist >= 0) & (dist < blk)
    return np.where(valid, bucket, -1).astype(np.int32)


def _rope_tables(seq, rope):
    half = rope // 2
    inv = _ROPE_THETA ** (-jnp.arange(half, dtype=_F32) / half)
    ang = jnp.arange(seq).astype(_F32)[:, None] * inv[None, :]
    cos, sin = jnp.cos(ang), jnp.sin(ang)
    cs = jnp.concatenate([cos, cos, -sin, sin], axis=1)
    return cs, cos.T, sin.T


def _row(v):
    return v.reshape(1, -1).astype(_F32)


def kernel(x, mla_w_in, mla_g_q, mla_g_kv, mla_w_uq, mla_w_uk, mla_w_uv, mla_w_o, kv_w_shared,
           swa_w_q, swa_sinks, swa_w_o, rel_bias, mlp_w_up, mlp_w_down, ln_mix_g, ln_mix_b,
           ln_mlp_g, ln_mlp_b):
    B, S, D = x.shape
    depth = mlp_w_up.shape[0]
    assert depth == 2 and mla_w_in.shape[0] == 1 and swa_w_q.shape[0] == 1
    alpha = (2 * depth) ** 0.25
    T = B * S
    tm = _TOKEN_TILE
    ta = _ATTN_TILE
    tk = ta // 2
    bf16_rows = 16
    assert ta % tm == 0 and tm % tk == 0 and S % ta == 0

    q_rank, heads, hd = mla_w_uq.shape[1:]
    kv_rank, _, nope = mla_w_uk.shape[1:]
    vdim = mla_w_uv.shape[3]
    vrows = vdim + bf16_rows
    rope = hd - nope
    half = rope // 2
    assert 2 * rope == _LANES and mla_w_in.shape[2] == q_rank + kv_rank + rope

    w_in = mla_w_in[0]
    r0 = q_rank + kv_rank
    w_in_ext = jnp.concatenate(
        [w_in, w_in[:, r0 + half:r0 + rope], w_in[:, r0:r0 + half]], axis=1).astype(_BF16)
    wqT = mla_w_uq[0].reshape(q_rank, heads * hd).T.astype(_BF16)
    wuk = mla_w_uk[0].reshape(kv_rank, heads * nope).astype(_BF16)
    wvT = mla_w_uv[0].reshape(kv_rank, heads * vdim).T.astype(_BF16)
    cs, cosT, sinT = _rope_tables(S, rope)
    n_t = S // tm
    q_scale = hd ** -0.5 * _LOG2E

    cparams2 = pltpu.CompilerParams(dimension_semantics=("arbitrary", "arbitrary"),
                                    vmem_limit_bytes=_VMEM_LIMIT_BYTES)
    cparams1 = pltpu.CompilerParams(dimension_semantics=("arbitrary",),
                                    vmem_limit_bytes=_VMEM_LIMIT_BYTES)

    qT_all, k_all, vT_all = pl.pallas_call(
        functools.partial(_mla_proj_kernel, q_rank=q_rank, kv_rank=kv_rank, heads=heads,
                          nope=nope, rope=rope, q_scale=q_scale),
        grid=(B, n_t),
        in_specs=[
            pl.BlockSpec((None, tm, D), lambda b, i: (b, i, 0)),
            _const_spec(w_in_ext.shape),
            _const_spec((1, q_rank)),
            _const_spec((1, kv_rank)),
            _const_spec(wqT.shape),
            _const_spec(wuk.shape),
            _const_spec(wvT.shape),
            pl.BlockSpec((tm, 2 * rope), lambda b, i: (i, 0)),
            pl.BlockSpec((half, tm), lambda b, i: (0, i)),
            pl.BlockSpec((half, tm), lambda b, i: (0, i)),
        ],
        out_specs=[
            pl.BlockSpec((None, heads, None, hd, tm), lambda b, i: (b, 0, i, 0, 0)),
            pl.BlockSpec((None, heads, tm, hd), lambda b, i: (b, 0, i, 0)),
            pl.BlockSpec((None, heads, tm // tk, vrows, tk), lambda b, i: (b, 0, i, 0, 0)),
        ],
        out_shape=[
            jax.ShapeDtypeStruct((B, heads, n_t, hd, tm), _BF16),
            jax.ShapeDtypeStruct((B, heads, S, hd), _BF16),
            jax.ShapeDtypeStruct((B, heads, S // tk, vrows, tk), _BF16),
        ],
        compiler_params=cparams2,
        name="mla_proj",
    )(x, w_in_ext, _row(mla_g_q[0]), _row(mla_g_kv[0]), wqT, wuk, wvT, cs, cosT, sinT)

    o = pl.pallas_call(
        _mla_attn_kernel,
        grid=(B, heads),
        in_specs=[
            pl.BlockSpec((None, None, n_t, hd, tm), lambda b, h: (b, h, 0, 0, 0)),
            pl.BlockSpec((None, None, S, hd), lambda b, h: (b, h, 0, 0)),
            pl.BlockSpec((None, None, S // tk, vrows, tk), lambda b, h: (b, h, 0, 0, 0)),
        ],
        out_specs=pl.BlockSpec((None, S, vdim), lambda b, h: (b, 0, h)),
        out_shape=jax.ShapeDtypeStruct((B, S, heads * vdim), _BF16),
        scratch_shapes=[
            pltpu.VMEM((1, ta), _F32),
            pltpu.VMEM((vrows, ta), _F32),
            pltpu.VMEM((tk, ta), _F32),
            pltpu.VMEM((tk, ta), _F32),
            pltpu.VMEM((tk, ta), _BF16),
            pltpu.VMEM((tk, ta), _BF16),
            pltpu.VMEM((1, ta), _F32),
            pltpu.VMEM((1, ta), _F32),
            pltpu.VMEM((1, ta), _F32),
            pltpu.VMEM((1, ta), _F32),
        ],
        compiler_params=cparams2,
        name="mla_attn",
    )(qT_all, k_all, vT_all)

    d_ff = mlp_w_up.shape[2]
    x1 = pl.pallas_call(
        functools.partial(_outproj_mlp_kernel, alpha=alpha),
        grid=(T // tm,),
        in_specs=[
            pl.BlockSpec((tm, heads * vdim), lambda i: (i, 0)),
            pl.BlockSpec((tm, D), lambda i: (i, 0)),
            _const_spec((heads * vdim, D)),
            _const_spec((1, D)),
            _const_spec((1, D)),
            _const_spec((D, d_ff)),
            _const_spec((d_ff, D)),
            _const_spec((1, D)),
            _const_spec((1, D)),
        ],
        out_specs=pl.BlockSpec((tm, D), lambda i: (i, 0)),
        out_shape=jax.ShapeDtypeStruct((T, D), _F32),
        compiler_params=cparams1,
        name="mla_out_mlp",
    )(o.reshape(T, heads * vdim), x.reshape(T, D), mla_w_o[0].astype(_BF16),
      _row(ln_mix_g[0]), _row(ln_mix_b[0]), mlp_w_up[0].astype(_BF16),
      mlp_w_down[0].astype(_BF16), _row(ln_mlp_g[0]), _row(ln_mlp_b[0]))

    q_heads = swa_sinks.shape[1]
    s_hd = swa_w_q.shape[2] // q_heads
    kv_heads = kv_w_shared.shape[1] // (2 * s_hd)
    group = q_heads // kv_heads
    blk = _SWA_BLOCK
    assert _LANES % s_hd == 0 and tm % blk == 0
    wqT1 = swa_w_q[0].T.astype(_BF16)
    wk1 = kv_w_shared[:, :kv_heads * s_hd].astype(_BF16)
    wvT1 = kv_w_shared[:, kv_heads * s_hd:].T.astype(_BF16)
    bucketT = jnp.asarray(_t5_bucket_table(blk))

    smem = pl.BlockSpec(memory_space=pltpu.SMEM)
    out = pl.pallas_call(
        functools.partial(_swa_mlp_kernel, alpha=alpha, q_heads=q_heads, kv_heads=kv_heads,
                          head_dim=s_hd, q_scale=s_hd ** -0.5 * _LOG2E),
        grid=(B, n_t),
        in_specs=[
            pl.BlockSpec((None, tm, D), lambda b, i: (b, i, 0)),
            _const_spec(wqT1.shape),
            _const_spec(wk1.shape),
            _const_spec(wvT1.shape),
            _const_spec((q_heads * s_hd, D)),
            _const_spec(bucketT.shape),
            smem,
            smem,
            _const_spec((1, D)),
            _const_spec((1, D)),
            _const_spec((D, d_ff)),
            _const_spec((d_ff, D)),
            _const_spec((1, D)),
            _const_spec((1, D)),
        ],
        out_specs=pl.BlockSpec((None, tm, D), lambda b, i: (b, i, 0)),
        out_shape=jax.ShapeDtypeStruct((B, S, D), _F32),
        scratch_shapes=[
            pltpu.VMEM((blk, kv_heads * s_hd), _BF16),
            pltpu.VMEM((kv_heads * s_hd, blk), _BF16),
            pltpu.VMEM((kv_heads, 2 * blk, group * blk), _F32),
        ],
        compiler_params=cparams2,
        name="swa_mlp",
    )(x1.reshape(B, S, D), wqT1, wk1, wvT1, swa_w_o[0].astype(_BF16), bucketT,
      rel_bias.astype(_F32), swa_sinks[0].astype(_F32),
      _row(ln_mix_g[1]), _row(ln_mix_b[1]), mlp_w_up[1].astype(_BF16),
      mlp_w_down[1].astype(_BF16), _row(ln_mlp_g[1]), _row(ln_mlp_b[1]))
    return out
```

```python
import functools
import math

import numpy as np
import jax
import jax.numpy as jnp
from jax import lax
from jax.experimental import pallas as pl
from jax.experimental.pallas import tpu as pltpu

_F32 = jnp.float32
_BF16 = jnp.bfloat16

_LN_EPS = 1e-5
_RMS_EPS = 1e-6
_ROPE_THETA = 10000.0
_SWA_BLOCK = 128
_REL_BUCKETS = 32
_REL_MAX_DIST = 128
_LOG2E = math.log2(math.e)

_V7X_VMEM_BYTES = 64 * 1024 * 1024
_VMEM_LIMIT_BYTES = _V7X_VMEM_BYTES - 8 * 1024 * 1024
_LANES = 128

_TOKEN_TILE = 512
_ATTN_TILE = 1024
_ATTN_ROW_BLOCK = 512
_FF_CHUNK = 1024

_NT = (((1,), (1,)), ((), ()))
_TN = (((0,), (0,)), ((), ()))


def _const_spec(shape):
    nd = len(shape)
    return pl.BlockSpec(shape, lambda *_: (0,) * nd, pipeline_mode=pl.Buffered(1))


def _layernorm(v, g, b):
    mu = jnp.mean(v, axis=-1, keepdims=True)
    d = v - mu
    var = jnp.mean(d * d, axis=-1, keepdims=True)
    return d * lax.rsqrt(var + _LN_EPS) * g + b


def _rmsnorm(v, g):
    return v * lax.rsqrt(jnp.mean(v * v, axis=-1, keepdims=True) + _RMS_EPS) * g


def _mlp_ln(x1, alpha, wup_ref, wdown_ref, g_ref, b_ref):
    x1b = x1.astype(_BF16)
    d_ff = wup_ref.shape[1]
    acc = None
    for c in range(d_ff // _FF_CHUNK):
        lo = c * _FF_CHUNK
        hmid = jnp.dot(x1b, wup_ref[:, lo:lo + _FF_CHUNK], preferred_element_type=_F32)
        hmid = jnp.maximum(hmid, 0.0)
        hmid = (hmid * hmid).astype(_BF16)
        part = jnp.dot(hmid, wdown_ref[lo:lo + _FF_CHUNK, :], preferred_element_type=_F32)
        acc = part if acc is None else acc + part
    return _layernorm(alpha * x1 + acc, g_ref[...], b_ref[...])


def _mla_proj_kernel(x_ref, win_ref, gq_ref, gkv_ref, wqT_ref, wuk_ref, wvT_ref,
                     cs_ref, cosT_ref, sinT_ref, qT_ref, k_ref, vT_ref,
                     *, q_rank, kv_rank, heads, nope, rope, q_scale):
    xb = x_ref[...].astype(_BF16)
    h = jnp.dot(xb, win_ref[...], preferred_element_type=_F32)
    cq = _rmsnorm(h[:, :q_rank], gq_ref[...]).astype(_BF16)
    ckv = _rmsnorm(h[:, q_rank:q_rank + kv_rank], gkv_ref[...]).astype(_BF16)
    t = h[:, q_rank + kv_rank:] * cs_ref[...]
    kr = (t + pltpu.roll(t, rope, 1))[:, :rope].astype(_BF16)

    half = rope // 2
    hd = nope + rope
    qT = lax.dot_general(wqT_ref[...], cq, _NT, preferred_element_type=_F32)
    cosT = cosT_ref[...]
    sinT = sinT_ref[...]
    for hh in range(heads):
        base = hh * hd
        qT_ref[hh, 0:nope, :] = (qT[base:base + nope] * q_scale).astype(_BF16)
        x1 = qT[base + nope:base + nope + half]
        x2 = qT[base + nope + half:base + hd]
        qT_ref[hh, nope:nope + half, :] = ((x1 * cosT - x2 * sinT) * q_scale).astype(_BF16)
        qT_ref[hh, nope + half:hd, :] = ((x1 * sinT + x2 * cosT) * q_scale).astype(_BF16)

    kn = jnp.dot(ckv, wuk_ref[...], preferred_element_type=_F32)
    for hh in range(heads):
        k_ref[hh, :, 0:nope] = kn[:, hh * nope:(hh + 1) * nope].astype(_BF16)
        k_ref[hh, :, nope:hd] = kr

    vdim = wvT_ref.shape[0] // heads
    vT = lax.dot_general(wvT_ref[...], ckv, _NT, preferred_element_type=_F32)
    tk = vT_ref.shape[3]
    pad = vT_ref.shape[2] - vdim
    ones_rows = (lax.broadcasted_iota(jnp.int32, (pad, tk), 0) == 0).astype(_BF16)
    for hh in range(heads):
        for j in range(vT_ref.shape[1]):
            vT_ref[hh, j, 0:vdim, :] = vT[hh * vdim:(hh + 1) * vdim, j * tk:(j + 1) * tk].astype(_BF16)
            vT_ref[hh, j, vdim:vdim + pad, :] = ones_rows


def _mla_attn_kernel(qT_ref, k_ref, vT_ref, o_ref, m_scr, acc_scr,
                     s_a, s_b, p_a, p_b, cm_a, cm_b, al_a, al_b):
    tk, tq = s_a.shape
    tsub = qT_ref.shape[2]
    nsub = tq // tsub
    n_q = qT_ref.shape[0] // nsub
    rb = _ATTN_ROW_BLOCK
    assert tq == 2 * tk and tk == vT_ref.shape[2] and tk % rb == 0

    def q_stage(qi, c, s_ref, cm_ref, col_lo=0):
        for r0 in range(0, tk, rb):
            k = k_ref[pl.ds(pl.multiple_of(c * tk + r0, rb), rb), :]
            for j in range(col_lo // tsub, nsub):
                s = jnp.dot(k, qT_ref[qi * nsub + j], preferred_element_type=_F32)
                s_ref[r0:r0 + rb, j * tsub:(j + 1) * tsub] = s
                cm = jnp.max(s, axis=0, keepdims=True)
                if r0 > 0:
                    cm = jnp.maximum(cm, cm_ref[:, j * tsub:(j + 1) * tsub])
                cm_ref[:, j * tsub:(j + 1) * tsub] = cm

    def x_stage(s_ref, cm_ref, p_ref, al_ref, mask_off=None, col_lo=0):
        cols = slice(col_lo, tq)
        s = s_ref[:, cols]
        if mask_off is None:
            cm = cm_ref[:, cols]
        else:
            krel = lax.broadcasted_iota(jnp.int32, s.shape, 0) + mask_off
            qrel = lax.broadcasted_iota(jnp.int32, s.shape, 1) + col_lo
            s = jnp.where(krel <= qrel, s, -jnp.inf)
            cm = jnp.max(s, axis=0, keepdims=True)
        m_prev = m_scr[:, cols]
        m_new = jnp.maximum(m_prev, cm)
        alpha = jnp.exp2(m_prev - m_new)
        p = jnp.exp2(s - m_new)
        p_ref[:, cols] = p.astype(_BF16)
        al_ref[:, cols] = alpha
        m_scr[:, cols] = m_new

    def v_stage(c, p_ref, al_ref, col_lo=0):
        cols = slice(col_lo, tq)
        acc_scr[:, cols] = al_ref[:, cols] * acc_scr[:, cols] + jnp.dot(
            vT_ref[c], p_ref[:, cols], preferred_element_type=_F32)

    def q_body(qi, carry):
        m_scr[...] = jnp.full(m_scr.shape, -jnp.inf, _F32)
        acc_scr[...] = jnp.zeros(acc_scr.shape, _F32)
        p_b[...] = jnp.zeros(p_b.shape, p_b.dtype)
        al_b[...] = jnp.ones(al_b.shape, _F32)
        q_stage(qi, 0, s_a, cm_a)

        def pair_body(u, c):
            a = 2 * u
            q_stage(qi, a + 1, s_b, cm_b)
            x_stage(s_a, cm_a, p_a, al_a)
            v_stage(jnp.maximum(a - 1, 0), p_b, al_b)
            q_stage(qi, a + 2, s_a, cm_a)
            x_stage(s_b, cm_b, p_b, al_b)
            v_stage(a, p_a, al_a)
            return c

        lax.fori_loop(0, qi, pair_body, 0)

        a = 2 * qi
        q_stage(qi, a + 1, s_b, cm_b, col_lo=tk)
        x_stage(s_a, cm_a, p_a, al_a, mask_off=0)
        v_stage(jnp.maximum(a - 1, 0), p_b, al_b)
        x_stage(s_b, cm_b, p_b, al_b, mask_off=tk, col_lo=tk)
        v_stage(a, p_a, al_a)
        v_stage(a + 1, p_b, al_b, col_lo=tk)

        vdim = o_ref.shape[1]
        inv_l = 1.0 / acc_scr[vdim:vdim + 1, :]
        o = (acc_scr[0:vdim, :] * inv_l).T
        o_ref[pl.ds(pl.multiple_of(qi * tq, tq), tq), :] = o.astype(o_ref.dtype)
        return carry

    lax.fori_loop(0, n_q, q_body, 0)


def _outproj_mlp_kernel(o_ref, x_ref, wo_ref, g1_ref, b1_ref, wup_ref, wdown_ref,
                        g2_ref, b2_ref, out_ref, *, alpha):
    y = jnp.dot(o_ref[...], wo_ref[...], preferred_element_type=_F32)
    x1 = _layernorm(alpha * x_ref[...] + y, g1_ref[...], b1_ref[...])
    out_ref[...] = _mlp_ln(x1, alpha, wup_ref, wdown_ref, g2_ref, b2_ref)


def _swa_mlp_kernel(x_ref, wqT_ref, wk_ref, wvT_ref, wo_ref, bucketT_ref, relb_ref, sink_ref,
                    g1_ref, b1_ref, wup_ref, wdown_ref, g2_ref, b2_ref, out_ref,
                    k_scr, vTe_scr, qT_scr, biasT_scr, s_scr, p_scr, m_scr, oT_scr,
                    *, alpha, q_heads, kv_heads, head_dim, q_scale):
    blk = _SWA_BLOCK
    group = q_heads // kv_heads
    tm = x_ref.shape[0]
    n_blk = tm // blk
    vrows = vTe_scr.shape[0] // kv_heads
    first_tile = pl.program_id(1) == 0
    always = pl.program_id(1) >= 0

    @pl.when((pl.program_id(0) == 0) & first_tile)
    def _():
        bkt = bucketT_ref[...]
        for hq in range(q_heads):
            bias = jnp.full(bkt.shape, -jnp.inf, _F32)
            for bb in range(_REL_BUCKETS):
                bias = jnp.where(bkt == bb, relb_ref[bb, hq] * _LOG2E, bias)
            g, gi = divmod(hq, group)
            biasT_scr[g, :, gi * blk:(gi + 1) * blk] = bias

    @pl.when(first_tile)
    def _():
        k_scr[0:blk, :] = jnp.zeros((blk, k_scr.shape[1]), k_scr.dtype)
        vTe_scr[:, 0:blk] = jnp.zeros((vTe_scr.shape[0], blk), vTe_scr.dtype)

    xb = x_ref[...].astype(_BF16)
    qT_scr[...] = (lax.dot_general(wqT_ref[...], xb, _NT, preferred_element_type=_F32)
                   * q_scale).astype(_BF16)
    k_scr[blk:blk + tm, :] = jnp.dot(xb, wk_ref[...], preferred_element_type=_F32).astype(_BF16)
    vT = lax.dot_general(wvT_ref[...], xb, _NT, preferred_element_type=_F32).astype(_BF16)
    ones_rows = (lax.broadcasted_iota(jnp.int32, (vrows - head_dim, tm), 0) == 0).astype(_BF16)
    for g in range(kv_heads):
        vTe_scr[g * vrows:g * vrows + head_dim, blk:blk + tm] = vT[g * head_dim:(g + 1) * head_dim]
        vTe_scr[g * vrows + head_dim:(g + 1) * vrows, blk:blk + tm] = ones_rows

    first_pen = jnp.where(first_tile, -jnp.inf, 0.0).astype(_F32)
    zeros_q = jnp.zeros((head_dim, group * blk), _BF16)
    heads_per_lane_tile = _LANES // head_dim
    units = [(n, g) for n in range(n_blk) for g in range(kv_heads)]

    def sink_row(g):
        return jnp.concatenate(
            [jnp.full((1, blk), sink_ref[g * group + gi] * _LOG2E, _F32) for gi in range(group)],
            axis=1)

    @pl.when(always)
    def _():
        sinks = [sink_row(g) for g in range(kv_heads)]
        for u, (n, g) in enumerate(units):
            lt = g // heads_per_lane_tile
            kpair = k_scr[n * blk:(n + 2) * blk, lt * _LANES:(lt + 1) * _LANES]
            qg = jnp.concatenate(
                [qT_scr[(g * group + gi) * head_dim:(g * group + gi + 1) * head_dim,
                        n * blk:(n + 1) * blk] for gi in range(group)], axis=1)
            pieces = [zeros_q] * heads_per_lane_tile
            pieces[g % heads_per_lane_tile] = qg
            qpad = jnp.concatenate(pieces, axis=0)
            s = jnp.dot(kpair, qpad, preferred_element_type=_F32) + biasT_scr[g]
            if n == 0:
                s = jnp.concatenate([s[:blk] + first_pen, s[blk:]], axis=0)
            s_scr[u] = s
            m_scr[u] = jnp.maximum(jnp.max(s, axis=0, keepdims=True), sinks[g])

    @pl.when(always)
    def _():
        for u in range(len(units)):
            p_scr[u] = jnp.exp2(s_scr[u] - m_scr[u]).astype(_BF16)

    @pl.when(always)
    def _():
        sinks = [sink_row(g) for g in range(kv_heads)]
        for u, (n, g) in enumerate(units):
            oT = jnp.dot(vTe_scr[g * vrows:(g + 1) * vrows, n * blk:(n + 2) * blk], p_scr[u],
                         preferred_element_type=_F32)
            denom = oT[head_dim:head_dim + 1] + jnp.exp2(sinks[g] - m_scr[u])
            o = (oT[:head_dim] * (1.0 / denom)).astype(_BF16)
            for gi in range(group):
                hq = g * group + gi
                oT_scr[hq * head_dim:(hq + 1) * head_dim, n * blk:(n + 1) * blk] = (
                    o[:, gi * blk:(gi + 1) * blk])
        k_scr[0:blk, :] = k_scr[tm:tm + blk, :]
        vTe_scr[:, 0:blk] = vTe_scr[:, tm:tm + blk]

    y = lax.dot_general(oT_scr[...], wo_ref[...], _TN, preferred_element_type=_F32)
    x1 = _layernorm(alpha * x_ref[...] + y, g1_ref[...], b1_ref[...])
    out_ref[...] = _mlp_ln(x1, alpha, wup_ref, wdown_ref, g2_ref, b2_ref)


def _t5_bucket_table(blk):
    i = np.arange(blk)[None, :]
    j = np.arange(2 * blk)[:, None]
    dist = i + blk - j
    max_exact = _REL_BUCKETS // 2
    nf = np.maximum(dist, 1).astype(np.float32)
    large = max_exact + (np.log(nf / np.float32(max_exact))
                         / np.float32(math.log(_REL_MAX_DIST / max_exact))
                         * np.float32(_REL_BUCKETS - max_exact)).astype(np.int32)
    large = np.minimum(large, _REL_BUCKETS - 1)
    bucket = np.where(dist < max_exact, np.maximum(dist, 0), large)
    valid = (dist >= 0) & (dist < blk)
    return np.where(valid, bucket, -1).astype(np.int32)


def _rope_tables(seq, rope):
    half = rope // 2
    inv = _ROPE_THETA ** (-jnp.arange(half, dtype=_F32) / half)
    ang = jnp.arange(seq).astype(_F32)[:, None] * inv[None, :]
    cos, sin = jnp.cos(ang), jnp.sin(ang)
    cs = jnp.concatenate([cos, cos, -sin, sin], axis=1)
    return cs, cos.T, sin.T


def _row(v):
    return v.reshape(1, -1).astype(_F32)


def kernel(x, mla_w_in, mla_g_q, mla_g_kv, mla_w_uq, mla_w_uk, mla_w_uv, mla_w_o, kv_w_shared,
           swa_w_q, swa_sinks, swa_w_o, rel_bias, mlp_w_up, mlp_w_down, ln_mix_g, ln_mix_b,
           ln_mlp_g, ln_mlp_b):
    B, S, D = x.shape
    depth = mlp_w_up.shape[0]
    assert depth == 2 and mla_w_in.shape[0] == 1 and swa_w_q.shape[0] == 1
    alpha = (2 * depth) ** 0.25
    T = B * S
    tm = _TOKEN_TILE
    ta = _ATTN_TILE
    tk = ta // 2
    bf16_rows = 16
    assert ta % tm == 0 and tm % tk == 0 and S % ta == 0

    q_rank, heads, hd = mla_w_uq.shape[1:]
    kv_rank, _, nope = mla_w_uk.shape[1:]
    vdim = mla_w_uv.shape[3]
    vrows = vdim + bf16_rows
    rope = hd - nope
    half = rope // 2
    assert 2 * rope == _LANES and mla_w_in.shape[2] == q_rank + kv_rank + rope

    w_in = mla_w_in[0]
    r0 = q_rank + kv_rank
    w_in_ext = jnp.concatenate(
        [w_in, w_in[:, r0 + half:r0 + rope], w_in[:, r0:r0 + half]], axis=1).astype(_BF16)
    wqT = mla_w_uq[0].reshape(q_rank, heads * hd).T.astype(_BF16)
    wuk = mla_w_uk[0].reshape(kv_rank, heads * nope).astype(_BF16)
    wvT = mla_w_uv[0].reshape(kv_rank, heads * vdim).T.astype(_BF16)
    cs, cosT, sinT = _rope_tables(S, rope)
    n_t = S // tm
    q_scale = hd ** -0.5 * _LOG2E

    cparams2 = pltpu.CompilerParams(dimension_semantics=("arbitrary", "arbitrary"),
                                    vmem_limit_bytes=_VMEM_LIMIT_BYTES)
    cparams1 = pltpu.CompilerParams(dimension_semantics=("arbitrary",),
                                    vmem_limit_bytes=_VMEM_LIMIT_BYTES)

    qT_all, k_all, vT_all = pl.pallas_call(
        functools.partial(_mla_proj_kernel, q_rank=q_rank, kv_rank=kv_rank, heads=heads,
                          nope=nope, rope=rope, q_scale=q_scale),
        grid=(B, n_t),
        in_specs=[
            pl.BlockSpec((None, tm, D), lambda b, i: (b, i, 0)),
            _const_spec(w_in_ext.shape),
            _const_spec((1, q_rank)),
            _const_spec((1, kv_rank)),
            _const_spec(wqT.shape),
            _const_spec(wuk.shape),
            _const_spec(wvT.shape),
            pl.BlockSpec((tm, 2 * rope), lambda b, i: (i, 0)),
            pl.BlockSpec((half, tm), lambda b, i: (0, i)),
            pl.BlockSpec((half, tm), lambda b, i: (0, i)),
        ],
        out_specs=[
            pl.BlockSpec((None, heads, None, hd, tm), lambda b, i: (b, 0, i, 0, 0)),
            pl.BlockSpec((None, heads, tm, hd), lambda b, i: (b, 0, i, 0)),
            pl.BlockSpec((None, heads, tm // tk, vrows, tk), lambda b, i: (b, 0, i, 0, 0)),
        ],
        out_shape=[
            jax.ShapeDtypeStruct((B, heads, n_t, hd, tm), _BF16),
            jax.ShapeDtypeStruct((B, heads, S, hd), _BF16),
            jax.ShapeDtypeStruct((B, heads, S // tk, vrows, tk), _BF16),
        ],
        compiler_params=cparams2,
        name="mla_proj",
    )(x, w_in_ext, _row(mla_g_q[0]), _row(mla_g_kv[0]), wqT, wuk, wvT, cs, cosT, sinT)

    o = pl.pallas_call(
        _mla_attn_kernel,
        grid=(B, heads),
        in_specs=[
            pl.BlockSpec((None, None, n_t, hd, tm), lambda b, h: (b, h, 0, 0, 0)),
            pl.BlockSpec((None, None, S, hd), lambda b, h: (b, h, 0, 0)),
            pl.BlockSpec((None, None, S // tk, vrows, tk), lambda b, h: (b, h, 0, 0, 0)),
        ],
        out_specs=pl.BlockSpec((None, S, vdim), lambda b, h: (b, 0, h)),
        out_shape=jax.ShapeDtypeStruct((B, S, heads * vdim), _BF16),
        scratch_shapes=[
            pltpu.VMEM((1, ta), _F32),
            pltpu.VMEM((vrows, ta), _F32),
            pltpu.VMEM((tk, ta), _F32),
            pltpu.VMEM((tk, ta), _F32),
            pltpu.VMEM((tk, ta), _BF16),
            pltpu.VMEM((tk, ta), _BF16),
            pltpu.VMEM((1, ta), _F32),
            pltpu.VMEM((1, ta), _F32),
            pltpu.VMEM((1, ta), _F32),
            pltpu.VMEM((1, ta), _F32),
        ],
        compiler_params=cparams2,
        name="mla_attn",
    )(qT_all, k_all, vT_all)

    d_ff = mlp_w_up.shape[2]
    x1 = pl.pallas_call(
        functools.partial(_outproj_mlp_kernel, alpha=alpha),
        grid=(T // tm,),
        in_specs=[
            pl.BlockSpec((tm, heads * vdim), lambda i: (i, 0)),
            pl.BlockSpec((tm, D), lambda i: (i, 0)),
            _const_spec((heads * vdim, D)),
            _const_spec((1, D)),
            _const_spec((1, D)),
            _const_spec((D, d_ff)),
            _const_spec((d_ff, D)),
            _const_spec((1, D)),
            _const_spec((1, D)),
        ],
        out_specs=pl.BlockSpec((tm, D), lambda i: (i, 0)),
        out_shape=jax.ShapeDtypeStruct((T, D), _F32),
        compiler_params=cparams1,
        name="mla_out_mlp",
    )(o.reshape(T, heads * vdim), x.reshape(T, D), mla_w_o[0].astype(_BF16),
      _row(ln_mix_g[0]), _row(ln_mix_b[0]), mlp_w_up[0].astype(_BF16),
      mlp_w_down[0].astype(_BF16), _row(ln_mlp_g[0]), _row(ln_mlp_b[0]))

    q_heads = swa_sinks.shape[1]
    s_hd = swa_w_q.shape[2] // q_heads
    kv_heads = kv_w_shared.shape[1] // (2 * s_hd)
    group = q_heads // kv_heads
    blk = _SWA_BLOCK
    assert _LANES % s_hd == 0 and tm % blk == 0
    wqT1 = swa_w_q[0].T.astype(_BF16)
    wk1 = kv_w_shared[:, :kv_heads * s_hd].astype(_BF16)
    wvT1 = kv_w_shared[:, kv_heads * s_hd:].T.astype(_BF16)
    bucketT = jnp.asarray(_t5_bucket_table(blk))
    n_units = (tm // blk) * kv_heads

    smem = pl.BlockSpec(memory_space=pltpu.SMEM)
    out = pl.pallas_call(
        functools.partial(_swa_mlp_kernel, alpha=alpha, q_heads=q_heads, kv_heads=kv_heads,
                          head_dim=s_hd, q_scale=s_hd ** -0.5 * _LOG2E),
        grid=(B, n_t),
        in_specs=[
            pl.BlockSpec((None, tm, D), lambda b, i: (b, i, 0)),
            _const_spec(wqT1.shape),
            _const_spec(wk1.shape),
            _const_spec(wvT1.shape),
            _const_spec((q_heads * s_hd, D)),
            _const_spec(bucketT.shape),
            smem,
            smem,
            _const_spec((1, D)),
            _const_spec((1, D)),
            _const_spec((D, d_ff)),
            _const_spec((d_ff, D)),
            _const_spec((1, D)),
            _const_spec((1, D)),
        ],
        out_specs=pl.BlockSpec((None, tm, D), lambda b, i: (b, i, 0)),
        out_shape=jax.ShapeDtypeStruct((B, S, D), _F32),
        scratch_shapes=[
            pltpu.VMEM((blk + tm, kv_heads * s_hd), _BF16),
            pltpu.VMEM((kv_heads * (s_hd + bf16_rows), blk + tm), _BF16),
            pltpu.VMEM((q_heads * s_hd, tm), _BF16),
            pltpu.VMEM((kv_heads, 2 * blk, group * blk), _F32),
            pltpu.VMEM((n_units, 2 * blk, group * blk), _F32),
            pltpu.VMEM((n_units, 2 * blk, group * blk), _BF16),
            pltpu.VMEM((n_units, 1, group * blk), _F32),
            pltpu.VMEM((q_heads * s_hd, tm), _BF16),
        ],
        compiler_params=cparams2,
        name="swa_mlp",
    )(x1.reshape(B, S, D), wqT1, wk1, wvT1, swa_w_o[0].astype(_BF16), bucketT,
      rel_bias.astype(_F32), swa_sinks[0].astype(_F32),
      _row(ln_mix_g[1]), _row(ln_mix_b[1]), mlp_w_up[1].astype(_BF16),
      mlp_w_down[1].astype(_BF16), _row(ln_mlp_g[1]), _row(ln_mlp_b[1]))
    return out
```

```python
import functools
import math

import numpy as np
import jax
import jax.numpy as jnp
from jax import lax
from jax.experimental import pallas as pl
from jax.experimental.pallas import tpu as pltpu

_F32 = jnp.float32
_BF16 = jnp.bfloat16

_LN_EPS = 1e-5
_RMS_EPS = 1e-6
_ROPE_THETA = 10000.0
_SWA_BLOCK = 128
_REL_BUCKETS = 32
_REL_MAX_DIST = 128
_LOG2E = math.log2(math.e)

_V7X_VMEM_BYTES = 64 * 1024 * 1024
_VMEM_LIMIT_BYTES = _V7X_VMEM_BYTES - 8 * 1024 * 1024
_LANES = 128

_TOKEN_TILE = 512
_ATTN_TILE = 1024
_ATTN_ROW_BLOCK = 512
_FF_CHUNK = 1024

_NT = (((1,), (1,)), ((), ()))
_TN = (((0,), (0,)), ((), ()))


def _const_spec(shape):
    nd = len(shape)
    return pl.BlockSpec(shape, lambda *_: (0,) * nd, pipeline_mode=pl.Buffered(1))


def _layernorm(v, g, b):
    mu = jnp.mean(v, axis=-1, keepdims=True)
    d = v - mu
    var = jnp.mean(d * d, axis=-1, keepdims=True)
    return d * lax.rsqrt(var + _LN_EPS) * g + b


def _rmsnorm(v, g):
    return v * lax.rsqrt(jnp.mean(v * v, axis=-1, keepdims=True) + _RMS_EPS) * g


def _mlp_ln(x1, alpha, wup_ref, wdown_ref, g_ref, b_ref):
    x1b = x1.astype(_BF16)
    d_ff = wup_ref.shape[1]
    acc = None
    for c in range(d_ff // _FF_CHUNK):
        lo = c * _FF_CHUNK
        hmid = jnp.dot(x1b, wup_ref[:, lo:lo + _FF_CHUNK], preferred_element_type=_F32)
        hmid = jnp.maximum(hmid, 0.0)
        hmid = (hmid * hmid).astype(_BF16)
        part = jnp.dot(hmid, wdown_ref[lo:lo + _FF_CHUNK, :], preferred_element_type=_F32)
        acc = part if acc is None else acc + part
    return _layernorm(alpha * x1 + acc, g_ref[...], b_ref[...])


def _mla_proj_kernel(x_ref, win_ref, gq_ref, gkv_ref, wqT_ref, wuk_ref, wvT_ref,
                     cs_ref, cosT_ref, sinT_ref, qT_ref, k_ref, vT_ref,
                     *, q_rank, kv_rank, heads, nope, rope, q_scale):
    xb = x_ref[...].astype(_BF16)
    h = jnp.dot(xb, win_ref[...], preferred_element_type=_F32)
    cq = _rmsnorm(h[:, :q_rank], gq_ref[...]).astype(_BF16)
    ckv = _rmsnorm(h[:, q_rank:q_rank + kv_rank], gkv_ref[...]).astype(_BF16)
    t = h[:, q_rank + kv_rank:] * cs_ref[...]
    kr = (t + pltpu.roll(t, rope, 1))[:, :rope].astype(_BF16)

    half = rope // 2
    hd = nope + rope
    qT = lax.dot_general(wqT_ref[...], cq, _NT, preferred_element_type=_F32)
    cosT = cosT_ref[...]
    sinT = sinT_ref[...]
    for hh in range(heads):
        base = hh * hd
        qT_ref[hh, 0:nope, :] = (qT[base:base + nope] * q_scale).astype(_BF16)
        x1 = qT[base + nope:base + nope + half]
        x2 = qT[base + nope + half:base + hd]
        qT_ref[hh, nope:nope + half, :] = ((x1 * cosT - x2 * sinT) * q_scale).astype(_BF16)
        qT_ref[hh, nope + half:hd, :] = ((x1 * sinT + x2 * cosT) * q_scale).astype(_BF16)

    kn = jnp.dot(ckv, wuk_ref[...], preferred_element_type=_F32)
    for hh in range(heads):
        k_ref[hh, :, 0:nope] = kn[:, hh * nope:(hh + 1) * nope].astype(_BF16)
        k_ref[hh, :, nope:hd] = kr

    vdim = wvT_ref.shape[0] // heads
    vT = lax.dot_general(wvT_ref[...], ckv, _NT, preferred_element_type=_F32)
    tk = vT_ref.shape[3]
    pad = vT_ref.shape[2] - vdim
    ones_rows = (lax.broadcasted_iota(jnp.int32, (pad, tk), 0) == 0).astype(_BF16)
    for hh in range(heads):
        for j in range(vT_ref.shape[1]):
            vT_ref[hh, j, 0:vdim, :] = vT[hh * vdim:(hh + 1) * vdim, j * tk:(j + 1) * tk].astype(_BF16)
            vT_ref[hh, j, vdim:vdim + pad, :] = ones_rows


def _mla_attn_kernel(qT_ref, k_ref, vT_ref, o_ref, m_scr, acc_scr,
                     s_a, s_b, p_a, p_b, cm_a, cm_b, al_a, al_b):
    tk, tq = s_a.shape
    tsub = qT_ref.shape[2]
    nsub = tq // tsub
    n_q = qT_ref.shape[0] // nsub
    rb = _ATTN_ROW_BLOCK
    assert tq == 2 * tk and tk == vT_ref.shape[2] and tk % rb == 0

    def q_stage(qi, c, s_ref, cm_ref, col_lo=0):
        for r0 in range(0, tk, rb):
            k = k_ref[pl.ds(pl.multiple_of(c * tk + r0, rb), rb), :]
            for j in range(col_lo // tsub, nsub):
                s = jnp.dot(k, qT_ref[qi * nsub + j], preferred_element_type=_F32)
                s = s.astype(s_ref.dtype)
                s_ref[r0:r0 + rb, j * tsub:(j + 1) * tsub] = s
                cm = jnp.max(s, axis=0, keepdims=True).astype(_F32)
                if r0 > 0:
                    cm = jnp.maximum(cm, cm_ref[:, j * tsub:(j + 1) * tsub])
                cm_ref[:, j * tsub:(j + 1) * tsub] = cm

    def x_stage(s_ref, cm_ref, p_ref, al_ref, mask_off=None, col_lo=0):
        cols = slice(col_lo, tq)
        s = s_ref[:, cols]
        if mask_off is None:
            cm = cm_ref[:, cols]
        else:
            krel = lax.broadcasted_iota(jnp.int32, s.shape, 0) + mask_off
            qrel = lax.broadcasted_iota(jnp.int32, s.shape, 1) + col_lo
            s = jnp.where(krel <= qrel, s, jnp.asarray(-jnp.inf, s.dtype))
            cm = jnp.max(s, axis=0, keepdims=True).astype(_F32)
        m_prev = m_scr[:, cols]
        m_new = jnp.maximum(m_prev, cm)
        alpha = jnp.exp2(m_prev - m_new)
        p = jnp.exp2(s - m_new.astype(s.dtype))
        p_ref[:, cols] = p.astype(_BF16)
        al_ref[:, cols] = alpha
        m_scr[:, cols] = m_new

    def v_stage(c, p_ref, al_ref, col_lo=0):
        cols = slice(col_lo, tq)
        acc_scr[:, cols] = al_ref[:, cols] * acc_scr[:, cols] + jnp.dot(
            vT_ref[c], p_ref[:, cols], preferred_element_type=_F32)

    def q_body(qi, carry):
        m_scr[...] = jnp.full(m_scr.shape, -jnp.inf, _F32)
        acc_scr[...] = jnp.zeros(acc_scr.shape, _F32)
        p_b[...] = jnp.zeros(p_b.shape, p_b.dtype)
        al_b[...] = jnp.ones(al_b.shape, _F32)
        q_stage(qi, 0, s_a, cm_a)

        def pair_body(u, c):
            a = 2 * u
            q_stage(qi, a + 1, s_b, cm_b)
            x_stage(s_a, cm_a, p_a, al_a)
            v_stage(jnp.maximum(a - 1, 0), p_b, al_b)
            q_stage(qi, a + 2, s_a, cm_a)
            x_stage(s_b, cm_b, p_b, al_b)
            v_stage(a, p_a, al_a)
            return c

        lax.fori_loop(0, qi, pair_body, 0)

        a = 2 * qi
        q_stage(qi, a + 1, s_b, cm_b, col_lo=tk)
        x_stage(s_a, cm_a, p_a, al_a, mask_off=0)
        v_stage(jnp.maximum(a - 1, 0), p_b, al_b)
        x_stage(s_b, cm_b, p_b, al_b, mask_off=tk, col_lo=tk)
        v_stage(a, p_a, al_a)
        v_stage(a + 1, p_b, al_b, col_lo=tk)

        vdim = o_ref.shape[1]
        inv_l = 1.0 / acc_scr[vdim:vdim + 1, :]
        o = (acc_scr[0:vdim, :] * inv_l).T
        o_ref[pl.ds(pl.multiple_of(qi * tq, tq), tq), :] = o.astype(o_ref.dtype)
        return carry

    lax.fori_loop(0, n_q, q_body, 0)


def _outproj_mlp_kernel(o_ref, x_ref, wo_ref, g1_ref, b1_ref, wup_ref, wdown_ref,
                        g2_ref, b2_ref, out_ref, *, alpha):
    y = jnp.dot(o_ref[...], wo_ref[...], preferred_element_type=_F32)
    x1 = _layernorm(alpha * x_ref[...] + y, g1_ref[...], b1_ref[...])
    out_ref[...] = _mlp_ln(x1, alpha, wup_ref, wdown_ref, g2_ref, b2_ref)


def _swa_mlp_kernel(x_ref, wqT_ref, wk_ref, wvT_ref, wo_ref, bucketT_ref, relb_ref, sink_ref,
                    g1_ref, b1_ref, wup_ref, wdown_ref, g2_ref, b2_ref, out_ref,
                    k_scr, vTe_scr, qT_scr, biasT_scr, s_scr, p_scr, m_scr, oT_scr,
                    *, alpha, q_heads, kv_heads, head_dim, q_scale):
    blk = _SWA_BLOCK
    group = q_heads // kv_heads
    tm = x_ref.shape[0]
    n_blk = tm // blk
    vrows = vTe_scr.shape[0] // kv_heads
    first_tile = pl.program_id(1) == 0
    always = pl.program_id(1) >= 0

    @pl.when((pl.program_id(0) == 0) & first_tile)
    def _():
        bkt = bucketT_ref[...]
        for hq in range(q_heads):
            bias = jnp.full(bkt.shape, -jnp.inf, _F32)
            for bb in range(_REL_BUCKETS):
                bias = jnp.where(bkt == bb, relb_ref[bb, hq] * _LOG2E, bias)
            g, gi = divmod(hq, group)
            biasT_scr[g, :, gi * blk:(gi + 1) * blk] = bias

    @pl.when(first_tile)
    def _():
        k_scr[0:blk, :] = jnp.zeros((blk, k_scr.shape[1]), k_scr.dtype)
        vTe_scr[:, 0:blk] = jnp.zeros((vTe_scr.shape[0], blk), vTe_scr.dtype)

    xb = x_ref[...].astype(_BF16)
    qT_scr[...] = (lax.dot_general(wqT_ref[...], xb, _NT, preferred_element_type=_F32)
                   * q_scale).astype(_BF16)
    k_scr[blk:blk + tm, :] = jnp.dot(xb, wk_ref[...], preferred_element_type=_F32).astype(_BF16)
    vT = lax.dot_general(wvT_ref[...], xb, _NT, preferred_element_type=_F32).astype(_BF16)
    ones_rows = (lax.broadcasted_iota(jnp.int32, (vrows - head_dim, tm), 0) == 0).astype(_BF16)
    for g in range(kv_heads):
        vTe_scr[g * vrows:g * vrows + head_dim, blk:blk + tm] = vT[g * head_dim:(g + 1) * head_dim]
        vTe_scr[g * vrows + head_dim:(g + 1) * vrows, blk:blk + tm] = ones_rows

    first_pen = jnp.where(first_tile, -jnp.inf, 0.0).astype(_F32)
    zeros_q = jnp.zeros((head_dim, group * blk), _BF16)
    heads_per_lane_tile = _LANES // head_dim
    units = [(n, g) for n in range(n_blk) for g in range(kv_heads)]

    def sink_row(g):
        return jnp.concatenate(
            [jnp.full((1, blk), sink_ref[g * group + gi] * _LOG2E, _F32) for gi in range(group)],
            axis=1)

    @pl.when(always)
    def _():
        sinks = [sink_row(g) for g in range(kv_heads)]
        for u, (n, g) in enumerate(units):
            lt = g // heads_per_lane_tile
            kpair = k_scr[n * blk:(n + 2) * blk, lt * _LANES:(lt + 1) * _LANES]
            qg = jnp.concatenate(
                [qT_scr[(g * group + gi) * head_dim:(g * group + gi + 1) * head_dim,
                        n * blk:(n + 1) * blk] for gi in range(group)], axis=1)
            pieces = [zeros_q] * heads_per_lane_tile
            pieces[g % heads_per_lane_tile] = qg
            qpad = jnp.concatenate(pieces, axis=0)
            s = jnp.dot(kpair, qpad, preferred_element_type=_F32) + biasT_scr[g]
            if n == 0:
                s = jnp.concatenate([s[:blk] + first_pen, s[blk:]], axis=0)
            s_scr[u] = s
            m_scr[u] = jnp.maximum(jnp.max(s, axis=0, keepdims=True), sinks[g])

    @pl.when(always)
    def _():
        for u in range(len(units)):
            p_scr[u] = jnp.exp2(s_scr[u] - m_scr[u]).astype(_BF16)

    @pl.when(always)
    def _():
        sinks = [sink_row(g) for g in range(kv_heads)]
        for u, (n, g) in enumerate(units):
            oT = jnp.dot(vTe_scr[g * vrows:(g + 1) * vrows, n * blk:(n + 2) * blk], p_scr[u],
                         preferred_element_type=_F32)
            denom = oT[head_dim:head_dim + 1] + jnp.exp2(sinks[g] - m_scr[u])
            o = (oT[:head_dim] * (1.0 / denom)).astype(_BF16)
            for gi in range(group):
                hq = g * group + gi
                oT_scr[hq * head_dim:(hq + 1) * head_dim, n * blk:(n + 1) * blk] = (
                    o[:, gi * blk:(gi + 1) * blk])
        k_scr[0:blk, :] = k_scr[tm:tm + blk, :]
        vTe_scr[:, 0:blk] = vTe_scr[:, tm:tm + blk]

    y = lax.dot_general(oT_scr[...], wo_ref[...], _TN, preferred_element_type=_F32)
    x1 = _layernorm(alpha * x_ref[...] + y, g1_ref[...], b1_ref[...])
    out_ref[...] = _mlp_ln(x1, alpha, wup_ref, wdown_ref, g2_ref, b2_ref)


def _t5_bucket_table(blk):
    i = np.arange(blk)[None, :]
    j = np.arange(2 * blk)[:, None]
    dist = i + blk - j
    max_exact = _REL_BUCKETS // 2
    nf = np.maximum(dist, 1).astype(np.float32)
    large = max_exact + (np.log(nf / np.float32(max_exact))
                         / np.float32(math.log(_REL_MAX_DIST / max_exact))
                         * np.float32(_REL_BUCKETS - max_exact)).astype(np.int32)
    large = np.minimum(large, _REL_BUCKETS - 1)
    bucket = np.where(dist < max_exact, np.maximum(dist, 0), large)
    valid = (dist >= 0) & (dist < blk)
    return np.where(valid, bucket, -1).astype(np.int32)


def _rope_tables(seq, rope):
    half = rope // 2
    inv = _ROPE_THETA ** (-jnp.arange(half, dtype=_F32) / half)
    ang = jnp.arange(seq).astype(_F32)[:, None] * inv[None, :]
    cos, sin = jnp.cos(ang), jnp.sin(ang)
    cs = jnp.concatenate([cos, cos, -sin, sin], axis=1)
    return cs, cos.T, sin.T


def _row(v):
    return v.reshape(1, -1).astype(_F32)


def kernel(x, mla_w_in, mla_g_q, mla_g_kv, mla_w_uq, mla_w_uk, mla_w_uv, mla_w_o, kv_w_shared,
           swa_w_q, swa_sinks, swa_w_o, rel_bias, mlp_w_up, mlp_w_down, ln_mix_g, ln_mix_b,
           ln_mlp_g, ln_mlp_b):
    B, S, D = x.shape
    depth = mlp_w_up.shape[0]
    assert depth == 2 and mla_w_in.shape[0] == 1 and swa_w_q.shape[0] == 1
    alpha = (2 * depth) ** 0.25
    T = B * S
    tm = _TOKEN_TILE
    ta = _ATTN_TILE
    tk = ta // 2
    bf16_rows = 16
    assert ta % tm == 0 and tm % tk == 0 and S % ta == 0

    q_rank, heads, hd = mla_w_uq.shape[1:]
    kv_rank, _, nope = mla_w_uk.shape[1:]
    vdim = mla_w_uv.shape[3]
    vrows = vdim + bf16_rows
    rope = hd - nope
    half = rope // 2
    assert 2 * rope == _LANES and mla_w_in.shape[2] == q_rank + kv_rank + rope

    w_in = mla_w_in[0]
    r0 = q_rank + kv_rank
    w_in_ext = jnp.concatenate(
        [w_in, w_in[:, r0 + half:r0 + rope], w_in[:, r0:r0 + half]], axis=1).astype(_BF16)
    wqT = mla_w_uq[0].reshape(q_rank, heads * hd).T.astype(_BF16)
    wuk = mla_w_uk[0].reshape(kv_rank, heads * nope).astype(_BF16)
    wvT = mla_w_uv[0].reshape(kv_rank, heads * vdim).T.astype(_BF16)
    cs, cosT, sinT = _rope_tables(S, rope)
    n_t = S // tm
    q_scale = hd ** -0.5 * _LOG2E

    cparams2 = pltpu.CompilerParams(dimension_semantics=("arbitrary", "arbitrary"),
                                    vmem_limit_bytes=_VMEM_LIMIT_BYTES)
    cparams1 = pltpu.CompilerParams(dimension_semantics=("arbitrary",),
                                    vmem_limit_bytes=_VMEM_LIMIT_BYTES)

    qT_all, k_all, vT_all = pl.pallas_call(
        functools.partial(_mla_proj_kernel, q_rank=q_rank, kv_rank=kv_rank, heads=heads,
                          nope=nope, rope=rope, q_scale=q_scale),
        grid=(B, n_t),
        in_specs=[
            pl.BlockSpec((None, tm, D), lambda b, i: (b, i, 0)),
            _const_spec(w_in_ext.shape),
            _const_spec((1, q_rank)),
            _const_spec((1, kv_rank)),
            _const_spec(wqT.shape),
            _const_spec(wuk.shape),
            _const_spec(wvT.shape),
            pl.BlockSpec((tm, 2 * rope), lambda b, i: (i, 0)),
            pl.BlockSpec((half, tm), lambda b, i: (0, i)),
            pl.BlockSpec((half, tm), lambda b, i: (0, i)),
        ],
        out_specs=[
            pl.BlockSpec((None, heads, None, hd, tm), lambda b, i: (b, 0, i, 0, 0)),
            pl.BlockSpec((None, heads, tm, hd), lambda b, i: (b, 0, i, 0)),
            pl.BlockSpec((None, heads, tm // tk, vrows, tk), lambda b, i: (b, 0, i, 0, 0)),
        ],
        out_shape=[
            jax.ShapeDtypeStruct((B, heads, n_t, hd, tm), _BF16),
            jax.ShapeDtypeStruct((B, heads, S, hd), _BF16),
            jax.ShapeDtypeStruct((B, heads, S // tk, vrows, tk), _BF16),
        ],
        compiler_params=cparams2,
        name="mla_proj",
    )(x, w_in_ext, _row(mla_g_q[0]), _row(mla_g_kv[0]), wqT, wuk, wvT, cs, cosT, sinT)

    o = pl.pallas_call(
        _mla_attn_kernel,
        grid=(B, heads),
        in_specs=[
            pl.BlockSpec((None, None, n_t, hd, tm), lambda b, h: (b, h, 0, 0, 0)),
            pl.BlockSpec((None, None, S, hd), lambda b, h: (b, h, 0, 0)),
            pl.BlockSpec((None, None, S // tk, vrows, tk), lambda b, h: (b, h, 0, 0, 0)),
        ],
        out_specs=pl.BlockSpec((None, S, vdim), lambda b, h: (b, 0, h)),
        out_shape=jax.ShapeDtypeStruct((B, S, heads * vdim), _BF16),
        scratch_shapes=[
            pltpu.VMEM((1, ta), _F32),
            pltpu.VMEM((vrows, ta), _F32),
            pltpu.VMEM((tk, ta), _BF16),
            pltpu.VMEM((tk, ta), _BF16),
            pltpu.VMEM((tk, ta), _BF16),
            pltpu.VMEM((tk, ta), _BF16),
            pltpu.VMEM((1, ta), _F32),
            pltpu.VMEM((1, ta), _F32),
            pltpu.VMEM((1, ta), _F32),
            pltpu.VMEM((1, ta), _F32),
        ],
        compiler_params=cparams2,
        name="mla_attn",
    )(qT_all, k_all, vT_all)

    d_ff = mlp_w_up.shape[2]
    x1 = pl.pallas_call(
        functools.partial(_outproj_mlp_kernel, alpha=alpha),
        grid=(T // tm,),
        in_specs=[
            pl.BlockSpec((tm, heads * vdim), lambda i: (i, 0)),
            pl.BlockSpec((tm, D), lambda i: (i, 0)),
            _const_spec((heads * vdim, D)),
            _const_spec((1, D)),
            _const_spec((1, D)),
            _const_spec((D, d_ff)),
            _const_spec((d_ff, D)),
            _const_spec((1, D)),
            _const_spec((1, D)),
        ],
        out_specs=pl.BlockSpec((tm, D), lambda i: (i, 0)),
        out_shape=jax.ShapeDtypeStruct((T, D), _F32),
        compiler_params=cparams1,
        name="mla_out_mlp",
    )(o.reshape(T, heads * vdim), x.reshape(T, D), mla_w_o[0].astype(_BF16),
      _row(ln_mix_g[0]), _row(ln_mix_b[0]), mlp_w_up[0].astype(_BF16),
      mlp_w_down[0].astype(_BF16), _row(ln_mlp_g[0]), _row(ln_mlp_b[0]))

    q_heads = swa_sinks.shape[1]
    s_hd = swa_w_q.shape[2] // q_heads
    kv_heads = kv_w_shared.shape[1] // (2 * s_hd)
    group = q_heads // kv_heads
    blk = _SWA_BLOCK
    assert _LANES % s_hd == 0 and tm % blk == 0
    wqT1 = swa_w_q[0].T.astype(_BF16)
    wk1 = kv_w_shared[:, :kv_heads * s_hd].astype(_BF16)
    wvT1 = kv_w_shared[:, kv_heads * s_hd:].T.astype(_BF16)
    bucketT = jnp.asarray(_t5_bucket_table(blk))
    n_units = (tm // blk) * kv_heads

    smem = pl.BlockSpec(memory_space=pltpu.SMEM)
    out = pl.pallas_call(
        functools.partial(_swa_mlp_kernel, alpha=alpha, q_heads=q_heads, kv_heads=kv_heads,
                          head_dim=s_hd, q_scale=s_hd ** -0.5 * _LOG2E),
        grid=(B, n_t),
        in_specs=[
            pl.BlockSpec((None, tm, D), lambda b, i: (b, i, 0)),
            _const_spec(wqT1.shape),
            _const_spec(wk1.shape),
            _const_spec(wvT1.shape),
            _const_spec((q_heads * s_hd, D)),
            _const_spec(bucketT.shape),
            smem,
            smem,
            _const_spec((1, D)),
            _const_spec((1, D)),
            _const_spec((D, d_ff)),
            _const_spec((d_ff, D)),
            _const_spec((1, D)),
            _const_spec((1, D)),
        ],
        out_specs=pl.BlockSpec((None, tm, D), lambda b, i: (b, i, 0)),
        out_shape=jax.ShapeDtypeStruct((B, S, D), _F32),
        scratch_shapes=[
            pltpu.VMEM((blk + tm, kv_heads * s_hd), _BF16),
            pltpu.VMEM((kv_heads * (s_hd + bf16_rows), blk + tm), _BF16),
            pltpu.VMEM((q_heads * s_hd, tm), _BF16),
            pltpu.VMEM((kv_heads, 2 * blk, group * blk), _F32),
            pltpu.VMEM((n_units, 2 * blk, group * blk), _F32),
            pltpu.VMEM((n_units, 2 * blk, group * blk), _BF16),
            pltpu.VMEM((n_units, 1, group * blk), _F32),
            pltpu.VMEM((q_heads * s_hd, tm), _BF16),
        ],
        compiler_params=cparams2,
        name="swa_mlp",
    )(x1.reshape(B, S, D), wqT1, wk1, wvT1, swa_w_o[0].astype(_BF16), bucketT,
      rel_bias.astype(_F32), swa_sinks[0].astype(_F32),
      _row(ln_mix_g[1]), _row(ln_mix_b[1]), mlp_w_up[1].astype(_BF16),
      mlp_w_down[1].astype(_BF16), _row(ln_mlp_g[1]), _row(ln_mlp_b[1]))
    return out
```

```python
import functools
import math

import numpy as np
import jax
import jax.numpy as jnp
from jax import lax
from jax.experimental import pallas as pl
from jax.experimental.pallas import tpu as pltpu

_F32 = jnp.float32
_BF16 = jnp.bfloat16

_LN_EPS = 1e-5
_RMS_EPS = 1e-6
_ROPE_THETA = 10000.0
_SWA_BLOCK = 128
_REL_BUCKETS = 32
_REL_MAX_DIST = 128
_LOG2E = math.log2(math.e)

_V7X_VMEM_BYTES = 64 * 1024 * 1024
_VMEM_LIMIT_BYTES = _V7X_VMEM_BYTES - 8 * 1024 * 1024
_LANES = 128

_TOKEN_TILE = 512
_ATTN_TILE = 1024
_ATTN_ROW_BLOCK = 512
_FF_CHUNK = 1024
_MLP_ROW_BLOCK = 256

_NT = (((1,), (1,)), ((), ()))
_TN = (((0,), (0,)), ((), ()))


def _const_spec(shape):
    nd = len(shape)
    return pl.BlockSpec(shape, lambda *_: (0,) * nd, pipeline_mode=pl.Buffered(1))


def _layernorm(v, g, b):
    mu = jnp.mean(v, axis=-1, keepdims=True)
    d = v - mu
    var = jnp.mean(d * d, axis=-1, keepdims=True)
    return d * lax.rsqrt(var + _LN_EPS) * g + b


def _rmsnorm(v, g):
    return v * lax.rsqrt(jnp.mean(v * v, axis=-1, keepdims=True) + _RMS_EPS) * g


def _mix_mlp_ln(mix_rows, x_ref, alpha, g1_ref, b1_ref, wup_ref, wdown_ref, g2_ref, b2_ref,
                out_ref, x1_scr, h_scr):
    tm = x_ref.shape[0]
    d_ff = wup_ref.shape[1]
    blocks = [slice(r, r + _MLP_ROW_BLOCK) for r in range(0, tm, _MLP_ROW_BLOCK)]
    for rows in blocks:
        x1_scr[rows, :] = _layernorm(alpha * x_ref[rows, :] + mix_rows(rows), g1_ref[...], b1_ref[...])
    for rows in blocks:
        x1b = x1_scr[rows, :].astype(_BF16)
        for lo in range(0, d_ff, _FF_CHUNK):
            hmid = jnp.dot(x1b, wup_ref[:, lo:lo + _FF_CHUNK], preferred_element_type=_F32)
            hmid = jnp.maximum(hmid, 0.0)
            h_scr[rows, lo:lo + _FF_CHUNK] = (hmid * hmid).astype(_BF16)
    for rows in blocks:
        y = jnp.dot(h_scr[rows, :], wdown_ref[...], preferred_element_type=_F32)
        out_ref[rows, :] = _layernorm(alpha * x1_scr[rows, :] + y, g2_ref[...], b2_ref[...])


def _mla_proj_kernel(x_ref, win_ref, gq_ref, gkv_ref, wqT_ref, wuk_ref, wvT_ref,
                     cs_ref, cosT_ref, sinT_ref, qT_ref, k_ref, vT_ref,
                     *, q_rank, kv_rank, heads, nope, rope, q_scale):
    xb = x_ref[...].astype(_BF16)
    h = jnp.dot(xb, win_ref[...], preferred_element_type=_F32)
    cq = _rmsnorm(h[:, :q_rank], gq_ref[...]).astype(_BF16)
    ckv = _rmsnorm(h[:, q_rank:q_rank + kv_rank], gkv_ref[...]).astype(_BF16)
    t = h[:, q_rank + kv_rank:] * cs_ref[...]
    kr = (t + pltpu.roll(t, rope, 1))[:, :rope].astype(_BF16)

    half = rope // 2
    hd = nope + rope
    qT = lax.dot_general(wqT_ref[...], cq, _NT, preferred_element_type=_F32)
    cosT = cosT_ref[...]
    sinT = sinT_ref[...]
    for hh in range(heads):
        base = hh * hd
        qT_ref[hh, 0:nope, :] = (qT[base:base + nope] * q_scale).astype(_BF16)
        x1 = qT[base + nope:base + nope + half]
        x2 = qT[base + nope + half:base + hd]
        qT_ref[hh, nope:nope + half, :] = ((x1 * cosT - x2 * sinT) * q_scale).astype(_BF16)
        qT_ref[hh, nope + half:hd, :] = ((x1 * sinT + x2 * cosT) * q_scale).astype(_BF16)

    kn = jnp.dot(ckv, wuk_ref[...], preferred_element_type=_F32)
    for hh in range(heads):
        k_ref[hh, :, 0:nope] = kn[:, hh * nope:(hh + 1) * nope].astype(_BF16)
        k_ref[hh, :, nope:hd] = kr

    vdim = wvT_ref.shape[0] // heads
    vT = lax.dot_general(wvT_ref[...], ckv, _NT, preferred_element_type=_F32)
    tk = vT_ref.shape[3]
    pad = vT_ref.shape[2] - vdim
    ones_rows = (lax.broadcasted_iota(jnp.int32, (pad, tk), 0) == 0).astype(_BF16)
    for hh in range(heads):
        for j in range(vT_ref.shape[1]):
            vT_ref[hh, j, 0:vdim, :] = vT[hh * vdim:(hh + 1) * vdim, j * tk:(j + 1) * tk].astype(_BF16)
            vT_ref[hh, j, vdim:vdim + pad, :] = ones_rows


def _mla_attn_kernel(qT_ref, k_ref, vT_ref, o_ref, m_scr, acc_scr,
                     s_a, s_b, p_a, p_b, cm_a, cm_b, al_a, al_b):
    tk, tq = s_a.shape
    tsub = qT_ref.shape[2]
    nsub = tq // tsub
    n_q = qT_ref.shape[0] // nsub
    rb = _ATTN_ROW_BLOCK
    assert tq == 2 * tk and tk == vT_ref.shape[2] and tk % rb == 0

    def q_stage(qi, c, s_ref, cm_ref, col_lo=0):
        for r0 in range(0, tk, rb):
            k = k_ref[pl.ds(pl.multiple_of(c * tk + r0, rb), rb), :]
            for j in range(col_lo // tsub, nsub):
                s = jnp.dot(k, qT_ref[qi * nsub + j], preferred_element_type=_F32)
                s_ref[r0:r0 + rb, j * tsub:(j + 1) * tsub] = s
                cm = jnp.max(s, axis=0, keepdims=True)
                if r0 > 0:
                    cm = jnp.maximum(cm, cm_ref[:, j * tsub:(j + 1) * tsub])
                cm_ref[:, j * tsub:(j + 1) * tsub] = cm

    def x_stage(s_ref, cm_ref, p_ref, al_ref, mask_off=None, col_lo=0):
        cols = slice(col_lo, tq)
        s = s_ref[:, cols]
        if mask_off is None:
            cm = cm_ref[:, cols]
        else:
            krel = lax.broadcasted_iota(jnp.int32, s.shape, 0) + mask_off
            qrel = lax.broadcasted_iota(jnp.int32, s.shape, 1) + col_lo
            s = jnp.where(krel <= qrel, s, -jnp.inf)
            cm = jnp.max(s, axis=0, keepdims=True)
        m_prev = m_scr[:, cols]
        m_new = jnp.maximum(m_prev, cm)
        alpha = jnp.exp2(m_prev - m_new)
        p = jnp.exp2(s - m_new)
        p_ref[:, cols] = p.astype(_BF16)
        al_ref[:, cols] = alpha
        m_scr[:, cols] = m_new

    def v_stage(c, p_ref, al_ref, col_lo=0):
        cols = slice(col_lo, tq)
        acc_scr[:, cols] = al_ref[:, cols] * acc_scr[:, cols] + jnp.dot(
            vT_ref[c], p_ref[:, cols], preferred_element_type=_F32)

    def q_body(qi, carry):
        m_scr[...] = jnp.full(m_scr.shape, -jnp.inf, _F32)
        acc_scr[...] = jnp.zeros(acc_scr.shape, _F32)
        p_b[...] = jnp.zeros(p_b.shape, p_b.dtype)
        al_b[...] = jnp.ones(al_b.shape, _F32)
        q_stage(qi, 0, s_a, cm_a)

        def pair_body(u, c):
            a = 2 * u
            q_stage(qi, a + 1, s_b, cm_b)
            x_stage(s_a, cm_a, p_a, al_a)
            v_stage(jnp.maximum(a - 1, 0), p_b, al_b)
            q_stage(qi, a + 2, s_a, cm_a)
            x_stage(s_b, cm_b, p_b, al_b)
            v_stage(a, p_a, al_a)
            return c

        lax.fori_loop(0, qi, pair_body, 0)

        a = 2 * qi
        q_stage(qi, a + 1, s_b, cm_b, col_lo=tk)
        x_stage(s_a, cm_a, p_a, al_a, mask_off=0)
        v_stage(jnp.maximum(a - 1, 0), p_b, al_b)
        x_stage(s_b, cm_b, p_b, al_b, mask_off=tk, col_lo=tk)
        v_stage(a, p_a, al_a)
        v_stage(a + 1, p_b, al_b, col_lo=tk)

        vdim = o_ref.shape[1]
        inv_l = 1.0 / acc_scr[vdim:vdim + 1, :]
        o = (acc_scr[0:vdim, :] * inv_l).T
        o_ref[pl.ds(pl.multiple_of(qi * tq, tq), tq), :] = o.astype(o_ref.dtype)
        return carry

    lax.fori_loop(0, n_q, q_body, 0)


def _outproj_mlp_kernel(o_ref, x_ref, wo_ref, g1_ref, b1_ref, wup_ref, wdown_ref,
                        g2_ref, b2_ref, out_ref, x1_scr, h_scr, *, alpha):
    def mix_rows(rows):
        return jnp.dot(o_ref[rows, :], wo_ref[...], preferred_element_type=_F32)

    _mix_mlp_ln(mix_rows, x_ref, alpha, g1_ref, b1_ref, wup_ref, wdown_ref, g2_ref, b2_ref,
                out_ref, x1_scr, h_scr)


def _swa_mlp_kernel(x_ref, wqT_ref, wk_ref, wvT_ref, wo_ref, bucketT_ref, relb_ref, sink_ref,
                    g1_ref, b1_ref, wup_ref, wdown_ref, g2_ref, b2_ref, out_ref,
                    k_scr, vTe_scr, qT_scr, biasT_scr, s_scr, p_scr, m_scr, oT_scr, x1_scr, h_scr,
                    *, alpha, q_heads, kv_heads, head_dim, q_scale):
    blk = _SWA_BLOCK
    group = q_heads // kv_heads
    tm = x_ref.shape[0]
    n_blk = tm // blk
    vrows = vTe_scr.shape[0] // kv_heads
    first_tile = pl.program_id(1) == 0
    always = pl.program_id(1) >= 0

    @pl.when((pl.program_id(0) == 0) & first_tile)
    def _():
        bkt = bucketT_ref[...]
        for hq in range(q_heads):
            bias = jnp.full(bkt.shape, -jnp.inf, _F32)
            for bb in range(_REL_BUCKETS):
                bias = jnp.where(bkt == bb, relb_ref[bb, hq] * _LOG2E, bias)
            g, gi = divmod(hq, group)
            biasT_scr[g, :, gi * blk:(gi + 1) * blk] = bias

    @pl.when(first_tile)
    def _():
        k_scr[0:blk, :] = jnp.zeros((blk, k_scr.shape[1]), k_scr.dtype)
        vTe_scr[:, 0:blk] = jnp.zeros((vTe_scr.shape[0], blk), vTe_scr.dtype)

    xb = x_ref[...].astype(_BF16)
    qT_scr[...] = (lax.dot_general(wqT_ref[...], xb, _NT, preferred_element_type=_F32)
                   * q_scale).astype(_BF16)
    k_scr[blk:blk + tm, :] = jnp.dot(xb, wk_ref[...], preferred_element_type=_F32).astype(_BF16)
    vT = lax.dot_general(wvT_ref[...], xb, _NT, preferred_element_type=_F32).astype(_BF16)
    ones_rows = (lax.broadcasted_iota(jnp.int32, (vrows - head_dim, tm), 0) == 0).astype(_BF16)
    for g in range(kv_heads):
        vTe_scr[g * vrows:g * vrows + head_dim, blk:blk + tm] = vT[g * head_dim:(g + 1) * head_dim]
        vTe_scr[g * vrows + head_dim:(g + 1) * vrows, blk:blk + tm] = ones_rows

    first_pen = jnp.where(first_tile, -jnp.inf, 0.0).astype(_F32)
    zeros_q = jnp.zeros((head_dim, group * blk), _BF16)
    heads_per_lane_tile = _LANES // head_dim
    units = [(n, g) for n in range(n_blk) for g in range(kv_heads)]

    def sink_row(g):
        return jnp.concatenate(
            [jnp.full((1, blk), sink_ref[g * group + gi] * _LOG2E, _F32) for gi in range(group)],
            axis=1)

    @pl.when(always)
    def _():
        sinks = [sink_row(g) for g in range(kv_heads)]
        for u, (n, g) in enumerate(units):
            lt = g // heads_per_lane_tile
            kpair = k_scr[n * blk:(n + 2) * blk, lt * _LANES:(lt + 1) * _LANES]
            qg = jnp.concatenate(
                [qT_scr[(g * group + gi) * head_dim:(g * group + gi + 1) * head_dim,
                        n * blk:(n + 1) * blk] for gi in range(group)], axis=1)
            pieces = [zeros_q] * heads_per_lane_tile
            pieces[g % heads_per_lane_tile] = qg
            qpad = jnp.concatenate(pieces, axis=0)
            s = jnp.dot(kpair, qpad, preferred_element_type=_F32) + biasT_scr[g]
            if n == 0:
                s = jnp.concatenate([s[:blk] + first_pen, s[blk:]], axis=0)
            s_scr[u] = s
            m_scr[u] = jnp.maximum(jnp.max(s, axis=0, keepdims=True), sinks[g])

    @pl.when(always)
    def _():
        for u in range(len(units)):
            p_scr[u] = jnp.exp2(s_scr[u] - m_scr[u]).astype(_BF16)

    @pl.when(always)
    def _():
        sinks = [sink_row(g) for g in range(kv_heads)]
        for u, (n, g) in enumerate(units):
            oT = jnp.dot(vTe_scr[g * vrows:(g + 1) * vrows, n * blk:(n + 2) * blk], p_scr[u],
                         preferred_element_type=_F32)
            denom = oT[head_dim:head_dim + 1] + jnp.exp2(sinks[g] - m_scr[u])
            o = (oT[:head_dim] * (1.0 / denom)).astype(_BF16)
            for gi in range(group):
                hq = g * group + gi
                oT_scr[hq * head_dim:(hq + 1) * head_dim, n * blk:(n + 1) * blk] = (
                    o[:, gi * blk:(gi + 1) * blk])
        k_scr[0:blk, :] = k_scr[tm:tm + blk, :]
        vTe_scr[:, 0:blk] = vTe_scr[:, tm:tm + blk]

    def mix_rows(rows):
        return lax.dot_general(oT_scr[:, rows], wo_ref[...], _TN, preferred_element_type=_F32)

    _mix_mlp_ln(mix_rows, x_ref, alpha, g1_ref, b1_ref, wup_ref, wdown_ref, g2_ref, b2_ref,
                out_ref, x1_scr, h_scr)


def _t5_bucket_table(blk):
    i = np.arange(blk)[None, :]
    j = np.arange(2 * blk)[:, None]
    dist = i + blk - j
    max_exact = _REL_BUCKETS // 2
    nf = np.maximum(dist, 1).astype(np.float32)
    large = max_exact + (np.log(nf / np.float32(max_exact))
                         / np.float32(math.log(_REL_MAX_DIST / max_exact))
                         * np.float32(_REL_BUCKETS - max_exact)).astype(np.int32)
    large = np.minimum(large, _REL_BUCKETS - 1)
    bucket = np.where(dist < max_exact, np.maximum(dist, 0), large)
    valid = (dist >= 0) & (dist < blk)
    return np.where(valid, bucket, -1).astype(np.int32)


def _rope_tables(seq, rope):
    half = rope // 2
    inv = _ROPE_THETA ** (-jnp.arange(half, dtype=_F32) / half)
    ang = jnp.arange(seq).astype(_F32)[:, None] * inv[None, :]
    cos, sin = jnp.cos(ang), jnp.sin(ang)
    cs = jnp.concatenate([cos, cos, -sin, sin], axis=1)
    return cs, cos.T, sin.T


def _row(v):
    return v.reshape(1, -1).astype(_F32)


def kernel(x, mla_w_in, mla_g_q, mla_g_kv, mla_w_uq, mla_w_uk, mla_w_uv, mla_w_o, kv_w_shared,
           swa_w_q, swa_sinks, swa_w_o, rel_bias, mlp_w_up, mlp_w_down, ln_mix_g, ln_mix_b,
           ln_mlp_g, ln_mlp_b):
    B, S, D = x.shape
    depth = mlp_w_up.shape[0]
    assert depth == 2 and mla_w_in.shape[0] == 1 and swa_w_q.shape[0] == 1
    alpha = (2 * depth) ** 0.25
    T = B * S
    tm = _TOKEN_TILE
    ta = _ATTN_TILE
    tk = ta // 2
    bf16_rows = 16
    assert ta % tm == 0 and tm % tk == 0 and S % ta == 0

    q_rank, heads, hd = mla_w_uq.shape[1:]
    kv_rank, _, nope = mla_w_uk.shape[1:]
    vdim = mla_w_uv.shape[3]
    vrows = vdim + bf16_rows
    rope = hd - nope
    half = rope // 2
    assert 2 * rope == _LANES and mla_w_in.shape[2] == q_rank + kv_rank + rope

    w_in = mla_w_in[0]
    r0 = q_rank + kv_rank
    w_in_ext = jnp.concatenate(
        [w_in, w_in[:, r0 + half:r0 + rope], w_in[:, r0:r0 + half]], axis=1).astype(_BF16)
    wqT = mla_w_uq[0].reshape(q_rank, heads * hd).T.astype(_BF16)
    wuk = mla_w_uk[0].reshape(kv_rank, heads * nope).astype(_BF16)
    wvT = mla_w_uv[0].reshape(kv_rank, heads * vdim).T.astype(_BF16)
    cs, cosT, sinT = _rope_tables(S, rope)
    n_t = S // tm
    q_scale = hd ** -0.5 * _LOG2E

    cparams2 = pltpu.CompilerParams(dimension_semantics=("arbitrary", "arbitrary"),
                                    vmem_limit_bytes=_VMEM_LIMIT_BYTES)
    cparams1 = pltpu.CompilerParams(dimension_semantics=("arbitrary",),
                                    vmem_limit_bytes=_VMEM_LIMIT_BYTES)

    qT_all, k_all, vT_all = pl.pallas_call(
        functools.partial(_mla_proj_kernel, q_rank=q_rank, kv_rank=kv_rank, heads=heads,
                          nope=nope, rope=rope, q_scale=q_scale),
        grid=(B, n_t),
        in_specs=[
            pl.BlockSpec((None, tm, D), lambda b, i: (b, i, 0)),
            _const_spec(w_in_ext.shape),
            _const_spec((1, q_rank)),
            _const_spec((1, kv_rank)),
            _const_spec(wqT.shape),
            _const_spec(wuk.shape),
            _const_spec(wvT.shape),
            pl.BlockSpec((tm, 2 * rope), lambda b, i: (i, 0)),
            pl.BlockSpec((half, tm), lambda b, i: (0, i)),
            pl.BlockSpec((half, tm), lambda b, i: (0, i)),
        ],
        out_specs=[
            pl.BlockSpec((None, heads, None, hd, tm), lambda b, i: (b, 0, i, 0, 0)),
            pl.BlockSpec((None, heads, tm, hd), lambda b, i: (b, 0, i, 0)),
            pl.BlockSpec((None, heads, tm // tk, vrows, tk), lambda b, i: (b, 0, i, 0, 0)),
        ],
        out_shape=[
            jax.ShapeDtypeStruct((B, heads, n_t, hd, tm), _BF16),
            jax.ShapeDtypeStruct((B, heads, S, hd), _BF16),
            jax.ShapeDtypeStruct((B, heads, S // tk, vrows, tk), _BF16),
        ],
        compiler_params=cparams2,
        name="mla_proj",
    )(x, w_in_ext, _row(mla_g_q[0]), _row(mla_g_kv[0]), wqT, wuk, wvT, cs, cosT, sinT)

    o = pl.pallas_call(
        _mla_attn_kernel,
        grid=(B, heads),
        in_specs=[
            pl.BlockSpec((None, None, n_t, hd, tm), lambda b, h: (b, h, 0, 0, 0)),
            pl.BlockSpec((None, None, S, hd), lambda b, h: (b, h, 0, 0)),
            pl.BlockSpec((None, None, S // tk, vrows, tk), lambda b, h: (b, h, 0, 0, 0)),
        ],
        out_specs=pl.BlockSpec((None, S, vdim), lambda b, h: (b, 0, h)),
        out_shape=jax.ShapeDtypeStruct((B, S, heads * vdim), _BF16),
        scratch_shapes=[
            pltpu.VMEM((1, ta), _F32),
            pltpu.VMEM((vrows, ta), _F32),
            pltpu.VMEM((tk, ta), _F32),
            pltpu.VMEM((tk, ta), _F32),
            pltpu.VMEM((tk, ta), _BF16),
            pltpu.VMEM((tk, ta), _BF16),
            pltpu.VMEM((1, ta), _F32),
            pltpu.VMEM((1, ta), _F32),
            pltpu.VMEM((1, ta), _F32),
            pltpu.VMEM((1, ta), _F32),
        ],
        compiler_params=cparams2,
        name="mla_attn",
    )(qT_all, k_all, vT_all)

    d_ff = mlp_w_up.shape[2]
    mlp_scratch = [
        pltpu.VMEM((tm, D), _F32),
        pltpu.VMEM((tm, d_ff), _BF16),
    ]
    x1 = pl.pallas_call(
        functools.partial(_outproj_mlp_kernel, alpha=alpha),
        grid=(T // tm,),
        in_specs=[
            pl.BlockSpec((tm, heads * vdim), lambda i: (i, 0)),
            pl.BlockSpec((tm, D), lambda i: (i, 0)),
            _const_spec((heads * vdim, D)),
            _const_spec((1, D)),
            _const_spec((1, D)),
            _const_spec((D, d_ff)),
            _const_spec((d_ff, D)),
            _const_spec((1, D)),
            _const_spec((1, D)),
        ],
        out_specs=pl.BlockSpec((tm, D), lambda i: (i, 0)),
        out_shape=jax.ShapeDtypeStruct((T, D), _F32),
        scratch_shapes=mlp_scratch,
        compiler_params=cparams1,
        name="mla_out_mlp",
    )(o.reshape(T, heads * vdim), x.reshape(T, D), mla_w_o[0].astype(_BF16),
      _row(ln_mix_g[0]), _row(ln_mix_b[0]), mlp_w_up[0].astype(_BF16),
      mlp_w_down[0].astype(_BF16), _row(ln_mlp_g[0]), _row(ln_mlp_b[0]))

    q_heads = swa_sinks.shape[1]
    s_hd = swa_w_q.shape[2] // q_heads
    kv_heads = kv_w_shared.shape[1] // (2 * s_hd)
    group = q_heads // kv_heads
    blk = _SWA_BLOCK
    assert _LANES % s_hd == 0 and tm % blk == 0
    wqT1 = swa_w_q[0].T.astype(_BF16)
    wk1 = kv_w_shared[:, :kv_heads * s_hd].astype(_BF16)
    wvT1 = kv_w_shared[:, kv_heads * s_hd:].T.astype(_BF16)
    bucketT = jnp.asarray(_t5_bucket_table(blk))
    n_units = (tm // blk) * kv_heads

    smem = pl.BlockSpec(memory_space=pltpu.SMEM)
    out = pl.pallas_call(
        functools.partial(_swa_mlp_kernel, alpha=alpha, q_heads=q_heads, kv_heads=kv_heads,
                          head_dim=s_hd, q_scale=s_hd ** -0.5 * _LOG2E),
        grid=(B, n_t),
        in_specs=[
            pl.BlockSpec((None, tm, D), lambda b, i: (b, i, 0)),
            _const_spec(wqT1.shape),
            _const_spec(wk1.shape),
            _const_spec(wvT1.shape),
            _const_spec((q_heads * s_hd, D)),
            _const_spec(bucketT.shape),
            smem,
            smem,
            _const_spec((1, D)),
            _const_spec((1, D)),
            _const_spec((D, d_ff)),
            _const_spec((d_ff, D)),
            _const_spec((1, D)),
            _const_spec((1, D)),
        ],
        out_specs=pl.BlockSpec((None, tm, D), lambda b, i: (b, i, 0)),
        out_shape=jax.ShapeDtypeStruct((B, S, D), _F32),
        scratch_shapes=[
            pltpu.VMEM((blk + tm, kv_heads * s_hd), _BF16),
            pltpu.VMEM((kv_heads * (s_hd + bf16_rows), blk + tm), _BF16),
            pltpu.VMEM((q_heads * s_hd, tm), _BF16),
            pltpu.VMEM((kv_heads, 2 * blk, group * blk), _F32),
            pltpu.VMEM((n_units, 2 * blk, group * blk), _F32),
            pltpu.VMEM((n_units, 2 * blk, group * blk), _BF16),
            pltpu.VMEM((n_units, 1, group * blk), _F32),
            pltpu.VMEM((q_heads * s_hd, tm), _BF16),
        ] + mlp_scratch,
        compiler_params=cparams2,
        name="swa_mlp",
    )(x1.reshape(B, S, D), wqT1, wk1, wvT1, swa_w_o[0].astype(_BF16), bucketT,
      rel_bias.astype(_F32), swa_sinks[0].astype(_F32),
      _row(ln_mix_g[1]), _row(ln_mix_b[1]), mlp_w_up[1].astype(_BF16),
      mlp_w_down[1].astype(_BF16), _row(ln_mlp_g[1]), _row(ln_mlp_b[1]))
    return out
```

```python
import functools
import math

import numpy as np
import jax
import jax.numpy as jnp
from jax import lax
from jax.experimental import pallas as pl
from jax.experimental.pallas import tpu as pltpu

_F32 = jnp.float32
_BF16 = jnp.bfloat16

_LN_EPS = 1e-5
_RMS_EPS = 1e-6
_ROPE_THETA = 10000.0
_SWA_BLOCK = 128
_REL_BUCKETS = 32
_REL_MAX_DIST = 128
_LOG2E = math.log2(math.e)

_V7X_VMEM_BYTES = 64 * 1024 * 1024
_VMEM_LIMIT_BYTES = _V7X_VMEM_BYTES - 8 * 1024 * 1024
_LANES = 128

_TOKEN_TILE = 512
_ATTN_TILE = 1024
_ATTN_ROW_BLOCK = 512
_ATTN_COL_BLOCK = 256
_FF_CHUNK = 1024
_MLP_ROW_BLOCK = 256

_NT = (((1,), (1,)), ((), ()))
_TN = (((0,), (0,)), ((), ()))


def _const_spec(shape):
    nd = len(shape)
    return pl.BlockSpec(shape, lambda *_: (0,) * nd, pipeline_mode=pl.Buffered(1))


def _layernorm(v, g, b):
    mu = jnp.mean(v, axis=-1, keepdims=True)
    d = v - mu
    var = jnp.mean(d * d, axis=-1, keepdims=True)
    return d * lax.rsqrt(var + _LN_EPS) * g + b


def _rmsnorm(v, g):
    return v * lax.rsqrt(jnp.mean(v * v, axis=-1, keepdims=True) + _RMS_EPS) * g


def _mix_mlp_ln(mix_rows, x_ref, alpha, g1_ref, b1_ref, wup_ref, wdown_ref, g2_ref, b2_ref,
                out_ref, x1_scr, h_scr):
    tm = x_ref.shape[0]
    d_ff = wup_ref.shape[1]
    blocks = [slice(r, r + _MLP_ROW_BLOCK) for r in range(0, tm, _MLP_ROW_BLOCK)]
    for rows in blocks:
        x1_scr[rows, :] = _layernorm(alpha * x_ref[rows, :] + mix_rows(rows), g1_ref[...], b1_ref[...])
    for rows in blocks:
        x1b = x1_scr[rows, :].astype(_BF16)
        for lo in range(0, d_ff, _FF_CHUNK):
            hmid = jnp.dot(x1b, wup_ref[:, lo:lo + _FF_CHUNK], preferred_element_type=_F32)
            hmid = jnp.maximum(hmid, 0.0)
            h_scr[rows, lo:lo + _FF_CHUNK] = (hmid * hmid).astype(_BF16)
    for rows in blocks:
        y = jnp.dot(h_scr[rows, :], wdown_ref[...], preferred_element_type=_F32)
        out_ref[rows, :] = _layernorm(alpha * x1_scr[rows, :] + y, g2_ref[...], b2_ref[...])


def _mla_proj_kernel(x_ref, win_ref, gq_ref, gkv_ref, wqT_ref, wuk_ref, wvT_ref,
                     cs_ref, cosT_ref, sinT_ref, qT_ref, k_ref, vT_ref,
                     *, q_rank, kv_rank, heads, nope, rope, q_scale):
    xb = x_ref[...].astype(_BF16)
    h = jnp.dot(xb, win_ref[...], preferred_element_type=_F32)
    cq = _rmsnorm(h[:, :q_rank], gq_ref[...]).astype(_BF16)
    ckv = _rmsnorm(h[:, q_rank:q_rank + kv_rank], gkv_ref[...]).astype(_BF16)
    t = h[:, q_rank + kv_rank:] * cs_ref[...]
    kr = (t + pltpu.roll(t, rope, 1))[:, :rope].astype(_BF16)

    half = rope // 2
    hd = nope + rope
    qT = lax.dot_general(wqT_ref[...], cq, _NT, preferred_element_type=_F32)
    cosT = cosT_ref[...]
    sinT = sinT_ref[...]
    for hh in range(heads):
        base = hh * hd
        qT_ref[hh, 0:nope, :] = (qT[base:base + nope] * q_scale).astype(_BF16)
        x1 = qT[base + nope:base + nope + half]
        x2 = qT[base + nope + half:base + hd]
        qT_ref[hh, nope:nope + half, :] = ((x1 * cosT - x2 * sinT) * q_scale).astype(_BF16)
        qT_ref[hh, nope + half:hd, :] = ((x1 * sinT + x2 * cosT) * q_scale).astype(_BF16)

    kn = jnp.dot(ckv, wuk_ref[...], preferred_element_type=_F32)
    for hh in range(heads):
        k_ref[hh, :, 0:nope] = kn[:, hh * nope:(hh + 1) * nope].astype(_BF16)
        k_ref[hh, :, nope:hd] = kr

    vdim = wvT_ref.shape[0] // heads
    vT = lax.dot_general(wvT_ref[...], ckv, _NT, preferred_element_type=_F32)
    tk = vT_ref.shape[3]
    pad = vT_ref.shape[2] - vdim
    ones_rows = (lax.broadcasted_iota(jnp.int32, (pad, tk), 0) == 0).astype(_BF16)
    for hh in range(heads):
        for j in range(vT_ref.shape[1]):
            vT_ref[hh, j, 0:vdim, :] = vT[hh * vdim:(hh + 1) * vdim, j * tk:(j + 1) * tk].astype(_BF16)
            vT_ref[hh, j, vdim:vdim + pad, :] = ones_rows


def _mla_attn_kernel(qT_ref, k_ref, vT_ref, o_ref, m_scr, acc_scr,
                     s_a, s_b, p_a, p_b, cm_a, cm_b, al_a, al_b):
    tk, tq = s_a.shape
    tsub = qT_ref.shape[2]
    nsub = tq // tsub
    n_q = qT_ref.shape[0] // nsub
    rb = _ATTN_ROW_BLOCK
    assert tq == 2 * tk and tk == vT_ref.shape[2] and tk % rb == 0

    cb = _ATTN_COL_BLOCK

    def q_stage(qi, c, s_ref, cm_ref, col_lo=0):
        for r0 in range(0, tk, rb):
            k = k_ref[pl.ds(pl.multiple_of(c * tk + r0, rb), rb), :]
            for lo in range(col_lo, tq, cb):
                j, off = divmod(lo, tsub)
                qT = qT_ref[qi * nsub + j, :, off:off + cb]
                s = jnp.dot(k, qT, preferred_element_type=_F32)
                s_ref[r0:r0 + rb, lo:lo + cb] = s
                cm = jnp.max(s, axis=0, keepdims=True)
                if r0 > 0:
                    cm = jnp.maximum(cm, cm_ref[:, lo:lo + cb])
                cm_ref[:, lo:lo + cb] = cm

    def x_stage(s_ref, cm_ref, p_ref, al_ref, mask_off=None, col_lo=0):
        for lo in range(col_lo, tq, cb):
            cols = slice(lo, lo + cb)
            s = s_ref[:, cols]
            if mask_off is None:
                cm = cm_ref[:, cols]
            else:
                krel = lax.broadcasted_iota(jnp.int32, s.shape, 0) + mask_off
                qrel = lax.broadcasted_iota(jnp.int32, s.shape, 1) + lo
                s = jnp.where(krel <= qrel, s, -jnp.inf)
                cm = jnp.max(s, axis=0, keepdims=True)
            m_prev = m_scr[:, cols]
            m_new = jnp.maximum(m_prev, cm)
            alpha = jnp.exp2(m_prev - m_new)
            p = jnp.exp2(s - m_new)
            p_ref[:, cols] = p.astype(_BF16)
            al_ref[:, cols] = alpha
            m_scr[:, cols] = m_new

    def v_stage(c, p_ref, al_ref, col_lo=0):
        for lo in range(col_lo, tq, cb):
            cols = slice(lo, lo + cb)
            acc_scr[:, cols] = al_ref[:, cols] * acc_scr[:, cols] + jnp.dot(
                vT_ref[c], p_ref[:, cols], preferred_element_type=_F32)

    def q_body(qi, carry):
        m_scr[...] = jnp.full(m_scr.shape, -jnp.inf, _F32)
        acc_scr[...] = jnp.zeros(acc_scr.shape, _F32)
        p_b[...] = jnp.zeros(p_b.shape, p_b.dtype)
        al_b[...] = jnp.ones(al_b.shape, _F32)
        q_stage(qi, 0, s_a, cm_a)

        def pair_body(u, c):
            a = 2 * u
            x_stage(s_a, cm_a, p_a, al_a)
            q_stage(qi, a + 1, s_b, cm_b)
            v_stage(jnp.maximum(a - 1, 0), p_b, al_b)
            q_stage(qi, a + 2, s_a, cm_a)
            x_stage(s_b, cm_b, p_b, al_b)
            v_stage(a, p_a, al_a)
            return c

        lax.fori_loop(0, qi, pair_body, 0)

        a = 2 * qi
        q_stage(qi, a + 1, s_b, cm_b, col_lo=tk)
        x_stage(s_a, cm_a, p_a, al_a, mask_off=0)
        v_stage(jnp.maximum(a - 1, 0), p_b, al_b)
        x_stage(s_b, cm_b, p_b, al_b, mask_off=tk, col_lo=tk)
        v_stage(a, p_a, al_a)
        v_stage(a + 1, p_b, al_b, col_lo=tk)

        vdim = o_ref.shape[1]
        inv_l = 1.0 / acc_scr[vdim:vdim + 1, :]
        o = (acc_scr[0:vdim, :] * inv_l).T
        o_ref[pl.ds(pl.multiple_of(qi * tq, tq), tq), :] = o.astype(o_ref.dtype)
        return carry

    lax.fori_loop(0, n_q, q_body, 0)


def _outproj_mlp_kernel(o_ref, x_ref, wo_ref, g1_ref, b1_ref, wup_ref, wdown_ref,
                        g2_ref, b2_ref, out_ref, x1_scr, h_scr, *, alpha):
    def mix_rows(rows):
        return jnp.dot(o_ref[rows, :], wo_ref[...], preferred_element_type=_F32)

    _mix_mlp_ln(mix_rows, x_ref, alpha, g1_ref, b1_ref, wup_ref, wdown_ref, g2_ref, b2_ref,
                out_ref, x1_scr, h_scr)


def _swa_mlp_kernel(x_ref, wqT_ref, wk_ref, wvT_ref, wo_ref, bucketT_ref, relb_ref, sink_ref,
                    g1_ref, b1_ref, wup_ref, wdown_ref, g2_ref, b2_ref, out_ref,
                    k_scr, vTe_scr, qT_scr, biasT_scr, s_scr, p_scr, m_scr, oT_scr, x1_scr, h_scr,
                    *, alpha, q_heads, kv_heads, head_dim, q_scale):
    blk = _SWA_BLOCK
    group = q_heads // kv_heads
    tm = x_ref.shape[0]
    n_blk = tm // blk
    vrows = vTe_scr.shape[0] // kv_heads
    first_tile = pl.program_id(1) == 0
    always = pl.program_id(1) >= 0

    @pl.when((pl.program_id(0) == 0) & first_tile)
    def _():
        bkt = bucketT_ref[...]
        for hq in range(q_heads):
            bias = jnp.full(bkt.shape, -jnp.inf, _F32)
            for bb in range(_REL_BUCKETS):
                bias = jnp.where(bkt == bb, relb_ref[bb, hq] * _LOG2E, bias)
            g, gi = divmod(hq, group)
            biasT_scr[g, :, gi * blk:(gi + 1) * blk] = bias

    @pl.when(first_tile)
    def _():
        k_scr[0:blk, :] = jnp.zeros((blk, k_scr.shape[1]), k_scr.dtype)
        vTe_scr[:, 0:blk] = jnp.zeros((vTe_scr.shape[0], blk), vTe_scr.dtype)

    xb = x_ref[...].astype(_BF16)
    qT_scr[...] = (lax.dot_general(wqT_ref[...], xb, _NT, preferred_element_type=_F32)
                   * q_scale).astype(_BF16)
    k_scr[blk:blk + tm, :] = jnp.dot(xb, wk_ref[...], preferred_element_type=_F32).astype(_BF16)
    vT = lax.dot_general(wvT_ref[...], xb, _NT, preferred_element_type=_F32).astype(_BF16)
    ones_rows = (lax.broadcasted_iota(jnp.int32, (vrows - head_dim, tm), 0) == 0).astype(_BF16)
    for g in range(kv_heads):
        vTe_scr[g * vrows:g * vrows + head_dim, blk:blk + tm] = vT[g * head_dim:(g + 1) * head_dim]
        vTe_scr[g * vrows + head_dim:(g + 1) * vrows, blk:blk + tm] = ones_rows

    first_pen = jnp.where(first_tile, -jnp.inf, 0.0).astype(_F32)
    zeros_q = jnp.zeros((head_dim, group * blk), _BF16)
    heads_per_lane_tile = _LANES // head_dim
    units = [(n, g) for n in range(n_blk) for g in range(kv_heads)]

    def sink_row(g):
        return jnp.concatenate(
            [jnp.full((1, blk), sink_ref[g * group + gi] * _LOG2E, _F32) for gi in range(group)],
            axis=1)

    @pl.when(always)
    def _():
        sinks = [sink_row(g) for g in range(kv_heads)]
        for u, (n, g) in enumerate(units):
            lt = g // heads_per_lane_tile
            kpair = k_scr[n * blk:(n + 2) * blk, lt * _LANES:(lt + 1) * _LANES]
            qg = jnp.concatenate(
                [qT_scr[(g * group + gi) * head_dim:(g * group + gi + 1) * head_dim,
                        n * blk:(n + 1) * blk] for gi in range(group)], axis=1)
            pieces = [zeros_q] * heads_per_lane_tile
            pieces[g % heads_per_lane_tile] = qg
            qpad = jnp.concatenate(pieces, axis=0)
            s = jnp.dot(kpair, qpad, preferred_element_type=_F32) + biasT_scr[g]
            if n == 0:
                s = jnp.concatenate([s[:blk] + first_pen, s[blk:]], axis=0)
            s_scr[u] = s
            m_scr[u] = jnp.maximum(jnp.max(s, axis=0, keepdims=True), sinks[g])

    @pl.when(always)
    def _():
        for u in range(len(units)):
            p_scr[u] = jnp.exp2(s_scr[u] - m_scr[u]).astype(_BF16)

    @pl.when(always)
    def _():
        sinks = [sink_row(g) for g in range(kv_heads)]
        for u, (n, g) in enumerate(units):
            oT = jnp.dot(vTe_scr[g * vrows:(g + 1) * vrows, n * blk:(n + 2) * blk], p_scr[u],
                         preferred_element_type=_F32)
            denom = oT[head_dim:head_dim + 1] + jnp.exp2(sinks[g] - m_scr[u])
            o = (oT[:head_dim] * (1.0 / denom)).astype(_BF16)
            for gi in range(group):
                hq = g * group + gi
                oT_scr[hq * head_dim:(hq + 1) * head_dim, n * blk:(n + 1) * blk] = (
                    o[:, gi * blk:(gi + 1) * blk])
        k_scr[0:blk, :] = k_scr[tm:tm + blk, :]
        vTe_scr[:, 0:blk] = vTe_scr[:, tm:tm + blk]

    def mix_rows(rows):
        return lax.dot_general(oT_scr[:, rows], wo_ref[...], _TN, preferred_element_type=_F32)

    _mix_mlp_ln(mix_rows, x_ref, alpha, g1_ref, b1_ref, wup_ref, wdown_ref, g2_ref, b2_ref,
                out_ref, x1_scr, h_scr)


def _t5_bucket_table(blk):
    i = np.arange(blk)[None, :]
    j = np.arange(2 * blk)[:, None]
    dist = i + blk - j
    max_exact = _REL_BUCKETS // 2
    nf = np.maximum(dist, 1).astype(np.float32)
    large = max_exact + (np.log(nf / np.float32(max_exact))
                         / np.float32(math.log(_REL_MAX_DIST / max_exact))
                         * np.float32(_REL_BUCKETS - max_exact)).astype(np.int32)
    large = np.minimum(large, _REL_BUCKETS - 1)
    bucket = np.where(dist < max_exact, np.maximum(dist, 0), large)
    valid = (dist >= 0) & (dist < blk)
    return np.where(valid, bucket, -1).astype(np.int32)


def _rope_tables(seq, rope):
    half = rope // 2
    inv = _ROPE_THETA ** (-jnp.arange(half, dtype=_F32) / half)
    ang = jnp.arange(seq).astype(_F32)[:, None] * inv[None, :]
    cos, sin = jnp.cos(ang), jnp.sin(ang)
    cs = jnp.concatenate([cos, cos, -sin, sin], axis=1)
    return cs, cos.T, sin.T


def _row(v):
    return v.reshape(1, -1).astype(_F32)


def kernel(x, mla_w_in, mla_g_q, mla_g_kv, mla_w_uq, mla_w_uk, mla_w_uv, mla_w_o, kv_w_shared,
           swa_w_q, swa_sinks, swa_w_o, rel_bias, mlp_w_up, mlp_w_down, ln_mix_g, ln_mix_b,
           ln_mlp_g, ln_mlp_b):
    B, S, D = x.shape
    depth = mlp_w_up.shape[0]
    assert depth == 2 and mla_w_in.shape[0] == 1 and swa_w_q.shape[0] == 1
    alpha = (2 * depth) ** 0.25
    T = B * S
    tm = _TOKEN_TILE
    ta = _ATTN_TILE
    tk = ta // 2
    bf16_rows = 16
    assert ta % tm == 0 and tm % tk == 0 and S % ta == 0

    q_rank, heads, hd = mla_w_uq.shape[1:]
    kv_rank, _, nope = mla_w_uk.shape[1:]
    vdim = mla_w_uv.shape[3]
    vrows = vdim + bf16_rows
    rope = hd - nope
    half = rope // 2
    assert 2 * rope == _LANES and mla_w_in.shape[2] == q_rank + kv_rank + rope

    w_in = mla_w_in[0]
    r0 = q_rank + kv_rank
    w_in_ext = jnp.concatenate(
        [w_in, w_in[:, r0 + half:r0 + rope], w_in[:, r0:r0 + half]], axis=1).astype(_BF16)
    wqT = mla_w_uq[0].reshape(q_rank, heads * hd).T.astype(_BF16)
    wuk = mla_w_uk[0].reshape(kv_rank, heads * nope).astype(_BF16)
    wvT = mla_w_uv[0].reshape(kv_rank, heads * vdim).T.astype(_BF16)
    cs, cosT, sinT = _rope_tables(S, rope)
    n_t = S // tm
    q_scale = hd ** -0.5 * _LOG2E

    cparams2 = pltpu.CompilerParams(dimension_semantics=("arbitrary", "arbitrary"),
                                    vmem_limit_bytes=_VMEM_LIMIT_BYTES)
    cparams1 = pltpu.CompilerParams(dimension_semantics=("arbitrary",),
                                    vmem_limit_bytes=_VMEM_LIMIT_BYTES)

    qT_all, k_all, vT_all = pl.pallas_call(
        functools.partial(_mla_proj_kernel, q_rank=q_rank, kv_rank=kv_rank, heads=heads,
                          nope=nope, rope=rope, q_scale=q_scale),
        grid=(B, n_t),
        in_specs=[
            pl.BlockSpec((None, tm, D), lambda b, i: (b, i, 0)),
            _const_spec(w_in_ext.shape),
            _const_spec((1, q_rank)),
            _const_spec((1, kv_rank)),
            _const_spec(wqT.shape),
            _const_spec(wuk.shape),
            _const_spec(wvT.shape),
            pl.BlockSpec((tm, 2 * rope), lambda b, i: (i, 0)),
            pl.BlockSpec((half, tm), lambda b, i: (0, i)),
            pl.BlockSpec((half, tm), lambda b, i: (0, i)),
        ],
        out_specs=[
            pl.BlockSpec((None, heads, None, hd, tm), lambda b, i: (b, 0, i, 0, 0)),
            pl.BlockSpec((None, heads, tm, hd), lambda b, i: (b, 0, i, 0)),
            pl.BlockSpec((None, heads, tm // tk, vrows, tk), lambda b, i: (b, 0, i, 0, 0)),
        ],
        out_shape=[
            jax.ShapeDtypeStruct((B, heads, n_t, hd, tm), _BF16),
            jax.ShapeDtypeStruct((B, heads, S, hd), _BF16),
            jax.ShapeDtypeStruct((B, heads, S // tk, vrows, tk), _BF16),
        ],
        compiler_params=cparams2,
        name="mla_proj",
    )(x, w_in_ext, _row(mla_g_q[0]), _row(mla_g_kv[0]), wqT, wuk, wvT, cs, cosT, sinT)

    o = pl.pallas_call(
        _mla_attn_kernel,
        grid=(B, heads),
        in_specs=[
            pl.BlockSpec((None, None, n_t, hd, tm), lambda b, h: (b, h, 0, 0, 0)),
            pl.BlockSpec((None, None, S, hd), lambda b, h: (b, h, 0, 0)),
            pl.BlockSpec((None, None, S // tk, vrows, tk), lambda b, h: (b, h, 0, 0, 0)),
        ],
        out_specs=pl.BlockSpec((None, S, vdim), lambda b, h: (b, 0, h)),
        out_shape=jax.ShapeDtypeStruct((B, S, heads * vdim), _BF16),
        scratch_shapes=[
            pltpu.VMEM((1, ta), _F32),
            pltpu.VMEM((vrows, ta), _F32),
            pltpu.VMEM((tk, ta), _F32),
            pltpu.VMEM((tk, ta), _F32),
            pltpu.VMEM((tk, ta), _BF16),
            pltpu.VMEM((tk, ta), _BF16),
            pltpu.VMEM((1, ta), _F32),
            pltpu.VMEM((1, ta), _F32),
            pltpu.VMEM((1, ta), _F32),
            pltpu.VMEM((1, ta), _F32),
        ],
        compiler_params=cparams2,
        name="mla_attn",
    )(qT_all, k_all, vT_all)

    d_ff = mlp_w_up.shape[2]
    mlp_scratch = [
        pltpu.VMEM((tm, D), _F32),
        pltpu.VMEM((tm, d_ff), _BF16),
    ]
    x1 = pl.pallas_call(
        functools.partial(_outproj_mlp_kernel, alpha=alpha),
        grid=(T // tm,),
        in_specs=[
            pl.BlockSpec((tm, heads * vdim), lambda i: (i, 0)),
            pl.BlockSpec((tm, D), lambda i: (i, 0)),
            _const_spec((heads * vdim, D)),
            _const_spec((1, D)),
            _const_spec((1, D)),
            _const_spec((D, d_ff)),
            _const_spec((d_ff, D)),
            _const_spec((1, D)),
            _const_spec((1, D)),
        ],
        out_specs=pl.BlockSpec((tm, D), lambda i: (i, 0)),
        out_shape=jax.ShapeDtypeStruct((T, D), _F32),
        scratch_shapes=mlp_scratch,
        compiler_params=cparams1,
        name="mla_out_mlp",
    )(o.reshape(T, heads * vdim), x.reshape(T, D), mla_w_o[0].astype(_BF16),
      _row(ln_mix_g[0]), _row(ln_mix_b[0]), mlp_w_up[0].astype(_BF16),
      mlp_w_down[0].astype(_BF16), _row(ln_mlp_g[0]), _row(ln_mlp_b[0]))

    q_heads = swa_sinks.shape[1]
    s_hd = swa_w_q.shape[2] // q_heads
    kv_heads = kv_w_shared.shape[1] // (2 * s_hd)
    group = q_heads // kv_heads
    blk = _SWA_BLOCK
    assert _LANES % s_hd == 0 and tm % blk == 0
    wqT1 = swa_w_q[0].T.astype(_BF16)
    wk1 = kv_w_shared[:, :kv_heads * s_hd].astype(_BF16)
    wvT1 = kv_w_shared[:, kv_heads * s_hd:].T.astype(_BF16)
    bucketT = jnp.asarray(_t5_bucket_table(blk))
    n_units = (tm // blk) * kv_heads

    smem = pl.BlockSpec(memory_space=pltpu.SMEM)
    out = pl.pallas_call(
        functools.partial(_swa_mlp_kernel, alpha=alpha, q_heads=q_heads, kv_heads=kv_heads,
                          head_dim=s_hd, q_scale=s_hd ** -0.5 * _LOG2E),
        grid=(B, n_t),
        in_specs=[
            pl.BlockSpec((None, tm, D), lambda b, i: (b, i, 0)),
            _const_spec(wqT1.shape),
            _const_spec(wk1.shape),
            _const_spec(wvT1.shape),
            _const_spec((q_heads * s_hd, D)),
            _const_spec(bucketT.shape),
            smem,
            smem,
            _const_spec((1, D)),
            _const_spec((1, D)),
            _const_spec((D, d_ff)),
            _const_spec((d_ff, D)),
            _const_spec((1, D)),
            _const_spec((1, D)),
        ],
        out_specs=pl.BlockSpec((None, tm, D), lambda b, i: (b, i, 0)),
        out_shape=jax.ShapeDtypeStruct((B, S, D), _F32),
        scratch_shapes=[
            pltpu.VMEM((blk + tm, kv_heads * s_hd), _BF16),
            pltpu.VMEM((kv_heads * (s_hd + bf16_rows), blk + tm), _BF16),
            pltpu.VMEM((q_heads * s_hd, tm), _BF16),
            pltpu.VMEM((kv_heads, 2 * blk, group * blk), _F32),
            pltpu.VMEM((n_units, 2 * blk, group * blk), _F32),
            pltpu.VMEM((n_units, 2 * blk, group * blk), _BF16),
            pltpu.VMEM((n_units, 1, group * blk), _F32),
            pltpu.VMEM((q_heads * s_hd, tm), _BF16),
        ] + mlp_scratch,
        compiler_params=cparams2,
        name="swa_mlp",
    )(x1.reshape(B, S, D), wqT1, wk1, wvT1, swa_w_o[0].astype(_BF16), bucketT,
      rel_bias.astype(_F32), swa_sinks[0].astype(_F32),
      _row(ln_mix_g[1]), _row(ln_mix_b[1]), mlp_w_up[1].astype(_BF16),
      mlp_w_down[1].astype(_BF16), _row(ln_mlp_g[1]), _row(ln_mlp_b[1]))
    return out
```

```python
import functools
import math

import numpy as np
import jax
import jax.numpy as jnp
from jax import lax
from jax.experimental import pallas as pl
from jax.experimental.pallas import tpu as pltpu

_F32 = jnp.float32
_BF16 = jnp.bfloat16

_LN_EPS = 1e-5
_RMS_EPS = 1e-6
_ROPE_THETA = 10000.0
_SWA_BLOCK = 128
_REL_BUCKETS = 32
_REL_MAX_DIST = 128
_LOG2E = math.log2(math.e)

_V7X_VMEM_BYTES = 64 * 1024 * 1024
_VMEM_LIMIT_BYTES = _V7X_VMEM_BYTES - 8 * 1024 * 1024
_LANES = 128

_TOKEN_TILE = 512
_ATTN_TILE = 1024
_ATTN_ROW_BLOCK = 512
_ATTN_COL_BLOCK = 256
_FF_CHUNK = 1024
_MLP_ROW_BLOCK = 256

_NT = (((1,), (1,)), ((), ()))
_TN = (((0,), (0,)), ((), ()))


def _const_spec(shape):
    nd = len(shape)
    return pl.BlockSpec(shape, lambda *_: (0,) * nd, pipeline_mode=pl.Buffered(1))


def _layernorm(v, g, b):
    mu = jnp.mean(v, axis=-1, keepdims=True)
    d = v - mu
    var = jnp.mean(d * d, axis=-1, keepdims=True)
    return d * lax.rsqrt(var + _LN_EPS) * g + b


def _rmsnorm(v, g):
    return v * lax.rsqrt(jnp.mean(v * v, axis=-1, keepdims=True) + _RMS_EPS) * g


def _mix_mlp_ln(mix_rows, x_ref, alpha, g1_ref, b1_ref, wup_ref, wdown_ref, g2_ref, b2_ref,
                out_ref, x1_scr, h_scr):
    tm = x_ref.shape[0]
    d_ff = wup_ref.shape[1]
    blocks = [slice(r, r + _MLP_ROW_BLOCK) for r in range(0, tm, _MLP_ROW_BLOCK)]
    for rows in blocks:
        x1_scr[rows, :] = _layernorm(alpha * x_ref[rows, :] + mix_rows(rows), g1_ref[...], b1_ref[...])
    for rows in blocks:
        x1b = x1_scr[rows, :].astype(_BF16)
        for lo in range(0, d_ff, _FF_CHUNK):
            hmid = jnp.dot(x1b, wup_ref[:, lo:lo + _FF_CHUNK], preferred_element_type=_F32)
            hmid = jnp.maximum(hmid, 0.0)
            h_scr[rows, lo:lo + _FF_CHUNK] = (hmid * hmid).astype(_BF16)
    for rows in blocks:
        y = jnp.dot(h_scr[rows, :], wdown_ref[...], preferred_element_type=_F32)
        out_ref[rows, :] = _layernorm(alpha * x1_scr[rows, :] + y, g2_ref[...], b2_ref[...])


def _mla_proj_kernel(x_ref, win_ref, gq_ref, gkv_ref, wqT_ref, wuk_ref, wvT_ref,
                     cs_ref, cosT_ref, sinT_ref, qT_ref, k_ref, vT_ref,
                     *, q_rank, kv_rank, heads, nope, rope, q_scale):
    xb = x_ref[...].astype(_BF16)
    h = jnp.dot(xb, win_ref[...], preferred_element_type=_F32)
    cq = _rmsnorm(h[:, :q_rank], gq_ref[...]).astype(_BF16)
    ckv = _rmsnorm(h[:, q_rank:q_rank + kv_rank], gkv_ref[...]).astype(_BF16)
    t = h[:, q_rank + kv_rank:] * cs_ref[...]
    kr = (t + pltpu.roll(t, rope, 1))[:, :rope].astype(_BF16)

    half = rope // 2
    hd = nope + rope
    qT = lax.dot_general(wqT_ref[...], cq, _NT, preferred_element_type=_F32)
    cosT = cosT_ref[...]
    sinT = sinT_ref[...]
    for hh in range(heads):
        base = hh * hd
        qT_ref[hh, 0:nope, :] = (qT[base:base + nope] * q_scale).astype(_BF16)
        x1 = qT[base + nope:base + nope + half]
        x2 = qT[base + nope + half:base + hd]
        qT_ref[hh, nope:nope + half, :] = ((x1 * cosT - x2 * sinT) * q_scale).astype(_BF16)
        qT_ref[hh, nope + half:hd, :] = ((x1 * sinT + x2 * cosT) * q_scale).astype(_BF16)

    kn = jnp.dot(ckv, wuk_ref[...], preferred_element_type=_F32)
    for hh in range(heads):
        k_ref[hh, :, 0:nope] = kn[:, hh * nope:(hh + 1) * nope].astype(_BF16)
        k_ref[hh, :, nope:hd] = kr

    vdim = wvT_ref.shape[0] // heads
    vT = lax.dot_general(wvT_ref[...], ckv, _NT, preferred_element_type=_F32)
    tk = vT_ref.shape[3]
    pad = vT_ref.shape[2] - vdim
    ones_rows = (lax.broadcasted_iota(jnp.int32, (pad, tk), 0) == 0).astype(_BF16)
    for hh in range(heads):
        for j in range(vT_ref.shape[1]):
            vT_ref[hh, j, 0:vdim, :] = vT[hh * vdim:(hh + 1) * vdim, j * tk:(j + 1) * tk].astype(_BF16)
            vT_ref[hh, j, vdim:vdim + pad, :] = ones_rows


def _mla_attn_kernel(qT_ref, k_ref, vT_ref, o_ref, m_scr, acc_scr,
                     s00, s01, s10, s11, cm00, cm01, cm10, cm11, p_a, p_b, al_a, al_b):
    tk, tq = s00.shape
    tsub = qT_ref.shape[2]
    nsub = tq // tsub
    n_q = qT_ref.shape[0] // nsub
    rb = _ATTN_ROW_BLOCK
    assert tq == 2 * tk and tk == vT_ref.shape[2] and tk % rb == 0

    cb = _ATTN_COL_BLOCK

    def q_stage(qi, c, s_ref, cm_ref, col_lo=0):
        for r0 in range(0, tk, rb):
            k = k_ref[pl.ds(pl.multiple_of(c * tk + r0, rb), rb), :]
            for lo in range(col_lo, tq, cb):
                j, off = divmod(lo, tsub)
                qT = qT_ref[qi * nsub + j, :, off:off + cb]
                s = jnp.dot(k, qT, preferred_element_type=_F32)
                s_ref[r0:r0 + rb, lo:lo + cb] = s
                cm = jnp.max(s, axis=0, keepdims=True)
                if r0 > 0:
                    cm = jnp.maximum(cm, cm_ref[:, lo:lo + cb])
                cm_ref[:, lo:lo + cb] = cm

    def x_stage(s_ref, cm_ref, p_ref, al_ref, mask_off=None, col_lo=0):
        for lo in range(col_lo, tq, cb):
            cols = slice(lo, lo + cb)
            s = s_ref[:, cols]
            if mask_off is None:
                cm = cm_ref[:, cols]
            else:
                krel = lax.broadcasted_iota(jnp.int32, s.shape, 0) + mask_off
                qrel = lax.broadcasted_iota(jnp.int32, s.shape, 1) + lo
                s = jnp.where(krel <= qrel, s, -jnp.inf)
                cm = jnp.max(s, axis=0, keepdims=True)
            m_prev = m_scr[:, cols]
            m_new = jnp.maximum(m_prev, cm)
            alpha = jnp.exp2(m_prev - m_new)
            p = jnp.exp2(s - m_new)
            p_ref[:, cols] = p.astype(_BF16)
            al_ref[:, cols] = alpha
            m_scr[:, cols] = m_new

    def v_stage(c, p_ref, al_ref, col_lo=0):
        for lo in range(col_lo, tq, cb):
            cols = slice(lo, lo + cb)
            acc_scr[:, cols] = al_ref[:, cols] * acc_scr[:, cols] + jnp.dot(
                vT_ref[c], p_ref[:, cols], preferred_element_type=_F32)

    s_bufs = (((s00, cm00), (s01, cm01)), ((s10, cm10), (s11, cm11)))

    def start_tile():
        m_scr[...] = jnp.full(m_scr.shape, -jnp.inf, _F32)
        acc_scr[...] = jnp.zeros(acc_scr.shape, _F32)
        p_b[...] = jnp.zeros(p_b.shape, p_b.dtype)
        al_b[...] = jnp.ones(al_b.shape, _F32)

    def regular_step(qi, a, par):
        (sa, cma), (sb, cmb) = s_bufs[par]
        (na, ncma), (nb, ncmb) = s_bufs[1 - par]
        q_stage(qi, a + 2, na, ncma)
        x_stage(sa, cma, p_a, al_a)
        v_stage(jnp.maximum(a - 1, 0), p_b, al_b)
        q_stage(qi, a + 3, nb, ncmb)
        x_stage(sb, cmb, p_b, al_b)
        v_stage(a, p_a, al_a)

    def diagonal_step(qi, par):
        a = 2 * qi
        (sa, cma), (sb, cmb) = s_bufs[par]
        (na, ncma), (nb, ncmb) = s_bufs[1 - par]
        q_next = jnp.minimum(qi + 1, n_q - 1)
        q_stage(q_next, 0, na, ncma)
        x_stage(sa, cma, p_a, al_a, mask_off=0)
        v_stage(jnp.maximum(a - 1, 0), p_b, al_b)
        q_stage(q_next, 1, nb, ncmb)
        x_stage(sb, cmb, p_b, al_b, mask_off=tk, col_lo=tk)
        v_stage(a, p_a, al_a)
        v_stage(a + 1, p_b, al_b, col_lo=tk)
        vdim = o_ref.shape[1]
        inv_l = 1.0 / acc_scr[vdim:vdim + 1, :]
        o = (acc_scr[0:vdim, :] * inv_l).T
        o_ref[pl.ds(pl.multiple_of(qi * tq, tq), tq), :] = o.astype(o_ref.dtype)
        start_tile()

    def by_parity(par, fn):
        for static_par in range(2):
            @pl.when(par == static_par)
            def _():
                fn(static_par)

    def tile_body(qi, par):
        def step_body(u, par):
            by_parity(par, functools.partial(regular_step, qi, 2 * u))
            return 1 - par

        par = lax.fori_loop(0, qi, step_body, par)
        by_parity(par, functools.partial(diagonal_step, qi))
        return 1 - par

    start_tile()
    q_stage(0, 0, s00, cm00)
    q_stage(0, 1, s01, cm01)
    lax.fori_loop(0, n_q, tile_body, jnp.int32(0))


def _outproj_mlp_kernel(o_ref, x_ref, wo_ref, g1_ref, b1_ref, wup_ref, wdown_ref,
                        g2_ref, b2_ref, out_ref, x1_scr, h_scr, *, alpha):
    def mix_rows(rows):
        return jnp.dot(o_ref[rows, :], wo_ref[...], preferred_element_type=_F32)

    _mix_mlp_ln(mix_rows, x_ref, alpha, g1_ref, b1_ref, wup_ref, wdown_ref, g2_ref, b2_ref,
                out_ref, x1_scr, h_scr)


def _swa_mlp_kernel(x_ref, wqT_ref, wk_ref, wvT_ref, wo_ref, bucketT_ref, relb_ref, sink_ref,
                    g1_ref, b1_ref, wup_ref, wdown_ref, g2_ref, b2_ref, out_ref,
                    k_scr, vTe_scr, qT_scr, biasT_scr, s_scr, p_scr, m_scr, oT_scr, x1_scr, h_scr,
                    *, alpha, q_heads, kv_heads, head_dim, q_scale):
    blk = _SWA_BLOCK
    group = q_heads // kv_heads
    tm = x_ref.shape[0]
    n_blk = tm // blk
    vrows = vTe_scr.shape[0] // kv_heads
    first_tile = pl.program_id(1) == 0
    always = pl.program_id(1) >= 0

    @pl.when((pl.program_id(0) == 0) & first_tile)
    def _():
        bkt = bucketT_ref[...]
        for hq in range(q_heads):
            bias = jnp.full(bkt.shape, -jnp.inf, _F32)
            for bb in range(_REL_BUCKETS):
                bias = jnp.where(bkt == bb, relb_ref[bb, hq] * _LOG2E, bias)
            g, gi = divmod(hq, group)
            biasT_scr[g, :, gi * blk:(gi + 1) * blk] = bias

    @pl.when(first_tile)
    def _():
        k_scr[0:blk, :] = jnp.zeros((blk, k_scr.shape[1]), k_scr.dtype)
        vTe_scr[:, 0:blk] = jnp.zeros((vTe_scr.shape[0], blk), vTe_scr.dtype)

    xb = x_ref[...].astype(_BF16)
    qT_scr[...] = (lax.dot_general(wqT_ref[...], xb, _NT, preferred_element_type=_F32)
                   * q_scale).astype(_BF16)
    k_scr[blk:blk + tm, :] = jnp.dot(xb, wk_ref[...], preferred_element_type=_F32).astype(_BF16)
    vT = lax.dot_general(wvT_ref[...], xb, _NT, preferred_element_type=_F32).astype(_BF16)
    ones_rows = (lax.broadcasted_iota(jnp.int32, (vrows - head_dim, tm), 0) == 0).astype(_BF16)
    for g in range(kv_heads):
        vTe_scr[g * vrows:g * vrows + head_dim, blk:blk + tm] = vT[g * head_dim:(g + 1) * head_dim]
        vTe_scr[g * vrows + head_dim:(g + 1) * vrows, blk:blk + tm] = ones_rows

    first_pen = jnp.where(first_tile, -jnp.inf, 0.0).astype(_F32)
    zeros_q = jnp.zeros((head_dim, group * blk), _BF16)
    heads_per_lane_tile = _LANES // head_dim
    units = [(n, g) for n in range(n_blk) for g in range(kv_heads)]

    def sink_row(g):
        return jnp.concatenate(
            [jnp.full((1, blk), sink_ref[g * group + gi] * _LOG2E, _F32) for gi in range(group)],
            axis=1)

    @pl.when(always)
    def _():
        sinks = [sink_row(g) for g in range(kv_heads)]
        for u, (n, g) in enumerate(units):
            lt = g // heads_per_lane_tile
            kpair = k_scr[n * blk:(n + 2) * blk, lt * _LANES:(lt + 1) * _LANES]
            qg = jnp.concatenate(
                [qT_scr[(g * group + gi) * head_dim:(g * group + gi + 1) * head_dim,
                        n * blk:(n + 1) * blk] for gi in range(group)], axis=1)
            pieces = [zeros_q] * heads_per_lane_tile
            pieces[g % heads_per_lane_tile] = qg
            qpad = jnp.concatenate(pieces, axis=0)
            s = jnp.dot(kpair, qpad, preferred_element_type=_F32) + biasT_scr[g]
            if n == 0:
                s = jnp.concatenate([s[:blk] + first_pen, s[blk:]], axis=0)
            s_scr[u] = s
            m_scr[u] = jnp.maximum(jnp.max(s, axis=0, keepdims=True), sinks[g])

    @pl.when(always)
    def _():
        for u in range(len(units)):
            p_scr[u] = jnp.exp2(s_scr[u] - m_scr[u]).astype(_BF16)

    @pl.when(always)
    def _():
        sinks = [sink_row(g) for g in range(kv_heads)]
        for u, (n, g) in enumerate(units):
            oT = jnp.dot(vTe_scr[g * vrows:(g + 1) * vrows, n * blk:(n + 2) * blk], p_scr[u],
                         preferred_element_type=_F32)
            denom = oT[head_dim:head_dim + 1] + jnp.exp2(sinks[g] - m_scr[u])
            o = (oT[:head_dim] * (1.0 / denom)).astype(_BF16)
            for gi in range(group):
                hq = g * group + gi
                oT_scr[hq * head_dim:(hq + 1) * head_dim, n * blk:(n + 1) * blk] = (
                    o[:, gi * blk:(gi + 1) * blk])
        k_scr[0:blk, :] = k_scr[tm:tm + blk, :]
        vTe_scr[:, 0:blk] = vTe_scr[:, tm:tm + blk]

    def mix_rows(rows):
        return lax.dot_general(oT_scr[:, rows], wo_ref[...], _TN, preferred_element_type=_F32)

    _mix_mlp_ln(mix_rows, x_ref, alpha, g1_ref, b1_ref, wup_ref, wdown_ref, g2_ref, b2_ref,
                out_ref, x1_scr, h_scr)


def _t5_bucket_table(blk):
    i = np.arange(blk)[None, :]
    j = np.arange(2 * blk)[:, None]
    dist = i + blk - j
    max_exact = _REL_BUCKETS // 2
    nf = np.maximum(dist, 1).astype(np.float32)
    large = max_exact + (np.log(nf / np.float32(max_exact))
                         / np.float32(math.log(_REL_MAX_DIST / max_exact))
                         * np.float32(_REL_BUCKETS - max_exact)).astype(np.int32)
    large = np.minimum(large, _REL_BUCKETS - 1)
    bucket = np.where(dist < max_exact, np.maximum(dist, 0), large)
    valid = (dist >= 0) & (dist < blk)
    return np.where(valid, bucket, -1).astype(np.int32)


def _rope_tables(seq, rope):
    half = rope // 2
    inv = _ROPE_THETA ** (-jnp.arange(half, dtype=_F32) / half)
    ang = jnp.arange(seq).astype(_F32)[:, None] * inv[None, :]
    cos, sin = jnp.cos(ang), jnp.sin(ang)
    cs = jnp.concatenate([cos, cos, -sin, sin], axis=1)
    return cs, cos.T, sin.T


def _row(v):
    return v.reshape(1, -1).astype(_F32)


def kernel(x, mla_w_in, mla_g_q, mla_g_kv, mla_w_uq, mla_w_uk, mla_w_uv, mla_w_o, kv_w_shared,
           swa_w_q, swa_sinks, swa_w_o, rel_bias, mlp_w_up, mlp_w_down, ln_mix_g, ln_mix_b,
           ln_mlp_g, ln_mlp_b):
    B, S, D = x.shape
    depth = mlp_w_up.shape[0]
    assert depth == 2 and mla_w_in.shape[0] == 1 and swa_w_q.shape[0] == 1
    alpha = (2 * depth) ** 0.25
    T = B * S
    tm = _TOKEN_TILE
    ta = _ATTN_TILE
    tk = ta // 2
    bf16_rows = 16
    assert ta % tm == 0 and tm % tk == 0 and S % ta == 0

    q_rank, heads, hd = mla_w_uq.shape[1:]
    kv_rank, _, nope = mla_w_uk.shape[1:]
    vdim = mla_w_uv.shape[3]
    vrows = vdim + bf16_rows
    rope = hd - nope
    half = rope // 2
    assert 2 * rope == _LANES and mla_w_in.shape[2] == q_rank + kv_rank + rope

    w_in = mla_w_in[0]
    r0 = q_rank + kv_rank
    w_in_ext = jnp.concatenate(
        [w_in, w_in[:, r0 + half:r0 + rope], w_in[:, r0:r0 + half]], axis=1).astype(_BF16)
    wqT = mla_w_uq[0].reshape(q_rank, heads * hd).T.astype(_BF16)
    wuk = mla_w_uk[0].reshape(kv_rank, heads * nope).astype(_BF16)
    wvT = mla_w_uv[0].reshape(kv_rank, heads * vdim).T.astype(_BF16)
    cs, cosT, sinT = _rope_tables(S, rope)
    n_t = S // tm
    q_scale = hd ** -0.5 * _LOG2E

    cparams2 = pltpu.CompilerParams(dimension_semantics=("arbitrary", "arbitrary"),
                                    vmem_limit_bytes=_VMEM_LIMIT_BYTES)
    cparams1 = pltpu.CompilerParams(dimension_semantics=("arbitrary",),
                                    vmem_limit_bytes=_VMEM_LIMIT_BYTES)

    qT_all, k_all, vT_all = pl.pallas_call(
        functools.partial(_mla_proj_kernel, q_rank=q_rank, kv_rank=kv_rank, heads=heads,
                          nope=nope, rope=rope, q_scale=q_scale),
        grid=(B, n_t),
        in_specs=[
            pl.BlockSpec((None, tm, D), lambda b, i: (b, i, 0)),
            _const_spec(w_in_ext.shape),
            _const_spec((1, q_rank)),
            _const_spec((1, kv_rank)),
            _const_spec(wqT.shape),
            _const_spec(wuk.shape),
            _const_spec(wvT.shape),
            pl.BlockSpec((tm, 2 * rope), lambda b, i: (i, 0)),
            pl.BlockSpec((half, tm), lambda b, i: (0, i)),
            pl.BlockSpec((half, tm), lambda b, i: (0, i)),
        ],
        out_specs=[
            pl.BlockSpec((None, heads, None, hd, tm), lambda b, i: (b, 0, i, 0, 0)),
            pl.BlockSpec((None, heads, tm, hd), lambda b, i: (b, 0, i, 0)),
            pl.BlockSpec((None, heads, tm // tk, vrows, tk), lambda b, i: (b, 0, i, 0, 0)),
        ],
        out_shape=[
            jax.ShapeDtypeStruct((B, heads, n_t, hd, tm), _BF16),
            jax.ShapeDtypeStruct((B, heads, S, hd), _BF16),
            jax.ShapeDtypeStruct((B, heads, S // tk, vrows, tk), _BF16),
        ],
        compiler_params=cparams2,
        name="mla_proj",
    )(x, w_in_ext, _row(mla_g_q[0]), _row(mla_g_kv[0]), wqT, wuk, wvT, cs, cosT, sinT)

    o = pl.pallas_call(
        _mla_attn_kernel,
        grid=(B, heads),
        in_specs=[
            pl.BlockSpec((None, None, n_t, hd, tm), lambda b, h: (b, h, 0, 0, 0)),
            pl.BlockSpec((None, None, S, hd), lambda b, h: (b, h, 0, 0)),
            pl.BlockSpec((None, None, S // tk, vrows, tk), lambda b, h: (b, h, 0, 0, 0)),
        ],
        out_specs=pl.BlockSpec((None, S, vdim), lambda b, h: (b, 0, h)),
        out_shape=jax.ShapeDtypeStruct((B, S, heads * vdim), _BF16),
        scratch_shapes=[
            pltpu.VMEM((1, ta), _F32),
            pltpu.VMEM((vrows, ta), _F32),
        ] + [pltpu.VMEM((tk, ta), _F32)] * 4
          + [pltpu.VMEM((1, ta), _F32)] * 4
          + [pltpu.VMEM((tk, ta), _BF16)] * 2
          + [pltpu.VMEM((1, ta), _F32)] * 2,
        compiler_params=cparams2,
        name="mla_attn",
    )(qT_all, k_all, vT_all)

    d_ff = mlp_w_up.shape[2]
    mlp_scratch = [
        pltpu.VMEM((tm, D), _F32),
        pltpu.VMEM((tm, d_ff), _BF16),
    ]
    x1 = pl.pallas_call(
        functools.partial(_outproj_mlp_kernel, alpha=alpha),
        grid=(T // tm,),
        in_specs=[
            pl.BlockSpec((tm, heads * vdim), lambda i: (i, 0)),
            pl.BlockSpec((tm, D), lambda i: (i, 0)),
            _const_spec((heads * vdim, D)),
            _const_spec((1, D)),
            _const_spec((1, D)),
            _const_spec((D, d_ff)),
            _const_spec((d_ff, D)),
            _const_spec((1, D)),
            _const_spec((1, D)),
        ],
        out_specs=pl.BlockSpec((tm, D), lambda i: (i, 0)),
        out_shape=jax.ShapeDtypeStruct((T, D), _F32),
        scratch_shapes=mlp_scratch,
        compiler_params=cparams1,
        name="mla_out_mlp",
    )(o.reshape(T, heads * vdim), x.reshape(T, D), mla_w_o[0].astype(_BF16),
      _row(ln_mix_g[0]), _row(ln_mix_b[0]), mlp_w_up[0].astype(_BF16),
      mlp_w_down[0].astype(_BF16), _row(ln_mlp_g[0]), _row(ln_mlp_b[0]))

    q_heads = swa_sinks.shape[1]
    s_hd = swa_w_q.shape[2] // q_heads
    kv_heads = kv_w_shared.shape[1] // (2 * s_hd)
    group = q_heads // kv_heads
    blk = _SWA_BLOCK
    assert _LANES % s_hd == 0 and tm % blk == 0
    wqT1 = swa_w_q[0].T.astype(_BF16)
    wk1 = kv_w_shared[:, :kv_heads * s_hd].astype(_BF16)
    wvT1 = kv_w_shared[:, kv_heads * s_hd:].T.astype(_BF16)
    bucketT = jnp.asarray(_t5_bucket_table(blk))
    n_units = (tm // blk) * kv_heads

    smem = pl.BlockSpec(memory_space=pltpu.SMEM)
    out = pl.pallas_call(
        functools.partial(_swa_mlp_kernel, alpha=alpha, q_heads=q_heads, kv_heads=kv_heads,
                          head_dim=s_hd, q_scale=s_hd ** -0.5 * _LOG2E),
        grid=(B, n_t),
        in_specs=[
            pl.BlockSpec((None, tm, D), lambda b, i: (b, i, 0)),
            _const_spec(wqT1.shape),
            _const_spec(wk1.shape),
            _const_spec(wvT1.shape),
            _const_spec((q_heads * s_hd, D)),
            _const_spec(bucketT.shape),
            smem,
            smem,
            _const_spec((1, D)),
            _const_spec((1, D)),
            _const_spec((D, d_ff)),
            _const_spec((d_ff, D)),
            _const_spec((1, D)),
            _const_spec((1, D)),
        ],
        out_specs=pl.BlockSpec((None, tm, D), lambda b, i: (b, i, 0)),
        out_shape=jax.ShapeDtypeStruct((B, S, D), _F32),
        scratch_shapes=[
            pltpu.VMEM((blk + tm, kv_heads * s_hd), _BF16),
            pltpu.VMEM((kv_heads * (s_hd + bf16_rows), blk + tm), _BF16),
            pltpu.VMEM((q_heads * s_hd, tm), _BF16),
            pltpu.VMEM((kv_heads, 2 * blk, group * blk), _F32),
            pltpu.VMEM((n_units, 2 * blk, group * blk), _F32),
            pltpu.VMEM((n_units, 2 * blk, group * blk), _BF16),
            pltpu.VMEM((n_units, 1, group * blk), _F32),
            pltpu.VMEM((q_heads * s_hd, tm), _BF16),
        ] + mlp_scratch,
        compiler_params=cparams2,
        name="swa_mlp",
    )(x1.reshape(B, S, D), wqT1, wk1, wvT1, swa_w_o[0].astype(_BF16), bucketT,
      rel_bias.astype(_F32), swa_sinks[0].astype(_F32),
      _row(ln_mix_g[1]), _row(ln_mix_b[1]), mlp_w_up[1].astype(_BF16),
      mlp_w_down[1].astype(_BF16), _row(ln_mlp_g[1]), _row(ln_mlp_b[1]))
    return out
```

```python
import functools
import math

import numpy as np
import jax
import jax.numpy as jnp
from jax import lax
from jax.experimental import pallas as pl
from jax.experimental.pallas import tpu as pltpu

_F32 = jnp.float32
_BF16 = jnp.bfloat16

_LN_EPS = 1e-5
_RMS_EPS = 1e-6
_ROPE_THETA = 10000.0
_SWA_BLOCK = 128
_REL_BUCKETS = 32
_REL_MAX_DIST = 128
_LOG2E = math.log2(math.e)

_V7X_VMEM_BYTES = 64 * 1024 * 1024
_VMEM_LIMIT_BYTES = _V7X_VMEM_BYTES - 8 * 1024 * 1024
_LANES = 128

_TOKEN_TILE = 512
_ATTN_TILE = 1024
_ATTN_ROW_BLOCK = 512
_ATTN_COL_BLOCK = 512
_FF_CHUNK = 1024
_MLP_ROW_BLOCK = 256

_NT = (((1,), (1,)), ((), ()))
_TN = (((0,), (0,)), ((), ()))


def _const_spec(shape):
    nd = len(shape)
    return pl.BlockSpec(shape, lambda *_: (0,) * nd, pipeline_mode=pl.Buffered(1))


def _layernorm(v, g, b):
    mu = jnp.mean(v, axis=-1, keepdims=True)
    d = v - mu
    var = jnp.mean(d * d, axis=-1, keepdims=True)
    return d * lax.rsqrt(var + _LN_EPS) * g + b


def _rmsnorm(v, g):
    return v * lax.rsqrt(jnp.mean(v * v, axis=-1, keepdims=True) + _RMS_EPS) * g


def _mix_mlp_ln(mix_rows, x_ref, alpha, g1_ref, b1_ref, wup_ref, wdown_ref, g2_ref, b2_ref,
                out_ref, x1_scr, h_scr):
    tm = x_ref.shape[0]
    d_ff = wup_ref.shape[1]
    blocks = [slice(r, r + _MLP_ROW_BLOCK) for r in range(0, tm, _MLP_ROW_BLOCK)]
    for rows in blocks:
        x1_scr[rows, :] = _layernorm(alpha * x_ref[rows, :] + mix_rows(rows), g1_ref[...], b1_ref[...])
    for rows in blocks:
        x1b = x1_scr[rows, :].astype(_BF16)
        for lo in range(0, d_ff, _FF_CHUNK):
            hmid = jnp.dot(x1b, wup_ref[:, lo:lo + _FF_CHUNK], preferred_element_type=_F32)
            hmid = jnp.maximum(hmid, 0.0)
            h_scr[rows, lo:lo + _FF_CHUNK] = (hmid * hmid).astype(_BF16)
    for rows in blocks:
        y = jnp.dot(h_scr[rows, :], wdown_ref[...], preferred_element_type=_F32)
        out_ref[rows, :] = _layernorm(alpha * x1_scr[rows, :] + y, g2_ref[...], b2_ref[...])


def _mla_proj_kernel(x_ref, win_ref, gq_ref, gkv_ref, wqT_ref, wuk_ref, wvT_ref,
                     cs_ref, cosT_ref, sinT_ref, qT_ref, k_ref, vT_ref,
                     *, q_rank, kv_rank, heads, nope, rope, q_scale):
    xb = x_ref[...].astype(_BF16)
    h = jnp.dot(xb, win_ref[...], preferred_element_type=_F32)
    cq = _rmsnorm(h[:, :q_rank], gq_ref[...]).astype(_BF16)
    ckv = _rmsnorm(h[:, q_rank:q_rank + kv_rank], gkv_ref[...]).astype(_BF16)
    t = h[:, q_rank + kv_rank:] * cs_ref[...]
    kr = (t + pltpu.roll(t, rope, 1))[:, :rope].astype(_BF16)

    half = rope // 2
    hd = nope + rope
    qT = lax.dot_general(wqT_ref[...], cq, _NT, preferred_element_type=_F32)
    cosT = cosT_ref[...]
    sinT = sinT_ref[...]
    for hh in range(heads):
        base = hh * hd
        qT_ref[hh, 0:nope, :] = (qT[base:base + nope] * q_scale).astype(_BF16)
        x1 = qT[base + nope:base + nope + half]
        x2 = qT[base + nope + half:base + hd]
        qT_ref[hh, nope:nope + half, :] = ((x1 * cosT - x2 * sinT) * q_scale).astype(_BF16)
        qT_ref[hh, nope + half:hd, :] = ((x1 * sinT + x2 * cosT) * q_scale).astype(_BF16)

    kn = jnp.dot(ckv, wuk_ref[...], preferred_element_type=_F32)
    for hh in range(heads):
        k_ref[hh, :, 0:nope] = kn[:, hh * nope:(hh + 1) * nope].astype(_BF16)
        k_ref[hh, :, nope:hd] = kr

    vdim = wvT_ref.shape[0] // heads
    vT = lax.dot_general(wvT_ref[...], ckv, _NT, preferred_element_type=_F32)
    tk = vT_ref.shape[3]
    pad = vT_ref.shape[2] - vdim
    ones_rows = (lax.broadcasted_iota(jnp.int32, (pad, tk), 0) == 0).astype(_BF16)
    for hh in range(heads):
        for j in range(vT_ref.shape[1]):
            vT_ref[hh, j, 0:vdim, :] = vT[hh * vdim:(hh + 1) * vdim, j * tk:(j + 1) * tk].astype(_BF16)
            vT_ref[hh, j, vdim:vdim + pad, :] = ones_rows


def _mla_attn_kernel(qT_ref, k_ref, vT_ref, o_ref, m_scr, acc_scr,
                     s00, s01, s10, s11, cm00, cm01, cm10, cm11, p_a, p_b, al_a, al_b):
    tk, tq = s00.shape
    tsub = qT_ref.shape[2]
    nsub = tq // tsub
    n_q = qT_ref.shape[0] // nsub
    rb = _ATTN_ROW_BLOCK
    assert tq == 2 * tk and tk == vT_ref.shape[2] and tk % rb == 0

    cb = _ATTN_COL_BLOCK

    def q_stage(qi, c, s_ref, cm_ref, col_lo=0):
        for r0 in range(0, tk, rb):
            k = k_ref[pl.ds(pl.multiple_of(c * tk + r0, rb), rb), :]
            for lo in range(col_lo, tq, cb):
                j, off = divmod(lo, tsub)
                qT = qT_ref[qi * nsub + j, :, off:off + cb]
                s = jnp.dot(k, qT, preferred_element_type=_F32)
                s_ref[r0:r0 + rb, lo:lo + cb] = s
                cm = jnp.max(s, axis=0, keepdims=True)
                if r0 > 0:
                    cm = jnp.maximum(cm, cm_ref[:, lo:lo + cb])
                cm_ref[:, lo:lo + cb] = cm

    def x_stage(s_ref, cm_ref, p_ref, al_ref, mask_off=None, col_lo=0):
        for lo in range(col_lo, tq, cb):
            cols = slice(lo, lo + cb)
            s = s_ref[:, cols]
            if mask_off is None:
                cm = cm_ref[:, cols]
            else:
                krel = lax.broadcasted_iota(jnp.int32, s.shape, 0) + mask_off
                qrel = lax.broadcasted_iota(jnp.int32, s.shape, 1) + lo
                s = jnp.where(krel <= qrel, s, -jnp.inf)
                cm = jnp.max(s, axis=0, keepdims=True)
            m_prev = m_scr[:, cols]
            m_new = jnp.maximum(m_prev, cm)
            alpha = jnp.exp2(m_prev - m_new)
            p = jnp.exp2(s - m_new)
            p_ref[:, cols] = p.astype(_BF16)
            al_ref[:, cols] = alpha
            m_scr[:, cols] = m_new

    def v_stage(c, p_ref, al_ref, col_lo=0):
        for lo in range(col_lo, tq, cb):
            cols = slice(lo, lo + cb)
            acc_scr[:, cols] = al_ref[:, cols] * acc_scr[:, cols] + jnp.dot(
                vT_ref[c], p_ref[:, cols], preferred_element_type=_F32)

    s_bufs = (((s00, cm00), (s01, cm01)), ((s10, cm10), (s11, cm11)))

    def start_tile():
        m_scr[...] = jnp.full(m_scr.shape, -jnp.inf, _F32)
        acc_scr[...] = jnp.zeros(acc_scr.shape, _F32)
        p_b[...] = jnp.zeros(p_b.shape, p_b.dtype)
        al_b[...] = jnp.ones(al_b.shape, _F32)

    def regular_step(qi, a, par):
        (sa, cma), (sb, cmb) = s_bufs[par]
        (na, ncma), (nb, ncmb) = s_bufs[1 - par]
        q_stage(qi, a + 2, na, ncma)
        x_stage(sa, cma, p_a, al_a)
        v_stage(jnp.maximum(a - 1, 0), p_b, al_b)
        q_stage(qi, a + 3, nb, ncmb)
        x_stage(sb, cmb, p_b, al_b)
        v_stage(a, p_a, al_a)

    def diagonal_step(qi, par):
        a = 2 * qi
        (sa, cma), (sb, cmb) = s_bufs[par]
        (na, ncma), (nb, ncmb) = s_bufs[1 - par]
        q_next = jnp.minimum(qi + 1, n_q - 1)
        q_stage(q_next, 0, na, ncma)
        x_stage(sa, cma, p_a, al_a, mask_off=0)
        v_stage(jnp.maximum(a - 1, 0), p_b, al_b)
        q_stage(q_next, 1, nb, ncmb)
        x_stage(sb, cmb, p_b, al_b, mask_off=tk, col_lo=tk)
        v_stage(a, p_a, al_a)
        v_stage(a + 1, p_b, al_b, col_lo=tk)
        vdim = o_ref.shape[1]
        inv_l = 1.0 / acc_scr[vdim:vdim + 1, :]
        o = (acc_scr[0:vdim, :] * inv_l).T
        o_ref[pl.ds(pl.multiple_of(qi * tq, tq), tq), :] = o.astype(o_ref.dtype)
        start_tile()

    def by_parity(par, fn):
        for static_par in range(2):
            @pl.when(par == static_par)
            def _():
                fn(static_par)

    def tile_body(qi, par):
        def step_body(u, par):
            by_parity(par, functools.partial(regular_step, qi, 2 * u))
            return 1 - par

        par = lax.fori_loop(0, qi, step_body, par)
        by_parity(par, functools.partial(diagonal_step, qi))
        return 1 - par

    start_tile()
    q_stage(0, 0, s00, cm00)
    q_stage(0, 1, s01, cm01)
    lax.fori_loop(0, n_q, tile_body, jnp.int32(0))


def _outproj_mlp_kernel(o_ref, x_ref, wo_ref, g1_ref, b1_ref, wup_ref, wdown_ref,
                        g2_ref, b2_ref, out_ref, x1_scr, h_scr, *, alpha):
    def mix_rows(rows):
        return jnp.dot(o_ref[rows, :], wo_ref[...], preferred_element_type=_F32)

    _mix_mlp_ln(mix_rows, x_ref, alpha, g1_ref, b1_ref, wup_ref, wdown_ref, g2_ref, b2_ref,
                out_ref, x1_scr, h_scr)


def _swa_mlp_kernel(x_ref, wqT_ref, wk_ref, wvT_ref, wo_ref, bucketT_ref, relb_ref, sink_ref,
                    g1_ref, b1_ref, wup_ref, wdown_ref, g2_ref, b2_ref, out_ref,
                    k_scr, vTe_scr, qT_scr, biasT_scr, s_scr, p_scr, m_scr, oT_scr, x1_scr, h_scr,
                    *, alpha, q_heads, kv_heads, head_dim, q_scale):
    blk = _SWA_BLOCK
    group = q_heads // kv_heads
    tm = x_ref.shape[0]
    n_blk = tm // blk
    vrows = vTe_scr.shape[0] // kv_heads
    first_tile = pl.program_id(1) == 0
    always = pl.program_id(1) >= 0

    @pl.when((pl.program_id(0) == 0) & first_tile)
    def _():
        bkt = bucketT_ref[...]
        for hq in range(q_heads):
            bias = jnp.full(bkt.shape, -jnp.inf, _F32)
            for bb in range(_REL_BUCKETS):
                bias = jnp.where(bkt == bb, relb_ref[bb, hq] * _LOG2E, bias)
            g, gi = divmod(hq, group)
            biasT_scr[g, :, gi * blk:(gi + 1) * blk] = bias

    @pl.when(first_tile)
    def _():
        k_scr[0:blk, :] = jnp.zeros((blk, k_scr.shape[1]), k_scr.dtype)
        vTe_scr[:, 0:blk] = jnp.zeros((vTe_scr.shape[0], blk), vTe_scr.dtype)

    xb = x_ref[...].astype(_BF16)
    qT_scr[...] = (lax.dot_general(wqT_ref[...], xb, _NT, preferred_element_type=_F32)
                   * q_scale).astype(_BF16)
    k_scr[blk:blk + tm, :] = jnp.dot(xb, wk_ref[...], preferred_element_type=_F32).astype(_BF16)
    vT = lax.dot_general(wvT_ref[...], xb, _NT, preferred_element_type=_F32).astype(_BF16)
    ones_rows = (lax.broadcasted_iota(jnp.int32, (vrows - head_dim, tm), 0) == 0).astype(_BF16)
    for g in range(kv_heads):
        vTe_scr[g * vrows:g * vrows + head_dim, blk:blk + tm] = vT[g * head_dim:(g + 1) * head_dim]
        vTe_scr[g * vrows + head_dim:(g + 1) * vrows, blk:blk + tm] = ones_rows

    first_pen = jnp.where(first_tile, -jnp.inf, 0.0).astype(_F32)
    zeros_q = jnp.zeros((head_dim, group * blk), _BF16)
    heads_per_lane_tile = _LANES // head_dim
    units = [(n, g) for n in range(n_blk) for g in range(kv_heads)]

    def sink_row(g):
        return jnp.concatenate(
            [jnp.full((1, blk), sink_ref[g * group + gi] * _LOG2E, _F32) for gi in range(group)],
            axis=1)

    @pl.when(always)
    def _():
        sinks = [sink_row(g) for g in range(kv_heads)]
        for u, (n, g) in enumerate(units):
            lt = g // heads_per_lane_tile
            kpair = k_scr[n * blk:(n + 2) * blk, lt * _LANES:(lt + 1) * _LANES]
            qg = jnp.concatenate(
                [qT_scr[(g * group + gi) * head_dim:(g * group + gi + 1) * head_dim,
                        n * blk:(n + 1) * blk] for gi in range(group)], axis=1)
            pieces = [zeros_q] * heads_per_lane_tile
            pieces[g % heads_per_lane_tile] = qg
            qpad = jnp.concatenate(pieces, axis=0)
            s = jnp.dot(kpair, qpad, preferred_element_type=_F32) + biasT_scr[g]
            if n == 0:
                s = jnp.concatenate([s[:blk] + first_pen, s[blk:]], axis=0)
            s_scr[u] = s
            m_scr[u] = jnp.maximum(jnp.max(s, axis=0, keepdims=True), sinks[g])

    @pl.when(always)
    def _():
        for u in range(len(units)):
            p_scr[u] = jnp.exp2(s_scr[u] - m_scr[u]).astype(_BF16)

    @pl.when(always)
    def _():
        sinks = [sink_row(g) for g in range(kv_heads)]
        for u, (n, g) in enumerate(units):
            oT = jnp.dot(vTe_scr[g * vrows:(g + 1) * vrows, n * blk:(n + 2) * blk], p_scr[u],
                         preferred_element_type=_F32)
            denom = oT[head_dim:head_dim + 1] + jnp.exp2(sinks[g] - m_scr[u])
            o = (oT[:head_dim] * (1.0 / denom)).astype(_BF16)
            for gi in range(group):
                hq = g * group + gi
                oT_scr[hq * head_dim:(hq + 1) * head_dim, n * blk:(n + 1) * blk] = (
                    o[:, gi * blk:(gi + 1) * blk])
        k_scr[0:blk, :] = k_scr[tm:tm + blk, :]
        vTe_scr[:, 0:blk] = vTe_scr[:, tm:tm + blk]

    def mix_rows(rows):
        return lax.dot_general(oT_scr[:, rows], wo_ref[...], _TN, preferred_element_type=_F32)

    _mix_mlp_ln(mix_rows, x_ref, alpha, g1_ref, b1_ref, wup_ref, wdown_ref, g2_ref, b2_ref,
                out_ref, x1_scr, h_scr)


def _t5_bucket_table(blk):
    i = np.arange(blk)[None, :]
    j = np.arange(2 * blk)[:, None]
    dist = i + blk - j
    max_exact = _REL_BUCKETS // 2
    nf = np.maximum(dist, 1).astype(np.float32)
    large = max_exact + (np.log(nf / np.float32(max_exact))
                         / np.float32(math.log(_REL_MAX_DIST / max_exact))
                         * np.float32(_REL_BUCKETS - max_exact)).astype(np.int32)
    large = np.minimum(large, _REL_BUCKETS - 1)
    bucket = np.where(dist < max_exact, np.maximum(dist, 0), large)
    valid = (dist >= 0) & (dist < blk)
    return np.where(valid, bucket, -1).astype(np.int32)


def _rope_tables(seq, rope):
    half = rope // 2
    inv = _ROPE_THETA ** (-jnp.arange(half, dtype=_F32) / half)
    ang = jnp.arange(seq).astype(_F32)[:, None] * inv[None, :]
    cos, sin = jnp.cos(ang), jnp.sin(ang)
    cs = jnp.concatenate([cos, cos, -sin, sin], axis=1)
    return cs, cos.T, sin.T


def _row(v):
    return v.reshape(1, -1).astype(_F32)


def kernel(x, mla_w_in, mla_g_q, mla_g_kv, mla_w_uq, mla_w_uk, mla_w_uv, mla_w_o, kv_w_shared,
           swa_w_q, swa_sinks, swa_w_o, rel_bias, mlp_w_up, mlp_w_down, ln_mix_g, ln_mix_b,
           ln_mlp_g, ln_mlp_b):
    B, S, D = x.shape
    depth = mlp_w_up.shape[0]
    assert depth == 2 and mla_w_in.shape[0] == 1 and swa_w_q.shape[0] == 1
    alpha = (2 * depth) ** 0.25
    T = B * S
    tm = _TOKEN_TILE
    ta = _ATTN_TILE
    tk = ta // 2
    bf16_rows = 16
    assert ta % tm == 0 and tm % tk == 0 and S % ta == 0

    q_rank, heads, hd = mla_w_uq.shape[1:]
    kv_rank, _, nope = mla_w_uk.shape[1:]
    vdim = mla_w_uv.shape[3]
    vrows = vdim + bf16_rows
    rope = hd - nope
    half = rope // 2
    assert 2 * rope == _LANES and mla_w_in.shape[2] == q_rank + kv_rank + rope

    w_in = mla_w_in[0]
    r0 = q_rank + kv_rank
    w_in_ext = jnp.concatenate(
        [w_in, w_in[:, r0 + half:r0 + rope], w_in[:, r0:r0 + half]], axis=1).astype(_BF16)
    wqT = mla_w_uq[0].reshape(q_rank, heads * hd).T.astype(_BF16)
    wuk = mla_w_uk[0].reshape(kv_rank, heads * nope).astype(_BF16)
    wvT = mla_w_uv[0].reshape(kv_rank, heads * vdim).T.astype(_BF16)
    cs, cosT, sinT = _rope_tables(S, rope)
    n_t = S // tm
    q_scale = hd ** -0.5 * _LOG2E

    cparams2 = pltpu.CompilerParams(dimension_semantics=("arbitrary", "arbitrary"),
                                    vmem_limit_bytes=_VMEM_LIMIT_BYTES)
    cparams1 = pltpu.CompilerParams(dimension_semantics=("arbitrary",),
                                    vmem_limit_bytes=_VMEM_LIMIT_BYTES)

    qT_all, k_all, vT_all = pl.pallas_call(
        functools.partial(_mla_proj_kernel, q_rank=q_rank, kv_rank=kv_rank, heads=heads,
                          nope=nope, rope=rope, q_scale=q_scale),
        grid=(B, n_t),
        in_specs=[
            pl.BlockSpec((None, tm, D), lambda b, i: (b, i, 0)),
            _const_spec(w_in_ext.shape),
            _const_spec((1, q_rank)),
            _const_spec((1, kv_rank)),
            _const_spec(wqT.shape),
            _const_spec(wuk.shape),
            _const_spec(wvT.shape),
            pl.BlockSpec((tm, 2 * rope), lambda b, i: (i, 0)),
            pl.BlockSpec((half, tm), lambda b, i: (0, i)),
            pl.BlockSpec((half, tm), lambda b, i: (0, i)),
        ],
        out_specs=[
            pl.BlockSpec((None, heads, None, hd, tm), lambda b, i: (b, 0, i, 0, 0)),
            pl.BlockSpec((None, heads, tm, hd), lambda b, i: (b, 0, i, 0)),
            pl.BlockSpec((None, heads, tm // tk, vrows, tk), lambda b, i: (b, 0, i, 0, 0)),
        ],
        out_shape=[
            jax.ShapeDtypeStruct((B, heads, n_t, hd, tm), _BF16),
            jax.ShapeDtypeStruct((B, heads, S, hd), _BF16),
            jax.ShapeDtypeStruct((B, heads, S // tk, vrows, tk), _BF16),
        ],
        compiler_params=cparams2,
        name="mla_proj",
    )(x, w_in_ext, _row(mla_g_q[0]), _row(mla_g_kv[0]), wqT, wuk, wvT, cs, cosT, sinT)

    o = pl.pallas_call(
        _mla_attn_kernel,
        grid=(B, heads),
        in_specs=[
            pl.BlockSpec((None, None, n_t, hd, tm), lambda b, h: (b, h, 0, 0, 0)),
            pl.BlockSpec((None, None, S, hd), lambda b, h: (b, h, 0, 0)),
            pl.BlockSpec((None, None, S // tk, vrows, tk), lambda b, h: (b, h, 0, 0, 0)),
        ],
        out_specs=pl.BlockSpec((None, S, vdim), lambda b, h: (b, 0, h)),
        out_shape=jax.ShapeDtypeStruct((B, S, heads * vdim), _BF16),
        scratch_shapes=[
            pltpu.VMEM((1, ta), _F32),
            pltpu.VMEM((vrows, ta), _F32),
        ] + [pltpu.VMEM((tk, ta), _F32)] * 4
          + [pltpu.VMEM((1, ta), _F32)] * 4
          + [pltpu.VMEM((tk, ta), _BF16)] * 2
          + [pltpu.VMEM((1, ta), _F32)] * 2,
        compiler_params=cparams2,
        name="mla_attn",
    )(qT_all, k_all, vT_all)

    d_ff = mlp_w_up.shape[2]
    mlp_scratch = [
        pltpu.VMEM((tm, D), _F32),
        pltpu.VMEM((tm, d_ff), _BF16),
    ]
    x1 = pl.pallas_call(
        functools.partial(_outproj_mlp_kernel, alpha=alpha),
        grid=(T // tm,),
        in_specs=[
            pl.BlockSpec((tm, heads * vdim), lambda i: (i, 0)),
            pl.BlockSpec((tm, D), lambda i: (i, 0)),
            _const_spec((heads * vdim, D)),
            _const_spec((1, D)),
            _const_spec((1, D)),
            _const_spec((D, d_ff)),
            _const_spec((d_ff, D)),
            _const_spec((1, D)),
            _const_spec((1, D)),
        ],
        out_specs=pl.BlockSpec((tm, D), lambda i: (i, 0)),
        out_shape=jax.ShapeDtypeStruct((T, D), _F32),
        scratch_shapes=mlp_scratch,
        compiler_params=cparams1,
        name="mla_out_mlp",
    )(o.reshape(T, heads * vdim), x.reshape(T, D), mla_w_o[0].astype(_BF16),
      _row(ln_mix_g[0]), _row(ln_mix_b[0]), mlp_w_up[0].astype(_BF16),
      mlp_w_down[0].astype(_BF16), _row(ln_mlp_g[0]), _row(ln_mlp_b[0]))

    q_heads = swa_sinks.shape[1]
    s_hd = swa_w_q.shape[2] // q_heads
    kv_heads = kv_w_shared.shape[1] // (2 * s_hd)
    group = q_heads // kv_heads
    blk = _SWA_BLOCK
    assert _LANES % s_hd == 0 and tm % blk == 0
    wqT1 = swa_w_q[0].T.astype(_BF16)
    wk1 = kv_w_shared[:, :kv_heads * s_hd].astype(_BF16)
    wvT1 = kv_w_shared[:, kv_heads * s_hd:].T.astype(_BF16)
    bucketT = jnp.asarray(_t5_bucket_table(blk))
    n_units = (tm // blk) * kv_heads

    smem = pl.BlockSpec(memory_space=pltpu.SMEM)
    out = pl.pallas_call(
        functools.partial(_swa_mlp_kernel, alpha=alpha, q_heads=q_heads, kv_heads=kv_heads,
                          head_dim=s_hd, q_scale=s_hd ** -0.5 * _LOG2E),
        grid=(B, n_t),
        in_specs=[
            pl.BlockSpec((None, tm, D), lambda b, i: (b, i, 0)),
            _const_spec(wqT1.shape),
            _const_spec(wk1.shape),
            _const_spec(wvT1.shape),
            _const_spec((q_heads * s_hd, D)),
            _const_spec(bucketT.shape),
            smem,
            smem,
            _const_spec((1, D)),
            _const_spec((1, D)),
            _const_spec((D, d_ff)),
            _const_spec((d_ff, D)),
            _const_spec((1, D)),
            _const_spec((1, D)),
        ],
        out_specs=pl.BlockSpec((None, tm, D), lambda b, i: (b, i, 0)),
        out_shape=jax.ShapeDtypeStruct((B, S, D), _F32),
        scratch_shapes=[
            pltpu.VMEM((blk + tm, kv_heads * s_hd), _BF16),
            pltpu.VMEM((kv_heads * (s_hd + bf16_rows), blk + tm), _BF16),
            pltpu.VMEM((q_heads * s_hd, tm), _BF16),
            pltpu.VMEM((kv_heads, 2 * blk, group * blk), _F32),
            pltpu.VMEM((n_units, 2 * blk, group * blk), _F32),
            pltpu.VMEM((n_units, 2 * blk, group * blk), _BF16),
            pltpu.VMEM((n_units, 1, group * blk), _F32),
            pltpu.VMEM((q_heads * s_hd, tm), _BF16),
        ] + mlp_scratch,
        compiler_params=cparams2,
        name="swa_mlp",
    )(x1.reshape(B, S, D), wqT1, wk1, wvT1, swa_w_o[0].astype(_BF16), bucketT,
      rel_bias.astype(_F32), swa_sinks[0].astype(_F32),
      _row(ln_mix_g[1]), _row(ln_mix_b[1]), mlp_w_up[1].astype(_BF16),
      mlp_w_down[1].astype(_BF16), _row(ln_mlp_g[1]), _row(ln_mlp_b[1]))
    return out
```

```python
import functools
import math

import numpy as np
import jax
import jax.numpy as jnp
from jax import lax
from jax.experimental import pallas as pl
from jax.experimental.pallas import tpu as pltpu

_F32 = jnp.float32
_BF16 = jnp.bfloat16

_LN_EPS = 1e-5
_RMS_EPS = 1e-6
_ROPE_THETA = 10000.0
_SWA_BLOCK = 128
_REL_BUCKETS = 32
_REL_MAX_DIST = 128
_LOG2E = math.log2(math.e)

_V7X_VMEM_BYTES = 64 * 1024 * 1024
_VMEM_LIMIT_BYTES = _V7X_VMEM_BYTES - 8 * 1024 * 1024
_LANES = 128

_TOKEN_TILE = 512
_ATTN_TILE = 1024
_ATTN_ROW_BLOCK = 512
_ATTN_COL_BLOCK = 256
_FF_CHUNK = 1024
_MLP_ROW_BLOCK = 256

_NT = (((1,), (1,)), ((), ()))
_TN = (((0,), (0,)), ((), ()))


def _const_spec(shape):
    nd = len(shape)
    return pl.BlockSpec(shape, lambda *_: (0,) * nd, pipeline_mode=pl.Buffered(1))


def _layernorm(v, g, b):
    mu = jnp.mean(v, axis=-1, keepdims=True)
    d = v - mu
    var = jnp.mean(d * d, axis=-1, keepdims=True)
    return d * lax.rsqrt(var + _LN_EPS) * g + b


def _rmsnorm(v, g):
    return v * lax.rsqrt(jnp.mean(v * v, axis=-1, keepdims=True) + _RMS_EPS) * g


def _mix_mlp_ln(mix_rows, x_ref, alpha, g1_ref, b1_ref, wup_ref, wdown_ref, g2_ref, b2_ref,
                out_ref, x1_scr, h_scr):
    tm = x_ref.shape[0]
    d_ff = wup_ref.shape[1]
    blocks = [slice(r, r + _MLP_ROW_BLOCK) for r in range(0, tm, _MLP_ROW_BLOCK)]
    for rows in blocks:
        x1_scr[rows, :] = _layernorm(alpha * x_ref[rows, :] + mix_rows(rows), g1_ref[...], b1_ref[...])
    for rows in blocks:
        x1b = x1_scr[rows, :].astype(_BF16)
        for lo in range(0, d_ff, _FF_CHUNK):
            hmid = jnp.dot(x1b, wup_ref[:, lo:lo + _FF_CHUNK], preferred_element_type=_F32)
            hmid = jnp.maximum(hmid, 0.0)
            h_scr[rows, lo:lo + _FF_CHUNK] = (hmid * hmid).astype(_BF16)
    for rows in blocks:
        y = jnp.dot(h_scr[rows, :], wdown_ref[...], preferred_element_type=_F32)
        out_ref[rows, :] = _layernorm(alpha * x1_scr[rows, :] + y, g2_ref[...], b2_ref[...])


def _mla_proj_kernel(x_ref, win_ref, gq_ref, gkv_ref, wqT_ref, wuk_ref, wvT_ref,
                     cs_ref, cosT_ref, sinT_ref, qT_ref, k_ref, vT_ref,
                     *, q_rank, kv_rank, heads, nope, rope, q_scale):
    xb = x_ref[...].astype(_BF16)
    h = jnp.dot(xb, win_ref[...], preferred_element_type=_F32)
    cq = _rmsnorm(h[:, :q_rank], gq_ref[...]).astype(_BF16)
    ckv = _rmsnorm(h[:, q_rank:q_rank + kv_rank], gkv_ref[...]).astype(_BF16)
    t = h[:, q_rank + kv_rank:] * cs_ref[...]
    kr = (t + pltpu.roll(t, rope, 1))[:, :rope].astype(_BF16)

    half = rope // 2
    hd = nope + rope
    qT = lax.dot_general(wqT_ref[...], cq, _NT, preferred_element_type=_F32)
    cosT = cosT_ref[...]
    sinT = sinT_ref[...]
    for hh in range(heads):
        base = hh * hd
        qT_ref[hh, 0:nope, :] = (qT[base:base + nope] * q_scale).astype(_BF16)
        x1 = qT[base + nope:base + nope + half]
        x2 = qT[base + nope + half:base + hd]
        qT_ref[hh, nope:nope + half, :] = ((x1 * cosT - x2 * sinT) * q_scale).astype(_BF16)
        qT_ref[hh, nope + half:hd, :] = ((x1 * sinT + x2 * cosT) * q_scale).astype(_BF16)

    kn = jnp.dot(ckv, wuk_ref[...], preferred_element_type=_F32)
    for hh in range(heads):
        k_ref[hh, :, 0:nope] = kn[:, hh * nope:(hh + 1) * nope].astype(_BF16)
        k_ref[hh, :, nope:hd] = kr

    vdim = wvT_ref.shape[0] // heads
    vT = lax.dot_general(wvT_ref[...], ckv, _NT, preferred_element_type=_F32)
    tk = vT_ref.shape[3]
    pad = vT_ref.shape[2] - vdim
    ones_rows = (lax.broadcasted_iota(jnp.int32, (pad, tk), 0) == 0).astype(_BF16)
    for hh in range(heads):
        for j in range(vT_ref.shape[1]):
            vT_ref[hh, j, 0:vdim, :] = vT[hh * vdim:(hh + 1) * vdim, j * tk:(j + 1) * tk].astype(_BF16)
            vT_ref[hh, j, vdim:vdim + pad, :] = ones_rows


def _mla_attn_kernel(qT_ref, k_ref, vT_ref, o_ref, m_scr, acc_scr,
                     s00, s01, s10, s11, cm00, cm01, cm10, cm11, p_a, p_b, al_a, al_b):
    tk = vT_ref.shape[2]
    tq = 2 * tk
    tsub = qT_ref.shape[2]
    nsub = tq // tsub
    n_q = qT_ref.shape[0] // nsub
    rb = _ATTN_ROW_BLOCK
    assert s00.shape[0] == tk and s00.shape[1] >= tq and tk % rb == 0

    cb = _ATTN_COL_BLOCK

    def q_stage(qi, c, s_ref, cm_ref, col_lo=0):
        for r0 in range(0, tk, rb):
            k = k_ref[pl.ds(pl.multiple_of(c * tk + r0, rb), rb), :]
            for lo in range(col_lo, tq, cb):
                j, off = divmod(lo, tsub)
                qT = qT_ref[qi * nsub + j, :, off:off + cb]
                s = jnp.dot(k, qT, preferred_element_type=_F32)
                s_ref[r0:r0 + rb, lo:lo + cb] = s
                cm = jnp.max(s, axis=0, keepdims=True)
                if r0 > 0:
                    cm = jnp.maximum(cm, cm_ref[:, lo:lo + cb])
                cm_ref[:, lo:lo + cb] = cm

    def x_stage(s_ref, cm_ref, p_ref, al_ref, mask_off=None, col_lo=0):
        for lo in range(col_lo, tq, cb):
            cols = slice(lo, lo + cb)
            s = s_ref[:, cols]
            if mask_off is None:
                cm = cm_ref[:, cols]
            else:
                krel = lax.broadcasted_iota(jnp.int32, s.shape, 0) + mask_off
                qrel = lax.broadcasted_iota(jnp.int32, s.shape, 1) + lo
                s = jnp.where(krel <= qrel, s, -jnp.inf)
                cm = jnp.max(s, axis=0, keepdims=True)
            m_prev = m_scr[:, cols]
            m_new = jnp.maximum(m_prev, cm)
            alpha = jnp.exp2(m_prev - m_new)
            p = jnp.exp2(s - m_new)
            p_ref[:, cols] = p.astype(_BF16)
            al_ref[:, cols] = alpha
            m_scr[:, cols] = m_new

    def v_stage(c, p_ref, al_ref, col_lo=0):
        for lo in range(col_lo, tq, cb):
            cols = slice(lo, lo + cb)
            acc_scr[:, cols] = al_ref[:, cols] * acc_scr[:, cols] + jnp.dot(
                vT_ref[c], p_ref[:, cols], preferred_element_type=_F32)

    s_bufs = (((s00, cm00), (s01, cm01)), ((s10, cm10), (s11, cm11)))

    def start_tile():
        m_scr[...] = jnp.full(m_scr.shape, -jnp.inf, _F32)
        acc_scr[...] = jnp.zeros(acc_scr.shape, _F32)
        p_b[...] = jnp.zeros(p_b.shape, p_b.dtype)
        al_b[...] = jnp.ones(al_b.shape, _F32)

    def regular_step(qi, a, par):
        (sa, cma), (sb, cmb) = s_bufs[par]
        (na, ncma), (nb, ncmb) = s_bufs[1 - par]
        q_stage(qi, a + 2, na, ncma)
        x_stage(sa, cma, p_a, al_a)
        v_stage(jnp.maximum(a - 1, 0), p_b, al_b)
        q_stage(qi, a + 3, nb, ncmb)
        x_stage(sb, cmb, p_b, al_b)
        v_stage(a, p_a, al_a)

    def diagonal_step(qi, par):
        a = 2 * qi
        (sa, cma), (sb, cmb) = s_bufs[par]
        (na, ncma), (nb, ncmb) = s_bufs[1 - par]
        q_next = jnp.minimum(qi + 1, n_q - 1)
        q_stage(q_next, 0, na, ncma)
        x_stage(sa, cma, p_a, al_a, mask_off=0)
        v_stage(jnp.maximum(a - 1, 0), p_b, al_b)
        q_stage(q_next, 1, nb, ncmb)
        x_stage(sb, cmb, p_b, al_b, mask_off=tk, col_lo=tk)
        v_stage(a, p_a, al_a)
        v_stage(a + 1, p_b, al_b, col_lo=tk)
        vdim = o_ref.shape[1]
        inv_l = 1.0 / acc_scr[vdim:vdim + 1, 0:tq]
        o = (acc_scr[0:vdim, 0:tq] * inv_l).T
        o_ref[pl.ds(pl.multiple_of(qi * tq, tq), tq), :] = o.astype(o_ref.dtype)
        start_tile()

    def by_parity(par, fn):
        for static_par in range(2):
            @pl.when(par == static_par)
            def _():
                fn(static_par)

    def tile_body(qi, par):
        def step_body(u, par):
            by_parity(par, functools.partial(regular_step, qi, 2 * u))
            return 1 - par

        par = lax.fori_loop(0, qi, step_body, par)
        by_parity(par, functools.partial(diagonal_step, qi))
        return 1 - par

    start_tile()
    q_stage(0, 0, s00, cm00)
    q_stage(0, 1, s01, cm01)
    lax.fori_loop(0, n_q, tile_body, jnp.int32(0))


def _outproj_mlp_kernel(o_ref, x_ref, wo_ref, g1_ref, b1_ref, wup_ref, wdown_ref,
                        g2_ref, b2_ref, out_ref, x1_scr, h_scr, *, alpha):
    def mix_rows(rows):
        return jnp.dot(o_ref[rows, :], wo_ref[...], preferred_element_type=_F32)

    _mix_mlp_ln(mix_rows, x_ref, alpha, g1_ref, b1_ref, wup_ref, wdown_ref, g2_ref, b2_ref,
                out_ref, x1_scr, h_scr)


def _swa_mlp_kernel(x_ref, wqT_ref, wk_ref, wvT_ref, wo_ref, bucketT_ref, relb_ref, sink_ref,
                    g1_ref, b1_ref, wup_ref, wdown_ref, g2_ref, b2_ref, out_ref,
                    k_scr, vTe_scr, qT_scr, biasT_scr, s_scr, p_scr, m_scr, oT_scr, x1_scr, h_scr,
                    *, alpha, q_heads, kv_heads, head_dim, q_scale):
    blk = _SWA_BLOCK
    group = q_heads // kv_heads
    tm = x_ref.shape[0]
    n_blk = tm // blk
    vrows = vTe_scr.shape[0] // kv_heads
    first_tile = pl.program_id(1) == 0
    always = pl.program_id(1) >= 0

    @pl.when((pl.program_id(0) == 0) & first_tile)
    def _():
        bkt = bucketT_ref[...]
        for hq in range(q_heads):
            bias = jnp.full(bkt.shape, -jnp.inf, _F32)
            for bb in range(_REL_BUCKETS):
                bias = jnp.where(bkt == bb, relb_ref[bb, hq] * _LOG2E, bias)
            g, gi = divmod(hq, group)
            biasT_scr[g, :, gi * blk:(gi + 1) * blk] = bias

    @pl.when(first_tile)
    def _():
        k_scr[0:blk, :] = jnp.zeros((blk, k_scr.shape[1]), k_scr.dtype)
        vTe_scr[:, 0:blk] = jnp.zeros((vTe_scr.shape[0], blk), vTe_scr.dtype)

    xb = x_ref[...].astype(_BF16)
    qT_scr[...] = (lax.dot_general(wqT_ref[...], xb, _NT, preferred_element_type=_F32)
                   * q_scale).astype(_BF16)
    k_scr[blk:blk + tm, :] = jnp.dot(xb, wk_ref[...], preferred_element_type=_F32).astype(_BF16)
    vT = lax.dot_general(wvT_ref[...], xb, _NT, preferred_element_type=_F32).astype(_BF16)
    ones_rows = (lax.broadcasted_iota(jnp.int32, (vrows - head_dim, tm), 0) == 0).astype(_BF16)
    for g in range(kv_heads):
        vTe_scr[g * vrows:g * vrows + head_dim, blk:blk + tm] = vT[g * head_dim:(g + 1) * head_dim]
        vTe_scr[g * vrows + head_dim:(g + 1) * vrows, blk:blk + tm] = ones_rows

    first_pen = jnp.where(first_tile, -jnp.inf, 0.0).astype(_F32)
    zeros_q = jnp.zeros((head_dim, group * blk), _BF16)
    heads_per_lane_tile = _LANES // head_dim
    units = [(n, g) for n in range(n_blk) for g in range(kv_heads)]

    def sink_row(g):
        return jnp.concatenate(
            [jnp.full((1, blk), sink_ref[g * group + gi] * _LOG2E, _F32) for gi in range(group)],
            axis=1)

    @pl.when(always)
    def _():
        sinks = [sink_row(g) for g in range(kv_heads)]
        for u, (n, g) in enumerate(units):
            lt = g // heads_per_lane_tile
            kpair = k_scr[n * blk:(n + 2) * blk, lt * _LANES:(lt + 1) * _LANES]
            qg = jnp.concatenate(
                [qT_scr[(g * group + gi) * head_dim:(g * group + gi + 1) * head_dim,
                        n * blk:(n + 1) * blk] for gi in range(group)], axis=1)
            pieces = [zeros_q] * heads_per_lane_tile
            pieces[g % heads_per_lane_tile] = qg
            qpad = jnp.concatenate(pieces, axis=0)
            s = jnp.dot(kpair, qpad, preferred_element_type=_F32) + biasT_scr[g]
            if n == 0:
                s = jnp.concatenate([s[:blk] + first_pen, s[blk:]], axis=0)
            s_scr[u] = s
            m_scr[u] = jnp.maximum(jnp.max(s, axis=0, keepdims=True), sinks[g])

    @pl.when(always)
    def _():
        for u in range(len(units)):
            p_scr[u] = jnp.exp2(s_scr[u] - m_scr[u]).astype(_BF16)

    @pl.when(always)
    def _():
        sinks = [sink_row(g) for g in range(kv_heads)]
        for u, (n, g) in enumerate(units):
            oT = jnp.dot(vTe_scr[g * vrows:(g + 1) * vrows, n * blk:(n + 2) * blk], p_scr[u],
                         preferred_element_type=_F32)
            denom = oT[head_dim:head_dim + 1] + jnp.exp2(sinks[g] - m_scr[u])
            o = (oT[:head_dim] * (1.0 / denom)).astype(_BF16)
            for gi in range(group):
                hq = g * group + gi
                oT_scr[hq * head_dim:(hq + 1) * head_dim, n * blk:(n + 1) * blk] = (
                    o[:, gi * blk:(gi + 1) * blk])
        k_scr[0:blk, :] = k_scr[tm:tm + blk, :]
        vTe_scr[:, 0:blk] = vTe_scr[:, tm:tm + blk]

    def mix_rows(rows):
        return lax.dot_general(oT_scr[:, rows], wo_ref[...], _TN, preferred_element_type=_F32)

    _mix_mlp_ln(mix_rows, x_ref, alpha, g1_ref, b1_ref, wup_ref, wdown_ref, g2_ref, b2_ref,
                out_ref, x1_scr, h_scr)


def _t5_bucket_table(blk):
    i = np.arange(blk)[None, :]
    j = np.arange(2 * blk)[:, None]
    dist = i + blk - j
    max_exact = _REL_BUCKETS // 2
    nf = np.maximum(dist, 1).astype(np.float32)
    large = max_exact + (np.log(nf / np.float32(max_exact))
                         / np.float32(math.log(_REL_MAX_DIST / max_exact))
                         * np.float32(_REL_BUCKETS - max_exact)).astype(np.int32)
    large = np.minimum(large, _REL_BUCKETS - 1)
    bucket = np.where(dist < max_exact, np.maximum(dist, 0), large)
    valid = (dist >= 0) & (dist < blk)
    return np.where(valid, bucket, -1).astype(np.int32)


def _rope_tables(seq, rope):
    half = rope // 2
    inv = _ROPE_THETA ** (-jnp.arange(half, dtype=_F32) / half)
    ang = jnp.arange(seq).astype(_F32)[:, None] * inv[None, :]
    cos, sin = jnp.cos(ang), jnp.sin(ang)
    cs = jnp.concatenate([cos, cos, -sin, sin], axis=1)
    return cs, cos.T, sin.T


def _row(v):
    return v.reshape(1, -1).astype(_F32)


def kernel(x, mla_w_in, mla_g_q, mla_g_kv, mla_w_uq, mla_w_uk, mla_w_uv, mla_w_o, kv_w_shared,
           swa_w_q, swa_sinks, swa_w_o, rel_bias, mlp_w_up, mlp_w_down, ln_mix_g, ln_mix_b,
           ln_mlp_g, ln_mlp_b):
    B, S, D = x.shape
    depth = mlp_w_up.shape[0]
    assert depth == 2 and mla_w_in.shape[0] == 1 and swa_w_q.shape[0] == 1
    alpha = (2 * depth) ** 0.25
    T = B * S
    tm = _TOKEN_TILE
    ta = _ATTN_TILE
    tk = ta // 2
    bf16_rows = 16
    ta_pad = ta + _LANES
    assert ta % tm == 0 and tm % tk == 0 and S % ta == 0

    q_rank, heads, hd = mla_w_uq.shape[1:]
    kv_rank, _, nope = mla_w_uk.shape[1:]
    vdim = mla_w_uv.shape[3]
    vrows = vdim + bf16_rows
    rope = hd - nope
    half = rope // 2
    assert 2 * rope == _LANES and mla_w_in.shape[2] == q_rank + kv_rank + rope

    w_in = mla_w_in[0]
    r0 = q_rank + kv_rank
    w_in_ext = jnp.concatenate(
        [w_in, w_in[:, r0 + half:r0 + rope], w_in[:, r0:r0 + half]], axis=1).astype(_BF16)
    wqT = mla_w_uq[0].reshape(q_rank, heads * hd).T.astype(_BF16)
    wuk = mla_w_uk[0].reshape(kv_rank, heads * nope).astype(_BF16)
    wvT = mla_w_uv[0].reshape(kv_rank, heads * vdim).T.astype(_BF16)
    cs, cosT, sinT = _rope_tables(S, rope)
    n_t = S // tm
    q_scale = hd ** -0.5 * _LOG2E

    cparams2 = pltpu.CompilerParams(dimension_semantics=("arbitrary", "arbitrary"),
                                    vmem_limit_bytes=_VMEM_LIMIT_BYTES)
    cparams1 = pltpu.CompilerParams(dimension_semantics=("arbitrary",),
                                    vmem_limit_bytes=_VMEM_LIMIT_BYTES)

    qT_all, k_all, vT_all = pl.pallas_call(
        functools.partial(_mla_proj_kernel, q_rank=q_rank, kv_rank=kv_rank, heads=heads,
                          nope=nope, rope=rope, q_scale=q_scale),
        grid=(B, n_t),
        in_specs=[
            pl.BlockSpec((None, tm, D), lambda b, i: (b, i, 0)),
            _const_spec(w_in_ext.shape),
            _const_spec((1, q_rank)),
            _const_spec((1, kv_rank)),
            _const_spec(wqT.shape),
            _const_spec(wuk.shape),
            _const_spec(wvT.shape),
            pl.BlockSpec((tm, 2 * rope), lambda b, i: (i, 0)),
            pl.BlockSpec((half, tm), lambda b, i: (0, i)),
            pl.BlockSpec((half, tm), lambda b, i: (0, i)),
        ],
        out_specs=[
            pl.BlockSpec((None, heads, None, hd, tm), lambda b, i: (b, 0, i, 0, 0)),
            pl.BlockSpec((None, heads, tm, hd), lambda b, i: (b, 0, i, 0)),
            pl.BlockSpec((None, heads, tm // tk, vrows, tk), lambda b, i: (b, 0, i, 0, 0)),
        ],
        out_shape=[
            jax.ShapeDtypeStruct((B, heads, n_t, hd, tm), _BF16),
            jax.ShapeDtypeStruct((B, heads, S, hd), _BF16),
            jax.ShapeDtypeStruct((B, heads, S // tk, vrows, tk), _BF16),
        ],
        compiler_params=cparams2,
        name="mla_proj",
    )(x, w_in_ext, _row(mla_g_q[0]), _row(mla_g_kv[0]), wqT, wuk, wvT, cs, cosT, sinT)

    o = pl.pallas_call(
        _mla_attn_kernel,
        grid=(B, heads),
        in_specs=[
            pl.BlockSpec((None, None, n_t, hd, tm), lambda b, h: (b, h, 0, 0, 0)),
            pl.BlockSpec((None, None, S, hd), lambda b, h: (b, h, 0, 0)),
            pl.BlockSpec((None, None, S // tk, vrows, tk), lambda b, h: (b, h, 0, 0, 0)),
        ],
        out_specs=pl.BlockSpec((None, S, vdim), lambda b, h: (b, 0, h)),
        out_shape=jax.ShapeDtypeStruct((B, S, heads * vdim), _BF16),
        scratch_shapes=[
            pltpu.VMEM((1, ta), _F32),
            pltpu.VMEM((vrows, ta_pad), _F32),
        ] + [pltpu.VMEM((tk, ta_pad), _F32)] * 4
          + [pltpu.VMEM((1, ta), _F32)] * 4
          + [pltpu.VMEM((tk, ta_pad), _BF16)] * 2
          + [pltpu.VMEM((1, ta), _F32)] * 2,
        compiler_params=cparams2,
        name="mla_attn",
    )(qT_all, k_all, vT_all)

    d_ff = mlp_w_up.shape[2]
    mlp_scratch = [
        pltpu.VMEM((tm, D), _F32),
        pltpu.VMEM((tm, d_ff), _BF16),
    ]
    x1 = pl.pallas_call(
        functools.partial(_outproj_mlp_kernel, alpha=alpha),
        grid=(T // tm,),
        in_specs=[
            pl.BlockSpec((tm, heads * vdim), lambda i: (i, 0)),
            pl.BlockSpec((tm, D), lambda i: (i, 0)),
            _const_spec((heads * vdim, D)),
            _const_spec((1, D)),
            _const_spec((1, D)),
            _const_spec((D, d_ff)),
            _const_spec((d_ff, D)),
            _const_spec((1, D)),
            _const_spec((1, D)),
        ],
        out_specs=pl.BlockSpec((tm, D), lambda i: (i, 0)),
        out_shape=jax.ShapeDtypeStruct((T, D), _F32),
        scratch_shapes=mlp_scratch,
        compiler_params=cparams1,
        name="mla_out_mlp",
    )(o.reshape(T, heads * vdim), x.reshape(T, D), mla_w_o[0].astype(_BF16),
      _row(ln_mix_g[0]), _row(ln_mix_b[0]), mlp_w_up[0].astype(_BF16),
      mlp_w_down[0].astype(_BF16), _row(ln_mlp_g[0]), _row(ln_mlp_b[0]))

    q_heads = swa_sinks.shape[1]
    s_hd = swa_w_q.shape[2] // q_heads
    kv_heads = kv_w_shared.shape[1] // (2 * s_hd)
    group = q_heads // kv_heads
    blk = _SWA_BLOCK
    assert _LANES % s_hd == 0 and tm % blk == 0
    wqT1 = swa_w_q[0].T.astype(_BF16)
    wk1 = kv_w_shared[:, :kv_heads * s_hd].astype(_BF16)
    wvT1 = kv_w_shared[:, kv_heads * s_hd:].T.astype(_BF16)
    bucketT = jnp.asarray(_t5_bucket_table(blk))
    n_units = (tm // blk) * kv_heads

    smem = pl.BlockSpec(memory_space=pltpu.SMEM)
    out = pl.pallas_call(
        functools.partial(_swa_mlp_kernel, alpha=alpha, q_heads=q_heads, kv_heads=kv_heads,
                          head_dim=s_hd, q_scale=s_hd ** -0.5 * _LOG2E),
        grid=(B, n_t),
        in_specs=[
            pl.BlockSpec((None, tm, D), lambda b, i: (b, i, 0)),
            _const_spec(wqT1.shape),
            _const_spec(wk1.shape),
            _const_spec(wvT1.shape),
            _const_spec((q_heads * s_hd, D)),
            _const_spec(bucketT.shape),
            smem,
            smem,
            _const_spec((1, D)),
            _const_spec((1, D)),
            _const_spec((D, d_ff)),
            _const_spec((d_ff, D)),
            _const_spec((1, D)),
            _const_spec((1, D)),
        ],
        out_specs=pl.BlockSpec((None, tm, D), lambda b, i: (b, i, 0)),
        out_shape=jax.ShapeDtypeStruct((B, S, D), _F32),
        scratch_shapes=[
            pltpu.VMEM((blk + tm, kv_heads * s_hd), _BF16),
            pltpu.VMEM((kv_heads * (s_hd + bf16_rows), blk + tm), _BF16),
            pltpu.VMEM((q_heads * s_hd, tm), _BF16),
            pltpu.VMEM((kv_heads, 2 * blk, group * blk), _F32),
            pltpu.VMEM((n_units, 2 * blk, group * blk), _F32),
            pltpu.VMEM((n_units, 2 * blk, group * blk), _BF16),
            pltpu.VMEM((n_units, 1, group * blk), _F32),
            pltpu.VMEM((q_heads * s_hd, tm), _BF16),
        ] + mlp_scratch,
        compiler_params=cparams2,
        name="swa_mlp",
    )(x1.reshape(B, S, D), wqT1, wk1, wvT1, swa_w_o[0].astype(_BF16), bucketT,
      rel_bias.astype(_F32), swa_sinks[0].astype(_F32),
      _row(ln_mix_g[1]), _row(ln_mix_b[1]), mlp_w_up[1].astype(_BF16),
      mlp_w_down[1].astype(_BF16), _row(ln_mlp_g[1]), _row(ln_mlp_b[1]))
    return out
```

```python
import functools
import math

import numpy as np
import jax
import jax.numpy as jnp
from jax import lax
from jax.experimental import pallas as pl
from jax.experimental.pallas import tpu as pltpu

_F32 = jnp.float32
_BF16 = jnp.bfloat16

_LN_EPS = 1e-5
_RMS_EPS = 1e-6
_ROPE_THETA = 10000.0
_SWA_BLOCK = 128
_REL_BUCKETS = 32
_REL_MAX_DIST = 128
_LOG2E = math.log2(math.e)

_V7X_VMEM_BYTES = 64 * 1024 * 1024
_VMEM_LIMIT_BYTES = _V7X_VMEM_BYTES - 8 * 1024 * 1024
_LANES = 128

_TOKEN_TILE = 512
_ATTN_TILE = 1024
_ATTN_ROW_BLOCK = 512
_ATTN_COL_BLOCK = 256
_FF_CHUNK = 1024
_MLP_ROW_BLOCK = 256

_NT = (((1,), (1,)), ((), ()))
_TN = (((0,), (0,)), ((), ()))


def _const_spec(shape):
    nd = len(shape)
    return pl.BlockSpec(shape, lambda *_: (0,) * nd, pipeline_mode=pl.Buffered(1))


def _layernorm(v, g, b):
    mu = jnp.mean(v, axis=-1, keepdims=True)
    d = v - mu
    var = jnp.mean(d * d, axis=-1, keepdims=True)
    return d * lax.rsqrt(var + _LN_EPS) * g + b


def _rmsnorm(v, g):
    return v * lax.rsqrt(jnp.mean(v * v, axis=-1, keepdims=True) + _RMS_EPS) * g


def _mix_mlp_ln(mix_rows, x_ref, alpha, g1_ref, b1_ref, wup_ref, wdown_ref, g2_ref, b2_ref,
                out_ref, x1_scr, h_scr):
    tm = x_ref.shape[0]
    d_ff = wup_ref.shape[1]
    blocks = [slice(r, r + _MLP_ROW_BLOCK) for r in range(0, tm, _MLP_ROW_BLOCK)]
    for rows in blocks:
        x1_scr[rows, :] = _layernorm(alpha * x_ref[rows, :] + mix_rows(rows), g1_ref[...], b1_ref[...])
    for rows in blocks:
        x1b = x1_scr[rows, :].astype(_BF16)
        for lo in range(0, d_ff, _FF_CHUNK):
            hmid = jnp.dot(x1b, wup_ref[:, lo:lo + _FF_CHUNK], preferred_element_type=_F32)
            hmid = jnp.maximum(hmid, 0.0)
            h_scr[rows, lo:lo + _FF_CHUNK] = (hmid * hmid).astype(_BF16)
    for rows in blocks:
        y = jnp.dot(h_scr[rows, :], wdown_ref[...], preferred_element_type=_F32)
        out_ref[rows, :] = _layernorm(alpha * x1_scr[rows, :] + y, g2_ref[...], b2_ref[...])


def _mla_proj_kernel(x_ref, win_ref, gq_ref, gkv_ref, wqT_ref, wuk_ref, wvT_ref,
                     cs_ref, cosT_ref, sinT_ref, qT_ref, k_ref, vT_ref,
                     *, q_rank, kv_rank, heads, nope, rope, q_scale):
    xb = x_ref[...].astype(_BF16)
    h = jnp.dot(xb, win_ref[...], preferred_element_type=_F32)
    cq = _rmsnorm(h[:, :q_rank], gq_ref[...]).astype(_BF16)
    ckv = _rmsnorm(h[:, q_rank:q_rank + kv_rank], gkv_ref[...]).astype(_BF16)
    t = h[:, q_rank + kv_rank:] * cs_ref[...]
    kr = (t + pltpu.roll(t, rope, 1))[:, :rope].astype(_BF16)

    half = rope // 2
    hd = nope + rope
    qT = lax.dot_general(wqT_ref[...], cq, _NT, preferred_element_type=_F32)
    cosT = cosT_ref[...]
    sinT = sinT_ref[...]
    for hh in range(heads):
        base = hh * hd
        qT_ref[hh, 0:nope, :] = (qT[base:base + nope] * q_scale).astype(_BF16)
        x1 = qT[base + nope:base + nope + half]
        x2 = qT[base + nope + half:base + hd]
        qT_ref[hh, nope:nope + half, :] = ((x1 * cosT - x2 * sinT) * q_scale).astype(_BF16)
        qT_ref[hh, nope + half:hd, :] = ((x1 * sinT + x2 * cosT) * q_scale).astype(_BF16)

    kn = jnp.dot(ckv, wuk_ref[...], preferred_element_type=_F32)
    for hh in range(heads):
        k_ref[hh, :, 0:nope] = kn[:, hh * nope:(hh + 1) * nope].astype(_BF16)
        k_ref[hh, :, nope:hd] = kr

    vdim = wvT_ref.shape[0] // heads
    vT = lax.dot_general(wvT_ref[...], ckv, _NT, preferred_element_type=_F32)
    tk = vT_ref.shape[3]
    pad = vT_ref.shape[2] - vdim
    ones_rows = (lax.broadcasted_iota(jnp.int32, (pad, tk), 0) == 0).astype(_BF16)
    for hh in range(heads):
        for j in range(vT_ref.shape[1]):
            vT_ref[hh, j, 0:vdim, :] = vT[hh * vdim:(hh + 1) * vdim, j * tk:(j + 1) * tk].astype(_BF16)
            vT_ref[hh, j, vdim:vdim + pad, :] = ones_rows


def _mla_attn_kernel(qT_ref, k_ref, vT_ref, o_ref, m_scr, acc_scr,
                     s00, s01, s10, s11, cm00, cm01, cm10, cm11, p_a, p_b, al_a, al_b):
    tk, tq = s00.shape
    tsub = qT_ref.shape[2]
    nsub = tq // tsub
    n_q = qT_ref.shape[0] // nsub
    rb = _ATTN_ROW_BLOCK
    assert tq == 2 * tk and tk == vT_ref.shape[2] and tk % rb == 0

    cb = _ATTN_COL_BLOCK

    def q_stage(qi, c, s_ref, cm_ref, col_lo=0):
        for r0 in range(0, tk, rb):
            k = k_ref[pl.ds(pl.multiple_of(c * tk + r0, rb), rb), :]
            for lo in range(col_lo, tq, cb):
                j, off = divmod(lo, tsub)
                qT = qT_ref[qi * nsub + j, :, off:off + cb]
                s = jnp.dot(k, qT, preferred_element_type=_F32)
                s = s.astype(s_ref.dtype)
                s_ref[r0:r0 + rb, lo:lo + cb] = s
                cm = jnp.max(s, axis=0, keepdims=True).astype(_F32)
                if r0 > 0:
                    cm = jnp.maximum(cm, cm_ref[:, lo:lo + cb])
                cm_ref[:, lo:lo + cb] = cm

    def x_stage(s_ref, cm_ref, p_ref, al_ref, mask_off=None, col_lo=0):
        for lo in range(col_lo, tq, cb):
            cols = slice(lo, lo + cb)
            s = s_ref[:, cols]
            if mask_off is None:
                cm = cm_ref[:, cols]
            else:
                krel = lax.broadcasted_iota(jnp.int32, s.shape, 0) + mask_off
                qrel = lax.broadcasted_iota(jnp.int32, s.shape, 1) + lo
                s = jnp.where(krel <= qrel, s, jnp.asarray(-jnp.inf, s.dtype))
                cm = jnp.max(s, axis=0, keepdims=True).astype(_F32)
            m_prev = m_scr[:, cols]
            m_new = jnp.maximum(m_prev, cm)
            alpha = jnp.exp2(m_prev - m_new)
            p = jnp.exp2(s - m_new.astype(s.dtype))
            p_ref[:, cols] = p.astype(_BF16)
            al_ref[:, cols] = alpha
            m_scr[:, cols] = m_new

    def v_stage(c, p_ref, al_ref, col_lo=0):
        for lo in range(col_lo, tq, cb):
            cols = slice(lo, lo + cb)
            acc_scr[:, cols] = al_ref[:, cols] * acc_scr[:, cols] + jnp.dot(
                vT_ref[c], p_ref[:, cols], preferred_element_type=_F32)

    s_bufs = (((s00, cm00), (s01, cm01)), ((s10, cm10), (s11, cm11)))

    def start_tile():
        m_scr[...] = jnp.full(m_scr.shape, -jnp.inf, _F32)
        acc_scr[...] = jnp.zeros(acc_scr.shape, _F32)
        p_b[...] = jnp.zeros(p_b.shape, p_b.dtype)
        al_b[...] = jnp.ones(al_b.shape, _F32)

    def regular_step(qi, a, par):
        (sa, cma), (sb, cmb) = s_bufs[par]
        (na, ncma), (nb, ncmb) = s_bufs[1 - par]
        q_stage(qi, a + 2, na, ncma)
        x_stage(sa, cma, p_a, al_a)
        v_stage(jnp.maximum(a - 1, 0), p_b, al_b)
        q_stage(qi, a + 3, nb, ncmb)
        x_stage(sb, cmb, p_b, al_b)
        v_stage(a, p_a, al_a)

    def diagonal_step(qi, par):
        a = 2 * qi
        (sa, cma), (sb, cmb) = s_bufs[par]
        (na, ncma), (nb, ncmb) = s_bufs[1 - par]
        q_next = jnp.minimum(qi + 1, n_q - 1)
        q_stage(q_next, 0, na, ncma)
        x_stage(sa, cma, p_a, al_a, mask_off=0)
        v_stage(jnp.maximum(a - 1, 0), p_b, al_b)
        q_stage(q_next, 1, nb, ncmb)
        x_stage(sb, cmb, p_b, al_b, mask_off=tk, col_lo=tk)
        v_stage(a, p_a, al_a)
        v_stage(a + 1, p_b, al_b, col_lo=tk)
        vdim = o_ref.shape[1]
        inv_l = 1.0 / acc_scr[vdim:vdim + 1, :]
        o = (acc_scr[0:vdim, :] * inv_l).T
        o_ref[pl.ds(pl.multiple_of(qi * tq, tq), tq), :] = o.astype(o_ref.dtype)
        start_tile()

    def by_parity(par, fn):
        for static_par in range(2):
            @pl.when(par == static_par)
            def _():
                fn(static_par)

    def tile_body(qi, par):
        def step_body(u, par):
            by_parity(par, functools.partial(regular_step, qi, 2 * u))
            return 1 - par

        par = lax.fori_loop(0, qi, step_body, par)
        by_parity(par, functools.partial(diagonal_step, qi))
        return 1 - par

    start_tile()
    q_stage(0, 0, s00, cm00)
    q_stage(0, 1, s01, cm01)
    lax.fori_loop(0, n_q, tile_body, jnp.int32(0))


def _outproj_mlp_kernel(o_ref, x_ref, wo_ref, g1_ref, b1_ref, wup_ref, wdown_ref,
                        g2_ref, b2_ref, out_ref, x1_scr, h_scr, *, alpha):
    def mix_rows(rows):
        return jnp.dot(o_ref[rows, :], wo_ref[...], preferred_element_type=_F32)

    _mix_mlp_ln(mix_rows, x_ref, alpha, g1_ref, b1_ref, wup_ref, wdown_ref, g2_ref, b2_ref,
                out_ref, x1_scr, h_scr)


def _swa_mlp_kernel(x_ref, wqT_ref, wk_ref, wvT_ref, wo_ref, bucketT_ref, relb_ref, sink_ref,
                    g1_ref, b1_ref, wup_ref, wdown_ref, g2_ref, b2_ref, out_ref,
                    k_scr, vTe_scr, qT_scr, biasT_scr, s_scr, p_scr, m_scr, oT_scr, x1_scr, h_scr,
                    *, alpha, q_heads, kv_heads, head_dim, q_scale):
    blk = _SWA_BLOCK
    group = q_heads // kv_heads
    tm = x_ref.shape[0]
    n_blk = tm // blk
    vrows = vTe_scr.shape[0] // kv_heads
    first_tile = pl.program_id(1) == 0
    always = pl.program_id(1) >= 0

    @pl.when((pl.program_id(0) == 0) & first_tile)
    def _():
        bkt = bucketT_ref[...]
        for hq in range(q_heads):
            bias = jnp.full(bkt.shape, -jnp.inf, _F32)
            for bb in range(_REL_BUCKETS):
                bias = jnp.where(bkt == bb, relb_ref[bb, hq] * _LOG2E, bias)
            g, gi = divmod(hq, group)
            biasT_scr[g, :, gi * blk:(gi + 1) * blk] = bias

    @pl.when(first_tile)
    def _():
        k_scr[0:blk, :] = jnp.zeros((blk, k_scr.shape[1]), k_scr.dtype)
        vTe_scr[:, 0:blk] = jnp.zeros((vTe_scr.shape[0], blk), vTe_scr.dtype)

    xb = x_ref[...].astype(_BF16)
    qT_scr[...] = (lax.dot_general(wqT_ref[...], xb, _NT, preferred_element_type=_F32)
                   * q_scale).astype(_BF16)
    k_scr[blk:blk + tm, :] = jnp.dot(xb, wk_ref[...], preferred_element_type=_F32).astype(_BF16)
    vT = lax.dot_general(wvT_ref[...], xb, _NT, preferred_element_type=_F32).astype(_BF16)
    ones_rows = (lax.broadcasted_iota(jnp.int32, (vrows - head_dim, tm), 0) == 0).astype(_BF16)
    for g in range(kv_heads):
        vTe_scr[g * vrows:g * vrows + head_dim, blk:blk + tm] = vT[g * head_dim:(g + 1) * head_dim]
        vTe_scr[g * vrows + head_dim:(g + 1) * vrows, blk:blk + tm] = ones_rows

    first_pen = jnp.where(first_tile, -jnp.inf, 0.0).astype(_F32)
    zeros_q = jnp.zeros((head_dim, group * blk), _BF16)
    heads_per_lane_tile = _LANES // head_dim
    units = [(n, g) for n in range(n_blk) for g in range(kv_heads)]

    def sink_row(g):
        return jnp.concatenate(
            [jnp.full((1, blk), sink_ref[g * group + gi] * _LOG2E, _F32) for gi in range(group)],
            axis=1)

    @pl.when(always)
    def _():
        sinks = [sink_row(g) for g in range(kv_heads)]
        for u, (n, g) in enumerate(units):
            lt = g // heads_per_lane_tile
            kpair = k_scr[n * blk:(n + 2) * blk, lt * _LANES:(lt + 1) * _LANES]
            qg = jnp.concatenate(
                [qT_scr[(g * group + gi) * head_dim:(g * group + gi + 1) * head_dim,
                        n * blk:(n + 1) * blk] for gi in range(group)], axis=1)
            pieces = [zeros_q] * heads_per_lane_tile
            pieces[g % heads_per_lane_tile] = qg
            qpad = jnp.concatenate(pieces, axis=0)
            s = jnp.dot(kpair, qpad, preferred_element_type=_F32) + biasT_scr[g]
            if n == 0:
                s = jnp.concatenate([s[:blk] + first_pen, s[blk:]], axis=0)
            s_scr[u] = s
            m_scr[u] = jnp.maximum(jnp.max(s, axis=0, keepdims=True), sinks[g])

    @pl.when(always)
    def _():
        for u in range(len(units)):
            p_scr[u] = jnp.exp2(s_scr[u] - m_scr[u]).astype(_BF16)

    @pl.when(always)
    def _():
        sinks = [sink_row(g) for g in range(kv_heads)]
        for u, (n, g) in enumerate(units):
            oT = jnp.dot(vTe_scr[g * vrows:(g + 1) * vrows, n * blk:(n + 2) * blk], p_scr[u],
                         preferred_element_type=_F32)
            denom = oT[head_dim:head_dim + 1] + jnp.exp2(sinks[g] - m_scr[u])
            o = (oT[:head_dim] * (1.0 / denom)).astype(_BF16)
            for gi in range(group):
                hq = g * group + gi
                oT_scr[hq * head_dim:(hq + 1) * head_dim, n * blk:(n + 1) * blk] = (
                    o[:, gi * blk:(gi + 1) * blk])
        k_scr[0:blk, :] = k_scr[tm:tm + blk, :]
        vTe_scr[:, 0:blk] = vTe_scr[:, tm:tm + blk]

    def mix_rows(rows):
        return lax.dot_general(oT_scr[:, rows], wo_ref[...], _TN, preferred_element_type=_F32)

    _mix_mlp_ln(mix_rows, x_ref, alpha, g1_ref, b1_ref, wup_ref, wdown_ref, g2_ref, b2_ref,
                out_ref, x1_scr, h_scr)


def _t5_bucket_table(blk):
    i = np.arange(blk)[None, :]
    j = np.arange(2 * blk)[:, None]
    dist = i + blk - j
    max_exact = _REL_BUCKETS // 2
    nf = np.maximum(dist, 1).astype(np.float32)
    large = max_exact + (np.log(nf / np.float32(max_exact))
                         / np.float32(math.log(_REL_MAX_DIST / max_exact))
                         * np.float32(_REL_BUCKETS - max_exact)).astype(np.int32)
    large = np.minimum(large, _REL_BUCKETS - 1)
    bucket = np.where(dist < max_exact, np.maximum(dist, 0), large)
    valid = (dist >= 0) & (dist < blk)
    return np.where(valid, bucket, -1).astype(np.int32)


def _rope_tables(seq, rope):
    half = rope // 2
    inv = _ROPE_THETA ** (-jnp.arange(half, dtype=_F32) / half)
    ang = jnp.arange(seq).astype(_F32)[:, None] * inv[None, :]
    cos, sin = jnp.cos(ang), jnp.sin(ang)
    cs = jnp.concatenate([cos, cos, -sin, sin], axis=1)
    return cs, cos.T, sin.T


def _row(v):
    return v.reshape(1, -1).astype(_F32)


def kernel(x, mla_w_in, mla_g_q, mla_g_kv, mla_w_uq, mla_w_uk, mla_w_uv, mla_w_o, kv_w_shared,
           swa_w_q, swa_sinks, swa_w_o, rel_bias, mlp_w_up, mlp_w_down, ln_mix_g, ln_mix_b,
           ln_mlp_g, ln_mlp_b):
    B, S, D = x.shape
    depth = mlp_w_up.shape[0]
    assert depth == 2 and mla_w_in.shape[0] == 1 and swa_w_q.shape[0] == 1
    alpha = (2 * depth) ** 0.25
    T = B * S
    tm = _TOKEN_TILE
    ta = _ATTN_TILE
    tk = ta // 2
    bf16_rows = 16
    assert ta % tm == 0 and tm % tk == 0 and S % ta == 0

    q_rank, heads, hd = mla_w_uq.shape[1:]
    kv_rank, _, nope = mla_w_uk.shape[1:]
    vdim = mla_w_uv.shape[3]
    vrows = vdim + bf16_rows
    rope = hd - nope
    half = rope // 2
    assert 2 * rope == _LANES and mla_w_in.shape[2] == q_rank + kv_rank + rope

    w_in = mla_w_in[0]
    r0 = q_rank + kv_rank
    w_in_ext = jnp.concatenate(
        [w_in, w_in[:, r0 + half:r0 + rope], w_in[:, r0:r0 + half]], axis=1).astype(_BF16)
    wqT = mla_w_uq[0].reshape(q_rank, heads * hd).T.astype(_BF16)
    wuk = mla_w_uk[0].reshape(kv_rank, heads * nope).astype(_BF16)
    wvT = mla_w_uv[0].reshape(kv_rank, heads * vdim).T.astype(_BF16)
    cs, cosT, sinT = _rope_tables(S, rope)
    n_t = S // tm
    q_scale = hd ** -0.5 * _LOG2E

    cparams2 = pltpu.CompilerParams(dimension_semantics=("arbitrary", "arbitrary"),
                                    vmem_limit_bytes=_VMEM_LIMIT_BYTES)
    cparams1 = pltpu.CompilerParams(dimension_semantics=("arbitrary",),
                                    vmem_limit_bytes=_VMEM_LIMIT_BYTES)

    qT_all, k_all, vT_all = pl.pallas_call(
        functools.partial(_mla_proj_kernel, q_rank=q_rank, kv_rank=kv_rank, heads=heads,
                          nope=nope, rope=rope, q_scale=q_scale),
        grid=(B, n_t),
        in_specs=[
            pl.BlockSpec((None, tm, D), lambda b, i: (b, i, 0)),
            _const_spec(w_in_ext.shape),
            _const_spec((1, q_rank)),
            _const_spec((1, kv_rank)),
            _const_spec(wqT.shape),
            _const_spec(wuk.shape),
            _const_spec(wvT.shape),
            pl.BlockSpec((tm, 2 * rope), lambda b, i: (i, 0)),
            pl.BlockSpec((half, tm), lambda b, i: (0, i)),
            pl.BlockSpec((half, tm), lambda b, i: (0, i)),
        ],
        out_specs=[
            pl.BlockSpec((None, heads, None, hd, tm), lambda b, i: (b, 0, i, 0, 0)),
            pl.BlockSpec((None, heads, tm, hd), lambda b, i: (b, 0, i, 0)),
            pl.BlockSpec((None, heads, tm // tk, vrows, tk), lambda b, i: (b, 0, i, 0, 0)),
        ],
        out_shape=[
            jax.ShapeDtypeStruct((B, heads, n_t, hd, tm), _BF16),
            jax.ShapeDtypeStruct((B, heads, S, hd), _BF16),
            jax.ShapeDtypeStruct((B, heads, S // tk, vrows, tk), _BF16),
        ],
        compiler_params=cparams2,
        name="mla_proj",
    )(x, w_in_ext, _row(mla_g_q[0]), _row(mla_g_kv[0]), wqT, wuk, wvT, cs, cosT, sinT)

    o = pl.pallas_call(
        _mla_attn_kernel,
        grid=(B, heads),
        in_specs=[
            pl.BlockSpec((None, None, n_t, hd, tm), lambda b, h: (b, h, 0, 0, 0)),
            pl.BlockSpec((None, None, S, hd), lambda b, h: (b, h, 0, 0)),
            pl.BlockSpec((None, None, S // tk, vrows, tk), lambda b, h: (b, h, 0, 0, 0)),
        ],
        out_specs=pl.BlockSpec((None, S, vdim), lambda b, h: (b, 0, h)),
        out_shape=jax.ShapeDtypeStruct((B, S, heads * vdim), _BF16),
        scratch_shapes=[
            pltpu.VMEM((1, ta), _F32),
            pltpu.VMEM((vrows, ta), _F32),
        ] + [pltpu.VMEM((tk, ta), _BF16)] * 4
          + [pltpu.VMEM((1, ta), _F32)] * 4
          + [pltpu.VMEM((tk, ta), _BF16)] * 2
          + [pltpu.VMEM((1, ta), _F32)] * 2,
        compiler_params=cparams2,
        name="mla_attn",
    )(qT_all, k_all, vT_all)

    d_ff = mlp_w_up.shape[2]
    mlp_scratch = [
        pltpu.VMEM((tm, D), _F32),
        pltpu.VMEM((tm, d_ff), _BF16),
    ]
    x1 = pl.pallas_call(
        functools.partial(_outproj_mlp_kernel, alpha=alpha),
        grid=(T // tm,),
        in_specs=[
            pl.BlockSpec((tm, heads * vdim), lambda i: (i, 0)),
            pl.BlockSpec((tm, D), lambda i: (i, 0)),
            _const_spec((heads * vdim, D)),
            _const_spec((1, D)),
            _const_spec((1, D)),
            _const_spec((D, d_ff)),
            _const_spec((d_ff, D)),
            _const_spec((1, D)),
            _const_spec((1, D)),
        ],
        out_specs=pl.BlockSpec((tm, D), lambda i: (i, 0)),
        out_shape=jax.ShapeDtypeStruct((T, D), _F32),
        scratch_shapes=mlp_scratch,
        compiler_params=cparams1,
        name="mla_out_mlp",
    )(o.reshape(T, heads * vdim), x.reshape(T, D), mla_w_o[0].astype(_BF16),
      _row(ln_mix_g[0]), _row(ln_mix_b[0]), mlp_w_up[0].astype(_BF16),
      mlp_w_down[0].astype(_BF16), _row(ln_mlp_g[0]), _row(ln_mlp_b[0]))

    q_heads = swa_sinks.shape[1]
    s_hd = swa_w_q.shape[2] // q_heads
    kv_heads = kv_w_shared.shape[1] // (2 * s_hd)
    group = q_heads // kv_heads
    blk = _SWA_BLOCK
    assert _LANES % s_hd == 0 and tm % blk == 0
    wqT1 = swa_w_q[0].T.astype(_BF16)
    wk1 = kv_w_shared[:, :kv_heads * s_hd].astype(_BF16)
    wvT1 = kv_w_shared[:, kv_heads * s_hd:].T.astype(_BF16)
    bucketT = jnp.asarray(_t5_bucket_table(blk))
    n_units = (tm // blk) * kv_heads

    smem = pl.BlockSpec(memory_space=pltpu.SMEM)
    out = pl.pallas_call(
        functools.partial(_swa_mlp_kernel, alpha=alpha, q_heads=q_heads, kv_heads=kv_heads,
                          head_dim=s_hd, q_scale=s_hd ** -0.5 * _LOG2E),
        grid=(B, n_t),
        in_specs=[
            pl.BlockSpec((None, tm, D), lambda b, i: (b, i, 0)),
            _const_spec(wqT1.shape),
            _const_spec(wk1.shape),
            _const_spec(wvT1.shape),
            _const_spec((q_heads * s_hd, D)),
            _const_spec(bucketT.shape),
            smem,
            smem,
            _const_spec((1, D)),
            _const_spec((1, D)),
            _const_spec((D, d_ff)),
            _const_spec((d_ff, D)),
            _const_spec((1, D)),
            _const_spec((1, D)),
        ],
        out_specs=pl.BlockSpec((None, tm, D), lambda b, i: (b, i, 0)),
        out_shape=jax.ShapeDtypeStruct((B, S, D), _F32),
        scratch_shapes=[
            pltpu.VMEM((blk + tm, kv_heads * s_hd), _BF16),
            pltpu.VMEM((kv_heads * (s_hd + bf16_rows), blk + tm), _BF16),
            pltpu.VMEM((q_heads * s_hd, tm), _BF16),
            pltpu.VMEM((kv_heads, 2 * blk, group * blk), _F32),
            pltpu.VMEM((n_units, 2 * blk, group * blk), _F32),
            pltpu.VMEM((n_units, 2 * blk, group * blk), _BF16),
            pltpu.VMEM((n_units, 1, group * blk), _F32),
            pltpu.VMEM((q_heads * s_hd, tm), _BF16),
        ] + mlp_scratch,
        compiler_params=cparams2,
        name="swa_mlp",
    )(x1.reshape(B, S, D), wqT1, wk1, wvT1, swa_w_o[0].astype(_BF16), bucketT,
      rel_bias.astype(_F32), swa_sinks[0].astype(_F32),
      _row(ln_mix_g[1]), _row(ln_mix_b[1]), mlp_w_up[1].astype(_BF16),
      mlp_w_down[1].astype(_BF16), _row(ln_mlp_g[1]), _row(ln_mlp_b[1]))
    return out
```

```python
import functools
import math

import numpy as np
import jax
import jax.numpy as jnp
from jax import lax
from jax.experimental import pallas as pl
from jax.experimental.pallas import tpu as pltpu

_F32 = jnp.float32
_BF16 = jnp.bfloat16

_LN_EPS = 1e-5
_RMS_EPS = 1e-6
_ROPE_THETA = 10000.0
_SWA_BLOCK = 128
_REL_BUCKETS = 32
_REL_MAX_DIST = 128
_LOG2E = math.log2(math.e)

_V7X_VMEM_BYTES = 64 * 1024 * 1024
_VMEM_LIMIT_BYTES = _V7X_VMEM_BYTES - 8 * 1024 * 1024
_LANES = 128
_MXU_DEPTH = 256

_TOKEN_TILE = 512
_ATTN_TILE = 1024
_ATTN_ROW_BLOCK = 512
_ATTN_COL_BLOCK = 256
_FF_CHUNK = 1024
_MLP_ROW_BLOCK = 256

_NT = (((1,), (1,)), ((), ()))
_TN = (((0,), (0,)), ((), ()))


def _const_spec(shape):
    nd = len(shape)
    return pl.BlockSpec(shape, lambda *_: (0,) * nd, pipeline_mode=pl.Buffered(1))


def _layernorm(v, g, b):
    mu = jnp.mean(v, axis=-1, keepdims=True)
    d = v - mu
    var = jnp.mean(d * d, axis=-1, keepdims=True)
    return d * lax.rsqrt(var + _LN_EPS) * g + b


def _rmsnorm(v, g):
    return v * lax.rsqrt(jnp.mean(v * v, axis=-1, keepdims=True) + _RMS_EPS) * g


def _mix_mlp_ln(mix_rows, x_ref, alpha, g1_ref, b1_ref, wup_ref, wdown_ref, g2_ref, b2_ref,
                out_ref, x1_scr, h_scr):
    tm = x_ref.shape[0]
    d_ff = wup_ref.shape[1]
    blocks = [slice(r, r + _MLP_ROW_BLOCK) for r in range(0, tm, _MLP_ROW_BLOCK)]
    for rows in blocks:
        x1_scr[rows, :] = _layernorm(alpha * x_ref[rows, :] + mix_rows(rows), g1_ref[...], b1_ref[...])
    for rows in blocks:
        x1b = x1_scr[rows, :].astype(_BF16)
        for lo in range(0, d_ff, _FF_CHUNK):
            hmid = jnp.dot(x1b, wup_ref[:, lo:lo + _FF_CHUNK], preferred_element_type=_F32)
            hmid = jnp.maximum(hmid, 0.0)
            h_scr[rows, lo:lo + _FF_CHUNK] = (hmid * hmid).astype(_BF16)
    for rows in blocks:
        y = jnp.dot(h_scr[rows, :], wdown_ref[...], preferred_element_type=_F32)
        out_ref[rows, :] = _layernorm(alpha * x1_scr[rows, :] + y, g2_ref[...], b2_ref[...])


def _mla_proj_kernel(x_ref, win_ref, gq_ref, gkv_ref, wqT_ref, wuk_ref, wvT_ref,
                     cs_ref, cosT_ref, sinT_ref, qT_ref, k_ref, vT_ref,
                     *, q_rank, kv_rank, heads, nope, rope, q_scale):
    xb = x_ref[...].astype(_BF16)
    h = jnp.dot(xb, win_ref[...], preferred_element_type=_F32)
    cq = _rmsnorm(h[:, :q_rank], gq_ref[...]).astype(_BF16)
    ckv = _rmsnorm(h[:, q_rank:q_rank + kv_rank], gkv_ref[...]).astype(_BF16)
    t = h[:, q_rank + kv_rank:] * cs_ref[...]
    kr = (t + pltpu.roll(t, rope, 1))[:, :rope].astype(_BF16)

    half = rope // 2
    hd = nope + rope
    qT = lax.dot_general(wqT_ref[...], cq, _NT, preferred_element_type=_F32)
    cosT = cosT_ref[...]
    sinT = sinT_ref[...]
    for hh in range(heads):
        base = hh * hd
        qT_ref[hh, 0:nope, :] = (qT[base:base + nope] * q_scale).astype(_BF16)
        x1 = qT[base + nope:base + nope + half]
        x2 = qT[base + nope + half:base + hd]
        qT_ref[hh, nope:nope + half, :] = ((x1 * cosT - x2 * sinT) * q_scale).astype(_BF16)
        qT_ref[hh, nope + half:hd, :] = ((x1 * sinT + x2 * cosT) * q_scale).astype(_BF16)
        if qT_ref.shape[1] > hd:
            qT_ref[hh, hd:, :] = jnp.zeros((qT_ref.shape[1] - hd, qT_ref.shape[2]), _BF16)

    kn = jnp.dot(ckv, wuk_ref[...], preferred_element_type=_F32)
    for hh in range(heads):
        k_ref[hh, :, 0:nope] = kn[:, hh * nope:(hh + 1) * nope].astype(_BF16)
        k_ref[hh, :, nope:hd] = kr
        if k_ref.shape[2] > hd:
            k_ref[hh, :, hd:] = jnp.zeros((k_ref.shape[1], k_ref.shape[2] - hd), _BF16)

    vdim = wvT_ref.shape[0] // heads
    vT = lax.dot_general(wvT_ref[...], ckv, _NT, preferred_element_type=_F32)
    tk = vT_ref.shape[3]
    pad = vT_ref.shape[2] - vdim
    ones_rows = (lax.broadcasted_iota(jnp.int32, (pad, tk), 0) == 0).astype(_BF16)
    for hh in range(heads):
        for j in range(vT_ref.shape[1]):
            vT_ref[hh, j, 0:vdim, :] = vT[hh * vdim:(hh + 1) * vdim, j * tk:(j + 1) * tk].astype(_BF16)
            vT_ref[hh, j, vdim:vdim + pad, :] = ones_rows


def _mla_attn_kernel(qT_ref, k_ref, vT_ref, o_ref, m_scr, acc_scr,
                     s00, s01, s10, s11, cm00, cm01, cm10, cm11, p_a, p_b, al_a, al_b):
    tk, tq = s00.shape
    tsub = qT_ref.shape[2]
    nsub = tq // tsub
    n_q = qT_ref.shape[0] // nsub
    rb = _ATTN_ROW_BLOCK
    assert tq == 2 * tk and tk == vT_ref.shape[2] and tk % rb == 0

    cb = _ATTN_COL_BLOCK

    def q_stage(qi, c, s_ref, cm_ref, col_lo=0):
        for r0 in range(0, tk, rb):
            k = k_ref[pl.ds(pl.multiple_of(c * tk + r0, rb), rb), :]
            for lo in range(col_lo, tq, cb):
                j, off = divmod(lo, tsub)
                qT = qT_ref[qi * nsub + j, :, off:off + cb]
                s = jnp.dot(k, qT, preferred_element_type=_F32)
                s_ref[r0:r0 + rb, lo:lo + cb] = s
                cm = jnp.max(s, axis=0, keepdims=True)
                if r0 > 0:
                    cm = jnp.maximum(cm, cm_ref[:, lo:lo + cb])
                cm_ref[:, lo:lo + cb] = cm

    def x_stage(s_ref, cm_ref, p_ref, al_ref, mask_off=None, col_lo=0):
        for lo in range(col_lo, tq, cb):
            cols = slice(lo, lo + cb)
            s = s_ref[:, cols]
            if mask_off is None:
                cm = cm_ref[:, cols]
            else:
                krel = lax.broadcasted_iota(jnp.int32, s.shape, 0) + mask_off
                qrel = lax.broadcasted_iota(jnp.int32, s.shape, 1) + lo
                s = jnp.where(krel <= qrel, s, -jnp.inf)
                cm = jnp.max(s, axis=0, keepdims=True)
            m_prev = m_scr[:, cols]
            m_new = jnp.maximum(m_prev, cm)
            alpha = jnp.exp2(m_prev - m_new)
            p = jnp.exp2(s - m_new)
            p_ref[:, cols] = p.astype(_BF16)
            al_ref[:, cols] = alpha
            m_scr[:, cols] = m_new

    def v_stage(c, p_ref, al_ref, col_lo=0):
        for lo in range(col_lo, tq, cb):
            cols = slice(lo, lo + cb)
            acc_scr[:, cols] = al_ref[:, cols] * acc_scr[:, cols] + jnp.dot(
                vT_ref[c], p_ref[:, cols], preferred_element_type=_F32)

    s_bufs = (((s00, cm00), (s01, cm01)), ((s10, cm10), (s11, cm11)))

    def start_tile():
        m_scr[...] = jnp.full(m_scr.shape, -jnp.inf, _F32)
        acc_scr[...] = jnp.zeros(acc_scr.shape, _F32)
        p_b[...] = jnp.zeros(p_b.shape, p_b.dtype)
        al_b[...] = jnp.ones(al_b.shape, _F32)

    def regular_step(qi, a, par):
        (sa, cma), (sb, cmb) = s_bufs[par]
        (na, ncma), (nb, ncmb) = s_bufs[1 - par]
        q_stage(qi, a + 2, na, ncma)
        x_stage(sa, cma, p_a, al_a)
        v_stage(jnp.maximum(a - 1, 0), p_b, al_b)
        q_stage(qi, a + 3, nb, ncmb)
        x_stage(sb, cmb, p_b, al_b)
        v_stage(a, p_a, al_a)

    def diagonal_step(qi, par):
        a = 2 * qi
        (sa, cma), (sb, cmb) = s_bufs[par]
        (na, ncma), (nb, ncmb) = s_bufs[1 - par]
        q_next = jnp.minimum(qi + 1, n_q - 1)
        q_stage(q_next, 0, na, ncma)
        x_stage(sa, cma, p_a, al_a, mask_off=0)
        v_stage(jnp.maximum(a - 1, 0), p_b, al_b)
        q_stage(q_next, 1, nb, ncmb)
        x_stage(sb, cmb, p_b, al_b, mask_off=tk, col_lo=tk)
        v_stage(a, p_a, al_a)
        v_stage(a + 1, p_b, al_b, col_lo=tk)
        vdim = o_ref.shape[1]
        inv_l = 1.0 / acc_scr[vdim:vdim + 1, :]
        o = (acc_scr[0:vdim, :] * inv_l).T
        o_ref[pl.ds(pl.multiple_of(qi * tq, tq), tq), :] = o.astype(o_ref.dtype)
        start_tile()

    def by_parity(par, fn):
        for static_par in range(2):
            @pl.when(par == static_par)
            def _():
                fn(static_par)

    def tile_body(qi, par):
        def step_body(u, par):
            by_parity(par, functools.partial(regular_step, qi, 2 * u))
            return 1 - par

        par = lax.fori_loop(0, qi, step_body, par)
        by_parity(par, functools.partial(diagonal_step, qi))
        return 1 - par

    start_tile()
    q_stage(0, 0, s00, cm00)
    q_stage(0, 1, s01, cm01)
    lax.fori_loop(0, n_q, tile_body, jnp.int32(0))


def _outproj_mlp_kernel(o_ref, x_ref, wo_ref, g1_ref, b1_ref, wup_ref, wdown_ref,
                        g2_ref, b2_ref, out_ref, x1_scr, h_scr, *, alpha):
    def mix_rows(rows):
        return jnp.dot(o_ref[rows, :], wo_ref[...], preferred_element_type=_F32)

    _mix_mlp_ln(mix_rows, x_ref, alpha, g1_ref, b1_ref, wup_ref, wdown_ref, g2_ref, b2_ref,
                out_ref, x1_scr, h_scr)


def _swa_mlp_kernel(x_ref, wqT_ref, wk_ref, wvT_ref, wo_ref, bucketT_ref, relb_ref, sink_ref,
                    g1_ref, b1_ref, wup_ref, wdown_ref, g2_ref, b2_ref, out_ref,
                    k_scr, vTe_scr, qT_scr, biasT_scr, s_scr, p_scr, m_scr, oT_scr, x1_scr, h_scr,
                    *, alpha, q_heads, kv_heads, head_dim, q_scale):
    blk = _SWA_BLOCK
    group = q_heads // kv_heads
    tm = x_ref.shape[0]
    n_blk = tm // blk
    vrows = vTe_scr.shape[0] // kv_heads
    first_tile = pl.program_id(1) == 0
    always = pl.program_id(1) >= 0

    @pl.when((pl.program_id(0) == 0) & first_tile)
    def _():
        bkt = bucketT_ref[...]
        for hq in range(q_heads):
            bias = jnp.full(bkt.shape, -jnp.inf, _F32)
            for bb in range(_REL_BUCKETS):
                bias = jnp.where(bkt == bb, relb_ref[bb, hq] * _LOG2E, bias)
            g, gi = divmod(hq, group)
            biasT_scr[g, :, gi * blk:(gi + 1) * blk] = bias

    @pl.when(first_tile)
    def _():
        k_scr[0:blk, :] = jnp.zeros((blk, k_scr.shape[1]), k_scr.dtype)
        vTe_scr[:, 0:blk] = jnp.zeros((vTe_scr.shape[0], blk), vTe_scr.dtype)

    xb = x_ref[...].astype(_BF16)
    qT_scr[...] = (lax.dot_general(wqT_ref[...], xb, _NT, preferred_element_type=_F32)
                   * q_scale).astype(_BF16)
    k_scr[blk:blk + tm, :] = jnp.dot(xb, wk_ref[...], preferred_element_type=_F32).astype(_BF16)
    vT = lax.dot_general(wvT_ref[...], xb, _NT, preferred_element_type=_F32).astype(_BF16)
    ones_rows = (lax.broadcasted_iota(jnp.int32, (vrows - head_dim, tm), 0) == 0).astype(_BF16)
    for g in range(kv_heads):
        vTe_scr[g * vrows:g * vrows + head_dim, blk:blk + tm] = vT[g * head_dim:(g + 1) * head_dim]
        vTe_scr[g * vrows + head_dim:(g + 1) * vrows, blk:blk + tm] = ones_rows

    first_pen = jnp.where(first_tile, -jnp.inf, 0.0).astype(_F32)
    zeros_q = jnp.zeros((head_dim, group * blk), _BF16)
    heads_per_lane_tile = _LANES // head_dim
    units = [(n, g) for n in range(n_blk) for g in range(kv_heads)]

    def sink_row(g):
        return jnp.concatenate(
            [jnp.full((1, blk), sink_ref[g * group + gi] * _LOG2E, _F32) for gi in range(group)],
            axis=1)

    @pl.when(always)
    def _():
        sinks = [sink_row(g) for g in range(kv_heads)]
        for u, (n, g) in enumerate(units):
            lt = g // heads_per_lane_tile
            kpair = k_scr[n * blk:(n + 2) * blk, lt * _LANES:(lt + 1) * _LANES]
            qg = jnp.concatenate(
                [qT_scr[(g * group + gi) * head_dim:(g * group + gi + 1) * head_dim,
                        n * blk:(n + 1) * blk] for gi in range(group)], axis=1)
            pieces = [zeros_q] * heads_per_lane_tile
            pieces[g % heads_per_lane_tile] = qg
            qpad = jnp.concatenate(pieces, axis=0)
            s = jnp.dot(kpair, qpad, preferred_element_type=_F32) + biasT_scr[g]
            if n == 0:
                s = jnp.concatenate([s[:blk] + first_pen, s[blk:]], axis=0)
            s_scr[u] = s
            m_scr[u] = jnp.maximum(jnp.max(s, axis=0, keepdims=True), sinks[g])

    @pl.when(always)
    def _():
        for u in range(len(units)):
            p_scr[u] = jnp.exp2(s_scr[u] - m_scr[u]).astype(_BF16)

    @pl.when(always)
    def _():
        sinks = [sink_row(g) for g in range(kv_heads)]
        for u, (n, g) in enumerate(units):
            oT = jnp.dot(vTe_scr[g * vrows:(g + 1) * vrows, n * blk:(n + 2) * blk], p_scr[u],
                         preferred_element_type=_F32)
            denom = oT[head_dim:head_dim + 1] + jnp.exp2(sinks[g] - m_scr[u])
            o = (oT[:head_dim] * (1.0 / denom)).astype(_BF16)
            for gi in range(group):
                hq = g * group + gi
                oT_scr[hq * head_dim:(hq + 1) * head_dim, n * blk:(n + 1) * blk] = (
                    o[:, gi * blk:(gi + 1) * blk])
        k_scr[0:blk, :] = k_scr[tm:tm + blk, :]
        vTe_scr[:, 0:blk] = vTe_scr[:, tm:tm + blk]

    def mix_rows(rows):
        return lax.dot_general(oT_scr[:, rows], wo_ref[...], _TN, preferred_element_type=_F32)

    _mix_mlp_ln(mix_rows, x_ref, alpha, g1_ref, b1_ref, wup_ref, wdown_ref, g2_ref, b2_ref,
                out_ref, x1_scr, h_scr)


def _t5_bucket_table(blk):
    i = np.arange(blk)[None, :]
    j = np.arange(2 * blk)[:, None]
    dist = i + blk - j
    max_exact = _REL_BUCKETS // 2
    nf = np.maximum(dist, 1).astype(np.float32)
    large = max_exact + (np.log(nf / np.float32(max_exact))
                         / np.float32(math.log(_REL_MAX_DIST / max_exact))
                         * np.float32(_REL_BUCKETS - max_exact)).astype(np.int32)
    large = np.minimum(large, _REL_BUCKETS - 1)
    bucket = np.where(dist < max_exact, np.maximum(dist, 0), large)
    valid = (dist >= 0) & (dist < blk)
    return np.where(valid, bucket, -1).astype(np.int32)


def _rope_tables(seq, rope):
    half = rope // 2
    inv = _ROPE_THETA ** (-jnp.arange(half, dtype=_F32) / half)
    ang = jnp.arange(seq).astype(_F32)[:, None] * inv[None, :]
    cos, sin = jnp.cos(ang), jnp.sin(ang)
    cs = jnp.concatenate([cos, cos, -sin, sin], axis=1)
    return cs, cos.T, sin.T


def _row(v):
    return v.reshape(1, -1).astype(_F32)


def kernel(x, mla_w_in, mla_g_q, mla_g_kv, mla_w_uq, mla_w_uk, mla_w_uv, mla_w_o, kv_w_shared,
           swa_w_q, swa_sinks, swa_w_o, rel_bias, mlp_w_up, mlp_w_down, ln_mix_g, ln_mix_b,
           ln_mlp_g, ln_mlp_b):
    B, S, D = x.shape
    depth = mlp_w_up.shape[0]
    assert depth == 2 and mla_w_in.shape[0] == 1 and swa_w_q.shape[0] == 1
    alpha = (2 * depth) ** 0.25
    T = B * S
    tm = _TOKEN_TILE
    ta = _ATTN_TILE
    tk = ta // 2
    bf16_rows = 16
    assert ta % tm == 0 and tm % tk == 0 and S % ta == 0

    q_rank, heads, hd = mla_w_uq.shape[1:]
    kv_rank, _, nope = mla_w_uk.shape[1:]
    vdim = mla_w_uv.shape[3]
    vrows = vdim + bf16_rows
    hdp = -(-hd // _MXU_DEPTH) * _MXU_DEPTH
    rope = hd - nope
    half = rope // 2
    assert 2 * rope == _LANES and mla_w_in.shape[2] == q_rank + kv_rank + rope

    w_in = mla_w_in[0]
    r0 = q_rank + kv_rank
    w_in_ext = jnp.concatenate(
        [w_in, w_in[:, r0 + half:r0 + rope], w_in[:, r0:r0 + half]], axis=1).astype(_BF16)
    wqT = mla_w_uq[0].reshape(q_rank, heads * hd).T.astype(_BF16)
    wuk = mla_w_uk[0].reshape(kv_rank, heads * nope).astype(_BF16)
    wvT = mla_w_uv[0].reshape(kv_rank, heads * vdim).T.astype(_BF16)
    cs, cosT, sinT = _rope_tables(S, rope)
    n_t = S // tm
    q_scale = hd ** -0.5 * _LOG2E

    cparams2 = pltpu.CompilerParams(dimension_semantics=("arbitrary", "arbitrary"),
                                    vmem_limit_bytes=_VMEM_LIMIT_BYTES)
    cparams1 = pltpu.CompilerParams(dimension_semantics=("arbitrary",),
                                    vmem_limit_bytes=_VMEM_LIMIT_BYTES)

    qT_all, k_all, vT_all = pl.pallas_call(
        functools.partial(_mla_proj_kernel, q_rank=q_rank, kv_rank=kv_rank, heads=heads,
                          nope=nope, rope=rope, q_scale=q_scale),
        grid=(B, n_t),
        in_specs=[
            pl.BlockSpec((None, tm, D), lambda b, i: (b, i, 0)),
            _const_spec(w_in_ext.shape),
            _const_spec((1, q_rank)),
            _const_spec((1, kv_rank)),
            _const_spec(wqT.shape),
            _const_spec(wuk.shape),
            _const_spec(wvT.shape),
            pl.BlockSpec((tm, 2 * rope), lambda b, i: (i, 0)),
            pl.BlockSpec((half, tm), lambda b, i: (0, i)),
            pl.BlockSpec((half, tm), lambda b, i: (0, i)),
        ],
        out_specs=[
            pl.BlockSpec((None, heads, None, hdp, tm), lambda b, i: (b, 0, i, 0, 0)),
            pl.BlockSpec((None, heads, tm, hdp), lambda b, i: (b, 0, i, 0)),
            pl.BlockSpec((None, heads, tm // tk, vrows, tk), lambda b, i: (b, 0, i, 0, 0)),
        ],
        out_shape=[
            jax.ShapeDtypeStruct((B, heads, n_t, hdp, tm), _BF16),
            jax.ShapeDtypeStruct((B, heads, S, hdp), _BF16),
            jax.ShapeDtypeStruct((B, heads, S // tk, vrows, tk), _BF16),
        ],
        compiler_params=cparams2,
        name="mla_proj",
    )(x, w_in_ext, _row(mla_g_q[0]), _row(mla_g_kv[0]), wqT, wuk, wvT, cs, cosT, sinT)

    o = pl.pallas_call(
        _mla_attn_kernel,
        grid=(B, heads),
        in_specs=[
            pl.BlockSpec((None, None, n_t, hdp, tm), lambda b, h: (b, h, 0, 0, 0)),
            pl.BlockSpec((None, None, S, hdp), lambda b, h: (b, h, 0, 0)),
            pl.BlockSpec((None, None, S // tk, vrows, tk), lambda b, h: (b, h, 0, 0, 0)),
        ],
        out_specs=pl.BlockSpec((None, S, vdim), lambda b, h: (b, 0, h)),
        out_shape=jax.ShapeDtypeStruct((B, S, heads * vdim), _BF16),
        scratch_shapes=[
            pltpu.VMEM((1, ta), _F32),
            pltpu.VMEM((vrows, ta), _F32),
        ] + [pltpu.VMEM((tk, ta), _F32)] * 4
          + [pltpu.VMEM((1, ta), _F32)] * 4
          + [pltpu.VMEM((tk, ta), _BF16)] * 2
          + [pltpu.VMEM((1, ta), _F32)] * 2,
        compiler_params=cparams2,
        name="mla_attn",
    )(qT_all, k_all, vT_all)

    d_ff = mlp_w_up.shape[2]
    mlp_scratch = [
        pltpu.VMEM((tm, D), _F32),
        pltpu.VMEM((tm, d_ff), _BF16),
    ]
    x1 = pl.pallas_call(
        functools.partial(_outproj_mlp_kernel, alpha=alpha),
        grid=(T // tm,),
        in_specs=[
            pl.BlockSpec((tm, heads * vdim), lambda i: (i, 0)),
            pl.BlockSpec((tm, D), lambda i: (i, 0)),
            _const_spec((heads * vdim, D)),
            _const_spec((1, D)),
            _const_spec((1, D)),
            _const_spec((D, d_ff)),
            _const_spec((d_ff, D)),
            _const_spec((1, D)),
            _const_spec((1, D)),
        ],
        out_specs=pl.BlockSpec((tm, D), lambda i: (i, 0)),
        out_shape=jax.ShapeDtypeStruct((T, D), _F32),
        scratch_shapes=mlp_scratch,
        compiler_params=cparams1,
        name="mla_out_mlp",
    )(o.reshape(T, heads * vdim), x.reshape(T, D), mla_w_o[0].astype(_BF16),
      _row(ln_mix_g[0]), _row(ln_mix_b[0]), mlp_w_up[0].astype(_BF16),
      mlp_w_down[0].astype(_BF16), _row(ln_mlp_g[0]), _row(ln_mlp_b[0]))

    q_heads = swa_sinks.shape[1]
    s_hd = swa_w_q.shape[2] // q_heads
    kv_heads = kv_w_shared.shape[1] // (2 * s_hd)
    group = q_heads // kv_heads
    blk = _SWA_BLOCK
    assert _LANES % s_hd == 0 and tm % blk == 0
    wqT1 = swa_w_q[0].T.astype(_BF16)
    wk1 = kv_w_shared[:, :kv_heads * s_hd].astype(_BF16)
    wvT1 = kv_w_shared[:, kv_heads * s_hd:].T.astype(_BF16)
    bucketT = jnp.asarray(_t5_bucket_table(blk))
    n_units = (tm // blk) * kv_heads

    smem = pl.BlockSpec(memory_space=pltpu.SMEM)
    out = pl.pallas_call(
        functools.partial(_swa_mlp_kernel, alpha=alpha, q_heads=q_heads, kv_heads=kv_heads,
                          head_dim=s_hd, q_scale=s_hd ** -0.5 * _LOG2E),
        grid=(B, n_t),
        in_specs=[
            pl.BlockSpec((None, tm, D), lambda b, i: (b, i, 0)),
            _const_spec(wqT1.shape),
            _const_spec(wk1.shape),
            _const_spec(wvT1.shape),
            _const_spec((q_heads * s_hd, D)),
            _const_spec(bucketT.shape),
            smem,
            smem,
            _const_spec((1, D)),
            _const_spec((1, D)),
            _const_spec((D, d_ff)),
            _const_spec((d_ff, D)),
            _const_spec((1, D)),
            _const_spec((1, D)),
        ],
        out_specs=pl.BlockSpec((None, tm, D), lambda b, i: (b, i, 0)),
        out_shape=jax.ShapeDtypeStruct((B, S, D), _F32),
        scratch_shapes=[
            pltpu.VMEM((blk + tm, kv_heads * s_hd), _BF16),
            pltpu.VMEM((kv_heads * (s_hd + bf16_rows), blk + tm), _BF16),
            pltpu.VMEM((q_heads * s_hd, tm), _BF16),
            pltpu.VMEM((kv_heads, 2 * blk, group * blk), _F32),
            pltpu.VMEM((n_units, 2 * blk, group * blk), _F32),
            pltpu.VMEM((n_units, 2 * blk, group * blk), _BF16),
            pltpu.VMEM((n_units, 1, group * blk), _F32),
            pltpu.VMEM((q_heads * s_hd, tm), _BF16),
        ] + mlp_scratch,
        compiler_params=cparams2,
        name="swa_mlp",
    )(x1.reshape(B, S, D), wqT1, wk1, wvT1, swa_w_o[0].astype(_BF16), bucketT,
      rel_bias.astype(_F32), swa_sinks[0].astype(_F32),
      _row(ln_mix_g[1]), _row(ln_mix_b[1]), mlp_w_up[1].astype(_BF16),
      mlp_w_down[1].astype(_BF16), _row(ln_mlp_g[1]), _row(ln_mlp_b[1]))
    return out
```

```python
import functools
import math

import numpy as np
import jax
import jax.numpy as jnp
from jax import lax
from jax.experimental import pallas as pl
from jax.experimental.pallas import tpu as pltpu

_F32 = jnp.float32
_BF16 = jnp.bfloat16

_LN_EPS = 1e-5
_RMS_EPS = 1e-6
_ROPE_THETA = 10000.0
_SWA_BLOCK = 128
_REL_BUCKETS = 32
_REL_MAX_DIST = 128
_LOG2E = math.log2(math.e)

_V7X_VMEM_BYTES = 64 * 1024 * 1024
_VMEM_LIMIT_BYTES = _V7X_VMEM_BYTES - 8 * 1024 * 1024
_LANES = 128

_TOKEN_TILE = 512
_ATTN_TILE = 1024
_ATTN_COL_BLOCK = 256
_FF_CHUNK = 1024
_MLP_ROW_BLOCK = 256

_NT = (((1,), (1,)), ((), ()))
_TN = (((0,), (0,)), ((), ()))


def _const_spec(shape):
    nd = len(shape)
    return pl.BlockSpec(shape, lambda *_: (0,) * nd, pipeline_mode=pl.Buffered(1))


def _layernorm(v, g, b):
    mu = jnp.mean(v, axis=-1, keepdims=True)
    d = v - mu
    var = jnp.mean(d * d, axis=-1, keepdims=True)
    return d * lax.rsqrt(var + _LN_EPS) * g + b


def _rmsnorm(v, g):
    return v * lax.rsqrt(jnp.mean(v * v, axis=-1, keepdims=True) + _RMS_EPS) * g


def _mix_mlp_ln(mix_rows, x_ref, alpha, g1_ref, b1_ref, wup_ref, wdown_ref, g2_ref, b2_ref,
                out_ref, x1_scr, h_scr):
    tm = x_ref.shape[0]
    d_ff = wup_ref.shape[1]
    blocks = [slice(r, r + _MLP_ROW_BLOCK) for r in range(0, tm, _MLP_ROW_BLOCK)]
    for rows in blocks:
        x1_scr[rows, :] = _layernorm(alpha * x_ref[rows, :] + mix_rows(rows), g1_ref[...], b1_ref[...])
    for rows in blocks:
        x1b = x1_scr[rows, :].astype(_BF16)
        for lo in range(0, d_ff, _FF_CHUNK):
            hmid = jnp.dot(x1b, wup_ref[:, lo:lo + _FF_CHUNK], preferred_element_type=_F32)
            hmid = jnp.maximum(hmid, 0.0)
            h_scr[rows, lo:lo + _FF_CHUNK] = (hmid * hmid).astype(_BF16)
    for rows in blocks:
        y = jnp.dot(h_scr[rows, :], wdown_ref[...], preferred_element_type=_F32)
        out_ref[rows, :] = _layernorm(alpha * x1_scr[rows, :] + y, g2_ref[...], b2_ref[...])


def _mla_proj_kernel(x_ref, win_ref, gq_ref, gkv_ref, wqT_ref, wuk_ref, wvT_ref,
                     cs_ref, cosT_ref, sinT_ref, qT_ref, k_ref, vT_ref,
                     *, q_rank, kv_rank, heads, nope, rope, q_scale):
    xb = x_ref[...].astype(_BF16)
    h = jnp.dot(xb, win_ref[...], preferred_element_type=_F32)
    cq = _rmsnorm(h[:, :q_rank], gq_ref[...]).astype(_BF16)
    ckv = _rmsnorm(h[:, q_rank:q_rank + kv_rank], gkv_ref[...]).astype(_BF16)
    t = h[:, q_rank + kv_rank:] * cs_ref[...]
    kr = (t + pltpu.roll(t, rope, 1))[:, :rope].astype(_BF16)

    half = rope // 2
    hd = nope + rope
    qT = lax.dot_general(wqT_ref[...], cq, _NT, preferred_element_type=_F32)
    cosT = cosT_ref[...]
    sinT = sinT_ref[...]
    for hh in range(heads):
        base = hh * hd
        qT_ref[hh, 0:nope, :] = (qT[base:base + nope] * q_scale).astype(_BF16)
        x1 = qT[base + nope:base + nope + half]
        x2 = qT[base + nope + half:base + hd]
        qT_ref[hh, nope:nope + half, :] = ((x1 * cosT - x2 * sinT) * q_scale).astype(_BF16)
        qT_ref[hh, nope + half:hd, :] = ((x1 * sinT + x2 * cosT) * q_scale).astype(_BF16)

    kn = jnp.dot(ckv, wuk_ref[...], preferred_element_type=_F32)
    for hh in range(heads):
        k_ref[hh, :, 0:nope] = kn[:, hh * nope:(hh + 1) * nope].astype(_BF16)
        k_ref[hh, :, nope:hd] = kr

    vdim = wvT_ref.shape[0] // heads
    vT = lax.dot_general(wvT_ref[...], ckv, _NT, preferred_element_type=_F32)
    tk = vT_ref.shape[3]
    pad = vT_ref.shape[2] - vdim
    ones_rows = (lax.broadcasted_iota(jnp.int32, (pad, tk), 0) == 0).astype(_BF16)
    for hh in range(heads):
        for j in range(vT_ref.shape[1]):
            vT_ref[hh, j, 0:vdim, :] = vT[hh * vdim:(hh + 1) * vdim, j * tk:(j + 1) * tk].astype(_BF16)
            vT_ref[hh, j, vdim:vdim + pad, :] = ones_rows


def _mla_attn_kernel(qT_ref, k_ref, vT_ref, o_ref, m_scr, acc_scr,
                     s00, s01, s10, s11, cm00, cm01, cm10, cm11, p_a, p_b, al_a, al_b):
    tk, tq = s00.shape
    tsub = qT_ref.shape[2]
    nsub = tq // tsub
    n_q = qT_ref.shape[0] // nsub
    assert tq == 2 * tk and tk == vT_ref.shape[2]

    cb = _ATTN_COL_BLOCK

    def q_stage(qi, c, s_ref, cm_ref, col_lo=0):
        k = k_ref[pl.ds(pl.multiple_of(c * tk, tk), tk), :]
        for lo in range(col_lo, tq, cb):
            j, off = divmod(lo, tsub)
            qT = qT_ref[qi * nsub + j, :, off:off + cb]
            s = jnp.dot(k, qT, preferred_element_type=_F32)
            s_ref[:, lo:lo + cb] = s
            cm_ref[:, lo:lo + cb] = jnp.max(s, axis=0, keepdims=True)

    def x_stage(s_ref, cm_ref, p_ref, al_ref, mask_off=None, col_lo=0):
        for lo in range(col_lo, tq, cb):
            cols = slice(lo, lo + cb)
            s = s_ref[:, cols]
            if mask_off is None:
                cm = cm_ref[:, cols]
            else:
                krel = lax.broadcasted_iota(jnp.int32, s.shape, 0) + mask_off
                qrel = lax.broadcasted_iota(jnp.int32, s.shape, 1) + lo
                s = jnp.where(krel <= qrel, s, -jnp.inf)
                cm = jnp.max(s, axis=0, keepdims=True)
            m_prev = m_scr[:, cols]
            m_new = jnp.maximum(m_prev, cm)
            alpha = jnp.exp2(m_prev - m_new)
            p = jnp.exp2(s - m_new)
            p_ref[:, cols] = p.astype(_BF16)
            al_ref[:, cols] = alpha
            m_scr[:, cols] = m_new

    def v_stage(c, p_ref, al_ref, col_lo=0):
        for lo in range(col_lo, tq, cb):
            cols = slice(lo, lo + cb)
            acc_scr[:, cols] = al_ref[:, cols] * acc_scr[:, cols] + jnp.dot(
                vT_ref[c], p_ref[:, cols], preferred_element_type=_F32)

    s_bufs = (((s00, cm00), (s01, cm01)), ((s10, cm10), (s11, cm11)))

    def start_tile():
        m_scr[...] = jnp.full(m_scr.shape, -jnp.inf, _F32)
        acc_scr[...] = jnp.zeros(acc_scr.shape, _F32)
        p_b[...] = jnp.zeros(p_b.shape, p_b.dtype)
        al_b[...] = jnp.ones(al_b.shape, _F32)

    def regular_step(qi, a, par):
        (sa, cma), (sb, cmb) = s_bufs[par]
        (na, ncma), (nb, ncmb) = s_bufs[1 - par]
        q_stage(qi, a + 2, na, ncma)
        x_stage(sa, cma, p_a, al_a)
        v_stage(jnp.maximum(a - 1, 0), p_b, al_b)
        q_stage(qi, a + 3, nb, ncmb)
        x_stage(sb, cmb, p_b, al_b)
        v_stage(a, p_a, al_a)

    def diagonal_step(qi, par):
        a = 2 * qi
        (sa, cma), (sb, cmb) = s_bufs[par]
        (na, ncma), (nb, ncmb) = s_bufs[1 - par]
        q_next = jnp.minimum(qi + 1, n_q - 1)
        q_stage(q_next, 0, na, ncma)
        x_stage(sa, cma, p_a, al_a, mask_off=0)
        v_stage(jnp.maximum(a - 1, 0), p_b, al_b)
        q_stage(q_next, 1, nb, ncmb)
        x_stage(sb, cmb, p_b, al_b, mask_off=tk, col_lo=tk)
        v_stage(a, p_a, al_a)
        v_stage(a + 1, p_b, al_b, col_lo=tk)
        vdim = o_ref.shape[1]
        inv_l = 1.0 / acc_scr[vdim:vdim + 1, :]
        o = (acc_scr[0:vdim, :] * inv_l).T
        o_ref[pl.ds(pl.multiple_of(qi * tq, tq), tq), :] = o.astype(o_ref.dtype)
        start_tile()

    def by_parity(par, fn):
        for static_par in range(2):
            @pl.when(par == static_par)
            def _():
                fn(static_par)

    def tile_body(qi, par):
        def step_body(u, par):
            by_parity(par, functools.partial(regular_step, qi, 2 * u))
            return 1 - par

        par = lax.fori_loop(0, qi, step_body, par)
        by_parity(par, functools.partial(diagonal_step, qi))
        return 1 - par

    start_tile()
    q_stage(0, 0, s00, cm00)
    q_stage(0, 1, s01, cm01)
    lax.fori_loop(0, n_q, tile_body, jnp.int32(0))


def _outproj_mlp_kernel(o_ref, x_ref, wo_ref, g1_ref, b1_ref, wup_ref, wdown_ref,
                        g2_ref, b2_ref, out_ref, x1_scr, h_scr, *, alpha):
    def mix_rows(rows):
        return jnp.dot(o_ref[rows, :], wo_ref[...], preferred_element_type=_F32)

    _mix_mlp_ln(mix_rows, x_ref, alpha, g1_ref, b1_ref, wup_ref, wdown_ref, g2_ref, b2_ref,
                out_ref, x1_scr, h_scr)


def _swa_mlp_kernel(x_ref, wqT_ref, wk_ref, wvT_ref, wo_ref, bucketT_ref, relb_ref, sink_ref,
                    g1_ref, b1_ref, wup_ref, wdown_ref, g2_ref, b2_ref, out_ref,
                    k_scr, vTe_scr, qT_scr, biasT_scr, s_scr, p_scr, m_scr, oT_scr, x1_scr, h_scr,
                    *, alpha, q_heads, kv_heads, head_dim, q_scale):
    blk = _SWA_BLOCK
    group = q_heads // kv_heads
    tm = x_ref.shape[0]
    n_blk = tm // blk
    vrows = vTe_scr.shape[0] // kv_heads
    first_tile = pl.program_id(1) == 0
    always = pl.program_id(1) >= 0

    @pl.when((pl.program_id(0) == 0) & first_tile)
    def _():
        bkt = bucketT_ref[...]
        for hq in range(q_heads):
            bias = jnp.full(bkt.shape, -jnp.inf, _F32)
            for bb in range(_REL_BUCKETS):
                bias = jnp.where(bkt == bb, relb_ref[bb, hq] * _LOG2E, bias)
            g, gi = divmod(hq, group)
            biasT_scr[g, :, gi * blk:(gi + 1) * blk] = bias

    @pl.when(first_tile)
    def _():
        k_scr[0:blk, :] = jnp.zeros((blk, k_scr.shape[1]), k_scr.dtype)
        vTe_scr[:, 0:blk] = jnp.zeros((vTe_scr.shape[0], blk), vTe_scr.dtype)

    xb = x_ref[...].astype(_BF16)
    qT_scr[...] = (lax.dot_general(wqT_ref[...], xb, _NT, preferred_element_type=_F32)
                   * q_scale).astype(_BF16)
    k_scr[blk:blk + tm, :] = jnp.dot(xb, wk_ref[...], preferred_element_type=_F32).astype(_BF16)
    vT = lax.dot_general(wvT_ref[...], xb, _NT, preferred_element_type=_F32).astype(_BF16)
    ones_rows = (lax.broadcasted_iota(jnp.int32, (vrows - head_dim, tm), 0) == 0).astype(_BF16)
    for g in range(kv_heads):
        vTe_scr[g * vrows:g * vrows + head_dim, blk:blk + tm] = vT[g * head_dim:(g + 1) * head_dim]
        vTe_scr[g * vrows + head_dim:(g + 1) * vrows, blk:blk + tm] = ones_rows

    first_pen = jnp.where(first_tile, -jnp.inf, 0.0).astype(_F32)
    zeros_q = jnp.zeros((head_dim, group * blk), _BF16)
    heads_per_lane_tile = _LANES // head_dim
    units = [(n, g) for n in range(n_blk) for g in range(kv_heads)]

    def sink_row(g):
        return jnp.concatenate(
            [jnp.full((1, blk), sink_ref[g * group + gi] * _LOG2E, _F32) for gi in range(group)],
            axis=1)

    @pl.when(always)
    def _():
        sinks = [sink_row(g) for g in range(kv_heads)]
        for u, (n, g) in enumerate(units):
            lt = g // heads_per_lane_tile
            kpair = k_scr[n * blk:(n + 2) * blk, lt * _LANES:(lt + 1) * _LANES]
            qg = jnp.concatenate(
                [qT_scr[(g * group + gi) * head_dim:(g * group + gi + 1) * head_dim,
                        n * blk:(n + 1) * blk] for gi in range(group)], axis=1)
            pieces = [zeros_q] * heads_per_lane_tile
            pieces[g % heads_per_lane_tile] = qg
            qpad = jnp.concatenate(pieces, axis=0)
            s = jnp.dot(kpair, qpad, preferred_element_type=_F32) + biasT_scr[g]
            if n == 0:
                s = jnp.concatenate([s[:blk] + first_pen, s[blk:]], axis=0)
            s_scr[u] = s
            m_scr[u] = jnp.maximum(jnp.max(s, axis=0, keepdims=True), sinks[g])

    @pl.when(always)
    def _():
        for u in range(len(units)):
            p_scr[u] = jnp.exp2(s_scr[u] - m_scr[u]).astype(_BF16)

    @pl.when(always)
    def _():
        sinks = [sink_row(g) for g in range(kv_heads)]
        for u, (n, g) in enumerate(units):
            oT = jnp.dot(vTe_scr[g * vrows:(g + 1) * vrows, n * blk:(n + 2) * blk], p_scr[u],
                         preferred_element_type=_F32)
            denom = oT[head_dim:head_dim + 1] + jnp.exp2(sinks[g] - m_scr[u])
            o = (oT[:head_dim] * (1.0 / denom)).astype(_BF16)
            for gi in range(group):
                hq = g * group + gi
                oT_scr[hq * head_dim:(hq + 1) * head_dim, n * blk:(n + 1) * blk] = (
                    o[:, gi * blk:(gi + 1) * blk])
        k_scr[0:blk, :] = k_scr[tm:tm + blk, :]
        vTe_scr[:, 0:blk] = vTe_scr[:, tm:tm + blk]

    def mix_rows(rows):
        return lax.dot_general(oT_scr[:, rows], wo_ref[...], _TN, preferred_element_type=_F32)

    _mix_mlp_ln(mix_rows, x_ref, alpha, g1_ref, b1_ref, wup_ref, wdown_ref, g2_ref, b2_ref,
                out_ref, x1_scr, h_scr)


def _t5_bucket_table(blk):
    i = np.arange(blk)[None, :]
    j = np.arange(2 * blk)[:, None]
    dist = i + blk - j
    max_exact = _REL_BUCKETS // 2
    nf = np.maximum(dist, 1).astype(np.float32)
    large = max_exact + (np.log(nf / np.float32(max_exact))
                         / np.float32(math.log(_REL_MAX_DIST / max_exact))
                         * np.float32(_REL_BUCKETS - max_exact)).astype(np.int32)
    large = np.minimum(large, _REL_BUCKETS - 1)
    bucket = np.where(dist < max_exact, np.maximum(dist, 0), large)
    valid = (dist >= 0) & (dist < blk)
    return np.where(valid, bucket, -1).astype(np.int32)


def _rope_tables(seq, rope):
    half = rope // 2
    inv = _ROPE_THETA ** (-jnp.arange(half, dtype=_F32) / half)
    ang = jnp.arange(seq).astype(_F32)[:, None] * inv[None, :]
    cos, sin = jnp.cos(ang), jnp.sin(ang)
    cs = jnp.concatenate([cos, cos, -sin, sin], axis=1)
    return cs, cos.T, sin.T


def _row(v):
    return v.reshape(1, -1).astype(_F32)


def kernel(x, mla_w_in, mla_g_q, mla_g_kv, mla_w_uq, mla_w_uk, mla_w_uv, mla_w_o, kv_w_shared,
           swa_w_q, swa_sinks, swa_w_o, rel_bias, mlp_w_up, mlp_w_down, ln_mix_g, ln_mix_b,
           ln_mlp_g, ln_mlp_b):
    B, S, D = x.shape
    depth = mlp_w_up.shape[0]
    assert depth == 2 and mla_w_in.shape[0] == 1 and swa_w_q.shape[0] == 1
    alpha = (2 * depth) ** 0.25
    T = B * S
    tm = _TOKEN_TILE
    ta = _ATTN_TILE
    tk = ta // 2
    bf16_rows = 16
    assert ta % tm == 0 and tm % tk == 0 and S % ta == 0

    q_rank, heads, hd = mla_w_uq.shape[1:]
    kv_rank, _, nope = mla_w_uk.shape[1:]
    vdim = mla_w_uv.shape[3]
    vrows = vdim + bf16_rows
    rope = hd - nope
    half = rope // 2
    assert 2 * rope == _LANES and mla_w_in.shape[2] == q_rank + kv_rank + rope

    w_in = mla_w_in[0]
    r0 = q_rank + kv_rank
    w_in_ext = jnp.concatenate(
        [w_in, w_in[:, r0 + half:r0 + rope], w_in[:, r0:r0 + half]], axis=1).astype(_BF16)
    wqT = mla_w_uq[0].reshape(q_rank, heads * hd).T.astype(_BF16)
    wuk = mla_w_uk[0].reshape(kv_rank, heads * nope).astype(_BF16)
    wvT = mla_w_uv[0].reshape(kv_rank, heads * vdim).T.astype(_BF16)
    cs, cosT, sinT = _rope_tables(S, rope)
    n_t = S // tm
    q_scale = hd ** -0.5 * _LOG2E

    cparams2 = pltpu.CompilerParams(dimension_semantics=("arbitrary", "arbitrary"),
                                    vmem_limit_bytes=_VMEM_LIMIT_BYTES)
    cparams1 = pltpu.CompilerParams(dimension_semantics=("arbitrary",),
                                    vmem_limit_bytes=_VMEM_LIMIT_BYTES)

    qT_all, k_all, vT_all = pl.pallas_call(
        functools.partial(_mla_proj_kernel, q_rank=q_rank, kv_rank=kv_rank, heads=heads,
                          nope=nope, rope=rope, q_scale=q_scale),
        grid=(B, n_t),
        in_specs=[
            pl.BlockSpec((None, tm, D), lambda b, i: (b, i, 0)),
            _const_spec(w_in_ext.shape),
            _const_spec((1, q_rank)),
            _const_spec((1, kv_rank)),
            _const_spec(wqT.shape),
            _const_spec(wuk.shape),
            _const_spec(wvT.shape),
            pl.BlockSpec((tm, 2 * rope), lambda b, i: (i, 0)),
            pl.BlockSpec((half, tm), lambda b, i: (0, i)),
            pl.BlockSpec((half, tm), lambda b, i: (0, i)),
        ],
        out_specs=[
            pl.BlockSpec((None, heads, None, hd, tm), lambda b, i: (b, 0, i, 0, 0)),
            pl.BlockSpec((None, heads, tm, hd), lambda b, i: (b, 0, i, 0)),
            pl.BlockSpec((None, heads, tm // tk, vrows, tk), lambda b, i: (b, 0, i, 0, 0)),
        ],
        out_shape=[
            jax.ShapeDtypeStruct((B, heads, n_t, hd, tm), _BF16),
            jax.ShapeDtypeStruct((B, heads, S, hd), _BF16),
            jax.ShapeDtypeStruct((B, heads, S // tk, vrows, tk), _BF16),
        ],
        compiler_params=cparams2,
        name="mla_proj",
    )(x, w_in_ext, _row(mla_g_q[0]), _row(mla_g_kv[0]), wqT, wuk, wvT, cs, cosT, sinT)

    o = pl.pallas_call(
        _mla_attn_kernel,
        grid=(B, heads),
        in_specs=[
            pl.BlockSpec((None, None, n_t, hd, tm), lambda b, h: (b, h, 0, 0, 0)),
            pl.BlockSpec((None, None, S, hd), lambda b, h: (b, h, 0, 0)),
            pl.BlockSpec((None, None, S // tk, vrows, tk), lambda b, h: (b, h, 0, 0, 0)),
        ],
        out_specs=pl.BlockSpec((None, S, vdim), lambda b, h: (b, 0, h)),
        out_shape=jax.ShapeDtypeStruct((B, S, heads * vdim), _BF16),
        scratch_shapes=[
            pltpu.VMEM((1, ta), _F32),
            pltpu.VMEM((vrows, ta), _F32),
        ] + [pltpu.VMEM((tk, ta), _F32)] * 4
          + [pltpu.VMEM((1, ta), _F32)] * 4
          + [pltpu.VMEM((tk, ta), _BF16)] * 2
          + [pltpu.VMEM((1, ta), _F32)] * 2,
        compiler_params=cparams2,
        name="mla_attn",
    )(qT_all, k_all, vT_all)

    d_ff = mlp_w_up.shape[2]
    mlp_scratch = [
        pltpu.VMEM((tm, D), _F32),
        pltpu.VMEM((tm, d_ff), _BF16),
    ]
    x1 = pl.pallas_call(
        functools.partial(_outproj_mlp_kernel, alpha=alpha),
        grid=(T // tm,),
        in_specs=[
            pl.BlockSpec((tm, heads * vdim), lambda i: (i, 0)),
            pl.BlockSpec((tm, D), lambda i: (i, 0)),
            _const_spec((heads * vdim, D)),
            _const_spec((1, D)),
            _const_spec((1, D)),
            _const_spec((D, d_ff)),
            _const_spec((d_ff, D)),
            _const_spec((1, D)),
            _const_spec((1, D)),
        ],
        out_specs=pl.BlockSpec((tm, D), lambda i: (i, 0)),
        out_shape=jax.ShapeDtypeStruct((T, D), _F32),
        scratch_shapes=mlp_scratch,
        compiler_params=cparams1,
        name="mla_out_mlp",
    )(o.reshape(T, heads * vdim), x.reshape(T, D), mla_w_o[0].astype(_BF16),
      _row(ln_mix_g[0]), _row(ln_mix_b[0]), mlp_w_up[0].astype(_BF16),
      mlp_w_down[0].astype(_BF16), _row(ln_mlp_g[0]), _row(ln_mlp_b[0]))

    q_heads = swa_sinks.shape[1]
    s_hd = swa_w_q.shape[2] // q_heads
    kv_heads = kv_w_shared.shape[1] // (2 * s_hd)
    group = q_heads // kv_heads
    blk = _SWA_BLOCK
    assert _LANES % s_hd == 0 and tm % blk == 0
    wqT1 = swa_w_q[0].T.astype(_BF16)
    wk1 = kv_w_shared[:, :kv_heads * s_hd].astype(_BF16)
    wvT1 = kv_w_shared[:, kv_heads * s_hd:].T.astype(_BF16)
    bucketT = jnp.asarray(_t5_bucket_table(blk))
    n_units = (tm // blk) * kv_heads

    smem = pl.BlockSpec(memory_space=pltpu.SMEM)
    out = pl.pallas_call(
        functools.partial(_swa_mlp_kernel, alpha=alpha, q_heads=q_heads, kv_heads=kv_heads,
                          head_dim=s_hd, q_scale=s_hd ** -0.5 * _LOG2E),
        grid=(B, n_t),
        in_specs=[
            pl.BlockSpec((None, tm, D), lambda b, i: (b, i, 0)),
            _const_spec(wqT1.shape),
            _const_spec(wk1.shape),
            _const_spec(wvT1.shape),
            _const_spec((q_heads * s_hd, D)),
            _const_spec(bucketT.shape),
            smem,
            smem,
            _const_spec((1, D)),
            _const_spec((1, D)),
            _const_spec((D, d_ff)),
            _const_spec((d_ff, D)),
            _const_spec((1, D)),
            _const_spec((1, D)),
        ],
        out_specs=pl.BlockSpec((None, tm, D), lambda b, i: (b, i, 0)),
        out_shape=jax.ShapeDtypeStruct((B, S, D), _F32),
        scratch_shapes=[
            pltpu.VMEM((blk + tm, kv_heads * s_hd), _BF16),
            pltpu.VMEM((kv_heads * (s_hd + bf16_rows), blk + tm), _BF16),
            pltpu.VMEM((q_heads * s_hd, tm), _BF16),
            pltpu.VMEM((kv_heads, 2 * blk, group * blk), _F32),
            pltpu.VMEM((n_units, 2 * blk, group * blk), _F32),
            pltpu.VMEM((n_units, 2 * blk, group * blk), _BF16),
            pltpu.VMEM((n_units, 1, group * blk), _F32),
            pltpu.VMEM((q_heads * s_hd, tm), _BF16),
        ] + mlp_scratch,
        compiler_params=cparams2,
        name="swa_mlp",
    )(x1.reshape(B, S, D), wqT1, wk1, wvT1, swa_w_o[0].astype(_BF16), bucketT,
      rel_bias.astype(_F32), swa_sinks[0].astype(_F32),
      _row(ln_mix_g[1]), _row(ln_mix_b[1]), mlp_w_up[1].astype(_BF16),
      mlp_w_down[1].astype(_BF16), _row(ln_mlp_g[1]), _row(ln_mlp_b[1]))
    return out
```

```python
import functools
import math

import numpy as np
import jax
import jax.numpy as jnp
from jax import lax
from jax.experimental import pallas as pl
from jax.experimental.pallas import tpu as pltpu

_F32 = jnp.float32
_BF16 = jnp.bfloat16

_LN_EPS = 1e-5
_RMS_EPS = 1e-6
_ROPE_THETA = 10000.0
_SWA_BLOCK = 128
_REL_BUCKETS = 32
_REL_MAX_DIST = 128
_LOG2E = math.log2(math.e)

_V7X_VMEM_BYTES = 64 * 1024 * 1024
_VMEM_LIMIT_BYTES = _V7X_VMEM_BYTES - 8 * 1024 * 1024
_LANES = 128

_TOKEN_TILE = 512
_ATTN_TILE = 1024
_ATTN_COL_BLOCK = 256
_FF_CHUNK = 1024
_MLP_ROW_BLOCK = 256

_NT = (((1,), (1,)), ((), ()))
_TN = (((0,), (0,)), ((), ()))


def _const_spec(shape):
    nd = len(shape)
    return pl.BlockSpec(shape, lambda *_: (0,) * nd, pipeline_mode=pl.Buffered(1))


def _layernorm(v, g, b):
    mu = jnp.mean(v, axis=-1, keepdims=True)
    d = v - mu
    var = jnp.mean(d * d, axis=-1, keepdims=True)
    return d * lax.rsqrt(var + _LN_EPS) * g + b


def _rmsnorm(v, g):
    return v * lax.rsqrt(jnp.mean(v * v, axis=-1, keepdims=True) + _RMS_EPS) * g


def _mix_mlp_ln(mix_rows, x_ref, alpha, g1_ref, b1_ref, wup_ref, wdown_ref, g2_ref, b2_ref,
                out_ref, x1_scr, h_scr):
    tm = x_ref.shape[0]
    d_ff = wup_ref.shape[1]
    blocks = [slice(r, r + _MLP_ROW_BLOCK) for r in range(0, tm, _MLP_ROW_BLOCK)]
    for rows in blocks:
        x1_scr[rows, :] = _layernorm(alpha * x_ref[rows, :] + mix_rows(rows), g1_ref[...], b1_ref[...])
    for rows in blocks:
        x1b = x1_scr[rows, :].astype(_BF16)
        for lo in range(0, d_ff, _FF_CHUNK):
            hmid = jnp.dot(x1b, wup_ref[:, lo:lo + _FF_CHUNK], preferred_element_type=_F32)
            hmid = jnp.maximum(hmid, 0.0)
            h_scr[rows, lo:lo + _FF_CHUNK] = (hmid * hmid).astype(_BF16)
    for rows in blocks:
        y = jnp.dot(h_scr[rows, :], wdown_ref[...], preferred_element_type=_F32)
        out_ref[rows, :] = _layernorm(alpha * x1_scr[rows, :] + y, g2_ref[...], b2_ref[...])


def _mla_proj_kernel(x_ref, win_ref, gq_ref, gkv_ref, wqT_ref, wuk_ref, wvT_ref,
                     cs_ref, cosT_ref, sinT_ref, qT_ref, k_ref, vT_ref,
                     *, q_rank, kv_rank, heads, nope, rope, q_scale):
    xb = x_ref[...].astype(_BF16)
    h = jnp.dot(xb, win_ref[...], preferred_element_type=_F32)
    cq = _rmsnorm(h[:, :q_rank], gq_ref[...]).astype(_BF16)
    ckv = _rmsnorm(h[:, q_rank:q_rank + kv_rank], gkv_ref[...]).astype(_BF16)
    t = h[:, q_rank + kv_rank:] * cs_ref[...]
    kr = (t + pltpu.roll(t, rope, 1))[:, :rope].astype(_BF16)

    half = rope // 2
    hd = nope + rope
    qT = lax.dot_general(wqT_ref[...], cq, _NT, preferred_element_type=_F32)
    cosT = cosT_ref[...]
    sinT = sinT_ref[...]
    for hh in range(heads):
        base = hh * hd
        qT_ref[hh, 0:nope, :] = (qT[base:base + nope] * q_scale).astype(_BF16)
        x1 = qT[base + nope:base + nope + half]
        x2 = qT[base + nope + half:base + hd]
        qT_ref[hh, nope:nope + half, :] = ((x1 * cosT - x2 * sinT) * q_scale).astype(_BF16)
        qT_ref[hh, nope + half:hd, :] = ((x1 * sinT + x2 * cosT) * q_scale).astype(_BF16)

    kn = jnp.dot(ckv, wuk_ref[...], preferred_element_type=_F32)
    for hh in range(heads):
        k_ref[hh, :, 0:nope] = kn[:, hh * nope:(hh + 1) * nope].astype(_BF16)
        k_ref[hh, :, nope:hd] = kr

    vdim = wvT_ref.shape[0] // heads
    vT = lax.dot_general(wvT_ref[...], ckv, _NT, preferred_element_type=_F32)
    tk = vT_ref.shape[3]
    pad = vT_ref.shape[2] - vdim
    ones_rows = (lax.broadcasted_iota(jnp.int32, (pad, tk), 0) == 0).astype(_BF16)
    for hh in range(heads):
        for j in range(vT_ref.shape[1]):
            vT_ref[hh, j, 0:vdim, :] = vT[hh * vdim:(hh + 1) * vdim, j * tk:(j + 1) * tk].astype(_BF16)
            vT_ref[hh, j, vdim:vdim + pad, :] = ones_rows


def _mla_attn_kernel(qT_ref, k_ref, vT_ref, o_ref, m_scr, acc_scr,
                     s00, s01, s10, s11, cm00, cm01, cm10, cm11, p_a, p_b, al_a, al_b):
    tk, tq = s00.shape[0], acc_scr.shape[1]
    tsub = qT_ref.shape[2]
    nsub = tq // tsub
    n_q = qT_ref.shape[0] // nsub
    assert tq == 2 * tk and tk == vT_ref.shape[2]

    cb = _ATTN_COL_BLOCK

    def q_stage(qi, c, s_ref, cm_ref, col_lo=0):
        k = k_ref[pl.ds(pl.multiple_of(c * tk, tk), tk), :]
        for lo in range(col_lo, tq, cb):
            j, off = divmod(lo, tsub)
            qT = qT_ref[qi * nsub + j, :, off:off + cb]
            s = jnp.dot(k, qT, preferred_element_type=_F32)
            s_ref[:, lo:lo + cb] = s
            cm_ref[:, lo:lo + cb] = jnp.max(s, axis=0, keepdims=True)

    def x_stage(s_ref, cm_ref, p_ref, al_ref, mask_off=None, col_lo=0):
        for lo in range(col_lo, tq, cb):
            cols = slice(lo, lo + cb)
            s = s_ref[:, cols]
            if mask_off is None:
                cm = cm_ref[:, cols]
            else:
                krel = lax.broadcasted_iota(jnp.int32, s.shape, 0) + mask_off
                qrel = lax.broadcasted_iota(jnp.int32, s.shape, 1) + lo
                s = jnp.where(krel <= qrel, s, -jnp.inf)
                cm = jnp.max(s, axis=0, keepdims=True)
            m_prev = m_scr[:, cols]
            m_new = jnp.maximum(m_prev, cm)
            alpha = jnp.exp2(m_prev - m_new)
            p = jnp.exp2(s - m_new)
            p_ref[:, cols] = p.astype(_BF16)
            al_ref[:, cols] = alpha
            m_scr[:, cols] = m_new

    def v_stage(c, p_ref, al_ref, col_lo=0):
        for lo in range(col_lo, tq, cb):
            cols = slice(lo, lo + cb)
            acc_scr[:, cols] = al_ref[:, cols] * acc_scr[:, cols] + jnp.dot(
                vT_ref[c], p_ref[:, cols], preferred_element_type=_F32)

    s_bufs = (((s00, cm00), (s01, cm01)), ((s10, cm10), (s11, cm11)))

    def start_tile():
        m_scr[...] = jnp.full(m_scr.shape, -jnp.inf, _F32)
        acc_scr[...] = jnp.zeros(acc_scr.shape, _F32)
        p_b[...] = jnp.zeros(p_b.shape, p_b.dtype)
        al_b[...] = jnp.ones(al_b.shape, _F32)

    def regular_step(qi, a, par):
        (sa, cma), (sb, cmb) = s_bufs[par]
        (na, ncma), (nb, ncmb) = s_bufs[1 - par]
        q_stage(qi, a + 2, na, ncma)
        x_stage(sa, cma, p_a, al_a)
        v_stage(jnp.maximum(a - 1, 0), p_b, al_b)
        q_stage(qi, a + 3, nb, ncmb)
        x_stage(sb, cmb, p_b, al_b)
        v_stage(a, p_a, al_a)

    def diagonal_step(qi, par):
        a = 2 * qi
        (sa, cma), (sb, cmb) = s_bufs[par]
        (na, ncma), (nb, ncmb) = s_bufs[1 - par]
        q_next = jnp.minimum(qi + 1, n_q - 1)
        q_stage(q_next, 0, na, ncma)
        x_stage(sa, cma, p_a, al_a, mask_off=0)
        v_stage(jnp.maximum(a - 1, 0), p_b, al_b)
        q_stage(q_next, 1, nb, ncmb)
        x_stage(sb, cmb, p_b, al_b, mask_off=tk, col_lo=tk)
        v_stage(a, p_a, al_a)
        v_stage(a + 1, p_b, al_b, col_lo=tk)
        vdim = o_ref.shape[1]
        inv_l = 1.0 / acc_scr[vdim:vdim + 1, :]
        o = (acc_scr[0:vdim, :] * inv_l).T
        o_ref[pl.ds(pl.multiple_of(qi * tq, tq), tq), :] = o.astype(o_ref.dtype)
        start_tile()

    def by_parity(par, fn):
        for static_par in range(2):
            @pl.when(par == static_par)
            def _():
                fn(static_par)

    def tile_body(qi, par):
        def step_body(u, par):
            by_parity(par, functools.partial(regular_step, qi, 2 * u))
            return 1 - par

        par = lax.fori_loop(0, qi, step_body, par)
        by_parity(par, functools.partial(diagonal_step, qi))
        return 1 - par

    start_tile()
    q_stage(0, 0, s00, cm00)
    q_stage(0, 1, s01, cm01)
    lax.fori_loop(0, n_q, tile_body, jnp.int32(0))


def _outproj_mlp_kernel(o_ref, x_ref, wo_ref, g1_ref, b1_ref, wup_ref, wdown_ref,
                        g2_ref, b2_ref, out_ref, x1_scr, h_scr, *, alpha):
    def mix_rows(rows):
        return jnp.dot(o_ref[rows, :], wo_ref[...], preferred_element_type=_F32)

    _mix_mlp_ln(mix_rows, x_ref, alpha, g1_ref, b1_ref, wup_ref, wdown_ref, g2_ref, b2_ref,
                out_ref, x1_scr, h_scr)


def _swa_mlp_kernel(x_ref, wqT_ref, wk_ref, wvT_ref, wo_ref, bucketT_ref, relb_ref, sink_ref,
                    g1_ref, b1_ref, wup_ref, wdown_ref, g2_ref, b2_ref, out_ref,
                    k_scr, vTe_scr, qT_scr, biasT_scr, s_scr, p_scr, m_scr, oT_scr, x1_scr, h_scr,
                    *, alpha, q_heads, kv_heads, head_dim, q_scale):
    blk = _SWA_BLOCK
    group = q_heads // kv_heads
    tm = x_ref.shape[0]
    n_blk = tm // blk
    vrows = vTe_scr.shape[0] // kv_heads
    first_tile = pl.program_id(1) == 0
    always = pl.program_id(1) >= 0

    @pl.when((pl.program_id(0) == 0) & first_tile)
    def _():
        bkt = bucketT_ref[...]
        for hq in range(q_heads):
            bias = jnp.full(bkt.shape, -jnp.inf, _F32)
            for bb in range(_REL_BUCKETS):
                bias = jnp.where(bkt == bb, relb_ref[bb, hq] * _LOG2E, bias)
            g, gi = divmod(hq, group)
            biasT_scr[g, :, gi * blk:(gi + 1) * blk] = bias

    @pl.when(first_tile)
    def _():
        k_scr[0:blk, :] = jnp.zeros((blk, k_scr.shape[1]), k_scr.dtype)
        vTe_scr[:, 0:blk] = jnp.zeros((vTe_scr.shape[0], blk), vTe_scr.dtype)

    xb = x_ref[...].astype(_BF16)
    qT_scr[...] = (lax.dot_general(wqT_ref[...], xb, _NT, preferred_element_type=_F32)
                   * q_scale).astype(_BF16)
    k_scr[blk:blk + tm, :] = jnp.dot(xb, wk_ref[...], preferred_element_type=_F32).astype(_BF16)
    vT = lax.dot_general(wvT_ref[...], xb, _NT, preferred_element_type=_F32).astype(_BF16)
    ones_rows = (lax.broadcasted_iota(jnp.int32, (vrows - head_dim, tm), 0) == 0).astype(_BF16)
    for g in range(kv_heads):
        vTe_scr[g * vrows:g * vrows + head_dim, blk:blk + tm] = vT[g * head_dim:(g + 1) * head_dim]
        vTe_scr[g * vrows + head_dim:(g + 1) * vrows, blk:blk + tm] = ones_rows

    first_pen = jnp.where(first_tile, -jnp.inf, 0.0).astype(_F32)
    zeros_q = jnp.zeros((head_dim, group * blk), _BF16)
    heads_per_lane_tile = _LANES // head_dim
    units = [(n, g) for n in range(n_blk) for g in range(kv_heads)]

    def sink_row(g):
        return jnp.concatenate(
            [jnp.full((1, blk), sink_ref[g * group + gi] * _LOG2E, _F32) for gi in range(group)],
            axis=1)

    @pl.when(always)
    def _():
        sinks = [sink_row(g) for g in range(kv_heads)]
        for u, (n, g) in enumerate(units):
            lt = g // heads_per_lane_tile
            kpair = k_scr[n * blk:(n + 2) * blk, lt * _LANES:(lt + 1) * _LANES]
            qg = jnp.concatenate(
                [qT_scr[(g * group + gi) * head_dim:(g * group + gi + 1) * head_dim,
                        n * blk:(n + 1) * blk] for gi in range(group)], axis=1)
            pieces = [zeros_q] * heads_per_lane_tile
            pieces[g % heads_per_lane_tile] = qg
            qpad = jnp.concatenate(pieces, axis=0)
            s = jnp.dot(kpair, qpad, preferred_element_type=_F32) + biasT_scr[g]
            if n == 0:
                s = jnp.concatenate([s[:blk] + first_pen, s[blk:]], axis=0)
            s_scr[u] = s
            m_scr[u] = jnp.maximum(jnp.max(s, axis=0, keepdims=True), sinks[g])

    @pl.when(always)
    def _():
        for u in range(len(units)):
            p_scr[u] = jnp.exp2(s_scr[u] - m_scr[u]).astype(_BF16)

    @pl.when(always)
    def _():
        sinks = [sink_row(g) for g in range(kv_heads)]
        for u, (n, g) in enumerate(units):
            oT = jnp.dot(vTe_scr[g * vrows:(g + 1) * vrows, n * blk:(n + 2) * blk], p_scr[u],
                         preferred_element_type=_F32)
            denom = oT[head_dim:head_dim + 1] + jnp.exp2(sinks[g] - m_scr[u])
            o = (oT[:head_dim] * (1.0 / denom)).astype(_BF16)
            for gi in range(group):
                hq = g * group + gi
                oT_scr[hq * head_dim:(hq + 1) * head_dim, n * blk:(n + 1) * blk] = (
                    o[:, gi * blk:(gi + 1) * blk])
        k_scr[0:blk, :] = k_scr[tm:tm + blk, :]
        vTe_scr[:, 0:blk] = vTe_scr[:, tm:tm + blk]

    def mix_rows(rows):
        return lax.dot_general(oT_scr[:, rows], wo_ref[...], _TN, preferred_element_type=_F32)

    _mix_mlp_ln(mix_rows, x_ref, alpha, g1_ref, b1_ref, wup_ref, wdown_ref, g2_ref, b2_ref,
                out_ref, x1_scr, h_scr)


def _t5_bucket_table(blk):
    i = np.arange(blk)[None, :]
    j = np.arange(2 * blk)[:, None]
    dist = i + blk - j
    max_exact = _REL_BUCKETS // 2
    nf = np.maximum(dist, 1).astype(np.float32)
    large = max_exact + (np.log(nf / np.float32(max_exact))
                         / np.float32(math.log(_REL_MAX_DIST / max_exact))
                         * np.float32(_REL_BUCKETS - max_exact)).astype(np.int32)
    large = np.minimum(large, _REL_BUCKETS - 1)
    bucket = np.where(dist < max_exact, np.maximum(dist, 0), large)
    valid = (dist >= 0) & (dist < blk)
    return np.where(valid, bucket, -1).astype(np.int32)


def _rope_tables(seq, rope):
    half = rope // 2
    inv = _ROPE_THETA ** (-jnp.arange(half, dtype=_F32) / half)
    ang = jnp.arange(seq).astype(_F32)[:, None] * inv[None, :]
    cos, sin = jnp.cos(ang), jnp.sin(ang)
    cs = jnp.concatenate([cos, cos, -sin, sin], axis=1)
    return cs, cos.T, sin.T


def _row(v):
    return v.reshape(1, -1).astype(_F32)


def kernel(x, mla_w_in, mla_g_q, mla_g_kv, mla_w_uq, mla_w_uk, mla_w_uv, mla_w_o, kv_w_shared,
           swa_w_q, swa_sinks, swa_w_o, rel_bias, mlp_w_up, mlp_w_down, ln_mix_g, ln_mix_b,
           ln_mlp_g, ln_mlp_b):
    B, S, D = x.shape
    depth = mlp_w_up.shape[0]
    assert depth == 2 and mla_w_in.shape[0] == 1 and swa_w_q.shape[0] == 1
    alpha = (2 * depth) ** 0.25
    T = B * S
    tm = _TOKEN_TILE
    ta = _ATTN_TILE
    tk = ta // 2
    bf16_rows = 16
    ta_pad = ta + _LANES
    assert ta % tm == 0 and tm % tk == 0 and S % ta == 0

    q_rank, heads, hd = mla_w_uq.shape[1:]
    kv_rank, _, nope = mla_w_uk.shape[1:]
    vdim = mla_w_uv.shape[3]
    vrows = vdim + bf16_rows
    rope = hd - nope
    half = rope // 2
    assert 2 * rope == _LANES and mla_w_in.shape[2] == q_rank + kv_rank + rope

    w_in = mla_w_in[0]
    r0 = q_rank + kv_rank
    w_in_ext = jnp.concatenate(
        [w_in, w_in[:, r0 + half:r0 + rope], w_in[:, r0:r0 + half]], axis=1).astype(_BF16)
    wqT = mla_w_uq[0].reshape(q_rank, heads * hd).T.astype(_BF16)
    wuk = mla_w_uk[0].reshape(kv_rank, heads * nope).astype(_BF16)
    wvT = mla_w_uv[0].reshape(kv_rank, heads * vdim).T.astype(_BF16)
    cs, cosT, sinT = _rope_tables(S, rope)
    n_t = S // tm
    q_scale = hd ** -0.5 * _LOG2E

    cparams2 = pltpu.CompilerParams(dimension_semantics=("arbitrary", "arbitrary"),
                                    vmem_limit_bytes=_VMEM_LIMIT_BYTES)
    cparams1 = pltpu.CompilerParams(dimension_semantics=("arbitrary",),
                                    vmem_limit_bytes=_VMEM_LIMIT_BYTES)

    qT_all, k_all, vT_all = pl.pallas_call(
        functools.partial(_mla_proj_kernel, q_rank=q_rank, kv_rank=kv_rank, heads=heads,
                          nope=nope, rope=rope, q_scale=q_scale),
        grid=(B, n_t),
        in_specs=[
            pl.BlockSpec((None, tm, D), lambda b, i: (b, i, 0)),
            _const_spec(w_in_ext.shape),
            _const_spec((1, q_rank)),
            _const_spec((1, kv_rank)),
            _const_spec(wqT.shape),
            _const_spec(wuk.shape),
            _const_spec(wvT.shape),
            pl.BlockSpec((tm, 2 * rope), lambda b, i: (i, 0)),
            pl.BlockSpec((half, tm), lambda b, i: (0, i)),
            pl.BlockSpec((half, tm), lambda b, i: (0, i)),
        ],
        out_specs=[
            pl.BlockSpec((None, heads, None, hd, tm), lambda b, i: (b, 0, i, 0, 0)),
            pl.BlockSpec((None, heads, tm, hd), lambda b, i: (b, 0, i, 0)),
            pl.BlockSpec((None, heads, tm // tk, vrows, tk), lambda b, i: (b, 0, i, 0, 0)),
        ],
        out_shape=[
            jax.ShapeDtypeStruct((B, heads, n_t, hd, tm), _BF16),
            jax.ShapeDtypeStruct((B, heads, S, hd), _BF16),
            jax.ShapeDtypeStruct((B, heads, S // tk, vrows, tk), _BF16),
        ],
        compiler_params=cparams2,
        name="mla_proj",
    )(x, w_in_ext, _row(mla_g_q[0]), _row(mla_g_kv[0]), wqT, wuk, wvT, cs, cosT, sinT)

    o = pl.pallas_call(
        _mla_attn_kernel,
        grid=(B, heads),
        in_specs=[
            pl.BlockSpec((None, None, n_t, hd, tm), lambda b, h: (b, h, 0, 0, 0)),
            pl.BlockSpec((None, None, S, hd), lambda b, h: (b, h, 0, 0)),
            pl.BlockSpec((None, None, S // tk, vrows, tk), lambda b, h: (b, h, 0, 0, 0)),
        ],
        out_specs=pl.BlockSpec((None, S, vdim), lambda b, h: (b, 0, h)),
        out_shape=jax.ShapeDtypeStruct((B, S, heads * vdim), _BF16),
        scratch_shapes=[
            pltpu.VMEM((1, ta), _F32),
            pltpu.VMEM((vrows, ta), _F32),
        ] + [pltpu.VMEM((tk, ta_pad), _F32)] * 4
          + [pltpu.VMEM((1, ta), _F32)] * 4
          + [pltpu.VMEM((tk, ta_pad), _BF16)] * 2
          + [pltpu.VMEM((1, ta), _F32)] * 2,
        compiler_params=cparams2,
        name="mla_attn",
    )(qT_all, k_all, vT_all)

    d_ff = mlp_w_up.shape[2]
    mlp_scratch = [
        pltpu.VMEM((tm, D), _F32),
        pltpu.VMEM((tm, d_ff), _BF16),
    ]
    x1 = pl.pallas_call(
        functools.partial(_outproj_mlp_kernel, alpha=alpha),
        grid=(T // tm,),
        in_specs=[
            pl.BlockSpec((tm, heads * vdim), lambda i: (i, 0)),
            pl.BlockSpec((tm, D), lambda i: (i, 0)),
            _const_spec((heads * vdim, D)),
            _const_spec((1, D)),
            _const_spec((1, D)),
            _const_spec((D, d_ff)),
            _const_spec((d_ff, D)),
            _const_spec((1, D)),
            _const_spec((1, D)),
        ],
        out_specs=pl.BlockSpec((tm, D), lambda i: (i, 0)),
        out_shape=jax.ShapeDtypeStruct((T, D), _F32),
        scratch_shapes=mlp_scratch,
        compiler_params=cparams1,
        name="mla_out_mlp",
    )(o.reshape(T, heads * vdim), x.reshape(T, D), mla_w_o[0].astype(_BF16),
      _row(ln_mix_g[0]), _row(ln_mix_b[0]), mlp_w_up[0].astype(_BF16),
      mlp_w_down[0].astype(_BF16), _row(ln_mlp_g[0]), _row(ln_mlp_b[0]))

    q_heads = swa_sinks.shape[1]
    s_hd = swa_w_q.shape[2] // q_heads
    kv_heads = kv_w_shared.shape[1] // (2 * s_hd)
    group = q_heads // kv_heads
    blk = _SWA_BLOCK
    assert _LANES % s_hd == 0 and tm % blk == 0
    wqT1 = swa_w_q[0].T.astype(_BF16)
    wk1 = kv_w_shared[:, :kv_heads * s_hd].astype(_BF16)
    wvT1 = kv_w_shared[:, kv_heads * s_hd:].T.astype(_BF16)
    bucketT = jnp.asarray(_t5_bucket_table(blk))
    n_units = (tm // blk) * kv_heads

    smem = pl.BlockSpec(memory_space=pltpu.SMEM)
    out = pl.pallas_call(
        functools.partial(_swa_mlp_kernel, alpha=alpha, q_heads=q_heads, kv_heads=kv_heads,
                          head_dim=s_hd, q_scale=s_hd ** -0.5 * _LOG2E),
        grid=(B, n_t),
        in_specs=[
            pl.BlockSpec((None, tm, D), lambda b, i: (b, i, 0)),
            _const_spec(wqT1.shape),
            _const_spec(wk1.shape),
            _const_spec(wvT1.shape),
            _const_spec((q_heads * s_hd, D)),
            _const_spec(bucketT.shape),
            smem,
            smem,
            _const_spec((1, D)),
            _const_spec((1, D)),
            _const_spec((D, d_ff)),
            _const_spec((d_ff, D)),
            _const_spec((1, D)),
            _const_spec((1, D)),
        ],
        out_specs=pl.BlockSpec((None, tm, D), lambda b, i: (b, i, 0)),
        out_shape=jax.ShapeDtypeStruct((B, S, D), _F32),
        scratch_shapes=[
            pltpu.VMEM((blk + tm, kv_heads * s_hd), _BF16),
            pltpu.VMEM((kv_heads * (s_hd + bf16_rows), blk + tm), _BF16),
            pltpu.VMEM((q_heads * s_hd, tm), _BF16),
            pltpu.VMEM((kv_heads, 2 * blk, group * blk), _F32),
            pltpu.VMEM((n_units, 2 * blk, group * blk), _F32),
            pltpu.VMEM((n_units, 2 * blk, group * blk), _BF16),
            pltpu.VMEM((n_units, 1, group * blk), _F32),
            pltpu.VMEM((q_heads * s_hd, tm), _BF16),
        ] + mlp_scratch,
        compiler_params=cparams2,
        name="swa_mlp",
    )(x1.reshape(B, S, D), wqT1, wk1, wvT1, swa_w_o[0].astype(_BF16), bucketT,
      rel_bias.astype(_F32), swa_sinks[0].astype(_F32),
      _row(ln_mix_g[1]), _row(ln_mix_b[1]), mlp_w_up[1].astype(_BF16),
      mlp_w_down[1].astype(_BF16), _row(ln_mlp_g[1]), _row(ln_mlp_b[1]))
    return out
```

```python
import functools
import math

import numpy as np
import jax
import jax.numpy as jnp
from jax import lax
from jax.experimental import pallas as pl
from jax.experimental.pallas import tpu as pltpu

_F32 = jnp.float32
_BF16 = jnp.bfloat16

_LN_EPS = 1e-5
_RMS_EPS = 1e-6
_ROPE_THETA = 10000.0
_SWA_BLOCK = 128
_REL_BUCKETS = 32
_REL_MAX_DIST = 128
_LOG2E = math.log2(math.e)

_V7X_VMEM_BYTES = 64 * 1024 * 1024
_VMEM_LIMIT_BYTES = _V7X_VMEM_BYTES - 8 * 1024 * 1024
_LANES = 128

_TOKEN_TILE = 512
_ATTN_TILE = 1024
_ATTN_COL_BLOCK = 256
_FF_CHUNK = 1024
_MLP_ROW_BLOCK = 256

_NT = (((1,), (1,)), ((), ()))
_TN = (((0,), (0,)), ((), ()))


def _const_spec(shape):
    nd = len(shape)
    return pl.BlockSpec(shape, lambda *_: (0,) * nd, pipeline_mode=pl.Buffered(1))


def _layernorm(v, g, b):
    mu = jnp.mean(v, axis=-1, keepdims=True)
    d = v - mu
    var = jnp.mean(d * d, axis=-1, keepdims=True)
    return d * lax.rsqrt(var + _LN_EPS) * g + b


def _rmsnorm(v, g):
    return v * lax.rsqrt(jnp.mean(v * v, axis=-1, keepdims=True) + _RMS_EPS) * g


def _mix_mlp_ln(mix_rows, x_ref, alpha, g1_ref, b1_ref, wup_ref, wdown_ref, g2_ref, b2_ref,
                out_ref, x1_scr, h_scr):
    tm = x_ref.shape[0]
    d_ff = wup_ref.shape[1]
    blocks = [slice(r, r + _MLP_ROW_BLOCK) for r in range(0, tm, _MLP_ROW_BLOCK)]
    for rows in blocks:
        x1_scr[rows, :] = _layernorm(alpha * x_ref[rows, :] + mix_rows(rows), g1_ref[...], b1_ref[...])
    for rows in blocks:
        x1b = x1_scr[rows, :].astype(_BF16)
        for lo in range(0, d_ff, _FF_CHUNK):
            hmid = jnp.dot(x1b, wup_ref[:, lo:lo + _FF_CHUNK], preferred_element_type=_F32)
            hmid = jnp.maximum(hmid, 0.0)
            h_scr[rows, lo:lo + _FF_CHUNK] = (hmid * hmid).astype(_BF16)
    for rows in blocks:
        y = jnp.dot(h_scr[rows, :], wdown_ref[...], preferred_element_type=_F32)
        out_ref[rows, :] = _layernorm(alpha * x1_scr[rows, :] + y, g2_ref[...], b2_ref[...])


def _mla_proj_kernel(x_ref, win_ref, gq_ref, gkv_ref, wqT_ref, wuk_ref, wvT_ref,
                     cs_ref, cosT_ref, sinT_ref, qT_ref, k_ref, vT_ref,
                     *, q_rank, kv_rank, heads, nope, rope, q_scale):
    xb = x_ref[...].astype(_BF16)
    h = jnp.dot(xb, win_ref[...], preferred_element_type=_F32)
    cq = _rmsnorm(h[:, :q_rank], gq_ref[...]).astype(_BF16)
    ckv = _rmsnorm(h[:, q_rank:q_rank + kv_rank], gkv_ref[...]).astype(_BF16)
    t = h[:, q_rank + kv_rank:] * cs_ref[...]
    kr = (t + pltpu.roll(t, rope, 1))[:, :rope].astype(_BF16)

    half = rope // 2
    hd = nope + rope
    qT = lax.dot_general(wqT_ref[...], cq, _NT, preferred_element_type=_F32)
    cosT = cosT_ref[...]
    sinT = sinT_ref[...]
    for hh in range(heads):
        base = hh * hd
        qT_ref[hh, 0:nope, :] = (qT[base:base + nope] * q_scale).astype(_BF16)
        x1 = qT[base + nope:base + nope + half]
        x2 = qT[base + nope + half:base + hd]
        qT_ref[hh, nope:nope + half, :] = ((x1 * cosT - x2 * sinT) * q_scale).astype(_BF16)
        qT_ref[hh, nope + half:hd, :] = ((x1 * sinT + x2 * cosT) * q_scale).astype(_BF16)

    kn = jnp.dot(ckv, wuk_ref[...], preferred_element_type=_F32)
    for hh in range(heads):
        k_ref[hh, :, 0:nope] = kn[:, hh * nope:(hh + 1) * nope].astype(_BF16)
        k_ref[hh, :, nope:hd] = kr

    vdim = wvT_ref.shape[0] // heads
    vT = lax.dot_general(wvT_ref[...], ckv, _NT, preferred_element_type=_F32)
    tk = vT_ref.shape[3]
    pad = vT_ref.shape[2] - vdim
    ones_rows = (lax.broadcasted_iota(jnp.int32, (pad, tk), 0) == 0).astype(_BF16)
    for hh in range(heads):
        for j in range(vT_ref.shape[1]):
            vT_ref[hh, j, 0:vdim, :] = vT[hh * vdim:(hh + 1) * vdim, j * tk:(j + 1) * tk].astype(_BF16)
            vT_ref[hh, j, vdim:vdim + pad, :] = ones_rows


def _mla_attn_kernel(qT_ref, k_ref, vT_ref, o_ref, m_scr, acc_scr,
                     s00, s01, s10, s11, cm00, cm01, cm10, cm11, p_a, p_b, al_a, al_b):
    tk, tq = s00.shape[0], acc_scr.shape[1]
    tsub = qT_ref.shape[2]
    nsub = tq // tsub
    n_q = qT_ref.shape[0] // nsub
    assert tq == 2 * tk and tk == vT_ref.shape[2]

    cb = _ATTN_COL_BLOCK

    def q_stage(qi, c, s_ref, cm_ref, col_lo=0):
        k = k_ref[pl.ds(pl.multiple_of(c * tk, tk), tk), :]
        for lo in range(col_lo, tq, cb):
            j, off = divmod(lo, tsub)
            qT = qT_ref[qi * nsub + j, :, off:off + cb]
            s = jnp.dot(k, qT, preferred_element_type=_F32)
            s_ref[:, lo:lo + cb] = s
            cm_ref[:, lo:lo + cb] = jnp.max(s, axis=0, keepdims=True)

    def x_stage(s_ref, cm_ref, p_ref, al_ref, mask_off=None, col_lo=0):
        for lo in range(col_lo, tq, cb):
            cols = slice(lo, lo + cb)
            s = s_ref[:, cols]
            if mask_off is None:
                cm = cm_ref[:, cols]
            else:
                krel = lax.broadcasted_iota(jnp.int32, s.shape, 0) + mask_off
                qrel = lax.broadcasted_iota(jnp.int32, s.shape, 1) + lo
                s = jnp.where(krel <= qrel, s, -jnp.inf)
                cm = jnp.max(s, axis=0, keepdims=True)
            m_prev = m_scr[:, cols]
            m_new = jnp.maximum(m_prev, cm)
            alpha = jnp.exp2(m_prev - m_new)
            p = jnp.exp2(s - m_new)
            p_ref[:, cols] = p.astype(_BF16)
            al_ref[:, cols] = alpha
            m_scr[:, cols] = m_new

    def v_stage(c, p_ref, al_ref, col_lo=0):
        for lo in range(col_lo, tq, cb):
            cols = slice(lo, lo + cb)
            acc_scr[:, cols] = al_ref[:, cols] * acc_scr[:, cols] + jnp.dot(
                vT_ref[c], p_ref[:, cols], preferred_element_type=_F32)

    s_bufs = (((s00, cm00), (s01, cm01)), ((s10, cm10), (s11, cm11)))

    def start_tile():
        m_scr[...] = jnp.full(m_scr.shape, -jnp.inf, _F32)
        acc_scr[...] = jnp.zeros(acc_scr.shape, _F32)
        p_b[...] = jnp.zeros(p_b.shape, p_b.dtype)
        al_b[...] = jnp.ones(al_b.shape, _F32)

    def regular_step(qi, a, par):
        (sa, cma), (sb, cmb) = s_bufs[par]
        (na, ncma), (nb, ncmb) = s_bufs[1 - par]
        q_stage(qi, a + 2, na, ncma)
        x_stage(sa, cma, p_a, al_a)
        v_stage(jnp.maximum(a - 1, 0), p_b, al_b)
        q_stage(qi, a + 3, nb, ncmb)
        x_stage(sb, cmb, p_b, al_b)
        v_stage(a, p_a, al_a)

    def diagonal_step(qi, par):
        a = 2 * qi
        (sa, cma), (sb, cmb) = s_bufs[par]
        (na, ncma), (nb, ncmb) = s_bufs[1 - par]
        q_next = jnp.minimum(qi + 1, n_q - 1)
        q_stage(q_next, 0, na, ncma)
        x_stage(sa, cma, p_a, al_a, mask_off=0)
        v_stage(jnp.maximum(a - 1, 0), p_b, al_b)
        q_stage(q_next, 1, nb, ncmb)
        x_stage(sb, cmb, p_b, al_b, mask_off=tk, col_lo=tk)
        v_stage(a, p_a, al_a)
        v_stage(a + 1, p_b, al_b, col_lo=tk)
        vdim = o_ref.shape[1]
        inv_l = 1.0 / acc_scr[vdim:vdim + 1, :]
        o = (acc_scr[0:vdim, :] * inv_l).T
        o_ref[pl.ds(pl.multiple_of(qi * tq, tq), tq), :] = o.astype(o_ref.dtype)
        start_tile()

    def by_parity(par, fn):
        for static_par in range(2):
            @pl.when(par == static_par)
            def _():
                fn(static_par)

    def tile_body(qi, par):
        def step_body(u, par):
            by_parity(par, functools.partial(regular_step, qi, 2 * u))
            return 1 - par

        par = lax.fori_loop(0, qi, step_body, par)
        by_parity(par, functools.partial(diagonal_step, qi))
        return 1 - par

    start_tile()
    q_stage(0, 0, s00, cm00)
    q_stage(0, 1, s01, cm01)
    lax.fori_loop(0, n_q, tile_body, jnp.int32(0))


def _outproj_mlp_kernel(o_ref, x_ref, wo_ref, g1_ref, b1_ref, wup_ref, wdown_ref,
                        g2_ref, b2_ref, out_ref, x1_scr, h_scr, *, alpha):
    def mix_rows(rows):
        return jnp.dot(o_ref[rows, :], wo_ref[...], preferred_element_type=_F32)

    _mix_mlp_ln(mix_rows, x_ref, alpha, g1_ref, b1_ref, wup_ref, wdown_ref, g2_ref, b2_ref,
                out_ref, x1_scr, h_scr)


def _swa_mlp_kernel(x_ref, wqT_ref, wk_ref, wvT_ref, wo_ref, bucketT_ref, relb_ref, sink_ref,
                    g1_ref, b1_ref, wup_ref, wdown_ref, g2_ref, b2_ref, out_ref,
                    k_scr, vTe_scr, qT_scr, biasT_scr, s_scr, p_scr, m_scr, oT_scr, x1_scr, h_scr,
                    *, alpha, q_heads, kv_heads, head_dim, q_scale):
    blk = _SWA_BLOCK
    group = q_heads // kv_heads
    tm = x_ref.shape[0]
    n_blk = tm // blk
    vrows = vTe_scr.shape[0] // kv_heads
    first_tile = pl.program_id(1) == 0
    always = pl.program_id(1) >= 0

    @pl.when((pl.program_id(0) == 0) & first_tile)
    def _():
        bkt = bucketT_ref[...]
        for hq in range(q_heads):
            bias = jnp.full(bkt.shape, -jnp.inf, _F32)
            for bb in range(_REL_BUCKETS):
                bias = jnp.where(bkt == bb, relb_ref[bb, hq] * _LOG2E, bias)
            g, gi = divmod(hq, group)
            biasT_scr[g, :, gi * blk:(gi + 1) * blk] = bias

    @pl.when(first_tile)
    def _():
        k_scr[0:blk, :] = jnp.zeros((blk, k_scr.shape[1]), k_scr.dtype)
        vTe_scr[:, 0:blk] = jnp.zeros((vTe_scr.shape[0], blk), vTe_scr.dtype)

    xb = x_ref[...].astype(_BF16)
    qT_scr[...] = (lax.dot_general(wqT_ref[...], xb, _NT, preferred_element_type=_F32)
                   * q_scale).astype(_BF16)
    k_scr[blk:blk + tm, :] = jnp.dot(xb, wk_ref[...], preferred_element_type=_F32).astype(_BF16)
    vT = lax.dot_general(wvT_ref[...], xb, _NT, preferred_element_type=_F32).astype(_BF16)
    ones_rows = (lax.broadcasted_iota(jnp.int32, (vrows - head_dim, tm), 0) == 0).astype(_BF16)
    for g in range(kv_heads):
        vTe_scr[g * vrows:g * vrows + head_dim, blk:blk + tm] = vT[g * head_dim:(g + 1) * head_dim]
        vTe_scr[g * vrows + head_dim:(g + 1) * vrows, blk:blk + tm] = ones_rows

    first_pen = jnp.where(first_tile, -jnp.inf, 0.0).astype(_F32)
    zeros_q = jnp.zeros((head_dim, group * blk), _BF16)
    heads_per_lane_tile = _LANES // head_dim
    units = [(n, g) for n in range(n_blk) for g in range(kv_heads)]

    def sink_row(g):
        return jnp.concatenate(
            [jnp.full((1, blk), sink_ref[g * group + gi] * _LOG2E, _F32) for gi in range(group)],
            axis=1)

    @pl.when(always)
    def _():
        sinks = [sink_row(g) for g in range(kv_heads)]
        for u, (n, g) in enumerate(units):
            lt = g // heads_per_lane_tile
            kpair = k_scr[n * blk:(n + 2) * blk, lt * _LANES:(lt + 1) * _LANES]
            qg = jnp.concatenate(
                [qT_scr[(g * group + gi) * head_dim:(g * group + gi + 1) * head_dim,
                        n * blk:(n + 1) * blk] for gi in range(group)], axis=1)
            pieces = [zeros_q] * heads_per_lane_tile
            pieces[g % heads_per_lane_tile] = qg
            qpad = jnp.concatenate(pieces, axis=0)
            s = jnp.dot(kpair, qpad, preferred_element_type=_F32) + biasT_scr[g]
            if n == 0:
                s = jnp.concatenate([s[:blk] + first_pen, s[blk:]], axis=0)
            s_scr[u] = s
            m_scr[u] = jnp.maximum(jnp.max(s, axis=0, keepdims=True), sinks[g])

    @pl.when(always)
    def _():
        for u in range(len(units)):
            p_scr[u] = jnp.exp2(s_scr[u] - m_scr[u]).astype(_BF16)

    @pl.when(always)
    def _():
        sinks = [sink_row(g) for g in range(kv_heads)]
        for u, (n, g) in enumerate(units):
            oT = jnp.dot(vTe_scr[g * vrows:(g + 1) * vrows, n * blk:(n + 2) * blk], p_scr[u],
                         preferred_element_type=_F32)
            denom = oT[head_dim:head_dim + 1] + jnp.exp2(sinks[g] - m_scr[u])
            o = (oT[:head_dim] * (1.0 / denom)).astype(_BF16)
            for gi in range(group):
                hq = g * group + gi
                oT_scr[hq * head_dim:(hq + 1) * head_dim, n * blk:(n + 1) * blk] = (
                    o[:, gi * blk:(gi + 1) * blk])
        k_scr[0:blk, :] = k_scr[tm:tm + blk, :]
        vTe_scr[:, 0:blk] = vTe_scr[:, tm:tm + blk]

    def mix_rows(rows):
        return lax.dot_general(oT_scr[:, rows], wo_ref[...], _TN, preferred_element_type=_F32)

    _mix_mlp_ln(mix_rows, x_ref, alpha, g1_ref, b1_ref, wup_ref, wdown_ref, g2_ref, b2_ref,
                out_ref, x1_scr, h_scr)


def _t5_bucket_table(blk):
    i = np.arange(blk)[None, :]
    j = np.arange(2 * blk)[:, None]
    dist = i + blk - j
    max_exact = _REL_BUCKETS // 2
    nf = np.maximum(dist, 1).astype(np.float32)
    large = max_exact + (np.log(nf / np.float32(max_exact))
                         / np.float32(math.log(_REL_MAX_DIST / max_exact))
                         * np.float32(_REL_BUCKETS - max_exact)).astype(np.int32)
    large = np.minimum(large, _REL_BUCKETS - 1)
    bucket = np.where(dist < max_exact, np.maximum(dist, 0), large)
    valid = (dist >= 0) & (dist < blk)
    return np.where(valid, bucket, -1).astype(np.int32)


def _rope_tables(seq, rope):
    half = rope // 2
    inv = _ROPE_THETA ** (-jnp.arange(half, dtype=_F32) / half)
    ang = jnp.arange(seq).astype(_F32)[:, None] * inv[None, :]
    cos, sin = jnp.cos(ang), jnp.sin(ang)
    cs = jnp.concatenate([cos, cos, -sin, sin], axis=1)
    return cs, cos.T, sin.T


def _row(v):
    return v.reshape(1, -1).astype(_F32)


def kernel(x, mla_w_in, mla_g_q, mla_g_kv, mla_w_uq, mla_w_uk, mla_w_uv, mla_w_o, kv_w_shared,
           swa_w_q, swa_sinks, swa_w_o, rel_bias, mlp_w_up, mlp_w_down, ln_mix_g, ln_mix_b,
           ln_mlp_g, ln_mlp_b):
    B, S, D = x.shape
    depth = mlp_w_up.shape[0]
    assert depth == 2 and mla_w_in.shape[0] == 1 and swa_w_q.shape[0] == 1
    alpha = (2 * depth) ** 0.25
    T = B * S
    tm = _TOKEN_TILE
    ta = _ATTN_TILE
    tk = ta // 2
    bf16_rows = 16
    ta_pad = ta + _LANES
    assert ta % tm == 0 and tm % tk == 0 and S % ta == 0

    q_rank, heads, hd = mla_w_uq.shape[1:]
    kv_rank, _, nope = mla_w_uk.shape[1:]
    vdim = mla_w_uv.shape[3]
    vrows = vdim + bf16_rows
    rope = hd - nope
    half = rope // 2
    assert 2 * rope == _LANES and mla_w_in.shape[2] == q_rank + kv_rank + rope

    w_in = mla_w_in[0]
    r0 = q_rank + kv_rank
    w_in_ext = jnp.concatenate(
        [w_in, w_in[:, r0 + half:r0 + rope], w_in[:, r0:r0 + half]], axis=1).astype(_BF16)
    wqT = mla_w_uq[0].reshape(q_rank, heads * hd).T.astype(_BF16)
    wuk = mla_w_uk[0].reshape(kv_rank, heads * nope).astype(_BF16)
    wvT = mla_w_uv[0].reshape(kv_rank, heads * vdim).T.astype(_BF16)
    cs, cosT, sinT = _rope_tables(S, rope)
    n_t = S // tm
    q_scale = hd ** -0.5 * _LOG2E

    cparams2 = pltpu.CompilerParams(dimension_semantics=("arbitrary", "arbitrary"),
                                    vmem_limit_bytes=_VMEM_LIMIT_BYTES)
    cparams1 = pltpu.CompilerParams(dimension_semantics=("arbitrary",),
                                    vmem_limit_bytes=_VMEM_LIMIT_BYTES)

    qT_all, k_all, vT_all = pl.pallas_call(
        functools.partial(_mla_proj_kernel, q_rank=q_rank, kv_rank=kv_rank, heads=heads,
                          nope=nope, rope=rope, q_scale=q_scale),
        grid=(B, n_t),
        in_specs=[
            pl.BlockSpec((None, tm, D), lambda b, i: (b, i, 0)),
            _const_spec(w_in_ext.shape),
            _const_spec((1, q_rank)),
            _const_spec((1, kv_rank)),
            _const_spec(wqT.shape),
            _const_spec(wuk.shape),
            _const_spec(wvT.shape),
            pl.BlockSpec((tm, 2 * rope), lambda b, i: (i, 0)),
            pl.BlockSpec((half, tm), lambda b, i: (0, i)),
            pl.BlockSpec((half, tm), lambda b, i: (0, i)),
        ],
        out_specs=[
            pl.BlockSpec((None, heads, None, hd, tm), lambda b, i: (b, 0, i, 0, 0)),
            pl.BlockSpec((None, heads, tm, hd), lambda b, i: (b, 0, i, 0)),
            pl.BlockSpec((None, heads, tm // tk, vrows, tk), lambda b, i: (b, 0, i, 0, 0)),
        ],
        out_shape=[
            jax.ShapeDtypeStruct((B, heads, n_t, hd, tm), _BF16),
            jax.ShapeDtypeStruct((B, heads, S, hd), _BF16),
            jax.ShapeDtypeStruct((B, heads, S // tk, vrows, tk), _BF16),
        ],
        compiler_params=cparams2,
        name="mla_proj",
    )(x, w_in_ext, _row(mla_g_q[0]), _row(mla_g_kv[0]), wqT, wuk, wvT, cs, cosT, sinT)

    o = pl.pallas_call(
        _mla_attn_kernel,
        grid=(B, heads),
        in_specs=[
            pl.BlockSpec((None, None, n_t, hd, tm), lambda b, h: (b, h, 0, 0, 0)),
            pl.BlockSpec((None, None, S, hd), lambda b, h: (b, h, 0, 0)),
            pl.BlockSpec((None, None, S // tk, vrows, tk), lambda b, h: (b, h, 0, 0, 0)),
        ],
        out_specs=pl.BlockSpec((None, S, vdim), lambda b, h: (b, 0, h)),
        out_shape=jax.ShapeDtypeStruct((B, S, heads * vdim), _BF16),
        scratch_shapes=[
            pltpu.VMEM((1, ta), _F32),
            pltpu.VMEM((vrows, ta), _F32),
        ] + [pltpu.VMEM((tk, ta_pad), _F32)] * 4
          + [pltpu.VMEM((1, ta), _F32)] * 4
          + [pltpu.VMEM((tk, ta), _BF16)] * 2
          + [pltpu.VMEM((1, ta), _F32)] * 2,
        compiler_params=cparams2,
        name="mla_attn",
    )(qT_all, k_all, vT_all)

    d_ff = mlp_w_up.shape[2]
    mlp_scratch = [
        pltpu.VMEM((tm, D), _F32),
        pltpu.VMEM((tm, d_ff), _BF16),
    ]
    x1 = pl.pallas_call(
        functools.partial(_outproj_mlp_kernel, alpha=alpha),
        grid=(T // tm,),
        in_specs=[
            pl.BlockSpec((tm, heads * vdim), lambda i: (i, 0)),
            pl.BlockSpec((tm, D), lambda i: (i, 0)),
            _const_spec((heads * vdim, D)),
            _const_spec((1, D)),
            _const_spec((1, D)),
            _const_spec((D, d_ff)),
            _const_spec((d_ff, D)),
            _const_spec((1, D)),
            _const_spec((1, D)),
        ],
        out_specs=pl.BlockSpec((tm, D), lambda i: (i, 0)),
        out_shape=jax.ShapeDtypeStruct((T, D), _F32),
        scratch_shapes=mlp_scratch,
        compiler_params=cparams1,
        name="mla_out_mlp",
    )(o.reshape(T, heads * vdim), x.reshape(T, D), mla_w_o[0].astype(_BF16),
      _row(ln_mix_g[0]), _row(ln_mix_b[0]), mlp_w_up[0].astype(_BF16),
      mlp_w_down[0].astype(_BF16), _row(ln_mlp_g[0]), _row(ln_mlp_b[0]))

    q_heads = swa_sinks.shape[1]
    s_hd = swa_w_q.shape[2] // q_heads
    kv_heads = kv_w_shared.shape[1] // (2 * s_hd)
    group = q_heads // kv_heads
    blk = _SWA_BLOCK
    assert _LANES % s_hd == 0 and tm % blk == 0
    wqT1 = swa_w_q[0].T.astype(_BF16)
    wk1 = kv_w_shared[:, :kv_heads * s_hd].astype(_BF16)
    wvT1 = kv_w_shared[:, kv_heads * s_hd:].T.astype(_BF16)
    bucketT = jnp.asarray(_t5_bucket_table(blk))
    n_units = (tm // blk) * kv_heads

    smem = pl.BlockSpec(memory_space=pltpu.SMEM)
    out = pl.pallas_call(
        functools.partial(_swa_mlp_kernel, alpha=alpha, q_heads=q_heads, kv_heads=kv_heads,
                          head_dim=s_hd, q_scale=s_hd ** -0.5 * _LOG2E),
        grid=(B, n_t),
        in_specs=[
            pl.BlockSpec((None, tm, D), lambda b, i: (b, i, 0)),
            _const_spec(wqT1.shape),
            _const_spec(wk1.shape),
            _const_spec(wvT1.shape),
            _const_spec((q_heads * s_hd, D)),
            _const_spec(bucketT.shape),
            smem,
            smem,
            _const_spec((1, D)),
            _const_spec((1, D)),
            _const_spec((D, d_ff)),
            _const_spec((d_ff, D)),
            _const_spec((1, D)),
            _const_spec((1, D)),
        ],
        out_specs=pl.BlockSpec((None, tm, D), lambda b, i: (b, i, 0)),
        out_shape=jax.ShapeDtypeStruct((B, S, D), _F32),
        scratch_shapes=[
            pltpu.VMEM((blk + tm, kv_heads * s_hd), _BF16),
            pltpu.VMEM((kv_heads * (s_hd + bf16_rows), blk + tm), _BF16),
            pltpu.VMEM((q_heads * s_hd, tm), _BF16),
            pltpu.VMEM((kv_heads, 2 * blk, group * blk), _F32),
            pltpu.VMEM((n_units, 2 * blk, group * blk), _F32),
            pltpu.VMEM((n_units, 2 * blk, group * blk), _BF16),
            pltpu.VMEM((n_units, 1, group * blk), _F32),
            pltpu.VMEM((q_heads * s_hd, tm), _BF16),
        ] + mlp_scratch,
        compiler_params=cparams2,
        name="swa_mlp",
    )(x1.reshape(B, S, D), wqT1, wk1, wvT1, swa_w_o[0].astype(_BF16), bucketT,
      rel_bias.astype(_F32), swa_sinks[0].astype(_F32),
      _row(ln_mix_g[1]), _row(ln_mix_b[1]), mlp_w_up[1].astype(_BF16),
      mlp_w_down[1].astype(_BF16), _row(ln_mlp_g[1]), _row(ln_mlp_b[1]))
    return out
```

```python
import functools
import math

import numpy as np
import jax
import jax.numpy as jnp
from jax import lax
from jax.experimental import pallas as pl
from jax.experimental.pallas import tpu as pltpu

_F32 = jnp.float32
_BF16 = jnp.bfloat16

_LN_EPS = 1e-5
_RMS_EPS = 1e-6
_ROPE_THETA = 10000.0
_SWA_BLOCK = 128
_REL_BUCKETS = 32
_REL_MAX_DIST = 128
_LOG2E = math.log2(math.e)

_V7X_VMEM_BYTES = 64 * 1024 * 1024
_VMEM_LIMIT_BYTES = _V7X_VMEM_BYTES - 8 * 1024 * 1024
_LANES = 128

_TOKEN_TILE = 512
_ATTN_TILE = 1024
_ATTN_COL_BLOCK = 256
_FF_CHUNK = 1024
_MLP_ROW_BLOCK = 256

_NT = (((1,), (1,)), ((), ()))
_TN = (((0,), (0,)), ((), ()))


def _const_spec(shape):
    nd = len(shape)
    return pl.BlockSpec(shape, lambda *_: (0,) * nd, pipeline_mode=pl.Buffered(1))


def _layernorm(v, g, b):
    mu = jnp.mean(v, axis=-1, keepdims=True)
    d = v - mu
    var = jnp.mean(d * d, axis=-1, keepdims=True)
    return d * lax.rsqrt(var + _LN_EPS) * g + b


def _rmsnorm(v, g):
    return v * lax.rsqrt(jnp.mean(v * v, axis=-1, keepdims=True) + _RMS_EPS) * g


def _mix_mlp_ln(mix_rows, x_ref, alpha, g1_ref, b1_ref, wup_ref, wdown_ref, g2_ref, b2_ref,
                out_ref, x1_scr, h_scr):
    tm = x_ref.shape[0]
    d_ff = wup_ref.shape[1]
    blocks = [slice(r, r + _MLP_ROW_BLOCK) for r in range(0, tm, _MLP_ROW_BLOCK)]
    for rows in blocks:
        x1_scr[rows, :] = _layernorm(alpha * x_ref[rows, :] + mix_rows(rows), g1_ref[...], b1_ref[...])
    for rows in blocks:
        x1b = x1_scr[rows, :].astype(_BF16)
        for lo in range(0, d_ff, _FF_CHUNK):
            hmid = jnp.dot(x1b, wup_ref[:, lo:lo + _FF_CHUNK], preferred_element_type=_F32)
            hmid = jnp.maximum(hmid, 0.0)
            h_scr[rows, lo:lo + _FF_CHUNK] = (hmid * hmid).astype(_BF16)
    for rows in blocks:
        y = jnp.dot(h_scr[rows, :], wdown_ref[...], preferred_element_type=_F32)
        out_ref[rows, :] = _layernorm(alpha * x1_scr[rows, :] + y, g2_ref[...], b2_ref[...])


def _mla_proj_kernel(x_ref, win_ref, gq_ref, gkv_ref, wqT_ref, wuk_ref, wvT_ref,
                     cs_ref, cosT_ref, sinT_ref, qT_ref, k_ref, vT_ref,
                     *, q_rank, kv_rank, heads, nope, rope, q_scale):
    xb = x_ref[...].astype(_BF16)
    h = jnp.dot(xb, win_ref[...], preferred_element_type=_F32)
    cq = _rmsnorm(h[:, :q_rank], gq_ref[...]).astype(_BF16)
    ckv = _rmsnorm(h[:, q_rank:q_rank + kv_rank], gkv_ref[...]).astype(_BF16)
    t = h[:, q_rank + kv_rank:] * cs_ref[...]
    kr = (t + pltpu.roll(t, rope, 1))[:, :rope].astype(_BF16)

    half = rope // 2
    hd = nope + rope
    qT = lax.dot_general(wqT_ref[...], cq, _NT, preferred_element_type=_F32)
    cosT = cosT_ref[...]
    sinT = sinT_ref[...]
    for hh in range(heads):
        base = hh * hd
        qT_ref[hh, 0:nope, :] = (qT[base:base + nope] * q_scale).astype(_BF16)
        x1 = qT[base + nope:base + nope + half]
        x2 = qT[base + nope + half:base + hd]
        qT_ref[hh, nope:nope + half, :] = ((x1 * cosT - x2 * sinT) * q_scale).astype(_BF16)
        qT_ref[hh, nope + half:hd, :] = ((x1 * sinT + x2 * cosT) * q_scale).astype(_BF16)

    kn = jnp.dot(ckv, wuk_ref[...], preferred_element_type=_F32)
    for hh in range(heads):
        k_ref[hh, :, 0:nope] = kn[:, hh * nope:(hh + 1) * nope].astype(_BF16)
        k_ref[hh, :, nope:hd] = kr

    vdim = wvT_ref.shape[0] // heads
    vT = lax.dot_general(wvT_ref[...], ckv, _NT, preferred_element_type=_F32)
    tk = vT_ref.shape[3]
    pad = vT_ref.shape[2] - vdim
    ones_rows = (lax.broadcasted_iota(jnp.int32, (pad, tk), 0) == 0).astype(_BF16)
    for hh in range(heads):
        for j in range(vT_ref.shape[1]):
            vT_ref[hh, j, 0:vdim, :] = vT[hh * vdim:(hh + 1) * vdim, j * tk:(j + 1) * tk].astype(_BF16)
            vT_ref[hh, j, vdim:vdim + pad, :] = ones_rows


def _mla_attn_kernel(qT_ref, k_ref, vT_ref, o_ref, m_scr, acc_scr,
                     s00, s01, s10, s11, cm00, cm01, cm10, cm11, p_a, p_b, al_a, al_b):
    tk, tq = s00.shape[0], acc_scr.shape[1]
    tsub = qT_ref.shape[2]
    nsub = tq // tsub
    n_q = qT_ref.shape[0] // nsub
    assert tq == 2 * tk and tk == vT_ref.shape[2]

    cb = _ATTN_COL_BLOCK

    def q_stage(qi, c, s_ref, cm_ref, col_lo=0):
        k = k_ref[pl.ds(pl.multiple_of(c * tk, tk), tk), :]
        for lo in range(col_lo, tq, cb):
            j, off = divmod(lo, tsub)
            qT = qT_ref[qi * nsub + j, :, off:off + cb]
            s = jnp.dot(k, qT, preferred_element_type=_F32)
            s_ref[:, lo:lo + cb] = s
            cm_ref[:, lo:lo + cb] = jnp.max(s, axis=0, keepdims=True)

    def x_stage(s_ref, cm_ref, p_ref, al_ref, mask_off=None, col_lo=0):
        for lo in range(col_lo, tq, cb):
            cols = slice(lo, lo + cb)
            s = s_ref[:, cols]
            if mask_off is None:
                cm = cm_ref[:, cols]
            else:
                krel = lax.broadcasted_iota(jnp.int32, s.shape, 0) + mask_off
                qrel = lax.broadcasted_iota(jnp.int32, s.shape, 1) + lo
                s = jnp.where(krel <= qrel, s, -jnp.inf)
                cm = jnp.max(s, axis=0, keepdims=True)
            m_prev = m_scr[:, cols]
            m_new = jnp.maximum(m_prev, cm)
            alpha = jnp.exp2(m_prev - m_new)
            p = jnp.exp2(s - m_new)
            p_ref[:, cols] = p.astype(_BF16)
            al_ref[:, cols] = alpha
            m_scr[:, cols] = m_new

    def v_stage(c, p_ref, al_ref, col_lo=0):
        for lo in range(col_lo, tq, cb):
            cols = slice(lo, lo + cb)
            acc_scr[:, cols] = al_ref[:, cols] * acc_scr[:, cols] + jnp.dot(
                vT_ref[c], p_ref[:, cols], preferred_element_type=_F32)

    s_bufs = (((s00, cm00), (s01, cm01)), ((s10, cm10), (s11, cm11)))

    def start_tile():
        m_scr[...] = jnp.full(m_scr.shape, -jnp.inf, _F32)
        acc_scr[...] = jnp.zeros(acc_scr.shape, _F32)
        p_b[...] = jnp.zeros(p_b.shape, p_b.dtype)
        al_b[...] = jnp.ones(al_b.shape, _F32)

    def regular_step(qi, a, par):
        (sa, cma), (sb, cmb) = s_bufs[par]
        (na, ncma), (nb, ncmb) = s_bufs[1 - par]
        q_stage(qi, a + 2, na, ncma)
        x_stage(sa, cma, p_a, al_a)
        v_stage(jnp.maximum(a - 1, 0), p_b, al_b)
        q_stage(qi, a + 3, nb, ncmb)
        x_stage(sb, cmb, p_b, al_b)
        v_stage(a, p_a, al_a)

    def diagonal_step(qi, par):
        a = 2 * qi
        (sa, cma), (sb, cmb) = s_bufs[par]
        (na, ncma), (nb, ncmb) = s_bufs[1 - par]
        q_next = jnp.minimum(qi + 1, n_q - 1)
        q_stage(q_next, 0, na, ncma)
        x_stage(sa, cma, p_a, al_a, mask_off=0)
        v_stage(jnp.maximum(a - 1, 0), p_b, al_b)
        q_stage(q_next, 1, nb, ncmb)
        x_stage(sb, cmb, p_b, al_b, mask_off=tk, col_lo=tk)
        v_stage(a, p_a, al_a)
        v_stage(a + 1, p_b, al_b, col_lo=tk)
        vdim = o_ref.shape[1]
        inv_l = 1.0 / acc_scr[vdim:vdim + 1, :]
        o = (acc_scr[0:vdim, :] * inv_l).T
        o_ref[pl.ds(pl.multiple_of(qi * tq, tq), tq), :] = o.astype(o_ref.dtype)
        start_tile()

    def by_parity(par, fn):
        for static_par in range(2):
            @pl.when(par == static_par)
            def _():
                fn(static_par)

    def tile_body(qi, par):
        def step_body(u, par):
            by_parity(par, functools.partial(regular_step, qi, 2 * u))
            return 1 - par

        par = lax.fori_loop(0, qi, step_body, par)
        by_parity(par, functools.partial(diagonal_step, qi))
        return 1 - par

    start_tile()
    q_stage(0, 0, s00, cm00)
    q_stage(0, 1, s01, cm01)
    lax.fori_loop(0, n_q, tile_body, jnp.int32(0))


def _outproj_mlp_kernel(o_ref, x_ref, wo_ref, g1_ref, b1_ref, wup_ref, wdown_ref,
                        g2_ref, b2_ref, out_ref, x1_scr, h_scr, *, alpha):
    def mix_rows(rows):
        return jnp.dot(o_ref[rows, :], wo_ref[...], preferred_element_type=_F32)

    _mix_mlp_ln(mix_rows, x_ref, alpha, g1_ref, b1_ref, wup_ref, wdown_ref, g2_ref, b2_ref,
                out_ref, x1_scr, h_scr)


def _swa_mlp_kernel(x_ref, wqT_ref, wk_ref, wvT_ref, wo_ref, bucketT_ref, relb_ref, sink_ref,
                    g1_ref, b1_ref, wup_ref, wdown_ref, g2_ref, b2_ref, out_ref,
                    k_scr, vTe_scr, qT_scr, biasT_scr, s_scr, p_scr, m_scr, oT_scr, x1_scr, h_scr,
                    *, alpha, q_heads, kv_heads, head_dim, q_scale):
    blk = _SWA_BLOCK
    group = q_heads // kv_heads
    tm = x_ref.shape[0]
    n_blk = tm // blk
    vrows = vTe_scr.shape[0] // kv_heads
    first_tile = pl.program_id(1) == 0
    always = pl.program_id(1) >= 0

    @pl.when((pl.program_id(0) == 0) & first_tile)
    def _():
        bkt = bucketT_ref[...]
        for hq in range(q_heads):
            bias = jnp.full(bkt.shape, -jnp.inf, _F32)
            for bb in range(_REL_BUCKETS):
                bias = jnp.where(bkt == bb, relb_ref[bb, hq] * _LOG2E, bias)
            g, gi = divmod(hq, group)
            biasT_scr[g, :, gi * blk:(gi + 1) * blk] = bias

    @pl.when(first_tile)
    def _():
        k_scr[0:blk, :] = jnp.zeros((blk, k_scr.shape[1]), k_scr.dtype)
        vTe_scr[:, 0:blk] = jnp.zeros((vTe_scr.shape[0], blk), vTe_scr.dtype)

    xb = x_ref[...].astype(_BF16)
    qT_scr[...] = (lax.dot_general(wqT_ref[...], xb, _NT, preferred_element_type=_F32)
                   * q_scale).astype(_BF16)
    k_scr[blk:blk + tm, :] = jnp.dot(xb, wk_ref[...], preferred_element_type=_F32).astype(_BF16)
    vT = lax.dot_general(wvT_ref[...], xb, _NT, preferred_element_type=_F32).astype(_BF16)
    ones_rows = (lax.broadcasted_iota(jnp.int32, (vrows - head_dim, tm), 0) == 0).astype(_BF16)
    for g in range(kv_heads):
        vTe_scr[g * vrows:g * vrows + head_dim, blk:blk + tm] = vT[g * head_dim:(g + 1) * head_dim]
        vTe_scr[g * vrows + head_dim:(g + 1) * vrows, blk:blk + tm] = ones_rows

    first_pen = jnp.where(first_tile, -jnp.inf, 0.0).astype(_F32)
    zeros_q = jnp.zeros((head_dim, group * blk), _BF16)
    heads_per_lane_tile = _LANES // head_dim
    units = [(n, g) for n in range(n_blk) for g in range(kv_heads)]

    def sink_row(g):
        return jnp.concatenate(
            [jnp.full((1, blk), sink_ref[g * group + gi] * _LOG2E, _F32) for gi in range(group)],
            axis=1)

    @pl.when(always)
    def _():
        sinks = [sink_row(g) for g in range(kv_heads)]
        for u, (n, g) in enumerate(units):
            lt = g // heads_per_lane_tile
            kpair = k_scr[n * blk:(n + 2) * blk, lt * _LANES:(lt + 1) * _LANES]
            qg = jnp.concatenate(
                [qT_scr[(g * group + gi) * head_dim:(g * group + gi + 1) * head_dim,
                        n * blk:(n + 1) * blk] for gi in range(group)], axis=1)
            pieces = [zeros_q] * heads_per_lane_tile
            pieces[g % heads_per_lane_tile] = qg
            qpad = jnp.concatenate(pieces, axis=0)
            s = jnp.dot(kpair, qpad, preferred_element_type=_F32) + biasT_scr[g]
            if n == 0:
                s = jnp.concatenate([s[:blk] + first_pen, s[blk:]], axis=0)
            s_scr[u, :, 0:group * blk] = s
            m_scr[u] = jnp.maximum(jnp.max(s, axis=0, keepdims=True), sinks[g])

    @pl.when(always)
    def _():
        for u in range(len(units)):
            p_scr[u] = jnp.exp2(s_scr[u, :, 0:group * blk] - m_scr[u]).astype(_BF16)

    @pl.when(always)
    def _():
        sinks = [sink_row(g) for g in range(kv_heads)]
        for u, (n, g) in enumerate(units):
            oT = jnp.dot(vTe_scr[g * vrows:(g + 1) * vrows, n * blk:(n + 2) * blk], p_scr[u],
                         preferred_element_type=_F32)
            denom = oT[head_dim:head_dim + 1] + jnp.exp2(sinks[g] - m_scr[u])
            o = (oT[:head_dim] * (1.0 / denom)).astype(_BF16)
            for gi in range(group):
                hq = g * group + gi
                oT_scr[hq * head_dim:(hq + 1) * head_dim, n * blk:(n + 1) * blk] = (
                    o[:, gi * blk:(gi + 1) * blk])
        k_scr[0:blk, :] = k_scr[tm:tm + blk, :]
        vTe_scr[:, 0:blk] = vTe_scr[:, tm:tm + blk]

    def mix_rows(rows):
        return lax.dot_general(oT_scr[:, rows], wo_ref[...], _TN, preferred_element_type=_F32)

    _mix_mlp_ln(mix_rows, x_ref, alpha, g1_ref, b1_ref, wup_ref, wdown_ref, g2_ref, b2_ref,
                out_ref, x1_scr, h_scr)


def _t5_bucket_table(blk):
    i = np.arange(blk)[None, :]
    j = np.arange(2 * blk)[:, None]
    dist = i + blk - j
    max_exact = _REL_BUCKETS // 2
    nf = np.maximum(dist, 1).astype(np.float32)
    large = max_exact + (np.log(nf / np.float32(max_exact))
                         / np.float32(math.log(_REL_MAX_DIST / max_exact))
                         * np.float32(_REL_BUCKETS - max_exact)).astype(np.int32)
    large = np.minimum(large, _REL_BUCKETS - 1)
    bucket = np.where(dist < max_exact, np.maximum(dist, 0), large)
    valid = (dist >= 0) & (dist < blk)
    return np.where(valid, bucket, -1).astype(np.int32)


def _rope_tables(seq, rope):
    half = rope // 2
    inv = _ROPE_THETA ** (-jnp.arange(half, dtype=_F32) / half)
    ang = jnp.arange(seq).astype(_F32)[:, None] * inv[None, :]
    cos, sin = jnp.cos(ang), jnp.sin(ang)
    cs = jnp.concatenate([cos, cos, -sin, sin], axis=1)
    return cs, cos.T, sin.T


def _row(v):
    return v.reshape(1, -1).astype(_F32)


def kernel(x, mla_w_in, mla_g_q, mla_g_kv, mla_w_uq, mla_w_uk, mla_w_uv, mla_w_o, kv_w_shared,
           swa_w_q, swa_sinks, swa_w_o, rel_bias, mlp_w_up, mlp_w_down, ln_mix_g, ln_mix_b,
           ln_mlp_g, ln_mlp_b):
    B, S, D = x.shape
    depth = mlp_w_up.shape[0]
    assert depth == 2 and mla_w_in.shape[0] == 1 and swa_w_q.shape[0] == 1
    alpha = (2 * depth) ** 0.25
    T = B * S
    tm = _TOKEN_TILE
    ta = _ATTN_TILE
    tk = ta // 2
    bf16_rows = 16
    ta_pad = ta + _LANES
    assert ta % tm == 0 and tm % tk == 0 and S % ta == 0

    q_rank, heads, hd = mla_w_uq.shape[1:]
    kv_rank, _, nope = mla_w_uk.shape[1:]
    vdim = mla_w_uv.shape[3]
    vrows = vdim + bf16_rows
    rope = hd - nope
    half = rope // 2
    assert 2 * rope == _LANES and mla_w_in.shape[2] == q_rank + kv_rank + rope

    w_in = mla_w_in[0]
    r0 = q_rank + kv_rank
    w_in_ext = jnp.concatenate(
        [w_in, w_in[:, r0 + half:r0 + rope], w_in[:, r0:r0 + half]], axis=1).astype(_BF16)
    wqT = mla_w_uq[0].reshape(q_rank, heads * hd).T.astype(_BF16)
    wuk = mla_w_uk[0].reshape(kv_rank, heads * nope).astype(_BF16)
    wvT = mla_w_uv[0].reshape(kv_rank, heads * vdim).T.astype(_BF16)
    cs, cosT, sinT = _rope_tables(S, rope)
    n_t = S // tm
    q_scale = hd ** -0.5 * _LOG2E

    cparams2 = pltpu.CompilerParams(dimension_semantics=("arbitrary", "arbitrary"),
                                    vmem_limit_bytes=_VMEM_LIMIT_BYTES)
    cparams1 = pltpu.CompilerParams(dimension_semantics=("arbitrary",),
                                    vmem_limit_bytes=_VMEM_LIMIT_BYTES)

    qT_all, k_all, vT_all = pl.pallas_call(
        functools.partial(_mla_proj_kernel, q_rank=q_rank, kv_rank=kv_rank, heads=heads,
                          nope=nope, rope=rope, q_scale=q_scale),
        grid=(B, n_t),
        in_specs=[
            pl.BlockSpec((None, tm, D), lambda b, i: (b, i, 0)),
            _const_spec(w_in_ext.shape),
            _const_spec((1, q_rank)),
            _const_spec((1, kv_rank)),
            _const_spec(wqT.shape),
            _const_spec(wuk.shape),
            _const_spec(wvT.shape),
            pl.BlockSpec((tm, 2 * rope), lambda b, i: (i, 0)),
            pl.BlockSpec((half, tm), lambda b, i: (0, i)),
            pl.BlockSpec((half, tm), lambda b, i: (0, i)),
        ],
        out_specs=[
            pl.BlockSpec((None, heads, None, hd, tm), lambda b, i: (b, 0, i, 0, 0)),
            pl.BlockSpec((None, heads, tm, hd), lambda b, i: (b, 0, i, 0)),
            pl.BlockSpec((None, heads, tm // tk, vrows, tk), lambda b, i: (b, 0, i, 0, 0)),
        ],
        out_shape=[
            jax.ShapeDtypeStruct((B, heads, n_t, hd, tm), _BF16),
            jax.ShapeDtypeStruct((B, heads, S, hd), _BF16),
            jax.ShapeDtypeStruct((B, heads, S // tk, vrows, tk), _BF16),
        ],
        compiler_params=cparams2,
        name="mla_proj",
    )(x, w_in_ext, _row(mla_g_q[0]), _row(mla_g_kv[0]), wqT, wuk, wvT, cs, cosT, sinT)

    o = pl.pallas_call(
        _mla_attn_kernel,
        grid=(B, heads),
        in_specs=[
            pl.BlockSpec((None, None, n_t, hd, tm), lambda b, h: (b, h, 0, 0, 0)),
            pl.BlockSpec((None, None, S, hd), lambda b, h: (b, h, 0, 0)),
            pl.BlockSpec((None, None, S // tk, vrows, tk), lambda b, h: (b, h, 0, 0, 0)),
        ],
        out_specs=pl.BlockSpec((None, S, vdim), lambda b, h: (b, 0, h)),
        out_shape=jax.ShapeDtypeStruct((B, S, heads * vdim), _BF16),
        scratch_shapes=[
            pltpu.VMEM((1, ta), _F32),
            pltpu.VMEM((vrows, ta), _F32),
        ] + [pltpu.VMEM((tk, ta_pad), _F32)] * 4
          + [pltpu.VMEM((1, ta), _F32)] * 4
          + [pltpu.VMEM((tk, ta), _BF16)] * 2
          + [pltpu.VMEM((1, ta), _F32)] * 2,
        compiler_params=cparams2,
        name="mla_attn",
    )(qT_all, k_all, vT_all)

    d_ff = mlp_w_up.shape[2]
    mlp_scratch = [
        pltpu.VMEM((tm, D), _F32),
        pltpu.VMEM((tm, d_ff), _BF16),
    ]
    x1 = pl.pallas_call(
        functools.partial(_outproj_mlp_kernel, alpha=alpha),
        grid=(T // tm,),
        in_specs=[
            pl.BlockSpec((tm, heads * vdim), lambda i: (i, 0)),
            pl.BlockSpec((tm, D), lambda i: (i, 0)),
            _const_spec((heads * vdim, D)),
            _const_spec((1, D)),
            _const_spec((1, D)),
            _const_spec((D, d_ff)),
            _const_spec((d_ff, D)),
            _const_spec((1, D)),
            _const_spec((1, D)),
        ],
        out_specs=pl.BlockSpec((tm, D), lambda i: (i, 0)),
        out_shape=jax.ShapeDtypeStruct((T, D), _F32),
        scratch_shapes=mlp_scratch,
        compiler_params=cparams1,
        name="mla_out_mlp",
    )(o.reshape(T, heads * vdim), x.reshape(T, D), mla_w_o[0].astype(_BF16),
      _row(ln_mix_g[0]), _row(ln_mix_b[0]), mlp_w_up[0].astype(_BF16),
      mlp_w_down[0].astype(_BF16), _row(ln_mlp_g[0]), _row(ln_mlp_b[0]))

    q_heads = swa_sinks.shape[1]
    s_hd = swa_w_q.shape[2] // q_heads
    kv_heads = kv_w_shared.shape[1] // (2 * s_hd)
    group = q_heads // kv_heads
    blk = _SWA_BLOCK
    assert _LANES % s_hd == 0 and tm % blk == 0
    wqT1 = swa_w_q[0].T.astype(_BF16)
    wk1 = kv_w_shared[:, :kv_heads * s_hd].astype(_BF16)
    wvT1 = kv_w_shared[:, kv_heads * s_hd:].T.astype(_BF16)
    bucketT = jnp.asarray(_t5_bucket_table(blk))
    n_units = (tm // blk) * kv_heads

    smem = pl.BlockSpec(memory_space=pltpu.SMEM)
    out = pl.pallas_call(
        functools.partial(_swa_mlp_kernel, alpha=alpha, q_heads=q_heads, kv_heads=kv_heads,
                          head_dim=s_hd, q_scale=s_hd ** -0.5 * _LOG2E),
        grid=(B, n_t),
        in_specs=[
            pl.BlockSpec((None, tm, D), lambda b, i: (b, i, 0)),
            _const_spec(wqT1.shape),
            _const_spec(wk1.shape),
            _const_spec(wvT1.shape),
            _const_spec((q_heads * s_hd, D)),
            _const_spec(bucketT.shape),
            smem,
            smem,
            _const_spec((1, D)),
            _const_spec((1, D)),
            _const_spec((D, d_ff)),
            _const_spec((d_ff, D)),
            _const_spec((1, D)),
            _const_spec((1, D)),
        ],
        out_specs=pl.BlockSpec((None, tm, D), lambda b, i: (b, i, 0)),
        out_shape=jax.ShapeDtypeStruct((B, S, D), _F32),
        scratch_shapes=[
            pltpu.VMEM((blk + tm, kv_heads * s_hd), _BF16),
            pltpu.VMEM((kv_heads * (s_hd + bf16_rows), blk + tm), _BF16),
            pltpu.VMEM((q_heads * s_hd, tm), _BF16),
            pltpu.VMEM((kv_heads, 2 * blk, group * blk), _F32),
            pltpu.VMEM((n_units, 2 * blk, group * blk + _LANES), _F32),
            pltpu.VMEM((n_units, 2 * blk, group * blk), _BF16),
            pltpu.VMEM((n_units, 1, group * blk), _F32),
            pltpu.VMEM((q_heads * s_hd, tm), _BF16),
        ] + mlp_scratch,
        compiler_params=cparams2,
        name="swa_mlp",
    )(x1.reshape(B, S, D), wqT1, wk1, wvT1, swa_w_o[0].astype(_BF16), bucketT,
      rel_bias.astype(_F32), swa_sinks[0].astype(_F32),
      _row(ln_mix_g[1]), _row(ln_mix_b[1]), mlp_w_up[1].astype(_BF16),
      mlp_w_down[1].astype(_BF16), _row(ln_mlp_g[1]), _row(ln_mlp_b[1]))
    return out
```

```python
import functools
import math

import numpy as np
import jax
import jax.numpy as jnp
from jax import lax
from jax.experimental import pallas as pl
from jax.experimental.pallas import tpu as pltpu

_F32 = jnp.float32
_BF16 = jnp.bfloat16

_LN_EPS = 1e-5
_RMS_EPS = 1e-6
_ROPE_THETA = 10000.0
_SWA_BLOCK = 128
_REL_BUCKETS = 32
_REL_MAX_DIST = 128
_LOG2E = math.log2(math.e)

_V7X_VMEM_BYTES = 64 * 1024 * 1024
_VMEM_LIMIT_BYTES = _V7X_VMEM_BYTES - 8 * 1024 * 1024
_LANES = 128

_TOKEN_TILE = 512
_ATTN_TILE = 1024
_ATTN_COL_BLOCK = 256
_FF_CHUNK = 1024
_MLP_ROW_BLOCK = 256

_NT = (((1,), (1,)), ((), ()))
_TN = (((0,), (0,)), ((), ()))


def _const_spec(shape):
    nd = len(shape)
    return pl.BlockSpec(shape, lambda *_: (0,) * nd, pipeline_mode=pl.Buffered(1))


def _layernorm(v, g, b):
    mu = jnp.mean(v, axis=-1, keepdims=True)
    d = v - mu
    var = jnp.mean(d * d, axis=-1, keepdims=True)
    return d * lax.rsqrt(var + _LN_EPS) * g + b


def _rmsnorm(v, g):
    return v * lax.rsqrt(jnp.mean(v * v, axis=-1, keepdims=True) + _RMS_EPS) * g


def _mix_mlp_ln(mix_rows, x_ref, alpha, g1_ref, b1_ref, wup_ref, wdown_ref, g2_ref, b2_ref,
                out_ref, x1_scr, h_scr):
    tm = x_ref.shape[0]
    d_ff = wup_ref.shape[1]
    blocks = [slice(r, r + _MLP_ROW_BLOCK) for r in range(0, tm, _MLP_ROW_BLOCK)]
    for rows in blocks:
        x1_scr[rows, :] = _layernorm(alpha * x_ref[rows, :] + mix_rows(rows), g1_ref[...], b1_ref[...])
    for rows in blocks:
        x1b = x1_scr[rows, :].astype(_BF16)
        for lo in range(0, d_ff, _FF_CHUNK):
            hmid = jnp.dot(x1b, wup_ref[:, lo:lo + _FF_CHUNK], preferred_element_type=_F32)
            hmid = jnp.maximum(hmid, 0.0)
            h_scr[rows, lo:lo + _FF_CHUNK] = (hmid * hmid).astype(_BF16)
    for rows in blocks:
        y = jnp.dot(h_scr[rows, :], wdown_ref[...], preferred_element_type=_F32)
        out_ref[rows, :] = _layernorm(alpha * x1_scr[rows, :] + y, g2_ref[...], b2_ref[...])


def _mla_proj_kernel(x_ref, win_ref, gq_ref, gkv_ref, wqT_ref, wuk_ref, wvT_ref,
                     cs_ref, cosT_ref, sinT_ref, qT_ref, k_ref, vT_ref,
                     *, q_rank, kv_rank, heads, nope, rope, q_scale):
    xb = x_ref[...].astype(_BF16)
    h = jnp.dot(xb, win_ref[...], preferred_element_type=_F32)
    cq = _rmsnorm(h[:, :q_rank], gq_ref[...]).astype(_BF16)
    ckv = _rmsnorm(h[:, q_rank:q_rank + kv_rank], gkv_ref[...]).astype(_BF16)
    t = h[:, q_rank + kv_rank:] * cs_ref[...]
    kr = (t + pltpu.roll(t, rope, 1))[:, :rope].astype(_BF16)

    half = rope // 2
    hd = nope + rope
    qT = lax.dot_general(wqT_ref[...], cq, _NT, preferred_element_type=_F32)
    cosT = cosT_ref[...]
    sinT = sinT_ref[...]
    for hh in range(heads):
        base = hh * hd
        qT_ref[hh, 0:nope, :] = (qT[base:base + nope] * q_scale).astype(_BF16)
        x1 = qT[base + nope:base + nope + half]
        x2 = qT[base + nope + half:base + hd]
        qT_ref[hh, nope:nope + half, :] = ((x1 * cosT - x2 * sinT) * q_scale).astype(_BF16)
        qT_ref[hh, nope + half:hd, :] = ((x1 * sinT + x2 * cosT) * q_scale).astype(_BF16)

    kn = jnp.dot(ckv, wuk_ref[...], preferred_element_type=_F32)
    for hh in range(heads):
        k_ref[hh, :, 0:nope] = kn[:, hh * nope:(hh + 1) * nope].astype(_BF16)
        k_ref[hh, :, nope:hd] = kr

    vdim = wvT_ref.shape[0] // heads
    vT = lax.dot_general(wvT_ref[...], ckv, _NT, preferred_element_type=_F32)
    tk = vT_ref.shape[3]
    pad = vT_ref.shape[2] - vdim
    ones_rows = (lax.broadcasted_iota(jnp.int32, (pad, tk), 0) == 0).astype(_BF16)
    for hh in range(heads):
        for j in range(vT_ref.shape[1]):
            vT_ref[hh, j, 0:vdim, :] = vT[hh * vdim:(hh + 1) * vdim, j * tk:(j + 1) * tk].astype(_BF16)
            vT_ref[hh, j, vdim:vdim + pad, :] = ones_rows


def _mla_attn_kernel(qT_ref, k_ref, vT_ref, o_ref, m_scr, acc_scr,
                     s00, s01, s10, s11, cm00, cm01, cm10, cm11, p_a, p_b, al_a, al_b):
    tk, tq = s00.shape[0], acc_scr.shape[1]
    tsub = qT_ref.shape[2]
    nsub = tq // tsub
    n_q = qT_ref.shape[0] // nsub
    assert tq == 2 * tk and tk == vT_ref.shape[2]

    cb = _ATTN_COL_BLOCK

    def q_stage(qi, c, s_ref, cm_ref, col_lo=0):
        k = k_ref[pl.ds(pl.multiple_of(c * tk, tk), tk), :]
        for lo in range(col_lo, tq, cb):
            j, off = divmod(lo, tsub)
            qT = qT_ref[qi * nsub + j, :, off:off + cb]
            s = jnp.dot(k, qT, preferred_element_type=_F32)
            s_ref[:, lo:lo + cb] = s
            cm_ref[:, lo:lo + cb] = jnp.max(s, axis=0, keepdims=True)

    def x_stage(s_ref, cm_ref, p_ref, al_ref, mask_off=None, col_lo=0):
        for lo in range(col_lo, tq, cb):
            cols = slice(lo, lo + cb)
            s = s_ref[:, cols]
            if mask_off is None:
                cm = cm_ref[:, cols]
            else:
                krel = lax.broadcasted_iota(jnp.int32, s.shape, 0) + mask_off
                qrel = lax.broadcasted_iota(jnp.int32, s.shape, 1) + lo
                s = jnp.where(krel <= qrel, s, -jnp.inf)
                cm = jnp.max(s, axis=0, keepdims=True)
            m_prev = m_scr[:, cols]
            m_new = jnp.maximum(m_prev, cm)
            alpha = jnp.exp2(m_prev - m_new)
            p = jnp.exp2(s - m_new)
            p_ref[:, cols] = p.astype(_BF16)
            al_ref[:, cols] = alpha
            m_scr[:, cols] = m_new

    def v_stage(c, p_ref, al_ref, col_lo=0):
        for lo in range(col_lo, tq, cb):
            cols = slice(lo, lo + cb)
            acc_scr[:, cols] = al_ref[:, cols] * acc_scr[:, cols] + jnp.dot(
                vT_ref[c], p_ref[:, cols], preferred_element_type=_F32)

    s_bufs = (((s00, cm00), (s01, cm01)), ((s10, cm10), (s11, cm11)))

    def start_tile():
        m_scr[...] = jnp.full(m_scr.shape, -jnp.inf, _F32)
        acc_scr[...] = jnp.zeros(acc_scr.shape, _F32)
        p_b[...] = jnp.zeros(p_b.shape, p_b.dtype)
        al_b[...] = jnp.ones(al_b.shape, _F32)

    def regular_step(qi, a, par):
        (sa, cma), (sb, cmb) = s_bufs[par]
        (na, ncma), (nb, ncmb) = s_bufs[1 - par]
        q_stage(qi, a + 2, na, ncma)
        x_stage(sa, cma, p_a, al_a)
        v_stage(jnp.maximum(a - 1, 0), p_b, al_b)
        q_stage(qi, a + 3, nb, ncmb)
        x_stage(sb, cmb, p_b, al_b)
        v_stage(a, p_a, al_a)

    def diagonal_step(qi, par):
        a = 2 * qi
        (sa, cma), (sb, cmb) = s_bufs[par]
        (na, ncma), (nb, ncmb) = s_bufs[1 - par]
        q_next = jnp.minimum(qi + 1, n_q - 1)
        q_stage(q_next, 0, na, ncma)
        x_stage(sa, cma, p_a, al_a, mask_off=0)
        v_stage(jnp.maximum(a - 1, 0), p_b, al_b)
        q_stage(q_next, 1, nb, ncmb)
        x_stage(sb, cmb, p_b, al_b, mask_off=tk, col_lo=tk)
        v_stage(a, p_a, al_a)
        v_stage(a + 1, p_b, al_b, col_lo=tk)
        vdim = o_ref.shape[1]
        inv_l = 1.0 / acc_scr[vdim:vdim + 1, :]
        o = (acc_scr[0:vdim, :] * inv_l).T
        o_ref[pl.ds(pl.multiple_of(qi * tq, tq), tq), :] = o.astype(o_ref.dtype)
        start_tile()

    def by_parity(par, fn):
        for static_par in range(2):
            @pl.when(par == static_par)
            def _():
                fn(static_par)

    def tile_body(qi, par):
        def step_body(u, par):
            by_parity(par, functools.partial(regular_step, qi, 2 * u))
            return 1 - par

        par = lax.fori_loop(0, qi, step_body, par)
        by_parity(par, functools.partial(diagonal_step, qi))
        return 1 - par

    start_tile()
    q_stage(0, 0, s00, cm00)
    q_stage(0, 1, s01, cm01)
    lax.fori_loop(0, n_q, tile_body, jnp.int32(0))


def _outproj_mlp_kernel(o_ref, x_ref, wo_ref, g1_ref, b1_ref, wup_ref, wdown_ref,
                        g2_ref, b2_ref, out_ref, x1_scr, h_scr, *, alpha):
    def mix_rows(rows):
        return jnp.dot(o_ref[rows, :], wo_ref[...], preferred_element_type=_F32)

    _mix_mlp_ln(mix_rows, x_ref, alpha, g1_ref, b1_ref, wup_ref, wdown_ref, g2_ref, b2_ref,
                out_ref, x1_scr, h_scr)


def _swa_mlp_kernel(x_ref, wqT_ref, wk_ref, wvT_ref, wo_ref, bucketT_ref, relb_ref, sink_ref,
                    g1_ref, b1_ref, wup_ref, wdown_ref, g2_ref, b2_ref, out_ref,
                    k_scr, vTe_scr, qT_scr, biasT_scr, s_scr, p_scr, m_scr, oT_scr, x1_scr, h_scr,
                    *, alpha, q_heads, kv_heads, head_dim, q_scale):
    blk = _SWA_BLOCK
    group = q_heads // kv_heads
    tm = x_ref.shape[0]
    n_blk = tm // blk
    vrows = vTe_scr.shape[0] // kv_heads
    first_tile = pl.program_id(1) == 0
    always = pl.program_id(1) >= 0

    @pl.when((pl.program_id(0) == 0) & first_tile)
    def _():
        bkt = bucketT_ref[...]
        for hq in range(q_heads):
            bias = jnp.full(bkt.shape, -jnp.inf, _F32)
            for bb in range(_REL_BUCKETS):
                bias = jnp.where(bkt == bb, relb_ref[bb, hq] * _LOG2E, bias)
            g, gi = divmod(hq, group)
            biasT_scr[g, :, gi * blk:(gi + 1) * blk] = bias

    @pl.when(first_tile)
    def _():
        k_scr[0:blk, :] = jnp.zeros((blk, k_scr.shape[1]), k_scr.dtype)
        vTe_scr[:, 0:blk] = jnp.zeros((vTe_scr.shape[0], blk), vTe_scr.dtype)

    xb = x_ref[...].astype(_BF16)
    qT_scr[...] = (lax.dot_general(wqT_ref[...], xb, _NT, preferred_element_type=_F32)
                   * q_scale).astype(_BF16)
    k_scr[blk:blk + tm, :] = jnp.dot(xb, wk_ref[...], preferred_element_type=_F32).astype(_BF16)
    vT = lax.dot_general(wvT_ref[...], xb, _NT, preferred_element_type=_F32).astype(_BF16)
    ones_rows = (lax.broadcasted_iota(jnp.int32, (vrows - head_dim, tm), 0) == 0).astype(_BF16)
    for g in range(kv_heads):
        vTe_scr[g * vrows:g * vrows + head_dim, blk:blk + tm] = vT[g * head_dim:(g + 1) * head_dim]
        vTe_scr[g * vrows + head_dim:(g + 1) * vrows, blk:blk + tm] = ones_rows

    first_pen = jnp.where(first_tile, -jnp.inf, 0.0).astype(_F32)
    zeros_q = jnp.zeros((head_dim, group * blk), _BF16)
    heads_per_lane_tile = _LANES // head_dim
    units = [(n, g) for n in range(n_blk) for g in range(kv_heads)]

    def sink_row(g):
        return jnp.concatenate(
            [jnp.full((1, blk), sink_ref[g * group + gi] * _LOG2E, _F32) for gi in range(group)],
            axis=1)

    sinks = [None] * kv_heads

    def scores(u):
        n, g = units[u]
        lt = g // heads_per_lane_tile
        kpair = k_scr[n * blk:(n + 2) * blk, lt * _LANES:(lt + 1) * _LANES]
        qg = jnp.concatenate(
            [qT_scr[(g * group + gi) * head_dim:(g * group + gi + 1) * head_dim,
                    n * blk:(n + 1) * blk] for gi in range(group)], axis=1)
        pieces = [zeros_q] * heads_per_lane_tile
        pieces[g % heads_per_lane_tile] = qg
        qpad = jnp.concatenate(pieces, axis=0)
        s = jnp.dot(kpair, qpad, preferred_element_type=_F32) + biasT_scr[g]
        if n == 0:
            s = jnp.concatenate([s[:blk] + first_pen, s[blk:]], axis=0)
        s_scr[u] = s
        m_scr[u] = jnp.maximum(jnp.max(s, axis=0, keepdims=True), sinks[g])

    def exponentials(u):
        p_scr[u] = jnp.exp2(s_scr[u] - m_scr[u]).astype(_BF16)

    def weighted_values(u):
        n, g = units[u]
        oT = jnp.dot(vTe_scr[g * vrows:(g + 1) * vrows, n * blk:(n + 2) * blk], p_scr[u],
                     preferred_element_type=_F32)
        denom = oT[head_dim:head_dim + 1] + jnp.exp2(sinks[g] - m_scr[u])
        o = (oT[:head_dim] * (1.0 / denom)).astype(_BF16)
        for gi in range(group):
            hq = g * group + gi
            oT_scr[hq * head_dim:(hq + 1) * head_dim, n * blk:(n + 1) * blk] = (
                o[:, gi * blk:(gi + 1) * blk])

    @pl.when(always)
    def _():
        for g in range(kv_heads):
            sinks[g] = sink_row(g)
        stages = (scores, exponentials, weighted_values)
        for t in range(len(units) + len(stages) - 1):
            for lag, stage in enumerate(stages):
                if 0 <= t - lag < len(units):
                    stage(t - lag)
        k_scr[0:blk, :] = k_scr[tm:tm + blk, :]
        vTe_scr[:, 0:blk] = vTe_scr[:, tm:tm + blk]

    def mix_rows(rows):
        return lax.dot_general(oT_scr[:, rows], wo_ref[...], _TN, preferred_element_type=_F32)

    _mix_mlp_ln(mix_rows, x_ref, alpha, g1_ref, b1_ref, wup_ref, wdown_ref, g2_ref, b2_ref,
                out_ref, x1_scr, h_scr)


def _t5_bucket_table(blk):
    i = np.arange(blk)[None, :]
    j = np.arange(2 * blk)[:, None]
    dist = i + blk - j
    max_exact = _REL_BUCKETS // 2
    nf = np.maximum(dist, 1).astype(np.float32)
    large = max_exact + (np.log(nf / np.float32(max_exact))
                         / np.float32(math.log(_REL_MAX_DIST / max_exact))
                         * np.float32(_REL_BUCKETS - max_exact)).astype(np.int32)
    large = np.minimum(large, _REL_BUCKETS - 1)
    bucket = np.where(dist < max_exact, np.maximum(dist, 0), large)
    valid = (dist >= 0) & (dist < blk)
    return np.where(valid, bucket, -1).astype(np.int32)


def _rope_tables(seq, rope):
    half = rope // 2
    inv = _ROPE_THETA ** (-jnp.arange(half, dtype=_F32) / half)
    ang = jnp.arange(seq).astype(_F32)[:, None] * inv[None, :]
    cos, sin = jnp.cos(ang), jnp.sin(ang)
    cs = jnp.concatenate([cos, cos, -sin, sin], axis=1)
    return cs, cos.T, sin.T


def _row(v):
    return v.reshape(1, -1).astype(_F32)


def kernel(x, mla_w_in, mla_g_q, mla_g_kv, mla_w_uq, mla_w_uk, mla_w_uv, mla_w_o, kv_w_shared,
           swa_w_q, swa_sinks, swa_w_o, rel_bias, mlp_w_up, mlp_w_down, ln_mix_g, ln_mix_b,
           ln_mlp_g, ln_mlp_b):
    B, S, D = x.shape
    depth = mlp_w_up.shape[0]
    assert depth == 2 and mla_w_in.shape[0] == 1 and swa_w_q.shape[0] == 1
    alpha = (2 * depth) ** 0.25
    T = B * S
    tm = _TOKEN_TILE
    ta = _ATTN_TILE
    tk = ta // 2
    bf16_rows = 16
    ta_pad = ta + _LANES
    assert ta % tm == 0 and tm % tk == 0 and S % ta == 0

    q_rank, heads, hd = mla_w_uq.shape[1:]
    kv_rank, _, nope = mla_w_uk.shape[1:]
    vdim = mla_w_uv.shape[3]
    vrows = vdim + bf16_rows
    rope = hd - nope
    half = rope // 2
    assert 2 * rope == _LANES and mla_w_in.shape[2] == q_rank + kv_rank + rope

    w_in = mla_w_in[0]
    r0 = q_rank + kv_rank
    w_in_ext = jnp.concatenate(
        [w_in, w_in[:, r0 + half:r0 + rope], w_in[:, r0:r0 + half]], axis=1).astype(_BF16)
    wqT = mla_w_uq[0].reshape(q_rank, heads * hd).T.astype(_BF16)
    wuk = mla_w_uk[0].reshape(kv_rank, heads * nope).astype(_BF16)
    wvT = mla_w_uv[0].reshape(kv_rank, heads * vdim).T.astype(_BF16)
    cs, cosT, sinT = _rope_tables(S, rope)
    n_t = S // tm
    q_scale = hd ** -0.5 * _LOG2E

    cparams2 = pltpu.CompilerParams(dimension_semantics=("arbitrary", "arbitrary"),
                                    vmem_limit_bytes=_VMEM_LIMIT_BYTES)
    cparams1 = pltpu.CompilerParams(dimension_semantics=("arbitrary",),
                                    vmem_limit_bytes=_VMEM_LIMIT_BYTES)

    qT_all, k_all, vT_all = pl.pallas_call(
        functools.partial(_mla_proj_kernel, q_rank=q_rank, kv_rank=kv_rank, heads=heads,
                          nope=nope, rope=rope, q_scale=q_scale),
        grid=(B, n_t),
        in_specs=[
            pl.BlockSpec((None, tm, D), lambda b, i: (b, i, 0)),
            _const_spec(w_in_ext.shape),
            _const_spec((1, q_rank)),
            _const_spec((1, kv_rank)),
            _const_spec(wqT.shape),
            _const_spec(wuk.shape),
            _const_spec(wvT.shape),
            pl.BlockSpec((tm, 2 * rope), lambda b, i: (i, 0)),
            pl.BlockSpec((half, tm), lambda b, i: (0, i)),
            pl.BlockSpec((half, tm), lambda b, i: (0, i)),
        ],
        out_specs=[
            pl.BlockSpec((None, heads, None, hd, tm), lambda b, i: (b, 0, i, 0, 0)),
            pl.BlockSpec((None, heads, tm, hd), lambda b, i: (b, 0, i, 0)),
            pl.BlockSpec((None, heads, tm // tk, vrows, tk), lambda b, i: (b, 0, i, 0, 0)),
        ],
        out_shape=[
            jax.ShapeDtypeStruct((B, heads, n_t, hd, tm), _BF16),
            jax.ShapeDtypeStruct((B, heads, S, hd), _BF16),
            jax.ShapeDtypeStruct((B, heads, S // tk, vrows, tk), _BF16),
        ],
        compiler_params=cparams2,
        name="mla_proj",
    )(x, w_in_ext, _row(mla_g_q[0]), _row(mla_g_kv[0]), wqT, wuk, wvT, cs, cosT, sinT)

    o = pl.pallas_call(
        _mla_attn_kernel,
        grid=(B, heads),
        in_specs=[
            pl.BlockSpec((None, None, n_t, hd, tm), lambda b, h: (b, h, 0, 0, 0)),
            pl.BlockSpec((None, None, S, hd), lambda b, h: (b, h, 0, 0)),
            pl.BlockSpec((None, None, S // tk, vrows, tk), lambda b, h: (b, h, 0, 0, 0)),
        ],
        out_specs=pl.BlockSpec((None, S, vdim), lambda b, h: (b, 0, h)),
        out_shape=jax.ShapeDtypeStruct((B, S, heads * vdim), _BF16),
        scratch_shapes=[
            pltpu.VMEM((1, ta), _F32),
            pltpu.VMEM((vrows, ta), _F32),
        ] + [pltpu.VMEM((tk, ta_pad), _F32)] * 4
          + [pltpu.VMEM((1, ta), _F32)] * 4
          + [pltpu.VMEM((tk, ta), _BF16)] * 2
          + [pltpu.VMEM((1, ta), _F32)] * 2,
        compiler_params=cparams2,
        name="mla_attn",
    )(qT_all, k_all, vT_all)

    d_ff = mlp_w_up.shape[2]
    mlp_scratch = [
        pltpu.VMEM((tm, D), _F32),
        pltpu.VMEM((tm, d_ff), _BF16),
    ]
    x1 = pl.pallas_call(
        functools.partial(_outproj_mlp_kernel, alpha=alpha),
        grid=(T // tm,),
        in_specs=[
            pl.BlockSpec((tm, heads * vdim), lambda i: (i, 0)),
            pl.BlockSpec((tm, D), lambda i: (i, 0)),
            _const_spec((heads * vdim, D)),
            _const_spec((1, D)),
            _const_spec((1, D)),
            _const_spec((D, d_ff)),
            _const_spec((d_ff, D)),
            _const_spec((1, D)),
            _const_spec((1, D)),
        ],
        out_specs=pl.BlockSpec((tm, D), lambda i: (i, 0)),
        out_shape=jax.ShapeDtypeStruct((T, D), _F32),
        scratch_shapes=mlp_scratch,
        compiler_params=cparams1,
        name="mla_out_mlp",
    )(o.reshape(T, heads * vdim), x.reshape(T, D), mla_w_o[0].astype(_BF16),
      _row(ln_mix_g[0]), _row(ln_mix_b[0]), mlp_w_up[0].astype(_BF16),
      mlp_w_down[0].astype(_BF16), _row(ln_mlp_g[0]), _row(ln_mlp_b[0]))

    q_heads = swa_sinks.shape[1]
    s_hd = swa_w_q.shape[2] // q_heads
    kv_heads = kv_w_shared.shape[1] // (2 * s_hd)
    group = q_heads // kv_heads
    blk = _SWA_BLOCK
    assert _LANES % s_hd == 0 and tm % blk == 0
    wqT1 = swa_w_q[0].T.astype(_BF16)
    wk1 = kv_w_shared[:, :kv_heads * s_hd].astype(_BF16)
    wvT1 = kv_w_shared[:, kv_heads * s_hd:].T.astype(_BF16)
    bucketT = jnp.asarray(_t5_bucket_table(blk))
    n_units = (tm // blk) * kv_heads

    smem = pl.BlockSpec(memory_space=pltpu.SMEM)
    out = pl.pallas_call(
        functools.partial(_swa_mlp_kernel, alpha=alpha, q_heads=q_heads, kv_heads=kv_heads,
                          head_dim=s_hd, q_scale=s_hd ** -0.5 * _LOG2E),
        grid=(B, n_t),
        in_specs=[
            pl.BlockSpec((None, tm, D), lambda b, i: (b, i, 0)),
            _const_spec(wqT1.shape),
            _const_spec(wk1.shape),
            _const_spec(wvT1.shape),
            _const_spec((q_heads * s_hd, D)),
            _const_spec(bucketT.shape),
            smem,
            smem,
            _const_spec((1, D)),
            _const_spec((1, D)),
            _const_spec((D, d_ff)),
            _const_spec((d_ff, D)),
            _const_spec((1, D)),
            _const_spec((1, D)),
        ],
        out_specs=pl.BlockSpec((None, tm, D), lambda b, i: (b, i, 0)),
        out_shape=jax.ShapeDtypeStruct((B, S, D), _F32),
        scratch_shapes=[
            pltpu.VMEM((blk + tm, kv_heads * s_hd), _BF16),
            pltpu.VMEM((kv_heads * (s_hd + bf16_rows), blk + tm), _BF16),
            pltpu.VMEM((q_heads * s_hd, tm), _BF16),
            pltpu.VMEM((kv_heads, 2 * blk, group * blk), _F32),
            pltpu.VMEM((n_units, 2 * blk, group * blk), _F32),
            pltpu.VMEM((n_units, 2 * blk, group * blk), _BF16),
            pltpu.VMEM((n_units, 1, group * blk), _F32),
            pltpu.VMEM((q_heads * s_hd, tm), _BF16),
        ] + mlp_scratch,
        compiler_params=cparams2,
        name="swa_mlp",
    )(x1.reshape(B, S, D), wqT1, wk1, wvT1, swa_w_o[0].astype(_BF16), bucketT,
      rel_bias.astype(_F32), swa_sinks[0].astype(_F32),
      _row(ln_mix_g[1]), _row(ln_mix_b[1]), mlp_w_up[1].astype(_BF16),
      mlp_w_down[1].astype(_BF16), _row(ln_mlp_g[1]), _row(ln_mlp_b[1]))
    return out
```

```python
import functools
import math

import numpy as np
import jax
import jax.numpy as jnp
from jax import lax
from jax.experimental import pallas as pl
from jax.experimental.pallas import tpu as pltpu

_F32 = jnp.float32
_BF16 = jnp.bfloat16

_LN_EPS = 1e-5
_RMS_EPS = 1e-6
_ROPE_THETA = 10000.0
_SWA_BLOCK = 128
_REL_BUCKETS = 32
_REL_MAX_DIST = 128
_LOG2E = math.log2(math.e)

_V7X_VMEM_BYTES = 64 * 1024 * 1024
_VMEM_LIMIT_BYTES = _V7X_VMEM_BYTES - 8 * 1024 * 1024
_LANES = 128

_TOKEN_TILE = 512
_ATTN_TILE = 1024
_ATTN_COL_BLOCK = 256
_FF_CHUNK = 1024
_MLP_ROW_BLOCK = 256

_NT = (((1,), (1,)), ((), ()))
_TN = (((0,), (0,)), ((), ()))


def _const_spec(shape):
    nd = len(shape)
    return pl.BlockSpec(shape, lambda *_: (0,) * nd, pipeline_mode=pl.Buffered(1))


def _layernorm(v, g, b):
    mu = jnp.mean(v, axis=-1, keepdims=True)
    d = v - mu
    var = jnp.mean(d * d, axis=-1, keepdims=True)
    return d * lax.rsqrt(var + _LN_EPS) * g + b


def _rmsnorm(v, g):
    return v * lax.rsqrt(jnp.mean(v * v, axis=-1, keepdims=True) + _RMS_EPS) * g


def _mix_mlp_ln(mix_rows, x_ref, alpha, g1_ref, b1_ref, wup_ref, wdown_ref, g2_ref, b2_ref,
                out_ref, x1_scr, h_scr):
    tm = x_ref.shape[0]
    d_ff = wup_ref.shape[1]
    blocks = [slice(r, r + _MLP_ROW_BLOCK) for r in range(0, tm, _MLP_ROW_BLOCK)]
    for rows in blocks:
        x1_scr[rows, :] = _layernorm(alpha * x_ref[rows, :] + mix_rows(rows), g1_ref[...], b1_ref[...])
    for rows in blocks:
        x1b = x1_scr[rows, :].astype(_BF16)
        for lo in range(0, d_ff, _FF_CHUNK):
            hmid = jnp.dot(x1b, wup_ref[:, lo:lo + _FF_CHUNK], preferred_element_type=_F32)
            hmid = jnp.maximum(hmid, 0.0)
            h_scr[rows, lo:lo + _FF_CHUNK] = (hmid * hmid).astype(_BF16)
    for rows in blocks:
        y = jnp.dot(h_scr[rows, :], wdown_ref[...], preferred_element_type=_F32)
        out_ref[rows, :] = _layernorm(alpha * x1_scr[rows, :] + y, g2_ref[...], b2_ref[...])


def _mla_proj_kernel(x_ref, win_ref, gq_ref, gkv_ref, wqT_ref, wuk_ref, wvT_ref,
                     cs_ref, cosT_ref, sinT_ref, qT_ref, k_ref, vT_ref,
                     *, q_rank, kv_rank, heads, nope, rope, q_scale):
    xb = x_ref[...].astype(_BF16)
    h = jnp.dot(xb, win_ref[...], preferred_element_type=_F32)
    cq = _rmsnorm(h[:, :q_rank], gq_ref[...]).astype(_BF16)
    ckv = _rmsnorm(h[:, q_rank:q_rank + kv_rank], gkv_ref[...]).astype(_BF16)
    t = h[:, q_rank + kv_rank:] * cs_ref[...]
    kr = (t + pltpu.roll(t, rope, 1))[:, :rope].astype(_BF16)

    half = rope // 2
    hd = nope + rope
    qT = lax.dot_general(wqT_ref[...], cq, _NT, preferred_element_type=_F32)
    cosT = cosT_ref[...]
    sinT = sinT_ref[...]
    for hh in range(heads):
        base = hh * hd
        qT_ref[hh, 0:nope, :] = (qT[base:base + nope] * q_scale).astype(_BF16)
        x1 = qT[base + nope:base + nope + half]
        x2 = qT[base + nope + half:base + hd]
        qT_ref[hh, nope:nope + half, :] = ((x1 * cosT - x2 * sinT) * q_scale).astype(_BF16)
        qT_ref[hh, nope + half:hd, :] = ((x1 * sinT + x2 * cosT) * q_scale).astype(_BF16)

    kn = jnp.dot(ckv, wuk_ref[...], preferred_element_type=_F32)
    for hh in range(heads):
        k_ref[hh, :, 0:nope] = kn[:, hh * nope:(hh + 1) * nope].astype(_BF16)
        k_ref[hh, :, nope:hd] = kr

    vdim = wvT_ref.shape[0] // heads
    vT = lax.dot_general(wvT_ref[...], ckv, _NT, preferred_element_type=_F32)
    tk = vT_ref.shape[3]
    pad = vT_ref.shape[2] - vdim
    ones_rows = (lax.broadcasted_iota(jnp.int32, (pad, tk), 0) == 0).astype(_BF16)
    for hh in range(heads):
        for j in range(vT_ref.shape[1]):
            vT_ref[hh, j, 0:vdim, :] = vT[hh * vdim:(hh + 1) * vdim, j * tk:(j + 1) * tk].astype(_BF16)
            vT_ref[hh, j, vdim:vdim + pad, :] = ones_rows


def _mla_attn_kernel(qT_ref, k_ref, vT_ref, o_ref, m_scr, acc_scr,
                     s00, s01, s10, s11, cm00, cm01, cm10, cm11, p_a, p_b, al_a, al_b):
    tk, tq = s00.shape[0], acc_scr.shape[1]
    tsub = qT_ref.shape[2]
    nsub = tq // tsub
    n_q = qT_ref.shape[0] // nsub
    assert tq == 2 * tk and tk == vT_ref.shape[2]

    cb = _ATTN_COL_BLOCK

    def q_stage(qi, c, s_ref, cm_ref, col_lo=0):
        k = k_ref[pl.ds(pl.multiple_of(c * tk, tk), tk), :]
        for lo in range(col_lo, tq, cb):
            j, off = divmod(lo, tsub)
            qT = qT_ref[qi * nsub + j, :, off:off + cb]
            s = jnp.dot(k, qT, preferred_element_type=_F32)
            s_ref[:, lo:lo + cb] = s
            cm_ref[:, lo:lo + cb] = jnp.max(s, axis=0, keepdims=True)

    def x_stage(s_ref, cm_ref, p_ref, al_ref, mask_off=None, col_lo=0):
        for lo in range(col_lo, tq, cb):
            cols = slice(lo, lo + cb)
            s = s_ref[:, cols]
            if mask_off is None:
                cm = cm_ref[:, cols]
            else:
                krel = lax.broadcasted_iota(jnp.int32, s.shape, 0) + mask_off
                qrel = lax.broadcasted_iota(jnp.int32, s.shape, 1) + lo
                s = jnp.where(krel <= qrel, s, -jnp.inf)
                cm = jnp.max(s, axis=0, keepdims=True)
            m_prev = m_scr[:, cols]
            m_new = jnp.maximum(m_prev, cm)
            alpha = jnp.exp2(m_prev - m_new)
            p = jnp.exp2(s - m_new)
            p_ref[:, cols] = p.astype(_BF16)
            al_ref[:, cols] = alpha
            m_scr[:, cols] = m_new

    def v_stage(c, p_ref, al_ref, col_lo=0):
        for lo in range(col_lo, tq, cb):
            cols = slice(lo, lo + cb)
            acc_scr[:, cols] = al_ref[:, cols] * acc_scr[:, cols] + jnp.dot(
                vT_ref[c], p_ref[:, cols], preferred_element_type=_F32)

    s_bufs = (((s00, cm00), (s01, cm01)), ((s10, cm10), (s11, cm11)))

    def start_tile():
        m_scr[...] = jnp.full(m_scr.shape, -jnp.inf, _F32)
        acc_scr[...] = jnp.zeros(acc_scr.shape, _F32)
        p_b[...] = jnp.zeros(p_b.shape, p_b.dtype)
        al_b[...] = jnp.ones(al_b.shape, _F32)

    def regular_step(qi, a, par):
        (sa, cma), (sb, cmb) = s_bufs[par]
        (na, ncma), (nb, ncmb) = s_bufs[1 - par]
        q_stage(qi, a + 2, na, ncma)
        x_stage(sa, cma, p_a, al_a)
        v_stage(jnp.maximum(a - 1, 0), p_b, al_b)
        q_stage(qi, a + 3, nb, ncmb)
        x_stage(sb, cmb, p_b, al_b)
        v_stage(a, p_a, al_a)

    def diagonal_step(qi, par):
        a = 2 * qi
        (sa, cma), (sb, cmb) = s_bufs[par]
        (na, ncma), (nb, ncmb) = s_bufs[1 - par]
        q_next = jnp.minimum(qi + 1, n_q - 1)
        q_stage(q_next, 0, na, ncma)
        x_stage(sa, cma, p_a, al_a, mask_off=0)
        v_stage(jnp.maximum(a - 1, 0), p_b, al_b)
        q_stage(q_next, 1, nb, ncmb)
        x_stage(sb, cmb, p_b, al_b, mask_off=tk, col_lo=tk)
        v_stage(a, p_a, al_a)
        v_stage(a + 1, p_b, al_b, col_lo=tk)
        vdim = o_ref.shape[1]
        inv_l = 1.0 / acc_scr[vdim:vdim + 1, :]
        o = (acc_scr[0:vdim, :] * inv_l).T
        o_ref[pl.ds(pl.multiple_of(qi * tq, tq), tq), :] = o.astype(o_ref.dtype)
        start_tile()

    def by_parity(par, fn):
        for static_par in range(2):
            @pl.when(par == static_par)
            def _():
                fn(static_par)

    def tile_body(qi, par):
        def step_body(u, par):
            by_parity(par, functools.partial(regular_step, qi, 2 * u))
            return 1 - par

        par = lax.fori_loop(0, qi, step_body, par)
        by_parity(par, functools.partial(diagonal_step, qi))
        return 1 - par

    start_tile()
    q_stage(0, 0, s00, cm00)
    q_stage(0, 1, s01, cm01)
    lax.fori_loop(0, n_q, tile_body, jnp.int32(0))


def _outproj_mlp_kernel(o_ref, x_ref, wo_ref, g1_ref, b1_ref, wup_ref, wdown_ref,
                        g2_ref, b2_ref, out_ref, x1_scr, h_scr, *, alpha):
    def mix_rows(rows):
        return jnp.dot(o_ref[rows, :], wo_ref[...], preferred_element_type=_F32)

    _mix_mlp_ln(mix_rows, x_ref, alpha, g1_ref, b1_ref, wup_ref, wdown_ref, g2_ref, b2_ref,
                out_ref, x1_scr, h_scr)


def _swa_mlp_kernel(x_ref, wqT_ref, wk_ref, wvT_ref, wo_ref, bucketT_ref, relb_ref, sink_ref,
                    g1_ref, b1_ref, wup_ref, wdown_ref, g2_ref, b2_ref, out_ref,
                    k_scr, vTe_scr, qT_scr, biasT_scr, s_scr, p_scr, m_scr, oT_scr, x1_scr, h_scr,
                    *, alpha, q_heads, kv_heads, head_dim, q_scale):
    blk = _SWA_BLOCK
    group = q_heads // kv_heads
    tm = x_ref.shape[0]
    n_blk = tm // blk
    vrows = vTe_scr.shape[0] // kv_heads
    first_tile = pl.program_id(1) == 0
    always = pl.program_id(1) >= 0

    @pl.when((pl.program_id(0) == 0) & first_tile)
    def _():
        bkt = bucketT_ref[...]
        for hq in range(q_heads):
            bias = jnp.full(bkt.shape, -jnp.inf, _F32)
            for bb in range(_REL_BUCKETS):
                bias = jnp.where(bkt == bb, relb_ref[bb, hq] * _LOG2E, bias)
            g, gi = divmod(hq, group)
            biasT_scr[g, :, gi * blk:(gi + 1) * blk] = bias

    @pl.when(first_tile)
    def _():
        k_scr[0:blk, :] = jnp.zeros((blk, k_scr.shape[1]), k_scr.dtype)
        vTe_scr[:, 0:blk] = jnp.zeros((vTe_scr.shape[0], blk), vTe_scr.dtype)

    xb = x_ref[...].astype(_BF16)
    qT_scr[...] = (lax.dot_general(wqT_ref[...], xb, _NT, preferred_element_type=_F32)
                   * q_scale).astype(_BF16)
    k_scr[blk:blk + tm, :] = jnp.dot(xb, wk_ref[...], preferred_element_type=_F32).astype(_BF16)
    vT = lax.dot_general(wvT_ref[...], xb, _NT, preferred_element_type=_F32).astype(_BF16)
    ones_rows = (lax.broadcasted_iota(jnp.int32, (vrows - head_dim, tm), 0) == 0).astype(_BF16)
    for g in range(kv_heads):
        vTe_scr[g * vrows:g * vrows + head_dim, blk:blk + tm] = vT[g * head_dim:(g + 1) * head_dim]
        vTe_scr[g * vrows + head_dim:(g + 1) * vrows, blk:blk + tm] = ones_rows

    first_pen = jnp.where(first_tile, -jnp.inf, 0.0).astype(_F32)
    zeros_q = jnp.zeros((head_dim, group * blk), _BF16)
    heads_per_lane_tile = _LANES // head_dim
    units = [(n, g) for n in range(n_blk) for g in range(kv_heads)]

    def sink_row(g):
        return jnp.concatenate(
            [jnp.full((1, blk), sink_ref[g * group + gi] * _LOG2E, _F32) for gi in range(group)],
            axis=1)

    sinks = [None] * kv_heads

    def scores(u):
        n, g = units[u]
        lt = g // heads_per_lane_tile
        kpair = k_scr[n * blk:(n + 2) * blk, lt * _LANES:(lt + 1) * _LANES]
        qg = jnp.concatenate(
            [qT_scr[(g * group + gi) * head_dim:(g * group + gi + 1) * head_dim,
                    n * blk:(n + 1) * blk] for gi in range(group)], axis=1)
        pieces = [zeros_q] * heads_per_lane_tile
        pieces[g % heads_per_lane_tile] = qg
        qpad = jnp.concatenate(pieces, axis=0)
        s = jnp.dot(kpair, qpad, preferred_element_type=_F32) + biasT_scr[g]
        if n == 0:
            s = jnp.concatenate([s[:blk] + first_pen, s[blk:]], axis=0)
        s_scr[u] = s
        m_scr[u] = jnp.maximum(jnp.max(s, axis=0, keepdims=True), sinks[g])

    def exponentials(u):
        p_scr[u] = jnp.exp2(s_scr[u] - m_scr[u]).astype(_BF16)

    def weighted_values(u):
        n, g = units[u]
        oT = jnp.dot(vTe_scr[g * vrows:(g + 1) * vrows, n * blk:(n + 2) * blk], p_scr[u],
                     preferred_element_type=_F32)
        denom = oT[head_dim:head_dim + 1] + jnp.exp2(sinks[g] - m_scr[u])
        o = (oT[:head_dim] * (1.0 / denom)).astype(_BF16)
        for gi in range(group):
            hq = g * group + gi
            oT_scr[hq * head_dim:(hq + 1) * head_dim, n * blk:(n + 1) * blk] = (
                o[:, gi * blk:(gi + 1) * blk])

    @pl.when(always)
    def _():
        for g in range(kv_heads):
            sinks[g] = sink_row(g)
        stages = (scores, exponentials, weighted_values)
        for t in range(len(units) + len(stages) - 1):
            for lag, stage in enumerate(stages):
                if 0 <= t - lag < len(units):
                    stage(t - lag)

    @pl.when(pl.program_id(0) >= 0)
    def _():
        k_scr[0:blk, :] = k_scr[tm:tm + blk, :]
        vTe_scr[:, 0:blk] = vTe_scr[:, tm:tm + blk]

    def mix_rows(rows):
        return lax.dot_general(oT_scr[:, rows], wo_ref[...], _TN, preferred_element_type=_F32)

    _mix_mlp_ln(mix_rows, x_ref, alpha, g1_ref, b1_ref, wup_ref, wdown_ref, g2_ref, b2_ref,
                out_ref, x1_scr, h_scr)


def _t5_bucket_table(blk):
    i = np.arange(blk)[None, :]
    j = np.arange(2 * blk)[:, None]
    dist = i + blk - j
    max_exact = _REL_BUCKETS // 2
    nf = np.maximum(dist, 1).astype(np.float32)
    large = max_exact + (np.log(nf / np.float32(max_exact))
                         / np.float32(math.log(_REL_MAX_DIST / max_exact))
                         * np.float32(_REL_BUCKETS - max_exact)).astype(np.int32)
    large = np.minimum(large, _REL_BUCKETS - 1)
    bucket = np.where(dist < max_exact, np.maximum(dist, 0), large)
    valid = (dist >= 0) & (dist < blk)
    return np.where(valid, bucket, -1).astype(np.int32)


def _rope_tables(seq, rope):
    half = rope // 2
    inv = _ROPE_THETA ** (-jnp.arange(half, dtype=_F32) / half)
    ang = jnp.arange(seq).astype(_F32)[:, None] * inv[None, :]
    cos, sin = jnp.cos(ang), jnp.sin(ang)
    cs = jnp.concatenate([cos, cos, -sin, sin], axis=1)
    return cs, cos.T, sin.T


def _row(v):
    return v.reshape(1, -1).astype(_F32)


def kernel(x, mla_w_in, mla_g_q, mla_g_kv, mla_w_uq, mla_w_uk, mla_w_uv, mla_w_o, kv_w_shared,
           swa_w_q, swa_sinks, swa_w_o, rel_bias, mlp_w_up, mlp_w_down, ln_mix_g, ln_mix_b,
           ln_mlp_g, ln_mlp_b):
    B, S, D = x.shape
    depth = mlp_w_up.shape[0]
    assert depth == 2 and mla_w_in.shape[0] == 1 and swa_w_q.shape[0] == 1
    alpha = (2 * depth) ** 0.25
    T = B * S
    tm = _TOKEN_TILE
    ta = _ATTN_TILE
    tk = ta // 2
    bf16_rows = 16
    ta_pad = ta + _LANES
    assert ta % tm == 0 and tm % tk == 0 and S % ta == 0

    q_rank, heads, hd = mla_w_uq.shape[1:]
    kv_rank, _, nope = mla_w_uk.shape[1:]
    vdim = mla_w_uv.shape[3]
    vrows = vdim + bf16_rows
    rope = hd - nope
    half = rope // 2
    assert 2 * rope == _LANES and mla_w_in.shape[2] == q_rank + kv_rank + rope

    w_in = mla_w_in[0]
    r0 = q_rank + kv_rank
    w_in_ext = jnp.concatenate(
        [w_in, w_in[:, r0 + half:r0 + rope], w_in[:, r0:r0 + half]], axis=1).astype(_BF16)
    wqT = mla_w_uq[0].reshape(q_rank, heads * hd).T.astype(_BF16)
    wuk = mla_w_uk[0].reshape(kv_rank, heads * nope).astype(_BF16)
    wvT = mla_w_uv[0].reshape(kv_rank, heads * vdim).T.astype(_BF16)
    cs, cosT, sinT = _rope_tables(S, rope)
    n_t = S // tm
    q_scale = hd ** -0.5 * _LOG2E

    cparams2 = pltpu.CompilerParams(dimension_semantics=("arbitrary", "arbitrary"),
                                    vmem_limit_bytes=_VMEM_LIMIT_BYTES)
    cparams1 = pltpu.CompilerParams(dimension_semantics=("arbitrary",),
                                    vmem_limit_bytes=_VMEM_LIMIT_BYTES)

    qT_all, k_all, vT_all = pl.pallas_call(
        functools.partial(_mla_proj_kernel, q_rank=q_rank, kv_rank=kv_rank, heads=heads,
                          nope=nope, rope=rope, q_scale=q_scale),
        grid=(B, n_t),
        in_specs=[
            pl.BlockSpec((None, tm, D), lambda b, i: (b, i, 0)),
            _const_spec(w_in_ext.shape),
            _const_spec((1, q_rank)),
            _const_spec((1, kv_rank)),
            _const_spec(wqT.shape),
            _const_spec(wuk.shape),
            _const_spec(wvT.shape),
            pl.BlockSpec((tm, 2 * rope), lambda b, i: (i, 0)),
            pl.BlockSpec((half, tm), lambda b, i: (0, i)),
            pl.BlockSpec((half, tm), lambda b, i: (0, i)),
        ],
        out_specs=[
            pl.BlockSpec((None, heads, None, hd, tm), lambda b, i: (b, 0, i, 0, 0)),
            pl.BlockSpec((None, heads, tm, hd), lambda b, i: (b, 0, i, 0)),
            pl.BlockSpec((None, heads, tm // tk, vrows, tk), lambda b, i: (b, 0, i, 0, 0)),
        ],
        out_shape=[
            jax.ShapeDtypeStruct((B, heads, n_t, hd, tm), _BF16),
            jax.ShapeDtypeStruct((B, heads, S, hd), _BF16),
            jax.ShapeDtypeStruct((B, heads, S // tk, vrows, tk), _BF16),
        ],
        compiler_params=cparams2,
        name="mla_proj",
    )(x, w_in_ext, _row(mla_g_q[0]), _row(mla_g_kv[0]), wqT, wuk, wvT, cs, cosT, sinT)

    o = pl.pallas_call(
        _mla_attn_kernel,
        grid=(B, heads),
        in_specs=[
            pl.BlockSpec((None, None, n_t, hd, tm), lambda b, h: (b, h, 0, 0, 0)),
            pl.BlockSpec((None, None, S, hd), lambda b, h: (b, h, 0, 0)),
            pl.BlockSpec((None, None, S // tk, vrows, tk), lambda b, h: (b, h, 0, 0, 0)),
        ],
        out_specs=pl.BlockSpec((None, S, vdim), lambda b, h: (b, 0, h)),
        out_shape=jax.ShapeDtypeStruct((B, S, heads * vdim), _BF16),
        scratch_shapes=[
            pltpu.VMEM((1, ta), _F32),
            pltpu.VMEM((vrows, ta), _F32),
        ] + [pltpu.VMEM((tk, ta_pad), _F32)] * 4
          + [pltpu.VMEM((1, ta), _F32)] * 4
          + [pltpu.VMEM((tk, ta), _BF16)] * 2
          + [pltpu.VMEM((1, ta), _F32)] * 2,
        compiler_params=cparams2,
        name="mla_attn",
    )(qT_all, k_all, vT_all)

    d_ff = mlp_w_up.shape[2]
    mlp_scratch = [
        pltpu.VMEM((tm, D), _F32),
        pltpu.VMEM((tm, d_ff), _BF16),
    ]
    x1 = pl.pallas_call(
        functools.partial(_outproj_mlp_kernel, alpha=alpha),
        grid=(T // tm,),
        in_specs=[
            pl.BlockSpec((tm, heads * vdim), lambda i: (i, 0)),
            pl.BlockSpec((tm, D), lambda i: (i, 0)),
            _const_spec((heads * vdim, D)),
            _const_spec((1, D)),
            _const_spec((1, D)),
            _const_spec((D, d_ff)),
            _const_spec((d_ff, D)),
            _const_spec((1, D)),
            _const_spec((1, D)),
        ],
        out_specs=pl.BlockSpec((tm, D), lambda i: (i, 0)),
        out_shape=jax.ShapeDtypeStruct((T, D), _F32),
        scratch_shapes=mlp_scratch,
        compiler_params=cparams1,
        name="mla_out_mlp",
    )(o.reshape(T, heads * vdim), x.reshape(T, D), mla_w_o[0].astype(_BF16),
      _row(ln_mix_g[0]), _row(ln_mix_b[0]), mlp_w_up[0].astype(_BF16),
      mlp_w_down[0].astype(_BF16), _row(ln_mlp_g[0]), _row(ln_mlp_b[0]))

    q_heads = swa_sinks.shape[1]
    s_hd = swa_w_q.shape[2] // q_heads
    kv_heads = kv_w_shared.shape[1] // (2 * s_hd)
    group = q_heads // kv_heads
    blk = _SWA_BLOCK
    assert _LANES % s_hd == 0 and tm % blk == 0
    wqT1 = swa_w_q[0].T.astype(_BF16)
    wk1 = kv_w_shared[:, :kv_heads * s_hd].astype(_BF16)
    wvT1 = kv_w_shared[:, kv_heads * s_hd:].T.astype(_BF16)
    bucketT = jnp.asarray(_t5_bucket_table(blk))
    n_units = (tm // blk) * kv_heads

    smem = pl.BlockSpec(memory_space=pltpu.SMEM)
    out = pl.pallas_call(
        functools.partial(_swa_mlp_kernel, alpha=alpha, q_heads=q_heads, kv_heads=kv_heads,
                          head_dim=s_hd, q_scale=s_hd ** -0.5 * _LOG2E),
        grid=(B, n_t),
        in_specs=[
            pl.BlockSpec((None, tm, D), lambda b, i: (b, i, 0)),
            _const_spec(wqT1.shape),
            _const_spec(wk1.shape),
            _const_spec(wvT1.shape),
            _const_spec((q_heads * s_hd, D)),
            _const_spec(bucketT.shape),
            smem,
            smem,
            _const_spec((1, D)),
            _const_spec((1, D)),
            _const_spec((D, d_ff)),
            _const_spec((d_ff, D)),
            _const_spec((1, D)),
            _const_spec((1, D)),
        ],
        out_specs=pl.BlockSpec((None, tm, D), lambda b, i: (b, i, 0)),
        out_shape=jax.ShapeDtypeStruct((B, S, D), _F32),
        scratch_shapes=[
            pltpu.VMEM((blk + tm, kv_heads * s_hd), _BF16),
            pltpu.VMEM((kv_heads * (s_hd + bf16_rows), blk + tm), _BF16),
            pltpu.VMEM((q_heads * s_hd, tm), _BF16),
            pltpu.VMEM((kv_heads, 2 * blk, group * blk), _F32),
            pltpu.VMEM((n_units, 2 * blk, group * blk), _F32),
            pltpu.VMEM((n_units, 2 * blk, group * blk), _BF16),
            pltpu.VMEM((n_units, 1, group * blk), _F32),
            pltpu.VMEM((q_heads * s_hd, tm), _BF16),
        ] + mlp_scratch,
        compiler_params=cparams2,
        name="swa_mlp",
    )(x1.reshape(B, S, D), wqT1, wk1, wvT1, swa_w_o[0].astype(_BF16), bucketT,
      rel_bias.astype(_F32), swa_sinks[0].astype(_F32),
      _row(ln_mix_g[1]), _row(ln_mix_b[1]), mlp_w_up[1].astype(_BF16),
      mlp_w_down[1].astype(_BF16), _row(ln_mlp_g[1]), _row(ln_mlp_b[1]))
    return out
```

```python
import functools
import math

import numpy as np
import jax
import jax.numpy as jnp
from jax import lax
from jax.experimental import pallas as pl
from jax.experimental.pallas import tpu as pltpu

_F32 = jnp.float32
_BF16 = jnp.bfloat16

_LN_EPS = 1e-5
_RMS_EPS = 1e-6
_ROPE_THETA = 10000.0
_SWA_BLOCK = 128
_REL_BUCKETS = 32
_REL_MAX_DIST = 128
_LOG2E = math.log2(math.e)

_V7X_VMEM_BYTES = 64 * 1024 * 1024
_VMEM_LIMIT_BYTES = _V7X_VMEM_BYTES - 8 * 1024 * 1024
_LANES = 128

_TOKEN_TILE = 512
_ATTN_TILE = 1024
_ATTN_COL_BLOCK = 256
_FF_CHUNK = 1024
_MLP_ROW_BLOCK = 256

_NT = (((1,), (1,)), ((), ()))
_TN = (((0,), (0,)), ((), ()))


def _const_spec(shape):
    nd = len(shape)
    return pl.BlockSpec(shape, lambda *_: (0,) * nd, pipeline_mode=pl.Buffered(1))


def _layernorm(v, g, b):
    mu = jnp.mean(v, axis=-1, keepdims=True)
    d = v - mu
    var = jnp.mean(d * d, axis=-1, keepdims=True)
    return d * lax.rsqrt(var + _LN_EPS) * g + b


def _rmsnorm(v, g):
    return v * lax.rsqrt(jnp.mean(v * v, axis=-1, keepdims=True) + _RMS_EPS) * g


def _mix_mlp_ln(mix_rows, x_ref, alpha, g1_ref, b1_ref, wup_ref, wdown_ref, g2_ref, b2_ref,
                out_ref, x1_scr, h_scr):
    tm = x_ref.shape[0]
    d_ff = wup_ref.shape[1]
    blocks = [slice(r, r + _MLP_ROW_BLOCK) for r in range(0, tm, _MLP_ROW_BLOCK)]
    for rows in blocks:
        x1_scr[rows, :] = _layernorm(alpha * x_ref[rows, :] + mix_rows(rows), g1_ref[...], b1_ref[...])
    for rows in blocks:
        x1b = x1_scr[rows, :].astype(_BF16)
        for lo in range(0, d_ff, _FF_CHUNK):
            hmid = jnp.dot(x1b, wup_ref[:, lo:lo + _FF_CHUNK], preferred_element_type=_F32)
            hmid = jnp.maximum(hmid, 0.0)
            h_scr[rows, lo:lo + _FF_CHUNK] = (hmid * hmid).astype(_BF16)
    for rows in blocks:
        y = jnp.dot(h_scr[rows, :], wdown_ref[...], preferred_element_type=_F32)
        out_ref[rows, :] = _layernorm(alpha * x1_scr[rows, :] + y, g2_ref[...], b2_ref[...])


def _mla_proj_kernel(x_ref, win_ref, gq_ref, gkv_ref, wqT_ref, wuk_ref, wvT_ref,
                     cs_ref, cosT_ref, sinT_ref, qT_ref, k_ref, vT_ref,
                     *, q_rank, kv_rank, heads, nope, rope, q_scale):
    xb = x_ref[...].astype(_BF16)
    h = jnp.dot(xb, win_ref[...], preferred_element_type=_F32)
    cq = _rmsnorm(h[:, :q_rank], gq_ref[...]).astype(_BF16)
    ckv = _rmsnorm(h[:, q_rank:q_rank + kv_rank], gkv_ref[...]).astype(_BF16)
    t = h[:, q_rank + kv_rank:] * cs_ref[...]
    kr = (t + pltpu.roll(t, rope, 1))[:, :rope].astype(_BF16)

    half = rope // 2
    hd = nope + rope
    qT = lax.dot_general(wqT_ref[...], cq, _NT, preferred_element_type=_F32)
    cosT = cosT_ref[...]
    sinT = sinT_ref[...]
    for hh in range(heads):
        base = hh * hd
        qT_ref[hh, 0:nope, :] = (qT[base:base + nope] * q_scale).astype(_BF16)
        x1 = qT[base + nope:base + nope + half]
        x2 = qT[base + nope + half:base + hd]
        qT_ref[hh, nope:nope + half, :] = ((x1 * cosT - x2 * sinT) * q_scale).astype(_BF16)
        qT_ref[hh, nope + half:hd, :] = ((x1 * sinT + x2 * cosT) * q_scale).astype(_BF16)

    kn = jnp.dot(ckv, wuk_ref[...], preferred_element_type=_F32)
    for hh in range(heads):
        k_ref[hh, :, 0:nope] = kn[:, hh * nope:(hh + 1) * nope].astype(_BF16)
        k_ref[hh, :, nope:hd] = kr

    vdim = wvT_ref.shape[0] // heads
    vT = lax.dot_general(wvT_ref[...], ckv, _NT, preferred_element_type=_F32)
    tk = vT_ref.shape[3]
    pad = vT_ref.shape[2] - vdim
    ones_rows = (lax.broadcasted_iota(jnp.int32, (pad, tk), 0) == 0).astype(_BF16)
    for hh in range(heads):
        for j in range(vT_ref.shape[1]):
            vT_ref[hh, j, 0:vdim, :] = vT[hh * vdim:(hh + 1) * vdim, j * tk:(j + 1) * tk].astype(_BF16)
            vT_ref[hh, j, vdim:vdim + pad, :] = ones_rows


def _mla_attn_kernel(qT_ref, k_ref, vT_ref, o_ref, m_scr, acc_scr,
                     s00, s01, s10, s11, cm00, cm01, cm10, cm11, p_a, p_b0, p_b1, al_a, al_b0, al_b1):
    tk, tq = s00.shape[0], acc_scr.shape[1]
    tsub = qT_ref.shape[2]
    nsub = tq // tsub
    n_q = qT_ref.shape[0] // nsub
    assert tq == 2 * tk and tk == vT_ref.shape[2]

    cb = _ATTN_COL_BLOCK

    def q_stage(qi, c, s_ref, cm_ref, col_lo=0):
        k = k_ref[pl.ds(pl.multiple_of(c * tk, tk), tk), :]
        for lo in range(col_lo, tq, cb):
            j, off = divmod(lo, tsub)
            qT = qT_ref[qi * nsub + j, :, off:off + cb]
            s = jnp.dot(k, qT, preferred_element_type=_F32)
            s_ref[:, lo:lo + cb] = s
            cm_ref[:, lo:lo + cb] = jnp.max(s, axis=0, keepdims=True)

    def x_stage(s_ref, cm_ref, p_ref, al_ref, mask_off=None, col_lo=0):
        for lo in range(col_lo, tq, cb):
            cols = slice(lo, lo + cb)
            s = s_ref[:, cols]
            if mask_off is None:
                cm = cm_ref[:, cols]
            else:
                krel = lax.broadcasted_iota(jnp.int32, s.shape, 0) + mask_off
                qrel = lax.broadcasted_iota(jnp.int32, s.shape, 1) + lo
                s = jnp.where(krel <= qrel, s, -jnp.inf)
                cm = jnp.max(s, axis=0, keepdims=True)
            m_prev = m_scr[:, cols]
            m_new = jnp.maximum(m_prev, cm)
            alpha = jnp.exp2(m_prev - m_new)
            p = jnp.exp2(s - m_new)
            p_ref[:, cols] = p.astype(_BF16)
            al_ref[:, cols] = alpha
            m_scr[:, cols] = m_new

    def v_stage(c, p_ref, al_ref, col_lo=0):
        for lo in range(col_lo, tq, cb):
            cols = slice(lo, lo + cb)
            acc_scr[:, cols] = al_ref[:, cols] * acc_scr[:, cols] + jnp.dot(
                vT_ref[c], p_ref[:, cols], preferred_element_type=_F32)

    s_bufs = (((s00, cm00), (s01, cm01)), ((s10, cm10), (s11, cm11)))

    p_bs, al_bs = (p_b0, p_b1), (al_b0, al_b1)

    def start_tile(par_read):
        m_scr[...] = jnp.full(m_scr.shape, -jnp.inf, _F32)
        acc_scr[...] = jnp.zeros(acc_scr.shape, _F32)
        p_bs[par_read][...] = jnp.zeros(p_b0.shape, p_b0.dtype)
        al_bs[par_read][...] = jnp.ones(al_b0.shape, _F32)

    def regular_step(qi, a, par):
        (sa, cma), (sb, cmb) = s_bufs[par]
        (na, ncma), (nb, ncmb) = s_bufs[1 - par]
        q_stage(qi, a + 2, na, ncma)
        x_stage(sa, cma, p_a, al_a)
        v_stage(jnp.maximum(a - 1, 0), p_bs[1 - par], al_bs[1 - par])
        q_stage(qi, a + 3, nb, ncmb)
        x_stage(sb, cmb, p_bs[par], al_bs[par])
        v_stage(a, p_a, al_a)

    def diagonal_step(qi, par):
        a = 2 * qi
        (sa, cma), (sb, cmb) = s_bufs[par]
        (na, ncma), (nb, ncmb) = s_bufs[1 - par]
        q_next = jnp.minimum(qi + 1, n_q - 1)
        q_stage(q_next, 0, na, ncma)
        x_stage(sa, cma, p_a, al_a, mask_off=0)
        v_stage(jnp.maximum(a - 1, 0), p_bs[1 - par], al_bs[1 - par])
        q_stage(q_next, 1, nb, ncmb)
        x_stage(sb, cmb, p_bs[par], al_bs[par], mask_off=tk, col_lo=tk)
        v_stage(a, p_a, al_a)
        v_stage(a + 1, p_bs[par], al_bs[par], col_lo=tk)
        vdim = o_ref.shape[1]
        inv_l = 1.0 / acc_scr[vdim:vdim + 1, :]
        o = (acc_scr[0:vdim, :] * inv_l).T
        o_ref[pl.ds(pl.multiple_of(qi * tq, tq), tq), :] = o.astype(o_ref.dtype)

        @pl.when(pl.program_id(0) >= 0)
        def _():
            start_tile(par)

    def by_parity(par, fn):
        for static_par in range(2):
            @pl.when(par == static_par)
            def _():
                fn(static_par)

    def tile_body(qi, par):
        def step_body(u, par):
            by_parity(par, functools.partial(regular_step, qi, 2 * u))
            return 1 - par

        par = lax.fori_loop(0, qi, step_body, par)
        by_parity(par, functools.partial(diagonal_step, qi))
        return 1 - par

    start_tile(1)
    q_stage(0, 0, s00, cm00)
    q_stage(0, 1, s01, cm01)
    lax.fori_loop(0, n_q, tile_body, jnp.int32(0))


def _outproj_mlp_kernel(o_ref, x_ref, wo_ref, g1_ref, b1_ref, wup_ref, wdown_ref,
                        g2_ref, b2_ref, out_ref, x1_scr, h_scr, *, alpha):
    def mix_rows(rows):
        return jnp.dot(o_ref[rows, :], wo_ref[...], preferred_element_type=_F32)

    _mix_mlp_ln(mix_rows, x_ref, alpha, g1_ref, b1_ref, wup_ref, wdown_ref, g2_ref, b2_ref,
                out_ref, x1_scr, h_scr)


def _swa_mlp_kernel(x_ref, wqT_ref, wk_ref, wvT_ref, wo_ref, bucketT_ref, relb_ref, sink_ref,
                    g1_ref, b1_ref, wup_ref, wdown_ref, g2_ref, b2_ref, out_ref,
                    k_scr, vTe_scr, qT_scr, biasT_scr, s_scr, p_scr, m_scr, oT_scr, x1_scr, h_scr,
                    *, alpha, q_heads, kv_heads, head_dim, q_scale):
    blk = _SWA_BLOCK
    group = q_heads // kv_heads
    tm = x_ref.shape[0]
    n_blk = tm // blk
    vrows = vTe_scr.shape[0] // kv_heads
    first_tile = pl.program_id(1) == 0
    always = pl.program_id(1) >= 0

    @pl.when((pl.program_id(0) == 0) & first_tile)
    def _():
        bkt = bucketT_ref[...]
        for hq in range(q_heads):
            bias = jnp.full(bkt.shape, -jnp.inf, _F32)
            for bb in range(_REL_BUCKETS):
                bias = jnp.where(bkt == bb, relb_ref[bb, hq] * _LOG2E, bias)
            g, gi = divmod(hq, group)
            biasT_scr[g, :, gi * blk:(gi + 1) * blk] = bias

    @pl.when(first_tile)
    def _():
        k_scr[0:blk, :] = jnp.zeros((blk, k_scr.shape[1]), k_scr.dtype)
        vTe_scr[:, 0:blk] = jnp.zeros((vTe_scr.shape[0], blk), vTe_scr.dtype)

    xb = x_ref[...].astype(_BF16)
    qT_scr[...] = (lax.dot_general(wqT_ref[...], xb, _NT, preferred_element_type=_F32)
                   * q_scale).astype(_BF16)
    k_scr[blk:blk + tm, :] = jnp.dot(xb, wk_ref[...], preferred_element_type=_F32).astype(_BF16)
    vT = lax.dot_general(wvT_ref[...], xb, _NT, preferred_element_type=_F32).astype(_BF16)
    ones_rows = (lax.broadcasted_iota(jnp.int32, (vrows - head_dim, tm), 0) == 0).astype(_BF16)
    for g in range(kv_heads):
        vTe_scr[g * vrows:g * vrows + head_dim, blk:blk + tm] = vT[g * head_dim:(g + 1) * head_dim]
        vTe_scr[g * vrows + head_dim:(g + 1) * vrows, blk:blk + tm] = ones_rows

    first_pen = jnp.where(first_tile, -jnp.inf, 0.0).astype(_F32)
    zeros_q = jnp.zeros((head_dim, group * blk), _BF16)
    heads_per_lane_tile = _LANES // head_dim
    units = [(n, g) for n in range(n_blk) for g in range(kv_heads)]

    def sink_row(g):
        return jnp.concatenate(
            [jnp.full((1, blk), sink_ref[g * group + gi] * _LOG2E, _F32) for gi in range(group)],
            axis=1)

    sinks = [None] * kv_heads

    def scores(u):
        n, g = units[u]
        lt = g // heads_per_lane_tile
        kpair = k_scr[n * blk:(n + 2) * blk, lt * _LANES:(lt + 1) * _LANES]
        qg = jnp.concatenate(
            [qT_scr[(g * group + gi) * head_dim:(g * group + gi + 1) * head_dim,
                    n * blk:(n + 1) * blk] for gi in range(group)], axis=1)
        pieces = [zeros_q] * heads_per_lane_tile
        pieces[g % heads_per_lane_tile] = qg
        qpad = jnp.concatenate(pieces, axis=0)
        s = jnp.dot(kpair, qpad, preferred_element_type=_F32) + biasT_scr[g]
        if n == 0:
            s = jnp.concatenate([s[:blk] + first_pen, s[blk:]], axis=0)
        s_scr[u] = s
        m_scr[u] = jnp.maximum(jnp.max(s, axis=0, keepdims=True), sinks[g])

    def exponentials(u):
        p_scr[u] = jnp.exp2(s_scr[u] - m_scr[u]).astype(_BF16)

    def weighted_values(u):
        n, g = units[u]
        oT = jnp.dot(vTe_scr[g * vrows:(g + 1) * vrows, n * blk:(n + 2) * blk], p_scr[u],
                     preferred_element_type=_F32)
        denom = oT[head_dim:head_dim + 1] + jnp.exp2(sinks[g] - m_scr[u])
        o = (oT[:head_dim] * (1.0 / denom)).astype(_BF16)
        for gi in range(group):
            hq = g * group + gi
            oT_scr[hq * head_dim:(hq + 1) * head_dim, n * blk:(n + 1) * blk] = (
                o[:, gi * blk:(gi + 1) * blk])

    @pl.when(always)
    def _():
        for g in range(kv_heads):
            sinks[g] = sink_row(g)
        stages = (scores, exponentials, weighted_values)
        for t in range(len(units) + len(stages) - 1):
            for lag, stage in enumerate(stages):
                if 0 <= t - lag < len(units):
                    stage(t - lag)

    @pl.when(pl.program_id(0) >= 0)
    def _():
        k_scr[0:blk, :] = k_scr[tm:tm + blk, :]
        vTe_scr[:, 0:blk] = vTe_scr[:, tm:tm + blk]

    def mix_rows(rows):
        return lax.dot_general(oT_scr[:, rows], wo_ref[...], _TN, preferred_element_type=_F32)

    _mix_mlp_ln(mix_rows, x_ref, alpha, g1_ref, b1_ref, wup_ref, wdown_ref, g2_ref, b2_ref,
                out_ref, x1_scr, h_scr)


def _t5_bucket_table(blk):
    i = np.arange(blk)[None, :]
    j = np.arange(2 * blk)[:, None]
    dist = i + blk - j
    max_exact = _REL_BUCKETS // 2
    nf = np.maximum(dist, 1).astype(np.float32)
    large = max_exact + (np.log(nf / np.float32(max_exact))
                         / np.float32(math.log(_REL_MAX_DIST / max_exact))
                         * np.float32(_REL_BUCKETS - max_exact)).astype(np.int32)
    large = np.minimum(large, _REL_BUCKETS - 1)
    bucket = np.where(dist < max_exact, np.maximum(dist, 0), large)
    valid = (dist >= 0) & (dist < blk)
    return np.where(valid, bucket, -1).astype(np.int32)


def _rope_tables(seq, rope):
    half = rope // 2
    inv = _ROPE_THETA ** (-jnp.arange(half, dtype=_F32) / half)
    ang = jnp.arange(seq).astype(_F32)[:, None] * inv[None, :]
    cos, sin = jnp.cos(ang), jnp.sin(ang)
    cs = jnp.concatenate([cos, cos, -sin, sin], axis=1)
    return cs, cos.T, sin.T


def _row(v):
    return v.reshape(1, -1).astype(_F32)


def kernel(x, mla_w_in, mla_g_q, mla_g_kv, mla_w_uq, mla_w_uk, mla_w_uv, mla_w_o, kv_w_shared,
           swa_w_q, swa_sinks, swa_w_o, rel_bias, mlp_w_up, mlp_w_down, ln_mix_g, ln_mix_b,
           ln_mlp_g, ln_mlp_b):
    B, S, D = x.shape
    depth = mlp_w_up.shape[0]
    assert depth == 2 and mla_w_in.shape[0] == 1 and swa_w_q.shape[0] == 1
    alpha = (2 * depth) ** 0.25
    T = B * S
    tm = _TOKEN_TILE
    ta = _ATTN_TILE
    tk = ta // 2
    bf16_rows = 16
    ta_pad = ta + _LANES
    assert ta % tm == 0 and tm % tk == 0 and S % ta == 0

    q_rank, heads, hd = mla_w_uq.shape[1:]
    kv_rank, _, nope = mla_w_uk.shape[1:]
    vdim = mla_w_uv.shape[3]
    vrows = vdim + bf16_rows
    rope = hd - nope
    half = rope // 2
    assert 2 * rope == _LANES and mla_w_in.shape[2] == q_rank + kv_rank + rope

    w_in = mla_w_in[0]
    r0 = q_rank + kv_rank
    w_in_ext = jnp.concatenate(
        [w_in, w_in[:, r0 + half:r0 + rope], w_in[:, r0:r0 + half]], axis=1).astype(_BF16)
    wqT = mla_w_uq[0].reshape(q_rank, heads * hd).T.astype(_BF16)
    wuk = mla_w_uk[0].reshape(kv_rank, heads * nope).astype(_BF16)
    wvT = mla_w_uv[0].reshape(kv_rank, heads * vdim).T.astype(_BF16)
    cs, cosT, sinT = _rope_tables(S, rope)
    n_t = S // tm
    q_scale = hd ** -0.5 * _LOG2E

    cparams2 = pltpu.CompilerParams(dimension_semantics=("arbitrary", "arbitrary"),
                                    vmem_limit_bytes=_VMEM_LIMIT_BYTES)
    cparams1 = pltpu.CompilerParams(dimension_semantics=("arbitrary",),
                                    vmem_limit_bytes=_VMEM_LIMIT_BYTES)

    qT_all, k_all, vT_all = pl.pallas_call(
        functools.partial(_mla_proj_kernel, q_rank=q_rank, kv_rank=kv_rank, heads=heads,
                          nope=nope, rope=rope, q_scale=q_scale),
        grid=(B, n_t),
        in_specs=[
            pl.BlockSpec((None, tm, D), lambda b, i: (b, i, 0)),
            _const_spec(w_in_ext.shape),
            _const_spec((1, q_rank)),
            _const_spec((1, kv_rank)),
            _const_spec(wqT.shape),
            _const_spec(wuk.shape),
            _const_spec(wvT.shape),
            pl.BlockSpec((tm, 2 * rope), lambda b, i: (i, 0)),
            pl.BlockSpec((half, tm), lambda b, i: (0, i)),
            pl.BlockSpec((half, tm), lambda b, i: (0, i)),
        ],
        out_specs=[
            pl.BlockSpec((None, heads, None, hd, tm), lambda b, i: (b, 0, i, 0, 0)),
            pl.BlockSpec((None, heads, tm, hd), lambda b, i: (b, 0, i, 0)),
            pl.BlockSpec((None, heads, tm // tk, vrows, tk), lambda b, i: (b, 0, i, 0, 0)),
        ],
        out_shape=[
            jax.ShapeDtypeStruct((B, heads, n_t, hd, tm), _BF16),
            jax.ShapeDtypeStruct((B, heads, S, hd), _BF16),
            jax.ShapeDtypeStruct((B, heads, S // tk, vrows, tk), _BF16),
        ],
        compiler_params=cparams2,
        name="mla_proj",
    )(x, w_in_ext, _row(mla_g_q[0]), _row(mla_g_kv[0]), wqT, wuk, wvT, cs, cosT, sinT)

    o = pl.pallas_call(
        _mla_attn_kernel,
        grid=(B, heads),
        in_specs=[
            pl.BlockSpec((None, None, n_t, hd, tm), lambda b, h: (b, h, 0, 0, 0)),
            pl.BlockSpec((None, None, S, hd), lambda b, h: (b, h, 0, 0)),
            pl.BlockSpec((None, None, S // tk, vrows, tk), lambda b, h: (b, h, 0, 0, 0)),
        ],
        out_specs=pl.BlockSpec((None, S, vdim), lambda b, h: (b, 0, h)),
        out_shape=jax.ShapeDtypeStruct((B, S, heads * vdim), _BF16),
        scratch_shapes=[
            pltpu.VMEM((1, ta), _F32),
            pltpu.VMEM((vrows, ta), _F32),
        ] + [pltpu.VMEM((tk, ta_pad), _F32)] * 4
          + [pltpu.VMEM((1, ta), _F32)] * 4
          + [pltpu.VMEM((tk, ta), _BF16)] * 3
          + [pltpu.VMEM((1, ta), _F32)] * 3,
        compiler_params=cparams2,
        name="mla_attn",
    )(qT_all, k_all, vT_all)

    d_ff = mlp_w_up.shape[2]
    mlp_scratch = [
        pltpu.VMEM((tm, D), _F32),
        pltpu.VMEM((tm, d_ff), _BF16),
    ]
    x1 = pl.pallas_call(
        functools.partial(_outproj_mlp_kernel, alpha=alpha),
        grid=(T // tm,),
        in_specs=[
            pl.BlockSpec((tm, heads * vdim), lambda i: (i, 0)),
            pl.BlockSpec((tm, D), lambda i: (i, 0)),
            _const_spec((heads * vdim, D)),
            _const_spec((1, D)),
            _const_spec((1, D)),
            _const_spec((D, d_ff)),
            _const_spec((d_ff, D)),
            _const_spec((1, D)),
            _const_spec((1, D)),
        ],
        out_specs=pl.BlockSpec((tm, D), lambda i: (i, 0)),
        out_shape=jax.ShapeDtypeStruct((T, D), _F32),
        scratch_shapes=mlp_scratch,
        compiler_params=cparams1,
        name="mla_out_mlp",
    )(o.reshape(T, heads * vdim), x.reshape(T, D), mla_w_o[0].astype(_BF16),
      _row(ln_mix_g[0]), _row(ln_mix_b[0]), mlp_w_up[0].astype(_BF16),
      mlp_w_down[0].astype(_BF16), _row(ln_mlp_g[0]), _row(ln_mlp_b[0]))

    q_heads = swa_sinks.shape[1]
    s_hd = swa_w_q.shape[2] // q_heads
    kv_heads = kv_w_shared.shape[1] // (2 * s_hd)
    group = q_heads // kv_heads
    blk = _SWA_BLOCK
    assert _LANES % s_hd == 0 and tm % blk == 0
    wqT1 = swa_w_q[0].T.astype(_BF16)
    wk1 = kv_w_shared[:, :kv_heads * s_hd].astype(_BF16)
    wvT1 = kv_w_shared[:, kv_heads * s_hd:].T.astype(_BF16)
    bucketT = jnp.asarray(_t5_bucket_table(blk))
    n_units = (tm // blk) * kv_heads

    smem = pl.BlockSpec(memory_space=pltpu.SMEM)
    out = pl.pallas_call(
        functools.partial(_swa_mlp_kernel, alpha=alpha, q_heads=q_heads, kv_heads=kv_heads,
                          head_dim=s_hd, q_scale=s_hd ** -0.5 * _LOG2E),
        grid=(B, n_t),
        in_specs=[
            pl.BlockSpec((None, tm, D), lambda b, i: (b, i, 0)),
            _const_spec(wqT1.shape),
            _const_spec(wk1.shape),
            _const_spec(wvT1.shape),
            _const_spec((q_heads * s_hd, D)),
            _const_spec(bucketT.shape),
            smem,
            smem,
            _const_spec((1, D)),
            _const_spec((1, D)),
            _const_spec((D, d_ff)),
            _const_spec((d_ff, D)),
            _const_spec((1, D)),
            _const_spec((1, D)),
        ],
        out_specs=pl.BlockSpec((None, tm, D), lambda b, i: (b, i, 0)),
        out_shape=jax.ShapeDtypeStruct((B, S, D), _F32),
        scratch_shapes=[
            pltpu.VMEM((blk + tm, kv_heads * s_hd), _BF16),
            pltpu.VMEM((kv_heads * (s_hd + bf16_rows), blk + tm), _BF16),
            pltpu.VMEM((q_heads * s_hd, tm), _BF16),
            pltpu.VMEM((kv_heads, 2 * blk, group * blk), _F32),
            pltpu.VMEM((n_units, 2 * blk, group * blk), _F32),
            pltpu.VMEM((n_units, 2 * blk, group * blk), _BF16),
            pltpu.VMEM((n_units, 1, group * blk), _F32),
            pltpu.VMEM((q_heads * s_hd, tm), _BF16),
        ] + mlp_scratch,
        compiler_params=cparams2,
        name="swa_mlp",
    )(x1.reshape(B, S, D), wqT1, wk1, wvT1, swa_w_o[0].astype(_BF16), bucketT,
      rel_bias.astype(_F32), swa_sinks[0].astype(_F32),
      _row(ln_mix_g[1]), _row(ln_mix_b[1]), mlp_w_up[1].astype(_BF16),
      mlp_w_down[1].astype(_BF16), _row(ln_mlp_g[1]), _row(ln_mlp_b[1]))
    return out
```

```python
import functools
import math

import numpy as np
import jax
import jax.numpy as jnp
from jax import lax
from jax.experimental import pallas as pl
from jax.experimental.pallas import tpu as pltpu

_F32 = jnp.float32
_BF16 = jnp.bfloat16

_LN_EPS = 1e-5
_RMS_EPS = 1e-6
_ROPE_THETA = 10000.0
_SWA_BLOCK = 128
_REL_BUCKETS = 32
_REL_MAX_DIST = 128
_LOG2E = math.log2(math.e)

_V7X_VMEM_BYTES = 64 * 1024 * 1024
_VMEM_LIMIT_BYTES = _V7X_VMEM_BYTES - 8 * 1024 * 1024
_LANES = 128

_TOKEN_TILE = 512
_ATTN_TILE = 1024
_ATTN_COL_BLOCK = 256
_FF_CHUNK = 1024
_MLP_ROW_BLOCK = 256

_NT = (((1,), (1,)), ((), ()))
_TN = (((0,), (0,)), ((), ()))


def _const_spec(shape):
    nd = len(shape)
    return pl.BlockSpec(shape, lambda *_: (0,) * nd, pipeline_mode=pl.Buffered(1))


def _layer_spec(shape, layer):
    nd = len(shape) - 1
    return pl.BlockSpec((None,) + tuple(shape[1:]), lambda *_: (layer,) + (0,) * nd,
                        pipeline_mode=pl.Buffered(1))


def _layernorm(v, g, b):
    mu = jnp.mean(v, axis=-1, keepdims=True)
    d = v - mu
    var = jnp.mean(d * d, axis=-1, keepdims=True)
    return d * lax.rsqrt(var + _LN_EPS) * g + b


def _rmsnorm(v, g):
    return v * lax.rsqrt(jnp.mean(v * v, axis=-1, keepdims=True) + _RMS_EPS) * g


def _mix_mlp_ln(mix_rows, x_ref, alpha, g1_ref, b1_ref, wup_ref, wdown_ref, g2_ref, b2_ref,
                out_ref, x1_scr, h_scr):
    tm = x_ref.shape[0]
    d_ff = wup_ref.shape[1]
    blocks = [slice(r, r + _MLP_ROW_BLOCK) for r in range(0, tm, _MLP_ROW_BLOCK)]
    for rows in blocks:
        x1_scr[rows, :] = _layernorm(alpha * x_ref[rows, :] + mix_rows(rows), g1_ref[...], b1_ref[...])
    for rows in blocks:
        x1b = x1_scr[rows, :].astype(_BF16)
        for lo in range(0, d_ff, _FF_CHUNK):
            hmid = jnp.dot(x1b, wup_ref[:, lo:lo + _FF_CHUNK], preferred_element_type=_F32)
            hmid = jnp.maximum(hmid, 0.0)
            h_scr[rows, lo:lo + _FF_CHUNK] = (hmid * hmid).astype(_BF16)
    for rows in blocks:
        y = jnp.dot(h_scr[rows, :], wdown_ref[...], preferred_element_type=_F32)
        out_ref[rows, :] = _layernorm(alpha * x1_scr[rows, :] + y, g2_ref[...], b2_ref[...])


def _mla_proj_kernel(x_ref, win_ref, gq_ref, gkv_ref, wqT_ref, wuk_ref, wvT_ref,
                     cs_ref, cosT_ref, sinT_ref, qT_ref, k_ref, vT_ref,
                     *, q_rank, kv_rank, heads, nope, rope, q_scale):
    xb = x_ref[...].astype(_BF16)
    h = jnp.dot(xb, win_ref[...], preferred_element_type=_F32)
    cq = _rmsnorm(h[:, :q_rank], gq_ref[...]).astype(_BF16)
    ckv = _rmsnorm(h[:, q_rank:q_rank + kv_rank], gkv_ref[...]).astype(_BF16)
    t = h[:, q_rank + kv_rank:] * cs_ref[...]
    kr = (t + pltpu.roll(t, rope, 1))[:, :rope].astype(_BF16)

    half = rope // 2
    hd = nope + rope
    qT = lax.dot_general(wqT_ref[...], cq, _NT, preferred_element_type=_F32)
    cosT = cosT_ref[...]
    sinT = sinT_ref[...]
    for hh in range(heads):
        base = hh * hd
        qT_ref[hh, 0:nope, :] = (qT[base:base + nope] * q_scale).astype(_BF16)
        x1 = qT[base + nope:base + nope + half]
        x2 = qT[base + nope + half:base + hd]
        qT_ref[hh, nope:nope + half, :] = ((x1 * cosT - x2 * sinT) * q_scale).astype(_BF16)
        qT_ref[hh, nope + half:hd, :] = ((x1 * sinT + x2 * cosT) * q_scale).astype(_BF16)

    kn = jnp.dot(ckv, wuk_ref[...], preferred_element_type=_F32)
    for hh in range(heads):
        k_ref[hh, :, 0:nope] = kn[:, hh * nope:(hh + 1) * nope].astype(_BF16)
        k_ref[hh, :, nope:hd] = kr

    vdim = wvT_ref.shape[0] // heads
    vT = lax.dot_general(wvT_ref[...], ckv, _NT, preferred_element_type=_F32)
    tk = vT_ref.shape[3]
    pad = vT_ref.shape[2] - vdim
    ones_rows = (lax.broadcasted_iota(jnp.int32, (pad, tk), 0) == 0).astype(_BF16)
    for hh in range(heads):
        for j in range(vT_ref.shape[1]):
            vT_ref[hh, j, 0:vdim, :] = vT[hh * vdim:(hh + 1) * vdim, j * tk:(j + 1) * tk].astype(_BF16)
            vT_ref[hh, j, vdim:vdim + pad, :] = ones_rows


def _mla_attn_kernel(qT_ref, k_ref, vT_ref, o_ref, m_scr, acc_scr,
                     s00, s01, s10, s11, cm00, cm01, cm10, cm11, p_a, p_b, al_a, al_b):
    tk, tq = s00.shape[0], acc_scr.shape[1]
    tsub = qT_ref.shape[2]
    nsub = tq // tsub
    n_q = qT_ref.shape[0] // nsub
    assert tq == 2 * tk and tk == vT_ref.shape[2]

    cb = _ATTN_COL_BLOCK

    def q_stage(qi, c, s_ref, cm_ref, col_lo=0):
        k = k_ref[pl.ds(pl.multiple_of(c * tk, tk), tk), :]
        for lo in range(col_lo, tq, cb):
            j, off = divmod(lo, tsub)
            qT = qT_ref[qi * nsub + j, :, off:off + cb]
            s = jnp.dot(k, qT, preferred_element_type=_F32)
            s_ref[:, lo:lo + cb] = s
            cm_ref[:, lo:lo + cb] = jnp.max(s, axis=0, keepdims=True)

    def x_stage(s_ref, cm_ref, p_ref, al_ref, mask_off=None, col_lo=0):
        for lo in range(col_lo, tq, cb):
            cols = slice(lo, lo + cb)
            s = s_ref[:, cols]
            if mask_off is None:
                cm = cm_ref[:, cols]
            else:
                krel = lax.broadcasted_iota(jnp.int32, s.shape, 0) + mask_off
                qrel = lax.broadcasted_iota(jnp.int32, s.shape, 1) + lo
                s = jnp.where(krel <= qrel, s, -jnp.inf)
                cm = jnp.max(s, axis=0, keepdims=True)
            m_prev = m_scr[:, cols]
            m_new = jnp.maximum(m_prev, cm)
            alpha = jnp.exp2(m_prev - m_new)
            p = jnp.exp2(s - m_new)
            p_ref[:, cols] = p.astype(_BF16)
            al_ref[:, cols] = alpha
            m_scr[:, cols] = m_new

    def v_stage(c, p_ref, al_ref, col_lo=0):
        for lo in range(col_lo, tq, cb):
            cols = slice(lo, lo + cb)
            acc_scr[:, cols] = al_ref[:, cols] * acc_scr[:, cols] + jnp.dot(
                vT_ref[c], p_ref[:, cols], preferred_element_type=_F32)

    s_bufs = (((s00, cm00), (s01, cm01)), ((s10, cm10), (s11, cm11)))

    def start_tile():
        m_scr[...] = jnp.full(m_scr.shape, -jnp.inf, _F32)
        acc_scr[...] = jnp.zeros(acc_scr.shape, _F32)
        p_b[...] = jnp.zeros(p_b.shape, p_b.dtype)
        al_b[...] = jnp.ones(al_b.shape, _F32)

    def regular_step(qi, a, par):
        (sa, cma), (sb, cmb) = s_bufs[par]
        (na, ncma), (nb, ncmb) = s_bufs[1 - par]
        q_stage(qi, a + 2, na, ncma)
        x_stage(sa, cma, p_a, al_a)
        v_stage(jnp.maximum(a - 1, 0), p_b, al_b)
        q_stage(qi, a + 3, nb, ncmb)
        x_stage(sb, cmb, p_b, al_b)
        v_stage(a, p_a, al_a)

    def diagonal_step(qi, par):
        a = 2 * qi
        (sa, cma), (sb, cmb) = s_bufs[par]
        (na, ncma), (nb, ncmb) = s_bufs[1 - par]
        q_next = jnp.minimum(qi + 1, n_q - 1)
        q_stage(q_next, 0, na, ncma)
        x_stage(sa, cma, p_a, al_a, mask_off=0)
        v_stage(jnp.maximum(a - 1, 0), p_b, al_b)
        q_stage(q_next, 1, nb, ncmb)
        x_stage(sb, cmb, p_b, al_b, mask_off=tk, col_lo=tk)
        v_stage(a, p_a, al_a)
        v_stage(a + 1, p_b, al_b, col_lo=tk)
        vdim = o_ref.shape[1]
        inv_l = 1.0 / acc_scr[vdim:vdim + 1, :]
        o = (acc_scr[0:vdim, :] * inv_l).T
        o_ref[pl.ds(pl.multiple_of(qi * tq, tq), tq), :] = o.astype(o_ref.dtype)
        start_tile()

    def by_parity(par, fn):
        for static_par in range(2):
            @pl.when(par == static_par)
            def _():
                fn(static_par)

    def tile_body(qi, par):
        def step_body(u, par):
            by_parity(par, functools.partial(regular_step, qi, 2 * u))
            return 1 - par

        par = lax.fori_loop(0, qi, step_body, par)
        by_parity(par, functools.partial(diagonal_step, qi))
        return 1 - par

    start_tile()
    q_stage(0, 0, s00, cm00)
    q_stage(0, 1, s01, cm01)
    lax.fori_loop(0, n_q, tile_body, jnp.int32(0))


def _outproj_mlp_kernel(o_ref, x_ref, wo_ref, g1_ref, b1_ref, wup_ref, wdown_ref,
                        g2_ref, b2_ref, out_ref, x1_scr, h_scr, *, alpha):
    def mix_rows(rows):
        return jnp.dot(o_ref[rows, :], wo_ref[...], preferred_element_type=_F32)

    _mix_mlp_ln(mix_rows, x_ref, alpha, g1_ref, b1_ref, wup_ref, wdown_ref, g2_ref, b2_ref,
                out_ref, x1_scr, h_scr)


def _swa_mlp_kernel(x_ref, wqT_ref, wk_ref, wvT_ref, wo_ref, bucketT_ref, relb_ref, sink_ref,
                    g1_ref, b1_ref, wup_ref, wdown_ref, g2_ref, b2_ref, out_ref,
                    k_scr, vTe_scr, qT_scr, biasT_scr, s_scr, p_scr, m_scr, oT_scr, x1_scr, h_scr,
                    *, alpha, q_heads, kv_heads, head_dim, q_scale):
    blk = _SWA_BLOCK
    group = q_heads // kv_heads
    tm = x_ref.shape[0]
    n_blk = tm // blk
    vrows = vTe_scr.shape[0] // kv_heads
    first_tile = pl.program_id(1) == 0
    always = pl.program_id(1) >= 0

    @pl.when((pl.program_id(0) == 0) & first_tile)
    def _():
        bkt = bucketT_ref[...]
        for hq in range(q_heads):
            bias = jnp.full(bkt.shape, -jnp.inf, _F32)
            for bb in range(_REL_BUCKETS):
                bias = jnp.where(bkt == bb, relb_ref[bb, hq] * _LOG2E, bias)
            g, gi = divmod(hq, group)
            biasT_scr[g, :, gi * blk:(gi + 1) * blk] = bias

    @pl.when(first_tile)
    def _():
        k_scr[0:blk, :] = jnp.zeros((blk, k_scr.shape[1]), k_scr.dtype)
        vTe_scr[:, 0:blk] = jnp.zeros((vTe_scr.shape[0], blk), vTe_scr.dtype)

    xb = x_ref[...].astype(_BF16)
    qT_scr[...] = (lax.dot_general(wqT_ref[...], xb, _NT, preferred_element_type=_F32)
                   * q_scale).astype(_BF16)
    k_scr[blk:blk + tm, :] = jnp.dot(xb, wk_ref[...], preferred_element_type=_F32).astype(_BF16)
    vT = lax.dot_general(wvT_ref[...], xb, _NT, preferred_element_type=_F32).astype(_BF16)
    ones_rows = (lax.broadcasted_iota(jnp.int32, (vrows - head_dim, tm), 0) == 0).astype(_BF16)
    for g in range(kv_heads):
        vTe_scr[g * vrows:g * vrows + head_dim, blk:blk + tm] = vT[g * head_dim:(g + 1) * head_dim]
        vTe_scr[g * vrows + head_dim:(g + 1) * vrows, blk:blk + tm] = ones_rows

    first_pen = jnp.where(first_tile, -jnp.inf, 0.0).astype(_F32)
    zeros_q = jnp.zeros((head_dim, group * blk), _BF16)
    heads_per_lane_tile = _LANES // head_dim
    units = [(n, g) for n in range(n_blk) for g in range(kv_heads)]

    def sink_row(g):
        return jnp.concatenate(
            [jnp.full((1, blk), sink_ref[g * group + gi] * _LOG2E, _F32) for gi in range(group)],
            axis=1)

    sinks = [None] * kv_heads

    def scores(u):
        n, g = units[u]
        lt = g // heads_per_lane_tile
        kpair = k_scr[n * blk:(n + 2) * blk, lt * _LANES:(lt + 1) * _LANES]
        qg = jnp.concatenate(
            [qT_scr[(g * group + gi) * head_dim:(g * group + gi + 1) * head_dim,
                    n * blk:(n + 1) * blk] for gi in range(group)], axis=1)
        pieces = [zeros_q] * heads_per_lane_tile
        pieces[g % heads_per_lane_tile] = qg
        qpad = jnp.concatenate(pieces, axis=0)
        s = jnp.dot(kpair, qpad, preferred_element_type=_F32) + biasT_scr[g]
        if n == 0:
            s = jnp.concatenate([s[:blk] + first_pen, s[blk:]], axis=0)
        s_scr[u] = s
        m_scr[u] = jnp.maximum(jnp.max(s, axis=0, keepdims=True), sinks[g])

    def exponentials(u):
        p_scr[u] = jnp.exp2(s_scr[u] - m_scr[u]).astype(_BF16)

    def weighted_values(u):
        n, g = units[u]
        oT = jnp.dot(vTe_scr[g * vrows:(g + 1) * vrows, n * blk:(n + 2) * blk], p_scr[u],
                     preferred_element_type=_F32)
        denom = oT[head_dim:head_dim + 1] + jnp.exp2(sinks[g] - m_scr[u])
        o = (oT[:head_dim] * (1.0 / denom)).astype(_BF16)
        for gi in range(group):
            hq = g * group + gi
            oT_scr[hq * head_dim:(hq + 1) * head_dim, n * blk:(n + 1) * blk] = (
                o[:, gi * blk:(gi + 1) * blk])

    for stage in (scores, exponentials, weighted_values):
        @pl.when(always)
        def _():
            for g in range(kv_heads):
                sinks[g] = sink_row(g)
            for u in range(len(units)):
                stage(u)

    @pl.when(pl.program_id(0) >= 0)
    def _():
        k_scr[0:blk, :] = k_scr[tm:tm + blk, :]
        vTe_scr[:, 0:blk] = vTe_scr[:, tm:tm + blk]

    def mix_rows(rows):
        return lax.dot_general(oT_scr[:, rows], wo_ref[...], _TN, preferred_element_type=_F32)

    _mix_mlp_ln(mix_rows, x_ref, alpha, g1_ref, b1_ref, wup_ref, wdown_ref, g2_ref, b2_ref,
                out_ref, x1_scr, h_scr)


def _t5_bucket_table(blk):
    i = np.arange(blk)[None, :]
    j = np.arange(2 * blk)[:, None]
    dist = i + blk - j
    max_exact = _REL_BUCKETS // 2
    nf = np.maximum(dist, 1).astype(np.float32)
    large = max_exact + (np.log(nf / np.float32(max_exact))
                         / np.float32(math.log(_REL_MAX_DIST / max_exact))
                         * np.float32(_REL_BUCKETS - max_exact)).astype(np.int32)
    large = np.minimum(large, _REL_BUCKETS - 1)
    bucket = np.where(dist < max_exact, np.maximum(dist, 0), large)
    valid = (dist >= 0) & (dist < blk)
    return np.where(valid, bucket, -1).astype(np.int32)


def _rope_tables(seq, rope):
    half = rope // 2
    inv = (np.float32(_ROPE_THETA) ** (-np.arange(half, dtype=np.float32) / np.float32(half)))
    ang = np.arange(seq, dtype=np.float32)[:, None] * inv.astype(np.float32)[None, :]
    cos = np.cos(ang.astype(np.float64)).astype(np.float32)
    sin = np.sin(ang.astype(np.float64)).astype(np.float32)
    cs = np.concatenate([cos, cos, -sin, sin], axis=1)
    return jnp.asarray(cs), jnp.asarray(cos.T.copy()), jnp.asarray(sin.T.copy())


def _row(v):
    return v.reshape(1, -1).astype(_F32)


def kernel(x, mla_w_in, mla_g_q, mla_g_kv, mla_w_uq, mla_w_uk, mla_w_uv, mla_w_o, kv_w_shared,
           swa_w_q, swa_sinks, swa_w_o, rel_bias, mlp_w_up, mlp_w_down, ln_mix_g, ln_mix_b,
           ln_mlp_g, ln_mlp_b):
    B, S, D = x.shape
    depth = mlp_w_up.shape[0]
    assert depth == 2 and mla_w_in.shape[0] == 1 and swa_w_q.shape[0] == 1
    alpha = (2 * depth) ** 0.25
    T = B * S
    tm = _TOKEN_TILE
    ta = _ATTN_TILE
    tk = ta // 2
    bf16_rows = 16
    ta_pad = ta + _LANES
    assert ta % tm == 0 and tm % tk == 0 and S % ta == 0

    q_rank, heads, hd = mla_w_uq.shape[1:]
    kv_rank, _, nope = mla_w_uk.shape[1:]
    vdim = mla_w_uv.shape[3]
    vrows = vdim + bf16_rows
    rope = hd - nope
    half = rope // 2
    assert 2 * rope == _LANES and mla_w_in.shape[2] == q_rank + kv_rank + rope

    w_in = mla_w_in[0]
    r0 = q_rank + kv_rank
    w_in_ext = jnp.concatenate(
        [w_in, w_in[:, r0 + half:r0 + rope], w_in[:, r0:r0 + half]], axis=1).astype(_BF16)
    wqT = mla_w_uq[0].reshape(q_rank, heads * hd).T.astype(_BF16)
    wuk = mla_w_uk[0].reshape(kv_rank, heads * nope).astype(_BF16)
    wvT = mla_w_uv[0].reshape(kv_rank, heads * vdim).T.astype(_BF16)
    cs, cosT, sinT = _rope_tables(S, rope)
    n_t = S // tm
    q_scale = hd ** -0.5 * _LOG2E

    cparams2 = pltpu.CompilerParams(dimension_semantics=("arbitrary", "arbitrary"),
                                    vmem_limit_bytes=_VMEM_LIMIT_BYTES)
    cparams1 = pltpu.CompilerParams(dimension_semantics=("arbitrary",),
                                    vmem_limit_bytes=_VMEM_LIMIT_BYTES)

    qT_all, k_all, vT_all = pl.pallas_call(
        functools.partial(_mla_proj_kernel, q_rank=q_rank, kv_rank=kv_rank, heads=heads,
                          nope=nope, rope=rope, q_scale=q_scale),
        grid=(B, n_t),
        in_specs=[
            pl.BlockSpec((None, tm, D), lambda b, i: (b, i, 0)),
            _const_spec(w_in_ext.shape),
            _const_spec((1, q_rank)),
            _const_spec((1, kv_rank)),
            _const_spec(wqT.shape),
            _const_spec(wuk.shape),
            _const_spec(wvT.shape),
            pl.BlockSpec((tm, 2 * rope), lambda b, i: (i, 0)),
            pl.BlockSpec((half, tm), lambda b, i: (0, i)),
            pl.BlockSpec((half, tm), lambda b, i: (0, i)),
        ],
        out_specs=[
            pl.BlockSpec((None, heads, None, hd, tm), lambda b, i: (b, 0, i, 0, 0)),
            pl.BlockSpec((None, heads, tm, hd), lambda b, i: (b, 0, i, 0)),
            pl.BlockSpec((None, heads, tm // tk, vrows, tk), lambda b, i: (b, 0, i, 0, 0)),
        ],
        out_shape=[
            jax.ShapeDtypeStruct((B, heads, n_t, hd, tm), _BF16),
            jax.ShapeDtypeStruct((B, heads, S, hd), _BF16),
            jax.ShapeDtypeStruct((B, heads, S // tk, vrows, tk), _BF16),
        ],
        compiler_params=cparams2,
        name="mla_proj",
    )(x, w_in_ext, _row(mla_g_q[0]), _row(mla_g_kv[0]), wqT, wuk, wvT, cs, cosT, sinT)

    o = pl.pallas_call(
        _mla_attn_kernel,
        grid=(B, heads),
        in_specs=[
            pl.BlockSpec((None, None, n_t, hd, tm), lambda b, h: (b, h, 0, 0, 0)),
            pl.BlockSpec((None, None, S, hd), lambda b, h: (b, h, 0, 0)),
            pl.BlockSpec((None, None, S // tk, vrows, tk), lambda b, h: (b, h, 0, 0, 0)),
        ],
        out_specs=pl.BlockSpec((None, S, vdim), lambda b, h: (b, 0, h)),
        out_shape=jax.ShapeDtypeStruct((B, S, heads * vdim), _BF16),
        scratch_shapes=[
            pltpu.VMEM((1, ta), _F32),
            pltpu.VMEM((vrows, ta), _F32),
        ] + [pltpu.VMEM((tk, ta_pad), _F32)] * 4
          + [pltpu.VMEM((1, ta), _F32)] * 4
          + [pltpu.VMEM((tk, ta), _BF16)] * 2
          + [pltpu.VMEM((1, ta), _F32)] * 2,
        compiler_params=cparams2,
        name="mla_attn",
    )(qT_all, k_all, vT_all)

    d_ff = mlp_w_up.shape[2]
    w_up_all = mlp_w_up.astype(_BF16)
    w_down_all = mlp_w_down.astype(_BF16)
    mlp_scratch = [
        pltpu.VMEM((tm, D), _F32),
        pltpu.VMEM((tm, d_ff), _BF16),
    ]
    x1 = pl.pallas_call(
        functools.partial(_outproj_mlp_kernel, alpha=alpha),
        grid=(T // tm,),
        in_specs=[
            pl.BlockSpec((tm, heads * vdim), lambda i: (i, 0)),
            pl.BlockSpec((tm, D), lambda i: (i, 0)),
            _const_spec((heads * vdim, D)),
            _const_spec((1, D)),
            _const_spec((1, D)),
            _layer_spec(w_up_all.shape, 0),
            _layer_spec(w_down_all.shape, 0),
            _const_spec((1, D)),
            _const_spec((1, D)),
        ],
        out_specs=pl.BlockSpec((tm, D), lambda i: (i, 0)),
        out_shape=jax.ShapeDtypeStruct((T, D), _F32),
        scratch_shapes=mlp_scratch,
        compiler_params=cparams1,
        name="mla_out_mlp",
    )(o.reshape(T, heads * vdim), x.reshape(T, D), mla_w_o[0].astype(_BF16),
      _row(ln_mix_g[0]), _row(ln_mix_b[0]), w_up_all, w_down_all,
      _row(ln_mlp_g[0]), _row(ln_mlp_b[0]))

    q_heads = swa_sinks.shape[1]
    s_hd = swa_w_q.shape[2] // q_heads
    kv_heads = kv_w_shared.shape[1] // (2 * s_hd)
    group = q_heads // kv_heads
    blk = _SWA_BLOCK
    assert _LANES % s_hd == 0 and tm % blk == 0
    wqT1 = swa_w_q[0].T.astype(_BF16)
    wk1 = kv_w_shared[:, :kv_heads * s_hd].astype(_BF16)
    wvT1 = kv_w_shared[:, kv_heads * s_hd:].T.astype(_BF16)
    bucketT = jnp.asarray(_t5_bucket_table(blk))
    n_units = (tm // blk) * kv_heads

    smem = pl.BlockSpec(memory_space=pltpu.SMEM)
    out = pl.pallas_call(
        functools.partial(_swa_mlp_kernel, alpha=alpha, q_heads=q_heads, kv_heads=kv_heads,
                          head_dim=s_hd, q_scale=s_hd ** -0.5 * _LOG2E),
        grid=(B, n_t),
        in_specs=[
            pl.BlockSpec((None, tm, D), lambda b, i: (b, i, 0)),
            _const_spec(wqT1.shape),
            _const_spec(wk1.shape),
            _const_spec(wvT1.shape),
            _const_spec((q_heads * s_hd, D)),
            _const_spec(bucketT.shape),
            smem,
            smem,
            _const_spec((1, D)),
            _const_spec((1, D)),
            _layer_spec(w_up_all.shape, 1),
            _layer_spec(w_down_all.shape, 1),
            _const_spec((1, D)),
            _const_spec((1, D)),
        ],
        out_specs=pl.BlockSpec((None, tm, D), lambda b, i: (b, i, 0)),
        out_shape=jax.ShapeDtypeStruct((B, S, D), _F32),
        scratch_shapes=[
            pltpu.VMEM((blk + tm, kv_heads * s_hd), _BF16),
            pltpu.VMEM((kv_heads * (s_hd + bf16_rows), blk + tm), _BF16),
            pltpu.VMEM((q_heads * s_hd, tm), _BF16),
            pltpu.VMEM((kv_heads, 2 * blk, group * blk), _F32),
            pltpu.VMEM((n_units, 2 * blk, group * blk), _F32),
            pltpu.VMEM((n_units, 2 * blk, group * blk), _BF16),
            pltpu.VMEM((n_units, 1, group * blk), _F32),
            pltpu.VMEM((q_heads * s_hd, tm), _BF16),
        ] + mlp_scratch,
        compiler_params=cparams2,
        name="swa_mlp",
    )(x1.reshape(B, S, D), wqT1, wk1, wvT1, swa_w_o[0].astype(_BF16), bucketT,
      rel_bias.astype(_F32), swa_sinks[0].astype(_F32),
      _row(ln_mix_g[1]), _row(ln_mix_b[1]), w_up_all, w_down_all,
      _row(ln_mlp_g[1]), _row(ln_mlp_b[1]))
    return out
```

```python
import functools
import math

import numpy as np
import jax
import jax.numpy as jnp
from jax import lax
from jax.experimental import pallas as pl
from jax.experimental.pallas import tpu as pltpu

_F32 = jnp.float32
_BF16 = jnp.bfloat16

_LN_EPS = 1e-5
_RMS_EPS = 1e-6
_ROPE_THETA = 10000.0
_SWA_BLOCK = 128
_REL_BUCKETS = 32
_REL_MAX_DIST = 128
_LOG2E = math.log2(math.e)

_V7X_VMEM_BYTES = 64 * 1024 * 1024
_VMEM_LIMIT_BYTES = _V7X_VMEM_BYTES - 8 * 1024 * 1024
_LANES = 128

_TOKEN_TILE = 512
_ATTN_TILE = 1024
_ATTN_COL_BLOCK = 256
_FF_CHUNK = 1024
_MLP_ROW_BLOCK = 256

_NT = (((1,), (1,)), ((), ()))
_TN = (((0,), (0,)), ((), ()))


def _const_spec(shape):
    nd = len(shape)
    return pl.BlockSpec(shape, lambda *_: (0,) * nd, pipeline_mode=pl.Buffered(1))


def _layer_spec(shape, layer):
    nd = len(shape) - 1
    return pl.BlockSpec((None,) + tuple(shape[1:]), lambda *_: (layer,) + (0,) * nd,
                        pipeline_mode=pl.Buffered(1))


def _layernorm(v, g, b):
    mu = jnp.mean(v, axis=-1, keepdims=True)
    d = v - mu
    var = jnp.mean(d * d, axis=-1, keepdims=True)
    return d * lax.rsqrt(var + _LN_EPS) * g + b


def _rmsnorm(v, g):
    return v * lax.rsqrt(jnp.mean(v * v, axis=-1, keepdims=True) + _RMS_EPS) * g


def _mix_mlp_ln(mix_rows, x_ref, alpha, g1_ref, b1_ref, wup_ref, wdown_ref, g2_ref, b2_ref,
                out_ref, x1_scr, h_scr):
    tm = x_ref.shape[0]
    d_ff = wup_ref.shape[1]
    blocks = [slice(r, r + _MLP_ROW_BLOCK) for r in range(0, tm, _MLP_ROW_BLOCK)]
    for rows in blocks:
        x1_scr[rows, :] = _layernorm(alpha * x_ref[rows, :] + mix_rows(rows), g1_ref[...], b1_ref[...])
    for rows in blocks:
        x1b = x1_scr[rows, :].astype(_BF16)
        for lo in range(0, d_ff, _FF_CHUNK):
            hmid = jnp.dot(x1b, wup_ref[:, lo:lo + _FF_CHUNK], preferred_element_type=_F32)
            hmid = jnp.maximum(hmid, 0.0)
            h_scr[rows, lo:lo + _FF_CHUNK] = (hmid * hmid).astype(_BF16)
    for rows in blocks:
        y = jnp.dot(h_scr[rows, :], wdown_ref[...], preferred_element_type=_F32)
        out_ref[rows, :] = _layernorm(alpha * x1_scr[rows, :] + y, g2_ref[...], b2_ref[...])


def _mla_proj_kernel(x_ref, win_ref, gq_ref, gkv_ref, wqT_ref, wuk_ref, wvT_ref,
                     cs_ref, cosT_ref, sinT_ref, qT_ref, k_ref, vT_ref,
                     *, q_rank, kv_rank, heads, nope, rope, q_scale):
    xb = x_ref[...].astype(_BF16)
    h = jnp.dot(xb, win_ref[...], preferred_element_type=_F32)
    cq = _rmsnorm(h[:, :q_rank], gq_ref[...]).astype(_BF16)
    ckv = _rmsnorm(h[:, q_rank:q_rank + kv_rank], gkv_ref[...]).astype(_BF16)
    t = h[:, q_rank + kv_rank:] * cs_ref[...]
    kr = (t + pltpu.roll(t, rope, 1))[:, :rope].astype(_BF16)

    half = rope // 2
    hd = nope + rope
    qT = lax.dot_general(wqT_ref[...], cq, _NT, preferred_element_type=_F32)
    cosT = cosT_ref[...]
    sinT = sinT_ref[...]
    for hh in range(heads):
        base = hh * hd
        qT_ref[hh, 0:nope, :] = (qT[base:base + nope] * q_scale).astype(_BF16)
        x1 = qT[base + nope:base + nope + half]
        x2 = qT[base + nope + half:base + hd]
        qT_ref[hh, nope:nope + half, :] = ((x1 * cosT - x2 * sinT) * q_scale).astype(_BF16)
        qT_ref[hh, nope + half:hd, :] = ((x1 * sinT + x2 * cosT) * q_scale).astype(_BF16)

    kn = jnp.dot(ckv, wuk_ref[...], preferred_element_type=_F32)
    for hh in range(heads):
        k_ref[hh, :, 0:nope] = kn[:, hh * nope:(hh + 1) * nope].astype(_BF16)
        k_ref[hh, :, nope:hd] = kr

    vdim = wvT_ref.shape[0] // heads
    vT = lax.dot_general(wvT_ref[...], ckv, _NT, preferred_element_type=_F32)
    tk = vT_ref.shape[3]
    pad = vT_ref.shape[2] - vdim
    ones_rows = (lax.broadcasted_iota(jnp.int32, (pad, tk), 0) == 0).astype(_BF16)
    for hh in range(heads):
        for j in range(vT_ref.shape[1]):
            vT_ref[hh, j, 0:vdim, :] = vT[hh * vdim:(hh + 1) * vdim, j * tk:(j + 1) * tk].astype(_BF16)
            vT_ref[hh, j, vdim:vdim + pad, :] = ones_rows


def _mla_attn_kernel(qT_ref, k_ref, vT_ref, o_ref, m_scr, acc_scr,
                     s00, s01, s10, s11, cm00, cm01, cm10, cm11, p_a, p_b, al_a, al_b):
    tk, tq = s00.shape[0], acc_scr.shape[1]
    tsub = qT_ref.shape[2]
    nsub = tq // tsub
    n_q = qT_ref.shape[0] // nsub
    assert tq == 2 * tk and tk == vT_ref.shape[2]

    cb = _ATTN_COL_BLOCK

    def q_stage(qi, c, s_ref, cm_ref, col_lo=0):
        k = k_ref[pl.ds(pl.multiple_of(c * tk, tk), tk), :]
        for lo in range(col_lo, tq, cb):
            j, off = divmod(lo, tsub)
            qT = qT_ref[qi * nsub + j, :, off:off + cb]
            s = jnp.dot(k, qT, preferred_element_type=_F32)
            s_ref[:, lo:lo + cb] = s
            cm_ref[:, lo:lo + cb] = jnp.max(s, axis=0, keepdims=True)

    def x_stage(s_ref, cm_ref, p_ref, al_ref, mask_off=None, col_lo=0):
        for lo in range(col_lo, tq, cb):
            cols = slice(lo, lo + cb)
            live = tk
            if mask_off is None or mask_off + tk - 1 <= lo:
                s = s_ref[:, cols]
                cm = cm_ref[:, cols]
            else:
                live = min(tk, lo + cb - mask_off)
                s = s_ref[0:live, cols]
                krel = lax.broadcasted_iota(jnp.int32, s.shape, 0) + mask_off
                qrel = lax.broadcasted_iota(jnp.int32, s.shape, 1) + lo
                s = jnp.where(krel <= qrel, s, -jnp.inf)
                cm = jnp.max(s, axis=0, keepdims=True)
            m_prev = m_scr[:, cols]
            m_new = jnp.maximum(m_prev, cm)
            alpha = jnp.exp2(m_prev - m_new)
            p = jnp.exp2(s - m_new)
            p_ref[0:live, cols] = p.astype(_BF16)
            if live < tk:
                p_ref[live:tk, cols] = jnp.zeros((tk - live, cb), _BF16)
            al_ref[:, cols] = alpha
            m_scr[:, cols] = m_new

    def v_stage(c, p_ref, al_ref, col_lo=0):
        for lo in range(col_lo, tq, cb):
            cols = slice(lo, lo + cb)
            acc_scr[:, cols] = al_ref[:, cols] * acc_scr[:, cols] + jnp.dot(
                vT_ref[c], p_ref[:, cols], preferred_element_type=_F32)

    s_bufs = (((s00, cm00), (s01, cm01)), ((s10, cm10), (s11, cm11)))

    def start_tile():
        m_scr[...] = jnp.full(m_scr.shape, -jnp.inf, _F32)
        acc_scr[...] = jnp.zeros(acc_scr.shape, _F32)


    def regular_step(qi, a, par, first=False):
        (sa, cma), (sb, cmb) = s_bufs[par]
        (na, ncma), (nb, ncmb) = s_bufs[1 - par]
        q_stage(qi, a + 2, na, ncma)
        x_stage(sa, cma, p_a, al_a)
        if not first:
            v_stage(a - 1, p_b, al_b)
        q_stage(qi, a + 3, nb, ncmb)
        x_stage(sb, cmb, p_b, al_b)
        v_stage(a, p_a, al_a)

    def diagonal_step(qi, par, first=False, last=False):
        a = 2 * qi
        (sa, cma), (sb, cmb) = s_bufs[par]
        (na, ncma), (nb, ncmb) = s_bufs[1 - par]
        if not last:
            q_stage(qi + 1, 0, na, ncma)
        x_stage(sa, cma, p_a, al_a, mask_off=0)
        if not first:
            v_stage(a - 1, p_b, al_b)
        if not last:
            q_stage(qi + 1, 1, nb, ncmb)
        x_stage(sb, cmb, p_b, al_b, mask_off=tk, col_lo=tk)
        v_stage(a, p_a, al_a)
        v_stage(a + 1, p_b, al_b, col_lo=tk)
        vdim = o_ref.shape[1]
        inv_l = 1.0 / acc_scr[vdim:vdim + 1, :]
        o = (acc_scr[0:vdim, :] * inv_l).T
        o_ref[pl.ds(pl.multiple_of(qi * tq, tq), tq), :] = o.astype(o_ref.dtype)
        if not last:
            start_tile()

    def by_parity(par, fn):
        for static_par in range(2):
            @pl.when(par == static_par)
            def _():
                fn(static_par)

    def tile_body(qi, par, last=False):
        def step_body(u, par):
            by_parity(par, functools.partial(regular_step, qi, 2 * u))
            return 1 - par

        by_parity(par, functools.partial(regular_step, qi, 0, first=True))
        par = lax.fori_loop(1, qi, step_body, 1 - par)
        by_parity(par, functools.partial(diagonal_step, qi, last=last))
        return 1 - par

    start_tile()
    q_stage(0, 0, s00, cm00)
    q_stage(0, 1, s01, cm01)
    diagonal_step(jnp.int32(0), 0, first=True, last=(n_q == 1))
    if n_q > 1:
        par = lax.fori_loop(1, n_q - 1, tile_body, jnp.int32(1))
        tile_body(jnp.int32(n_q - 1), par, last=True)


def _outproj_mlp_kernel(o_ref, x_ref, wo_ref, g1_ref, b1_ref, wup_ref, wdown_ref,
                        g2_ref, b2_ref, out_ref, x1_scr, h_scr, *, alpha):
    def mix_rows(rows):
        return jnp.dot(o_ref[rows, :], wo_ref[...], preferred_element_type=_F32)

    _mix_mlp_ln(mix_rows, x_ref, alpha, g1_ref, b1_ref, wup_ref, wdown_ref, g2_ref, b2_ref,
                out_ref, x1_scr, h_scr)


def _swa_mlp_kernel(x_ref, wqT_ref, wk_ref, wvT_ref, wo_ref, bucketT_ref, relb_ref, sink_ref,
                    g1_ref, b1_ref, wup_ref, wdown_ref, g2_ref, b2_ref, out_ref,
                    k_scr, vTe_scr, qT_scr, biasT_scr, s_scr, p_scr, m_scr, oT_scr, x1_scr, h_scr,
                    *, alpha, q_heads, kv_heads, head_dim, q_scale):
    blk = _SWA_BLOCK
    group = q_heads // kv_heads
    tm = x_ref.shape[0]
    n_blk = tm // blk
    vrows = vTe_scr.shape[0] // kv_heads
    first_tile = pl.program_id(1) == 0
    always = pl.program_id(1) >= 0

    @pl.when((pl.program_id(0) == 0) & first_tile)
    def _():
        bkt = bucketT_ref[...]
        for hq in range(q_heads):
            bias = jnp.full(bkt.shape, -jnp.inf, _F32)
            for bb in range(_REL_BUCKETS):
                bias = jnp.where(bkt == bb, relb_ref[bb, hq] * _LOG2E, bias)
            g, gi = divmod(hq, group)
            biasT_scr[g, :, gi * blk:(gi + 1) * blk] = bias

    @pl.when(first_tile)
    def _():
        k_scr[0:blk, :] = jnp.zeros((blk, k_scr.shape[1]), k_scr.dtype)
        vTe_scr[:, 0:blk] = jnp.zeros((vTe_scr.shape[0], blk), vTe_scr.dtype)

    xb = x_ref[...].astype(_BF16)
    qT_scr[...] = (lax.dot_general(wqT_ref[...], xb, _NT, preferred_element_type=_F32)
                   * q_scale).astype(_BF16)
    k_scr[blk:blk + tm, :] = jnp.dot(xb, wk_ref[...], preferred_element_type=_F32).astype(_BF16)
    vT = lax.dot_general(wvT_ref[...], xb, _NT, preferred_element_type=_F32).astype(_BF16)
    ones_rows = (lax.broadcasted_iota(jnp.int32, (vrows - head_dim, tm), 0) == 0).astype(_BF16)
    for g in range(kv_heads):
        vTe_scr[g * vrows:g * vrows + head_dim, blk:blk + tm] = vT[g * head_dim:(g + 1) * head_dim]
        vTe_scr[g * vrows + head_dim:(g + 1) * vrows, blk:blk + tm] = ones_rows

    first_pen = jnp.where(first_tile, -jnp.inf, 0.0).astype(_F32)
    zeros_q = jnp.zeros((head_dim, group * blk), _BF16)
    heads_per_lane_tile = _LANES // head_dim
    units = [(n, g) for n in range(n_blk) for g in range(kv_heads)]

    def sink_row(g):
        return jnp.concatenate(
            [jnp.full((1, blk), sink_ref[g * group + gi] * _LOG2E, _F32) for gi in range(group)],
            axis=1)

    sinks = [None] * kv_heads

    def scores(u):
        n, g = units[u]
        lt = g // heads_per_lane_tile
        kpair = k_scr[n * blk:(n + 2) * blk, lt * _LANES:(lt + 1) * _LANES]
        qg = jnp.concatenate(
            [qT_scr[(g * group + gi) * head_dim:(g * group + gi + 1) * head_dim,
                    n * blk:(n + 1) * blk] for gi in range(group)], axis=1)
        pieces = [zeros_q] * heads_per_lane_tile
        pieces[g % heads_per_lane_tile] = qg
        qpad = jnp.concatenate(pieces, axis=0)
        s = jnp.dot(kpair, qpad, preferred_element_type=_F32) + biasT_scr[g]
        if n == 0:
            s = jnp.concatenate([s[:blk] + first_pen, s[blk:]], axis=0)
        s_scr[u] = s
        m_scr[u] = jnp.maximum(jnp.max(s, axis=0, keepdims=True), sinks[g])

    def exponentials(u):
        p_scr[u] = jnp.exp2(s_scr[u] - m_scr[u]).astype(_BF16)

    def weighted_values(u):
        n, g = units[u]
        oT = jnp.dot(vTe_scr[g * vrows:(g + 1) * vrows, n * blk:(n + 2) * blk], p_scr[u],
                     preferred_element_type=_F32)
        denom = oT[head_dim:head_dim + 1] + jnp.exp2(sinks[g] - m_scr[u])
        o = (oT[:head_dim] * (1.0 / denom)).astype(_BF16)
        for gi in range(group):
            hq = g * group + gi
            oT_scr[hq * head_dim:(hq + 1) * head_dim, n * blk:(n + 1) * blk] = (
                o[:, gi * blk:(gi + 1) * blk])

    for stage in (scores, exponentials, weighted_values):
        @pl.when(always)
        def _():
            for g in range(kv_heads):
                sinks[g] = sink_row(g)
            for u in range(len(units)):
                stage(u)

    @pl.when(pl.program_id(0) >= 0)
    def _():
        k_scr[0:blk, :] = k_scr[tm:tm + blk, :]
        vTe_scr[:, 0:blk] = vTe_scr[:, tm:tm + blk]

    def mix_rows(rows):
        return lax.dot_general(oT_scr[:, rows], wo_ref[...], _TN, preferred_element_type=_F32)

    _mix_mlp_ln(mix_rows, x_ref, alpha, g1_ref, b1_ref, wup_ref, wdown_ref, g2_ref, b2_ref,
                out_ref, x1_scr, h_scr)


def _t5_bucket_table(blk):
    i = np.arange(blk)[None, :]
    j = np.arange(2 * blk)[:, None]
    dist = i + blk - j
    max_exact = _REL_BUCKETS // 2
    nf = np.maximum(dist, 1).astype(np.float32)
    large = max_exact + (np.log(nf / np.float32(max_exact))
                         / np.float32(math.log(_REL_MAX_DIST / max_exact))
                         * np.float32(_REL_BUCKETS - max_exact)).astype(np.int32)
    large = np.minimum(large, _REL_BUCKETS - 1)
    bucket = np.where(dist < max_exact, np.maximum(dist, 0), large)
    valid = (dist >= 0) & (dist < blk)
    return np.where(valid, bucket, -1).astype(np.int32)


def _rope_tables(seq, rope):
    half = rope // 2
    inv = (np.float32(_ROPE_THETA) ** (-np.arange(half, dtype=np.float32) / np.float32(half)))
    ang = np.arange(seq, dtype=np.float32)[:, None] * inv.astype(np.float32)[None, :]
    cos = np.cos(ang.astype(np.float64)).astype(np.float32)
    sin = np.sin(ang.astype(np.float64)).astype(np.float32)
    cs = np.concatenate([cos, cos, -sin, sin], axis=1)
    return jnp.asarray(cs), jnp.asarray(cos.T.copy()), jnp.asarray(sin.T.copy())


def _row(v):
    return v.reshape(1, -1).astype(_F32)


def kernel(x, mla_w_in, mla_g_q, mla_g_kv, mla_w_uq, mla_w_uk, mla_w_uv, mla_w_o, kv_w_shared,
           swa_w_q, swa_sinks, swa_w_o, rel_bias, mlp_w_up, mlp_w_down, ln_mix_g, ln_mix_b,
           ln_mlp_g, ln_mlp_b):
    B, S, D = x.shape
    depth = mlp_w_up.shape[0]
    assert depth == 2 and mla_w_in.shape[0] == 1 and swa_w_q.shape[0] == 1
    alpha = (2 * depth) ** 0.25
    T = B * S
    tm = _TOKEN_TILE
    ta = _ATTN_TILE
    tk = ta // 2
    bf16_rows = 16
    ta_pad = ta + _LANES
    assert ta % tm == 0 and tm % tk == 0 and S % ta == 0

    q_rank, heads, hd = mla_w_uq.shape[1:]
    kv_rank, _, nope = mla_w_uk.shape[1:]
    vdim = mla_w_uv.shape[3]
    vrows = vdim + bf16_rows
    rope = hd - nope
    half = rope // 2
    assert 2 * rope == _LANES and mla_w_in.shape[2] == q_rank + kv_rank + rope

    w_in = mla_w_in[0]
    r0 = q_rank + kv_rank
    w_in_ext = jnp.concatenate(
        [w_in, w_in[:, r0 + half:r0 + rope], w_in[:, r0:r0 + half]], axis=1).astype(_BF16)
    wqT = mla_w_uq[0].reshape(q_rank, heads * hd).T.astype(_BF16)
    wuk = mla_w_uk[0].reshape(kv_rank, heads * nope).astype(_BF16)
    wvT = mla_w_uv[0].reshape(kv_rank, heads * vdim).T.astype(_BF16)
    cs, cosT, sinT = _rope_tables(S, rope)
    n_t = S // tm
    q_scale = hd ** -0.5 * _LOG2E

    cparams2 = pltpu.CompilerParams(dimension_semantics=("arbitrary", "arbitrary"),
                                    vmem_limit_bytes=_VMEM_LIMIT_BYTES)
    cparams1 = pltpu.CompilerParams(dimension_semantics=("arbitrary",),
                                    vmem_limit_bytes=_VMEM_LIMIT_BYTES)

    qT_all, k_all, vT_all = pl.pallas_call(
        functools.partial(_mla_proj_kernel, q_rank=q_rank, kv_rank=kv_rank, heads=heads,
                          nope=nope, rope=rope, q_scale=q_scale),
        grid=(B, n_t),
        in_specs=[
            pl.BlockSpec((None, tm, D), lambda b, i: (b, i, 0)),
            _const_spec(w_in_ext.shape),
            _const_spec((1, q_rank)),
            _const_spec((1, kv_rank)),
            _const_spec(wqT.shape),
            _const_spec(wuk.shape),
            _const_spec(wvT.shape),
            pl.BlockSpec((tm, 2 * rope), lambda b, i: (i, 0)),
            pl.BlockSpec((half, tm), lambda b, i: (0, i)),
            pl.BlockSpec((half, tm), lambda b, i: (0, i)),
        ],
        out_specs=[
            pl.BlockSpec((None, heads, None, hd, tm), lambda b, i: (b, 0, i, 0, 0)),
            pl.BlockSpec((None, heads, tm, hd), lambda b, i: (b, 0, i, 0)),
            pl.BlockSpec((None, heads, tm // tk, vrows, tk), lambda b, i: (b, 0, i, 0, 0)),
        ],
        out_shape=[
            jax.ShapeDtypeStruct((B, heads, n_t, hd, tm), _BF16),
            jax.ShapeDtypeStruct((B, heads, S, hd), _BF16),
            jax.ShapeDtypeStruct((B, heads, S // tk, vrows, tk), _BF16),
        ],
        compiler_params=cparams2,
        name="mla_proj",
    )(x, w_in_ext, _row(mla_g_q[0]), _row(mla_g_kv[0]), wqT, wuk, wvT, cs, cosT, sinT)

    o = pl.pallas_call(
        _mla_attn_kernel,
        grid=(B, heads),
        in_specs=[
            pl.BlockSpec((None, None, n_t, hd, tm), lambda b, h: (b, h, 0, 0, 0)),
            pl.BlockSpec((None, None, S, hd), lambda b, h: (b, h, 0, 0)),
            pl.BlockSpec((None, None, S // tk, vrows, tk), lambda b, h: (b, h, 0, 0, 0)),
        ],
        out_specs=pl.BlockSpec((None, S, vdim), lambda b, h: (b, 0, h)),
        out_shape=jax.ShapeDtypeStruct((B, S, heads * vdim), _BF16),
        scratch_shapes=[
            pltpu.VMEM((1, ta), _F32),
            pltpu.VMEM((vrows, ta), _F32),
        ] + [pltpu.VMEM((tk, ta_pad), _F32)] * 4
          + [pltpu.VMEM((1, ta), _F32)] * 4
          + [pltpu.VMEM((tk, ta), _BF16)] * 2
          + [pltpu.VMEM((1, ta), _F32)] * 2,
        compiler_params=cparams2,
        name="mla_attn",
    )(qT_all, k_all, vT_all)

    d_ff = mlp_w_up.shape[2]
    w_up_all = mlp_w_up.astype(_BF16)
    w_down_all = mlp_w_down.astype(_BF16)
    mlp_scratch = [
        pltpu.VMEM((tm, D), _F32),
        pltpu.VMEM((tm, d_ff), _BF16),
    ]
    x1 = pl.pallas_call(
        functools.partial(_outproj_mlp_kernel, alpha=alpha),
        grid=(T // tm,),
        in_specs=[
            pl.BlockSpec((tm, heads * vdim), lambda i: (i, 0)),
            pl.BlockSpec((tm, D), lambda i: (i, 0)),
            _const_spec((heads * vdim, D)),
            _const_spec((1, D)),
            _const_spec((1, D)),
            _layer_spec(w_up_all.shape, 0),
            _layer_spec(w_down_all.shape, 0),
            _const_spec((1, D)),
            _const_spec((1, D)),
        ],
        out_specs=pl.BlockSpec((tm, D), lambda i: (i, 0)),
        out_shape=jax.ShapeDtypeStruct((T, D), _F32),
        scratch_shapes=mlp_scratch,
        compiler_params=cparams1,
        name="mla_out_mlp",
    )(o.reshape(T, heads * vdim), x.reshape(T, D), mla_w_o[0].astype(_BF16),
      _row(ln_mix_g[0]), _row(ln_mix_b[0]), w_up_all, w_down_all,
      _row(ln_mlp_g[0]), _row(ln_mlp_b[0]))

    q_heads = swa_sinks.shape[1]
    s_hd = swa_w_q.shape[2] // q_heads
    kv_heads = kv_w_shared.shape[1] // (2 * s_hd)
    group = q_heads // kv_heads
    blk = _SWA_BLOCK
    assert _LANES % s_hd == 0 and tm % blk == 0
    wqT1 = swa_w_q[0].T.astype(_BF16)
    wk1 = kv_w_shared[:, :kv_heads * s_hd].astype(_BF16)
    wvT1 = kv_w_shared[:, kv_heads * s_hd:].T.astype(_BF16)
    bucketT = jnp.asarray(_t5_bucket_table(blk))
    n_units = (tm // blk) * kv_heads

    smem = pl.BlockSpec(memory_space=pltpu.SMEM)
    out = pl.pallas_call(
        functools.partial(_swa_mlp_kernel, alpha=alpha, q_heads=q_heads, kv_heads=kv_heads,
                          head_dim=s_hd, q_scale=s_hd ** -0.5 * _LOG2E),
        grid=(B, n_t),
        in_specs=[
            pl.BlockSpec((None, tm, D), lambda b, i: (b, i, 0)),
            _const_spec(wqT1.shape),
            _const_spec(wk1.shape),
            _const_spec(wvT1.shape),
            _const_spec((q_heads * s_hd, D)),
            _const_spec(bucketT.shape),
            smem,
            smem,
            _const_spec((1, D)),
            _const_spec((1, D)),
            _layer_spec(w_up_all.shape, 1),
            _layer_spec(w_down_all.shape, 1),
            _const_spec((1, D)),
            _const_spec((1, D)),
        ],
        out_specs=pl.BlockSpec((None, tm, D), lambda b, i: (b, i, 0)),
        out_shape=jax.ShapeDtypeStruct((B, S, D), _F32),
        scratch_shapes=[
            pltpu.VMEM((blk + tm, kv_heads * s_hd), _BF16),
            pltpu.VMEM((kv_heads * (s_hd + bf16_rows), blk + tm), _BF16),
            pltpu.VMEM((q_heads * s_hd, tm), _BF16),
            pltpu.VMEM((kv_heads, 2 * blk, group * blk), _F32),
            pltpu.VMEM((n_units, 2 * blk, group * blk), _F32),
            pltpu.VMEM((n_units, 2 * blk, group * blk), _BF16),
            pltpu.VMEM((n_units, 1, group * blk), _F32),
            pltpu.VMEM((q_heads * s_hd, tm), _BF16),
        ] + mlp_scratch,
        compiler_params=cparams2,
        name="swa_mlp",
    )(x1.reshape(B, S, D), wqT1, wk1, wvT1, swa_w_o[0].astype(_BF16), bucketT,
      rel_bias.astype(_F32), swa_sinks[0].astype(_F32),
      _row(ln_mix_g[1]), _row(ln_mix_b[1]), w_up_all, w_down_all,
      _row(ln_mlp_g[1]), _row(ln_mlp_b[1]))
    return out
```

```python
import functools
import math

import numpy as np
import jax
import jax.numpy as jnp
from jax import lax
from jax.experimental import pallas as pl
from jax.experimental.pallas import tpu as pltpu

_F32 = jnp.float32
_BF16 = jnp.bfloat16

_LN_EPS = 1e-5
_RMS_EPS = 1e-6
_ROPE_THETA = 10000.0
_SWA_BLOCK = 128
_REL_BUCKETS = 32
_REL_MAX_DIST = 128
_LOG2E = math.log2(math.e)

_V7X_VMEM_BYTES = 64 * 1024 * 1024
_VMEM_LIMIT_BYTES = _V7X_VMEM_BYTES - 8 * 1024 * 1024
_LANES = 128

_TOKEN_TILE = 512
_ATTN_TILE = 1024
_ATTN_COL_BLOCK = 256
_FF_CHUNK = 1024
_MLP_ROW_BLOCK = 256

_NT = (((1,), (1,)), ((), ()))
_TN = (((0,), (0,)), ((), ()))


def _const_spec(shape):
    nd = len(shape)
    return pl.BlockSpec(shape, lambda *_: (0,) * nd, pipeline_mode=pl.Buffered(1))


def _layer_spec(shape, layer):
    nd = len(shape) - 1
    return pl.BlockSpec((None,) + tuple(shape[1:]), lambda *_: (layer,) + (0,) * nd,
                        pipeline_mode=pl.Buffered(1))


def _layernorm(v, g, b):
    mu = jnp.mean(v, axis=-1, keepdims=True)
    d = v - mu
    var = jnp.mean(d * d, axis=-1, keepdims=True)
    return d * lax.rsqrt(var + _LN_EPS) * g + b


def _rmsnorm(v, g):
    return v * lax.rsqrt(jnp.mean(v * v, axis=-1, keepdims=True) + _RMS_EPS) * g


def _mix_mlp_ln(mix_rows, x_ref, alpha, g1_ref, b1_ref, wup_ref, wdown_ref, g2_ref, b2_ref,
                out_ref, x1_scr, h_scr):
    tm = x_ref.shape[0]
    d_ff = wup_ref.shape[1]
    blocks = [slice(r, r + _MLP_ROW_BLOCK) for r in range(0, tm, _MLP_ROW_BLOCK)]
    for rows in blocks:
        x1_scr[rows, :] = _layernorm(alpha * x_ref[rows, :] + mix_rows(rows), g1_ref[...], b1_ref[...])
    for rows in blocks:
        x1b = x1_scr[rows, :].astype(_BF16)
        for lo in range(0, d_ff, _FF_CHUNK):
            hmid = jnp.dot(x1b, wup_ref[:, lo:lo + _FF_CHUNK], preferred_element_type=_F32)
            hmid = jnp.maximum(hmid, 0.0)
            h_scr[rows, lo:lo + _FF_CHUNK] = (hmid * hmid).astype(_BF16)
    for rows in blocks:
        y = jnp.dot(h_scr[rows, :], wdown_ref[...], preferred_element_type=_F32)
        out_ref[rows, :] = _layernorm(alpha * x1_scr[rows, :] + y, g2_ref[...], b2_ref[...])


def _mla_proj_kernel(x_ref, win_ref, gq_ref, gkv_ref, wqT_ref, wuk_ref, wvT_ref,
                     cs_ref, cosT_ref, sinT_ref, qT_ref, k_ref, vT_ref,
                     *, q_rank, kv_rank, heads, nope, rope, q_scale):
    xb = x_ref[...].astype(_BF16)
    h = jnp.dot(xb, win_ref[...], preferred_element_type=_F32)
    cq = _rmsnorm(h[:, :q_rank], gq_ref[...]).astype(_BF16)
    ckv = _rmsnorm(h[:, q_rank:q_rank + kv_rank], gkv_ref[...]).astype(_BF16)
    t = h[:, q_rank + kv_rank:] * cs_ref[...]
    kr = (t + pltpu.roll(t, rope, 1))[:, :rope].astype(_BF16)

    half = rope // 2
    hd = nope + rope
    qT = lax.dot_general(wqT_ref[...], cq, _NT, preferred_element_type=_F32)
    cosT = cosT_ref[...]
    sinT = sinT_ref[...]
    for hh in range(heads):
        base = hh * hd
        qT_ref[hh, 0:nope, :] = (qT[base:base + nope] * q_scale).astype(_BF16)
        x1 = qT[base + nope:base + nope + half]
        x2 = qT[base + nope + half:base + hd]
        qT_ref[hh, nope:nope + half, :] = ((x1 * cosT - x2 * sinT) * q_scale).astype(_BF16)
        qT_ref[hh, nope + half:hd, :] = ((x1 * sinT + x2 * cosT) * q_scale).astype(_BF16)

    kn = jnp.dot(ckv, wuk_ref[...], preferred_element_type=_F32)
    for hh in range(heads):
        k_ref[hh, :, 0:nope] = kn[:, hh * nope:(hh + 1) * nope].astype(_BF16)
        k_ref[hh, :, nope:hd] = kr

    vdim = wvT_ref.shape[0] // heads
    vT = lax.dot_general(wvT_ref[...], ckv, _NT, preferred_element_type=_F32)
    tk = vT_ref.shape[3]
    pad = vT_ref.shape[2] - vdim
    ones_rows = (lax.broadcasted_iota(jnp.int32, (pad, tk), 0) == 0).astype(_BF16)
    for hh in range(heads):
        for j in range(vT_ref.shape[1]):
            vT_ref[hh, j, 0:vdim, :] = vT[hh * vdim:(hh + 1) * vdim, j * tk:(j + 1) * tk].astype(_BF16)
            vT_ref[hh, j, vdim:vdim + pad, :] = ones_rows


def _mla_attn_kernel(qT_ref, k_ref, vT_ref, o_ref, m_scr, acc_scr,
                     s00, s01, s10, s11, cm00, cm01, cm10, cm11, p_a, p_b, al_a, al_b):
    tk, tq = s00.shape[0], acc_scr.shape[1]
    tsub = qT_ref.shape[2]
    nsub = tq // tsub
    n_q = qT_ref.shape[0] // nsub
    assert tq == 2 * tk and tk == vT_ref.shape[2]

    cb = _ATTN_COL_BLOCK

    def q_stage(qi, c, s_ref, cm_ref, col_lo=0):
        k = k_ref[pl.ds(pl.multiple_of(c * tk, tk), tk), :]
        for lo in range(col_lo, tq, cb):
            j, off = divmod(lo, tsub)
            qT = qT_ref[qi * nsub + j, :, off:off + cb]
            s = jnp.dot(k, qT, preferred_element_type=_F32)
            s_ref[:, lo:lo + cb] = s
            cm_ref[:, lo:lo + cb] = jnp.max(s, axis=0, keepdims=True)

    def x_stage(s_ref, cm_ref, p_ref, al_ref, mask_off=None, col_lo=0):
        for lo in range(col_lo, tq, cb):
            cols = slice(lo, lo + cb)
            live = tk
            if mask_off is None or mask_off + tk - 1 <= lo:
                s = s_ref[:, cols]
                cm = cm_ref[:, cols]
            else:
                live = min(tk, lo + cb - mask_off)
                s = s_ref[0:live, cols]
                krel = lax.broadcasted_iota(jnp.int32, s.shape, 0) + mask_off
                qrel = lax.broadcasted_iota(jnp.int32, s.shape, 1) + lo
                s = jnp.where(krel <= qrel, s, -jnp.inf)
                cm = jnp.max(s, axis=0, keepdims=True)
            m_prev = m_scr[:, cols]
            m_new = jnp.maximum(m_prev, cm)
            alpha = jnp.exp2(m_prev - m_new)
            p = jnp.exp2(s - m_new)
            p_ref[0:live, cols] = p.astype(_BF16)
            al_ref[:, cols] = alpha
            m_scr[:, cols] = m_new

    def v_stage(c, p_ref, al_ref, mask_off=None, col_lo=0):
        for lo in range(col_lo, tq, cb):
            cols = slice(lo, lo + cb)
            live = tk if mask_off is None else min(tk, lo + cb - mask_off)
            acc_scr[:, cols] = al_ref[:, cols] * acc_scr[:, cols] + jnp.dot(
                vT_ref[c, :, 0:live], p_ref[0:live, cols], preferred_element_type=_F32)

    s_bufs = (((s00, cm00), (s01, cm01)), ((s10, cm10), (s11, cm11)))

    def start_tile():
        m_scr[...] = jnp.full(m_scr.shape, -jnp.inf, _F32)
        acc_scr[...] = jnp.zeros(acc_scr.shape, _F32)


    def regular_step(qi, a, par, first=False):
        (sa, cma), (sb, cmb) = s_bufs[par]
        (na, ncma), (nb, ncmb) = s_bufs[1 - par]
        q_stage(qi, a + 2, na, ncma)
        x_stage(sa, cma, p_a, al_a)
        if not first:
            v_stage(a - 1, p_b, al_b)
        q_stage(qi, a + 3, nb, ncmb)
        x_stage(sb, cmb, p_b, al_b)
        v_stage(a, p_a, al_a)

    def diagonal_step(qi, par, first=False, last=False):
        a = 2 * qi
        (sa, cma), (sb, cmb) = s_bufs[par]
        (na, ncma), (nb, ncmb) = s_bufs[1 - par]
        if not last:
            q_stage(qi + 1, 0, na, ncma)
        x_stage(sa, cma, p_a, al_a, mask_off=0)
        if not first:
            v_stage(a - 1, p_b, al_b)
        if not last:
            q_stage(qi + 1, 1, nb, ncmb)
        x_stage(sb, cmb, p_b, al_b, mask_off=tk, col_lo=tk)
        v_stage(a, p_a, al_a, mask_off=0)
        v_stage(a + 1, p_b, al_b, mask_off=tk, col_lo=tk)
        vdim = o_ref.shape[1]
        inv_l = 1.0 / acc_scr[vdim:vdim + 1, :]
        o = (acc_scr[0:vdim, :] * inv_l).T
        o_ref[pl.ds(pl.multiple_of(qi * tq, tq), tq), :] = o.astype(o_ref.dtype)
        if not last:
            start_tile()

    def by_parity(par, fn):
        for static_par in range(2):
            @pl.when(par == static_par)
            def _():
                fn(static_par)

    def tile_body(qi, par, last=False):
        def step_body(u, par):
            by_parity(par, functools.partial(regular_step, qi, 2 * u))
            return 1 - par

        by_parity(par, functools.partial(regular_step, qi, 0, first=True))
        par = lax.fori_loop(1, qi, step_body, 1 - par)
        by_parity(par, functools.partial(diagonal_step, qi, last=last))
        return 1 - par

    start_tile()
    q_stage(0, 0, s00, cm00)
    q_stage(0, 1, s01, cm01)
    diagonal_step(jnp.int32(0), 0, first=True, last=(n_q == 1))
    if n_q > 1:
        par = lax.fori_loop(1, n_q - 1, tile_body, jnp.int32(1))
        tile_body(jnp.int32(n_q - 1), par, last=True)


def _outproj_mlp_kernel(o_ref, x_ref, wo_ref, g1_ref, b1_ref, wup_ref, wdown_ref,
                        g2_ref, b2_ref, out_ref, x1_scr, h_scr, *, alpha):
    def mix_rows(rows):
        return jnp.dot(o_ref[rows, :], wo_ref[...], preferred_element_type=_F32)

    _mix_mlp_ln(mix_rows, x_ref, alpha, g1_ref, b1_ref, wup_ref, wdown_ref, g2_ref, b2_ref,
                out_ref, x1_scr, h_scr)


def _swa_mlp_kernel(x_ref, wqT_ref, wk_ref, wvT_ref, wo_ref, bucketT_ref, relb_ref, sink_ref,
                    g1_ref, b1_ref, wup_ref, wdown_ref, g2_ref, b2_ref, out_ref,
                    k_scr, vTe_scr, qT_scr, biasT_scr, s_scr, p_scr, m_scr, oT_scr, x1_scr, h_scr,
                    *, alpha, q_heads, kv_heads, head_dim, q_scale):
    blk = _SWA_BLOCK
    group = q_heads // kv_heads
    tm = x_ref.shape[0]
    n_blk = tm // blk
    vrows = vTe_scr.shape[0] // kv_heads
    first_tile = pl.program_id(1) == 0
    always = pl.program_id(1) >= 0

    @pl.when((pl.program_id(0) == 0) & first_tile)
    def _():
        bkt = bucketT_ref[...]
        for hq in range(q_heads):
            bias = jnp.full(bkt.shape, -jnp.inf, _F32)
            for bb in range(_REL_BUCKETS):
                bias = jnp.where(bkt == bb, relb_ref[bb, hq] * _LOG2E, bias)
            g, gi = divmod(hq, group)
            biasT_scr[g, :, gi * blk:(gi + 1) * blk] = bias

    @pl.when(first_tile)
    def _():
        k_scr[0:blk, :] = jnp.zeros((blk, k_scr.shape[1]), k_scr.dtype)
        vTe_scr[:, 0:blk] = jnp.zeros((vTe_scr.shape[0], blk), vTe_scr.dtype)

    xb = x_ref[...].astype(_BF16)
    qT_scr[...] = (lax.dot_general(wqT_ref[...], xb, _NT, preferred_element_type=_F32)
                   * q_scale).astype(_BF16)
    k_scr[blk:blk + tm, :] = jnp.dot(xb, wk_ref[...], preferred_element_type=_F32).astype(_BF16)
    vT = lax.dot_general(wvT_ref[...], xb, _NT, preferred_element_type=_F32).astype(_BF16)
    ones_rows = (lax.broadcasted_iota(jnp.int32, (vrows - head_dim, tm), 0) == 0).astype(_BF16)
    for g in range(kv_heads):
        vTe_scr[g * vrows:g * vrows + head_dim, blk:blk + tm] = vT[g * head_dim:(g + 1) * head_dim]
        vTe_scr[g * vrows + head_dim:(g + 1) * vrows, blk:blk + tm] = ones_rows

    first_pen = jnp.where(first_tile, -jnp.inf, 0.0).astype(_F32)
    zeros_q = jnp.zeros((head_dim, group * blk), _BF16)
    heads_per_lane_tile = _LANES // head_dim
    units = [(n, g) for n in range(n_blk) for g in range(kv_heads)]

    def sink_row(g):
        return jnp.concatenate(
            [jnp.full((1, blk), sink_ref[g * group + gi] * _LOG2E, _F32) for gi in range(group)],
            axis=1)

    sinks = [None] * kv_heads

    def scores(u):
        n, g = units[u]
        lt = g // heads_per_lane_tile
        kpair = k_scr[n * blk:(n + 2) * blk, lt * _LANES:(lt + 1) * _LANES]
        qg = jnp.concatenate(
            [qT_scr[(g * group + gi) * head_dim:(g * group + gi + 1) * head_dim,
                    n * blk:(n + 1) * blk] for gi in range(group)], axis=1)
        pieces = [zeros_q] * heads_per_lane_tile
        pieces[g % heads_per_lane_tile] = qg
        qpad = jnp.concatenate(pieces, axis=0)
        s = jnp.dot(kpair, qpad, preferred_element_type=_F32) + biasT_scr[g]
        if n == 0:
            s = jnp.concatenate([s[:blk] + first_pen, s[blk:]], axis=0)
        s_scr[u] = s
        m_scr[u] = jnp.maximum(jnp.max(s, axis=0, keepdims=True), sinks[g])

    def exponentials(u):
        p_scr[u] = jnp.exp2(s_scr[u] - m_scr[u]).astype(_BF16)

    def weighted_values(u):
        n, g = units[u]
        oT = jnp.dot(vTe_scr[g * vrows:(g + 1) * vrows, n * blk:(n + 2) * blk], p_scr[u],
                     preferred_element_type=_F32)
        denom = oT[head_dim:head_dim + 1] + jnp.exp2(sinks[g] - m_scr[u])
        o = (oT[:head_dim] * (1.0 / denom)).astype(_BF16)
        for gi in range(group):
            hq = g * group + gi
            oT_scr[hq * head_dim:(hq + 1) * head_dim, n * blk:(n + 1) * blk] = (
                o[:, gi * blk:(gi + 1) * blk])

    for stage in (scores, exponentials, weighted_values):
        @pl.when(always)
        def _():
            for g in range(kv_heads):
                sinks[g] = sink_row(g)
            for u in range(len(units)):
                stage(u)

    @pl.when(pl.program_id(0) >= 0)
    def _():
        k_scr[0:blk, :] = k_scr[tm:tm + blk, :]
        vTe_scr[:, 0:blk] = vTe_scr[:, tm:tm + blk]

    def mix_rows(rows):
        return lax.dot_general(oT_scr[:, rows], wo_ref[...], _TN, preferred_element_type=_F32)

    _mix_mlp_ln(mix_rows, x_ref, alpha, g1_ref, b1_ref, wup_ref, wdown_ref, g2_ref, b2_ref,
                out_ref, x1_scr, h_scr)


def _t5_bucket_table(blk):
    i = np.arange(blk)[None, :]
    j = np.arange(2 * blk)[:, None]
    dist = i + blk - j
    max_exact = _REL_BUCKETS // 2
    nf = np.maximum(dist, 1).astype(np.float32)
    large = max_exact + (np.log(nf / np.float32(max_exact))
                         / np.float32(math.log(_REL_MAX_DIST / max_exact))
                         * np.float32(_REL_BUCKETS - max_exact)).astype(np.int32)
    large = np.minimum(large, _REL_BUCKETS - 1)
    bucket = np.where(dist < max_exact, np.maximum(dist, 0), large)
    valid = (dist >= 0) & (dist < blk)
    return np.where(valid, bucket, -1).astype(np.int32)


def _rope_tables(seq, rope):
    half = rope // 2
    inv = (np.float32(_ROPE_THETA) ** (-np.arange(half, dtype=np.float32) / np.float32(half)))
    ang = np.arange(seq, dtype=np.float32)[:, None] * inv.astype(np.float32)[None, :]
    cos = np.cos(ang.astype(np.float64)).astype(np.float32)
    sin = np.sin(ang.astype(np.float64)).astype(np.float32)
    cs = np.concatenate([cos, cos, -sin, sin], axis=1)
    return jnp.asarray(cs), jnp.asarray(cos.T.copy()), jnp.asarray(sin.T.copy())


def _row(v):
    return v.reshape(1, -1).astype(_F32)


def kernel(x, mla_w_in, mla_g_q, mla_g_kv, mla_w_uq, mla_w_uk, mla_w_uv, mla_w_o, kv_w_shared,
           swa_w_q, swa_sinks, swa_w_o, rel_bias, mlp_w_up, mlp_w_down, ln_mix_g, ln_mix_b,
           ln_mlp_g, ln_mlp_b):
    B, S, D = x.shape
    depth = mlp_w_up.shape[0]
    assert depth == 2 and mla_w_in.shape[0] == 1 and swa_w_q.shape[0] == 1
    alpha = (2 * depth) ** 0.25
    T = B * S
    tm = _TOKEN_TILE
    ta = _ATTN_TILE
    tk = ta // 2
    bf16_rows = 16
    ta_pad = ta + _LANES
    assert ta % tm == 0 and tm % tk == 0 and S % ta == 0

    q_rank, heads, hd = mla_w_uq.shape[1:]
    kv_rank, _, nope = mla_w_uk.shape[1:]
    vdim = mla_w_uv.shape[3]
    vrows = vdim + bf16_rows
    rope = hd - nope
    half = rope // 2
    assert 2 * rope == _LANES and mla_w_in.shape[2] == q_rank + kv_rank + rope

    w_in = mla_w_in[0]
    r0 = q_rank + kv_rank
    w_in_ext = jnp.concatenate(
        [w_in, w_in[:, r0 + half:r0 + rope], w_in[:, r0:r0 + half]], axis=1).astype(_BF16)
    wqT = mla_w_uq[0].reshape(q_rank, heads * hd).T.astype(_BF16)
    wuk = mla_w_uk[0].reshape(kv_rank, heads * nope).astype(_BF16)
    wvT = mla_w_uv[0].reshape(kv_rank, heads * vdim).T.astype(_BF16)
    cs, cosT, sinT = _rope_tables(S, rope)
    n_t = S // tm
    q_scale = hd ** -0.5 * _LOG2E

    cparams2 = pltpu.CompilerParams(dimension_semantics=("arbitrary", "arbitrary"),
                                    vmem_limit_bytes=_VMEM_LIMIT_BYTES)
    cparams1 = pltpu.CompilerParams(dimension_semantics=("arbitrary",),
                                    vmem_limit_bytes=_VMEM_LIMIT_BYTES)

    qT_all, k_all, vT_all = pl.pallas_call(
        functools.partial(_mla_proj_kernel, q_rank=q_rank, kv_rank=kv_rank, heads=heads,
                          nope=nope, rope=rope, q_scale=q_scale),
        grid=(B, n_t),
        in_specs=[
            pl.BlockSpec((None, tm, D), lambda b, i: (b, i, 0)),
            _const_spec(w_in_ext.shape),
            _const_spec((1, q_rank)),
            _const_spec((1, kv_rank)),
            _const_spec(wqT.shape),
            _const_spec(wuk.shape),
            _const_spec(wvT.shape),
            pl.BlockSpec((tm, 2 * rope), lambda b, i: (i, 0)),
            pl.BlockSpec((half, tm), lambda b, i: (0, i)),
            pl.BlockSpec((half, tm), lambda b, i: (0, i)),
        ],
        out_specs=[
            pl.BlockSpec((None, heads, None, hd, tm), lambda b, i: (b, 0, i, 0, 0)),
            pl.BlockSpec((None, heads, tm, hd), lambda b, i: (b, 0, i, 0)),
            pl.BlockSpec((None, heads, tm // tk, vrows, tk), lambda b, i: (b, 0, i, 0, 0)),
        ],
        out_shape=[
            jax.ShapeDtypeStruct((B, heads, n_t, hd, tm), _BF16),
            jax.ShapeDtypeStruct((B, heads, S, hd), _BF16),
            jax.ShapeDtypeStruct((B, heads, S // tk, vrows, tk), _BF16),
        ],
        compiler_params=cparams2,
        name="mla_proj",
    )(x, w_in_ext, _row(mla_g_q[0]), _row(mla_g_kv[0]), wqT, wuk, wvT, cs, cosT, sinT)

    o = pl.pallas_call(
        _mla_attn_kernel,
        grid=(B, heads),
        in_specs=[
            pl.BlockSpec((None, None, n_t, hd, tm), lambda b, h: (b, h, 0, 0, 0)),
            pl.BlockSpec((None, None, S, hd), lambda b, h: (b, h, 0, 0)),
            pl.BlockSpec((None, None, S // tk, vrows, tk), lambda b, h: (b, h, 0, 0, 0)),
        ],
        out_specs=pl.BlockSpec((None, S, vdim), lambda b, h: (b, 0, h)),
        out_shape=jax.ShapeDtypeStruct((B, S, heads * vdim), _BF16),
        scratch_shapes=[
            pltpu.VMEM((1, ta), _F32),
            pltpu.VMEM((vrows, ta), _F32),
        ] + [pltpu.VMEM((tk, ta_pad), _F32)] * 4
          + [pltpu.VMEM((1, ta), _F32)] * 4
          + [pltpu.VMEM((tk, ta), _BF16)] * 2
          + [pltpu.VMEM((1, ta), _F32)] * 2,
        compiler_params=cparams2,
        name="mla_attn",
    )(qT_all, k_all, vT_all)

    d_ff = mlp_w_up.shape[2]
    w_up_all = mlp_w_up.astype(_BF16)
    w_down_all = mlp_w_down.astype(_BF16)
    mlp_scratch = [
        pltpu.VMEM((tm, D), _F32),
        pltpu.VMEM((tm, d_ff), _BF16),
    ]
    x1 = pl.pallas_call(
        functools.partial(_outproj_mlp_kernel, alpha=alpha),
        grid=(T // tm,),
        in_specs=[
            pl.BlockSpec((tm, heads * vdim), lambda i: (i, 0)),
            pl.BlockSpec((tm, D), lambda i: (i, 0)),
            _const_spec((heads * vdim, D)),
            _const_spec((1, D)),
            _const_spec((1, D)),
            _layer_spec(w_up_all.shape, 0),
            _layer_spec(w_down_all.shape, 0),
            _const_spec((1, D)),
            _const_spec((1, D)),
        ],
        out_specs=pl.BlockSpec((tm, D), lambda i: (i, 0)),
        out_shape=jax.ShapeDtypeStruct((T, D), _F32),
        scratch_shapes=mlp_scratch,
        compiler_params=cparams1,
        name="mla_out_mlp",
    )(o.reshape(T, heads * vdim), x.reshape(T, D), mla_w_o[0].astype(_BF16),
      _row(ln_mix_g[0]), _row(ln_mix_b[0]), w_up_all, w_down_all,
      _row(ln_mlp_g[0]), _row(ln_mlp_b[0]))

    q_heads = swa_sinks.shape[1]
    s_hd = swa_w_q.shape[2] // q_heads
    kv_heads = kv_w_shared.shape[1] // (2 * s_hd)
    group = q_heads // kv_heads
    blk = _SWA_BLOCK
    assert _LANES % s_hd == 0 and tm % blk == 0
    wqT1 = swa_w_q[0].T.astype(_BF16)
    wk1 = kv_w_shared[:, :kv_heads * s_hd].astype(_BF16)
    wvT1 = kv_w_shared[:, kv_heads * s_hd:].T.astype(_BF16)
    bucketT = jnp.asarray(_t5_bucket_table(blk))
    n_units = (tm // blk) * kv_heads

    smem = pl.BlockSpec(memory_space=pltpu.SMEM)
    out = pl.pallas_call(
        functools.partial(_swa_mlp_kernel, alpha=alpha, q_heads=q_heads, kv_heads=kv_heads,
                          head_dim=s_hd, q_scale=s_hd ** -0.5 * _LOG2E),
        grid=(B, n_t),
        in_specs=[
            pl.BlockSpec((None, tm, D), lambda b, i: (b, i, 0)),
            _const_spec(wqT1.shape),
            _const_spec(wk1.shape),
            _const_spec(wvT1.shape),
            _const_spec((q_heads * s_hd, D)),
            _const_spec(bucketT.shape),
            smem,
            smem,
            _const_spec((1, D)),
            _const_spec((1, D)),
            _layer_spec(w_up_all.shape, 1),
            _layer_spec(w_down_all.shape, 1),
            _const_spec((1, D)),
            _const_spec((1, D)),
        ],
        out_specs=pl.BlockSpec((None, tm, D), lambda b, i: (b, i, 0)),
        out_shape=jax.ShapeDtypeStruct((B, S, D), _F32),
        scratch_shapes=[
            pltpu.VMEM((blk + tm, kv_heads * s_hd), _BF16),
            pltpu.VMEM((kv_heads * (s_hd + bf16_rows), blk + tm), _BF16),
            pltpu.VMEM((q_heads * s_hd, tm), _BF16),
            pltpu.VMEM((kv_heads, 2 * blk, group * blk), _F32),
            pltpu.VMEM((n_units, 2 * blk, group * blk), _F32),
            pltpu.VMEM((n_units, 2 * blk, group * blk), _BF16),
            pltpu.VMEM((n_units, 1, group * blk), _F32),
            pltpu.VMEM((q_heads * s_hd, tm), _BF16),
        ] + mlp_scratch,
        compiler_params=cparams2,
        name="swa_mlp",
    )(x1.reshape(B, S, D), wqT1, wk1, wvT1, swa_w_o[0].astype(_BF16), bucketT,
      rel_bias.astype(_F32), swa_sinks[0].astype(_F32),
      _row(ln_mix_g[1]), _row(ln_mix_b[1]), w_up_all, w_down_all,
      _row(ln_mlp_g[1]), _row(ln_mlp_b[1]))
    return out
```

```python
import functools
import math

import numpy as np
import jax
import jax.numpy as jnp
from jax import lax
from jax.experimental import pallas as pl
from jax.experimental.pallas import tpu as pltpu

_F32 = jnp.float32
_BF16 = jnp.bfloat16

_LN_EPS = 1e-5
_RMS_EPS = 1e-6
_ROPE_THETA = 10000.0
_SWA_BLOCK = 128
_REL_BUCKETS = 32
_REL_MAX_DIST = 128
_LOG2E = math.log2(math.e)

_V7X_VMEM_BYTES = 64 * 1024 * 1024
_VMEM_LIMIT_BYTES = _V7X_VMEM_BYTES - 8 * 1024 * 1024
_LANES = 128

_TOKEN_TILE = 512
_ATTN_TILE = 1024
_ATTN_COL_BLOCK = 256
_FF_CHUNK = 1024
_MLP_ROW_BLOCK = 256

_NT = (((1,), (1,)), ((), ()))
_TN = (((0,), (0,)), ((), ()))


def _const_spec(shape):
    nd = len(shape)
    return pl.BlockSpec(shape, lambda *_: (0,) * nd, pipeline_mode=pl.Buffered(1))


def _layer_spec(shape, layer):
    nd = len(shape) - 1
    return pl.BlockSpec((None,) + tuple(shape[1:]), lambda *_: (layer,) + (0,) * nd,
                        pipeline_mode=pl.Buffered(1))


def _layernorm(v, g, b):
    mu = jnp.mean(v, axis=-1, keepdims=True)
    d = v - mu
    var = jnp.mean(d * d, axis=-1, keepdims=True)
    return d * lax.rsqrt(var + _LN_EPS) * g + b


def _rmsnorm(v, g):
    return v * lax.rsqrt(jnp.mean(v * v, axis=-1, keepdims=True) + _RMS_EPS) * g


def _mix_mlp_ln(mix_rows, x_ref, alpha, g1_ref, b1_ref, wup_ref, wdown_ref, g2_ref, b2_ref,
                out_ref, x1_scr, h_scr):
    tm = x_ref.shape[0]
    d_ff = wup_ref.shape[1]
    blocks = [slice(r, r + _MLP_ROW_BLOCK) for r in range(0, tm, _MLP_ROW_BLOCK)]
    for rows in blocks:
        x1_scr[rows, :] = _layernorm(alpha * x_ref[rows, :] + mix_rows(rows), g1_ref[...], b1_ref[...])
    for rows in blocks:
        x1b = x1_scr[rows, :].astype(_BF16)
        for lo in range(0, d_ff, _FF_CHUNK):
            hmid = jnp.dot(x1b, wup_ref[:, lo:lo + _FF_CHUNK], preferred_element_type=_F32)
            hmid = jnp.maximum(hmid, 0.0)
            h_scr[rows, lo:lo + _FF_CHUNK] = (hmid * hmid).astype(_BF16)
    for rows in blocks:
        y = jnp.dot(h_scr[rows, :], wdown_ref[...], preferred_element_type=_F32)
        out_ref[rows, :] = _layernorm(alpha * x1_scr[rows, :] + y, g2_ref[...], b2_ref[...])


def _mla_proj_kernel(x_ref, win_ref, gq_ref, gkv_ref, wqT_ref, wuk_ref, wvT_ref,
                     cs_ref, cosT_ref, sinT_ref, qT_ref, k_ref, vT_ref,
                     *, q_rank, kv_rank, heads, nope, rope, q_scale):
    xb = x_ref[...].astype(_BF16)
    h = jnp.dot(xb, win_ref[...], preferred_element_type=_F32)
    cq = _rmsnorm(h[:, :q_rank], gq_ref[...]).astype(_BF16)
    ckv = _rmsnorm(h[:, q_rank:q_rank + kv_rank], gkv_ref[...]).astype(_BF16)
    t = h[:, q_rank + kv_rank:] * cs_ref[...]
    kr = (t + pltpu.roll(t, rope, 1))[:, :rope].astype(_BF16)

    half = rope // 2
    hd = nope + rope
    qT = lax.dot_general(wqT_ref[...], cq, _NT, preferred_element_type=_F32)
    cosT = cosT_ref[...]
    sinT = sinT_ref[...]
    for hh in range(heads):
        base = hh * hd
        qT_ref[hh, 0:nope, :] = (qT[base:base + nope] * q_scale).astype(_BF16)
        x1 = qT[base + nope:base + nope + half]
        x2 = qT[base + nope + half:base + hd]
        qT_ref[hh, nope:nope + half, :] = ((x1 * cosT - x2 * sinT) * q_scale).astype(_BF16)
        qT_ref[hh, nope + half:hd, :] = ((x1 * sinT + x2 * cosT) * q_scale).astype(_BF16)

    kn = jnp.dot(ckv, wuk_ref[...], preferred_element_type=_F32)
    for hh in range(heads):
        k_ref[hh, :, 0:nope] = kn[:, hh * nope:(hh + 1) * nope].astype(_BF16)
        k_ref[hh, :, nope:hd] = kr

    vdim = wvT_ref.shape[0] // heads
    vT = lax.dot_general(wvT_ref[...], ckv, _NT, preferred_element_type=_F32)
    tk = vT_ref.shape[3]
    pad = vT_ref.shape[2] - vdim
    ones_rows = (lax.broadcasted_iota(jnp.int32, (pad, tk), 0) == 0).astype(_BF16)
    for hh in range(heads):
        for j in range(vT_ref.shape[1]):
            vT_ref[hh, j, 0:vdim, :] = vT[hh * vdim:(hh + 1) * vdim, j * tk:(j + 1) * tk].astype(_BF16)
            vT_ref[hh, j, vdim:vdim + pad, :] = ones_rows


def _mla_attn_kernel(qT_ref, k_ref, vT_ref, o_ref, m_scr, acc_scr,
                     s00, s01, s10, s11, cm00, cm01, cm10, cm11, p_a, p_b, al_a, al_b):
    tk, tq = s00.shape[0], acc_scr.shape[1]
    tsub = qT_ref.shape[2]
    nsub = tq // tsub
    n_q = qT_ref.shape[0] // nsub
    assert tq == 2 * tk and tk == vT_ref.shape[2]

    cb = _ATTN_COL_BLOCK


    def aligned(v, m):
        return v if isinstance(v, int) else pl.multiple_of(v, m)

    def live_rows(mask_off, lo):
        return tk if mask_off is None else min(tk, lo + cb - mask_off)

    def q_stage(qi, c, s_ref, cm_ref, mask_off=None, col_lo=0):
        k = k_ref[pl.ds(aligned(c * tk, tk), tk), :]
        for lo in range(col_lo, tq, cb):
            live = live_rows(mask_off, lo)
            j, off = divmod(lo, tsub)
            qT = qT_ref[qi * nsub + j, :, off:off + cb]
            s = jnp.dot(k[0:live], qT, preferred_element_type=_F32)
            s_ref[0:live, lo:lo + cb] = s
            cm_ref[:, lo:lo + cb] = jnp.max(s, axis=0, keepdims=True)

    def x_stage(s_ref, cm_ref, p_ref, al_ref, mask_off=None, col_lo=0):
        for lo in range(col_lo, tq, cb):
            cols = slice(lo, lo + cb)
            live = live_rows(mask_off, lo)
            s = s_ref[0:live, cols]
            if mask_off is None or mask_off + tk - 1 <= lo:
                cm = cm_ref[:, cols]
            else:
                krel = lax.broadcasted_iota(jnp.int32, s.shape, 0) + mask_off
                qrel = lax.broadcasted_iota(jnp.int32, s.shape, 1) + lo
                s = jnp.where(krel <= qrel, s, -jnp.inf)
                cm = jnp.max(s, axis=0, keepdims=True)
            m_prev = m_scr[:, cols]
            m_new = jnp.maximum(m_prev, cm)
            alpha = jnp.exp2(m_prev - m_new)
            p = jnp.exp2(s - m_new)
            p_ref[0:live, cols] = p.astype(_BF16)
            al_ref[:, cols] = alpha
            m_scr[:, cols] = m_new

    def v_stage(c, p_ref, al_ref, mask_off=None, col_lo=0):
        for lo in range(col_lo, tq, cb):
            cols = slice(lo, lo + cb)
            live = live_rows(mask_off, lo)
            acc_scr[:, cols] = al_ref[:, cols] * acc_scr[:, cols] + jnp.dot(
                vT_ref[c, :, 0:live], p_ref[0:live, cols], preferred_element_type=_F32)

    s_bufs = (((s00, cm00), (s01, cm01)), ((s10, cm10), (s11, cm11)))

    def start_tile():
        m_scr[...] = jnp.full(m_scr.shape, -jnp.inf, _F32)
        acc_scr[...] = jnp.zeros(acc_scr.shape, _F32)


    def regular_step(qi, a, par, first=False, pre_diag=False):
        (sa, cma), (sb, cmb) = s_bufs[par]
        (na, ncma), (nb, ncmb) = s_bufs[1 - par]
        q_stage(qi, a + 2, na, ncma, mask_off=0 if pre_diag else None)
        x_stage(sa, cma, p_a, al_a)
        if not first:
            v_stage(a - 1, p_b, al_b)
        q_stage(qi, a + 3, nb, ncmb, mask_off=tk if pre_diag else None,
                col_lo=tk if pre_diag else 0)
        x_stage(sb, cmb, p_b, al_b)
        v_stage(a, p_a, al_a)

    def diagonal_step(qi, par, first=False, last=False):
        a = 2 * qi
        (sa, cma), (sb, cmb) = s_bufs[par]
        (na, ncma), (nb, ncmb) = s_bufs[1 - par]
        if not last:
            q_stage(qi + 1, 0, na, ncma)
        x_stage(sa, cma, p_a, al_a, mask_off=0)
        if not first:
            v_stage(a - 1, p_b, al_b)
        if not last:
            q_stage(qi + 1, 1, nb, ncmb)
        x_stage(sb, cmb, p_b, al_b, mask_off=tk, col_lo=tk)
        v_stage(a, p_a, al_a, mask_off=0)
        v_stage(a + 1, p_b, al_b, mask_off=tk, col_lo=tk)
        vdim = o_ref.shape[1]
        inv_l = 1.0 / acc_scr[vdim:vdim + 1, :]
        o = (acc_scr[0:vdim, :] * inv_l).T
        o_ref[pl.ds(aligned(qi * tq, tq), tq), :] = o.astype(o_ref.dtype)
        if not last:
            start_tile()

    def by_parity(par, fn):
        if isinstance(par, int):
            fn(par)
            return
        for static_par in range(2):
            @pl.when(par == static_par)
            def _():
                fn(static_par)

    def start_parity(qi):
        return (qi * (qi + 1) // 2) % 2

    def tile(qi, last=False):
        par = start_parity(qi)
        if isinstance(qi, int) and qi == 1:
            by_parity(par, functools.partial(regular_step, qi, 0, first=True, pre_diag=True))
        else:
            def step_body(u, p):
                by_parity(p, functools.partial(regular_step, qi, 2 * u))
                return 1 - p

            by_parity(par, functools.partial(regular_step, qi, 0, first=True))
            lax.fori_loop(1, qi - 1, step_body, jnp.int32(1) - par)
            by_parity((par + qi - 1) % 2,
                      functools.partial(regular_step, qi, 2 * (qi - 1), pre_diag=True))
        by_parity((par + qi) % 2, functools.partial(diagonal_step, qi, last=last))

    def tile_body(qi, carry):
        tile(qi)
        return carry

    start_tile()
    q_stage(0, 0, s00, cm00, mask_off=0)
    q_stage(0, 1, s01, cm01, mask_off=tk, col_lo=tk)
    diagonal_step(0, 0, first=True, last=(n_q == 1))
    if n_q > 2:
        tile(1)
    if n_q > 3:
        lax.fori_loop(2, n_q - 1, tile_body, jnp.int32(0))
    if n_q > 1:
        tile(n_q - 1, last=True)


def _outproj_mlp_kernel(o_ref, x_ref, wo_ref, g1_ref, b1_ref, wup_ref, wdown_ref,
                        g2_ref, b2_ref, out_ref, x1_scr, h_scr, *, alpha):
    def mix_rows(rows):
        return jnp.dot(o_ref[rows, :], wo_ref[...], preferred_element_type=_F32)

    _mix_mlp_ln(mix_rows, x_ref, alpha, g1_ref, b1_ref, wup_ref, wdown_ref, g2_ref, b2_ref,
                out_ref, x1_scr, h_scr)


def _swa_mlp_kernel(x_ref, wqT_ref, wk_ref, wvT_ref, wo_ref, bucketT_ref, relb_ref, sink_ref,
                    g1_ref, b1_ref, wup_ref, wdown_ref, g2_ref, b2_ref, out_ref,
                    k_scr, vTe_scr, qT_scr, biasT_scr, s_scr, p_scr, m_scr, oT_scr, x1_scr, h_scr,
                    *, alpha, q_heads, kv_heads, head_dim, q_scale):
    blk = _SWA_BLOCK
    group = q_heads // kv_heads
    tm = x_ref.shape[0]
    n_blk = tm // blk
    vrows = vTe_scr.shape[0] // kv_heads
    first_tile = pl.program_id(1) == 0
    always = pl.program_id(1) >= 0

    @pl.when((pl.program_id(0) == 0) & first_tile)
    def _():
        bkt = bucketT_ref[...]
        for hq in range(q_heads):
            bias = jnp.full(bkt.shape, -jnp.inf, _F32)
            for bb in range(_REL_BUCKETS):
                bias = jnp.where(bkt == bb, relb_ref[bb, hq] * _LOG2E, bias)
            g, gi = divmod(hq, group)
            biasT_scr[g, :, gi * blk:(gi + 1) * blk] = bias

    @pl.when(first_tile)
    def _():
        k_scr[0:blk, :] = jnp.zeros((blk, k_scr.shape[1]), k_scr.dtype)
        vTe_scr[:, 0:blk] = jnp.zeros((vTe_scr.shape[0], blk), vTe_scr.dtype)

    xb = x_ref[...].astype(_BF16)
    qT_scr[...] = (lax.dot_general(wqT_ref[...], xb, _NT, preferred_element_type=_F32)
                   * q_scale).astype(_BF16)
    k_scr[blk:blk + tm, :] = jnp.dot(xb, wk_ref[...], preferred_element_type=_F32).astype(_BF16)
    vT = lax.dot_general(wvT_ref[...], xb, _NT, preferred_element_type=_F32).astype(_BF16)
    ones_rows = (lax.broadcasted_iota(jnp.int32, (vrows - head_dim, tm), 0) == 0).astype(_BF16)
    for g in range(kv_heads):
        vTe_scr[g * vrows:g * vrows + head_dim, blk:blk + tm] = vT[g * head_dim:(g + 1) * head_dim]
        vTe_scr[g * vrows + head_dim:(g + 1) * vrows, blk:blk + tm] = ones_rows

    first_pen = jnp.where(first_tile, -jnp.inf, 0.0).astype(_F32)
    zeros_q = jnp.zeros((head_dim, group * blk), _BF16)
    heads_per_lane_tile = _LANES // head_dim
    units = [(n, g) for n in range(n_blk) for g in range(kv_heads)]

    def sink_row(g):
        return jnp.concatenate(
            [jnp.full((1, blk), sink_ref[g * group + gi] * _LOG2E, _F32) for gi in range(group)],
            axis=1)

    sinks = [None] * kv_heads

    def scores(u):
        n, g = units[u]
        lt = g // heads_per_lane_tile
        kpair = k_scr[n * blk:(n + 2) * blk, lt * _LANES:(lt + 1) * _LANES]
        qg = jnp.concatenate(
            [qT_scr[(g * group + gi) * head_dim:(g * group + gi + 1) * head_dim,
                    n * blk:(n + 1) * blk] for gi in range(group)], axis=1)
        pieces = [zeros_q] * heads_per_lane_tile
        pieces[g % heads_per_lane_tile] = qg
        qpad = jnp.concatenate(pieces, axis=0)
        s = jnp.dot(kpair, qpad, preferred_element_type=_F32) + biasT_scr[g]
        if n == 0:
            s = jnp.concatenate([s[:blk] + first_pen, s[blk:]], axis=0)
        s_scr[u] = s
        m_scr[u] = jnp.maximum(jnp.max(s, axis=0, keepdims=True), sinks[g])

    def exponentials(u):
        p_scr[u] = jnp.exp2(s_scr[u] - m_scr[u]).astype(_BF16)

    def weighted_values(u):
        n, g = units[u]
        oT = jnp.dot(vTe_scr[g * vrows:(g + 1) * vrows, n * blk:(n + 2) * blk], p_scr[u],
                     preferred_element_type=_F32)
        denom = oT[head_dim:head_dim + 1] + jnp.exp2(sinks[g] - m_scr[u])
        o = (oT[:head_dim] * (1.0 / denom)).astype(_BF16)
        for gi in range(group):
            hq = g * group + gi
            oT_scr[hq * head_dim:(hq + 1) * head_dim, n * blk:(n + 1) * blk] = (
                o[:, gi * blk:(gi + 1) * blk])

    for stage in (scores, exponentials, weighted_values):
        @pl.when(always)
        def _():
            for g in range(kv_heads):
                sinks[g] = sink_row(g)
            for u in range(len(units)):
                stage(u)

    @pl.when(pl.program_id(0) >= 0)
    def _():
        k_scr[0:blk, :] = k_scr[tm:tm + blk, :]
        vTe_scr[:, 0:blk] = vTe_scr[:, tm:tm + blk]

    def mix_rows(rows):
        return lax.dot_general(oT_scr[:, rows], wo_ref[...], _TN, preferred_element_type=_F32)

    _mix_mlp_ln(mix_rows, x_ref, alpha, g1_ref, b1_ref, wup_ref, wdown_ref, g2_ref, b2_ref,
                out_ref, x1_scr, h_scr)


def _t5_bucket_table(blk):
    i = np.arange(blk)[None, :]
    j = np.arange(2 * blk)[:, None]
    dist = i + blk - j
    max_exact = _REL_BUCKETS // 2
    nf = np.maximum(dist, 1).astype(np.float32)
    large = max_exact + (np.log(nf / np.float32(max_exact))
                         / np.float32(math.log(_REL_MAX_DIST / max_exact))
                         * np.float32(_REL_BUCKETS - max_exact)).astype(np.int32)
    large = np.minimum(large, _REL_BUCKETS - 1)
    bucket = np.where(dist < max_exact, np.maximum(dist, 0), large)
    valid = (dist >= 0) & (dist < blk)
    return np.where(valid, bucket, -1).astype(np.int32)


def _rope_tables(seq, rope):
    half = rope // 2
    inv = (np.float32(_ROPE_THETA) ** (-np.arange(half, dtype=np.float32) / np.float32(half)))
    ang = np.arange(seq, dtype=np.float32)[:, None] * inv.astype(np.float32)[None, :]
    cos = np.cos(ang.astype(np.float64)).astype(np.float32)
    sin = np.sin(ang.astype(np.float64)).astype(np.float32)
    cs = np.concatenate([cos, cos, -sin, sin], axis=1)
    return jnp.asarray(cs), jnp.asarray(cos.T.copy()), jnp.asarray(sin.T.copy())


def _row(v):
    return v.reshape(1, -1).astype(_F32)


def kernel(x, mla_w_in, mla_g_q, mla_g_kv, mla_w_uq, mla_w_uk, mla_w_uv, mla_w_o, kv_w_shared,
           swa_w_q, swa_sinks, swa_w_o, rel_bias, mlp_w_up, mlp_w_down, ln_mix_g, ln_mix_b,
           ln_mlp_g, ln_mlp_b):
    B, S, D = x.shape
    depth = mlp_w_up.shape[0]
    assert depth == 2 and mla_w_in.shape[0] == 1 and swa_w_q.shape[0] == 1
    alpha = (2 * depth) ** 0.25
    T = B * S
    tm = _TOKEN_TILE
    ta = _ATTN_TILE
    tk = ta // 2
    bf16_rows = 16
    ta_pad = ta + _LANES
    assert ta % tm == 0 and tm % tk == 0 and S % ta == 0

    q_rank, heads, hd = mla_w_uq.shape[1:]
    kv_rank, _, nope = mla_w_uk.shape[1:]
    vdim = mla_w_uv.shape[3]
    vrows = vdim + bf16_rows
    rope = hd - nope
    half = rope // 2
    assert 2 * rope == _LANES and mla_w_in.shape[2] == q_rank + kv_rank + rope

    w_in = mla_w_in[0]
    r0 = q_rank + kv_rank
    w_in_ext = jnp.concatenate(
        [w_in, w_in[:, r0 + half:r0 + rope], w_in[:, r0:r0 + half]], axis=1).astype(_BF16)
    wqT = mla_w_uq[0].reshape(q_rank, heads * hd).T.astype(_BF16)
    wuk = mla_w_uk[0].reshape(kv_rank, heads * nope).astype(_BF16)
    wvT = mla_w_uv[0].reshape(kv_rank, heads * vdim).T.astype(_BF16)
    cs, cosT, sinT = _rope_tables(S, rope)
    n_t = S // tm
    q_scale = hd ** -0.5 * _LOG2E

    cparams2 = pltpu.CompilerParams(dimension_semantics=("arbitrary", "arbitrary"),
                                    vmem_limit_bytes=_VMEM_LIMIT_BYTES)
    cparams1 = pltpu.CompilerParams(dimension_semantics=("arbitrary",),
                                    vmem_limit_bytes=_VMEM_LIMIT_BYTES)

    qT_all, k_all, vT_all = pl.pallas_call(
        functools.partial(_mla_proj_kernel, q_rank=q_rank, kv_rank=kv_rank, heads=heads,
                          nope=nope, rope=rope, q_scale=q_scale),
        grid=(B, n_t),
        in_specs=[
            pl.BlockSpec((None, tm, D), lambda b, i: (b, i, 0)),
            _const_spec(w_in_ext.shape),
            _const_spec((1, q_rank)),
            _const_spec((1, kv_rank)),
            _const_spec(wqT.shape),
            _const_spec(wuk.shape),
            _const_spec(wvT.shape),
            pl.BlockSpec((tm, 2 * rope), lambda b, i: (i, 0)),
            pl.BlockSpec((half, tm), lambda b, i: (0, i)),
            pl.BlockSpec((half, tm), lambda b, i: (0, i)),
        ],
        out_specs=[
            pl.BlockSpec((None, heads, None, hd, tm), lambda b, i: (b, 0, i, 0, 0)),
            pl.BlockSpec((None, heads, tm, hd), lambda b, i: (b, 0, i, 0)),
            pl.BlockSpec((None, heads, tm // tk, vrows, tk), lambda b, i: (b, 0, i, 0, 0)),
        ],
        out_shape=[
            jax.ShapeDtypeStruct((B, heads, n_t, hd, tm), _BF16),
            jax.ShapeDtypeStruct((B, heads, S, hd), _BF16),
            jax.ShapeDtypeStruct((B, heads, S // tk, vrows, tk), _BF16),
        ],
        compiler_params=cparams2,
        name="mla_proj",
    )(x, w_in_ext, _row(mla_g_q[0]), _row(mla_g_kv[0]), wqT, wuk, wvT, cs, cosT, sinT)

    o = pl.pallas_call(
        _mla_attn_kernel,
        grid=(B, heads),
        in_specs=[
            pl.BlockSpec((None, None, n_t, hd, tm), lambda b, h: (b, h, 0, 0, 0)),
            pl.BlockSpec((None, None, S, hd), lambda b, h: (b, h, 0, 0)),
            pl.BlockSpec((None, None, S // tk, vrows, tk), lambda b, h: (b, h, 0, 0, 0)),
        ],
        out_specs=pl.BlockSpec((None, S, vdim), lambda b, h: (b, 0, h)),
        out_shape=jax.ShapeDtypeStruct((B, S, heads * vdim), _BF16),
        scratch_shapes=[
            pltpu.VMEM((1, ta), _F32),
            pltpu.VMEM((vrows, ta), _F32),
        ] + [pltpu.VMEM((tk, ta_pad), _F32)] * 4
          + [pltpu.VMEM((1, ta), _F32)] * 4
          + [pltpu.VMEM((tk, ta), _BF16)] * 2
          + [pltpu.VMEM((1, ta), _F32)] * 2,
        compiler_params=cparams2,
        name="mla_attn",
    )(qT_all, k_all, vT_all)

    d_ff = mlp_w_up.shape[2]
    w_up_all = mlp_w_up.astype(_BF16)
    w_down_all = mlp_w_down.astype(_BF16)
    mlp_scratch = [
        pltpu.VMEM((tm, D), _F32),
        pltpu.VMEM((tm, d_ff), _BF16),
    ]
    x1 = pl.pallas_call(
        functools.partial(_outproj_mlp_kernel, alpha=alpha),
        grid=(T // tm,),
        in_specs=[
            pl.BlockSpec((tm, heads * vdim), lambda i: (i, 0)),
            pl.BlockSpec((tm, D), lambda i: (i, 0)),
            _const_spec((heads * vdim, D)),
            _const_spec((1, D)),
            _const_spec((1, D)),
            _layer_spec(w_up_all.shape, 0),
            _layer_spec(w_down_all.shape, 0),
            _const_spec((1, D)),
            _const_spec((1, D)),
        ],
        out_specs=pl.BlockSpec((tm, D), lambda i: (i, 0)),
        out_shape=jax.ShapeDtypeStruct((T, D), _F32),
        scratch_shapes=mlp_scratch,
        compiler_params=cparams1,
        name="mla_out_mlp",
    )(o.reshape(T, heads * vdim), x.reshape(T, D), mla_w_o[0].astype(_BF16),
      _row(ln_mix_g[0]), _row(ln_mix_b[0]), w_up_all, w_down_all,
      _row(ln_mlp_g[0]), _row(ln_mlp_b[0]))

    q_heads = swa_sinks.shape[1]
    s_hd = swa_w_q.shape[2] // q_heads
    kv_heads = kv_w_shared.shape[1] // (2 * s_hd)
    group = q_heads // kv_heads
    blk = _SWA_BLOCK
    assert _LANES % s_hd == 0 and tm % blk == 0
    wqT1 = swa_w_q[0].T.astype(_BF16)
    wk1 = kv_w_shared[:, :kv_heads * s_hd].astype(_BF16)
    wvT1 = kv_w_shared[:, kv_heads * s_hd:].T.astype(_BF16)
    bucketT = jnp.asarray(_t5_bucket_table(blk))
    n_units = (tm // blk) * kv_heads

    smem = pl.BlockSpec(memory_space=pltpu.SMEM)
    out = pl.pallas_call(
        functools.partial(_swa_mlp_kernel, alpha=alpha, q_heads=q_heads, kv_heads=kv_heads,
                          head_dim=s_hd, q_scale=s_hd ** -0.5 * _LOG2E),
        grid=(B, n_t),
        in_specs=[
            pl.BlockSpec((None, tm, D), lambda b, i: (b, i, 0)),
            _const_spec(wqT1.shape),
            _const_spec(wk1.shape),
            _const_spec(wvT1.shape),
            _const_spec((q_heads * s_hd, D)),
            _const_spec(bucketT.shape),
            smem,
            smem,
            _const_spec((1, D)),
            _const_spec((1, D)),
            _layer_spec(w_up_all.shape, 1),
            _layer_spec(w_down_all.shape, 1),
            _const_spec((1, D)),
            _const_spec((1, D)),
        ],
        out_specs=pl.BlockSpec((None, tm, D), lambda b, i: (b, i, 0)),
        out_shape=jax.ShapeDtypeStruct((B, S, D), _F32),
        scratch_shapes=[
            pltpu.VMEM((blk + tm, kv_heads * s_hd), _BF16),
            pltpu.VMEM((kv_heads * (s_hd + bf16_rows), blk + tm), _BF16),
            pltpu.VMEM((q_heads * s_hd, tm), _BF16),
            pltpu.VMEM((kv_heads, 2 * blk, group * blk), _F32),
            pltpu.VMEM((n_units, 2 * blk, group * blk), _F32),
            pltpu.VMEM((n_units, 2 * blk, group * blk), _BF16),
            pltpu.VMEM((n_units, 1, group * blk), _F32),
            pltpu.VMEM((q_heads * s_hd, tm), _BF16),
        ] + mlp_scratch,
        compiler_params=cparams2,
        name="swa_mlp",
    )(x1.reshape(B, S, D), wqT1, wk1, wvT1, swa_w_o[0].astype(_BF16), bucketT,
      rel_bias.astype(_F32), swa_sinks[0].astype(_F32),
      _row(ln_mix_g[1]), _row(ln_mix_b[1]), w_up_all, w_down_all,
      _row(ln_mlp_g[1]), _row(ln_mlp_b[1]))
    return out
```

```python
import functools
import math

import numpy as np
import jax
import jax.numpy as jnp
from jax import lax
from jax.experimental import pallas as pl
from jax.experimental.pallas import tpu as pltpu

_F32 = jnp.float32
_BF16 = jnp.bfloat16

_LN_EPS = 1e-5
_RMS_EPS = 1e-6
_ROPE_THETA = 10000.0
_SWA_BLOCK = 128
_REL_BUCKETS = 32
_REL_MAX_DIST = 128
_LOG2E = math.log2(math.e)

_V7X_VMEM_BYTES = 64 * 1024 * 1024
_VMEM_LIMIT_BYTES = _V7X_VMEM_BYTES - 8 * 1024 * 1024
_LANES = 128

_TOKEN_TILE = 512
_ATTN_TILE = 1024
_ATTN_COL_BLOCK = 256
_FF_CHUNK = 1024
_MLP_ROW_BLOCK = 256

_NT = (((1,), (1,)), ((), ()))
_TN = (((0,), (0,)), ((), ()))


def _const_spec(shape):
    nd = len(shape)
    return pl.BlockSpec(shape, lambda *_: (0,) * nd, pipeline_mode=pl.Buffered(1))


def _layer_spec(shape, layer):
    nd = len(shape) - 1
    return pl.BlockSpec((None,) + tuple(shape[1:]), lambda *_: (layer,) + (0,) * nd,
                        pipeline_mode=pl.Buffered(1))


def _layernorm(v, g, b):
    mu = jnp.mean(v, axis=-1, keepdims=True)
    d = v - mu
    var = jnp.mean(d * d, axis=-1, keepdims=True)
    return d * lax.rsqrt(var + _LN_EPS) * g + b


def _rmsnorm(v, g):
    return v * lax.rsqrt(jnp.mean(v * v, axis=-1, keepdims=True) + _RMS_EPS) * g


def _mix_mlp_ln(mix_rows, x_ref, alpha, g1_ref, b1_ref, wup_ref, wdown_ref, g2_ref, b2_ref,
                out_ref, x1_scr, h_scr):
    tm = x_ref.shape[0]
    d_ff = wup_ref.shape[1]
    blocks = [slice(r, r + _MLP_ROW_BLOCK) for r in range(0, tm, _MLP_ROW_BLOCK)]
    for rows in blocks:
        x1_scr[rows, :] = _layernorm(alpha * x_ref[rows, :] + mix_rows(rows), g1_ref[...], b1_ref[...])
    for rows in blocks:
        x1b = x1_scr[rows, :].astype(_BF16)
        for lo in range(0, d_ff, _FF_CHUNK):
            hmid = jnp.dot(x1b, wup_ref[:, lo:lo + _FF_CHUNK], preferred_element_type=_F32)
            hmid = jnp.maximum(hmid, 0.0)
            h_scr[rows, lo:lo + _FF_CHUNK] = (hmid * hmid).astype(_BF16)
    for rows in blocks:
        y = jnp.dot(h_scr[rows, :], wdown_ref[...], preferred_element_type=_F32)
        out_ref[rows, :] = _layernorm(alpha * x1_scr[rows, :] + y, g2_ref[...], b2_ref[...])


def _mla_proj_kernel(x_ref, win_ref, gq_ref, gkv_ref, wqT_ref, wuk_ref, wvT_ref,
                     cs_ref, cosT_ref, sinT_ref, qT_ref, k_ref, vT_ref,
                     *, q_rank, kv_rank, heads, nope, rope, q_scale):
    xb = x_ref[...].astype(_BF16)
    h = jnp.dot(xb, win_ref[...], preferred_element_type=_F32)
    cq = _rmsnorm(h[:, :q_rank], gq_ref[...]).astype(_BF16)
    ckv = _rmsnorm(h[:, q_rank:q_rank + kv_rank], gkv_ref[...]).astype(_BF16)
    t = h[:, q_rank + kv_rank:] * cs_ref[...]
    kr = (t + pltpu.roll(t, rope, 1))[:, :rope].astype(_BF16)

    half = rope // 2
    hd = nope + rope
    qT = lax.dot_general(wqT_ref[...], cq, _NT, preferred_element_type=_F32)
    cosT = cosT_ref[...]
    sinT = sinT_ref[...]
    for hh in range(heads):
        base = hh * hd
        qT_ref[hh, 0:nope, :] = (qT[base:base + nope] * q_scale).astype(_BF16)
        x1 = qT[base + nope:base + nope + half]
        x2 = qT[base + nope + half:base + hd]
        qT_ref[hh, nope:nope + half, :] = ((x1 * cosT - x2 * sinT) * q_scale).astype(_BF16)
        qT_ref[hh, nope + half:hd, :] = ((x1 * sinT + x2 * cosT) * q_scale).astype(_BF16)

    kn = jnp.dot(ckv, wuk_ref[...], preferred_element_type=_F32)
    for hh in range(heads):
        k_ref[hh, :, 0:nope] = kn[:, hh * nope:(hh + 1) * nope].astype(_BF16)
        k_ref[hh, :, nope:hd] = kr

    vdim = wvT_ref.shape[0] // heads
    vT = lax.dot_general(wvT_ref[...], ckv, _NT, preferred_element_type=_F32)
    tk = vT_ref.shape[3]
    pad = vT_ref.shape[2] - vdim
    ones_rows = (lax.broadcasted_iota(jnp.int32, (pad, tk), 0) == 0).astype(_BF16)
    for hh in range(heads):
        for j in range(vT_ref.shape[1]):
            vT_ref[hh, j, 0:vdim, :] = vT[hh * vdim:(hh + 1) * vdim, j * tk:(j + 1) * tk].astype(_BF16)
            vT_ref[hh, j, vdim:vdim + pad, :] = ones_rows


def _mla_attn_kernel(qT_ref, k_ref, vT_ref, o_ref, m_scr, acc_scr,
                     s00, s01, s10, s11, cm00, cm01, cm10, cm11, p_a, p_b, al_a, al_b):
    tk, tq = s00.shape[0], acc_scr.shape[1]
    tsub = qT_ref.shape[2]
    nsub = tq // tsub
    n_q = qT_ref.shape[0] // nsub
    assert tq == 2 * tk and tk == vT_ref.shape[2]

    cb = _ATTN_COL_BLOCK


    def aligned(v, m):
        return v if isinstance(v, int) else pl.multiple_of(v, m)

    def live_rows(mask_off, lo):
        return tk if mask_off is None else min(tk, lo + cb - mask_off)

    def q_stage(qi, c, s_ref, cm_ref, mask_off=None, col_lo=0):
        k = k_ref[pl.ds(aligned(c * tk, tk), tk), :]
        for lo in range(col_lo, tq, cb):
            live = live_rows(mask_off, lo)
            j, off = divmod(lo, tsub)
            qT = qT_ref[qi * nsub + j, :, off:off + cb]
            s = jnp.dot(k[0:live], qT, preferred_element_type=_F32)
            s_ref[0:live, lo:lo + cb] = s
            cm_ref[:, lo:lo + cb] = jnp.max(s, axis=0, keepdims=True)

    def x_stage(s_ref, cm_ref, p_ref, al_ref, mask_off=None, col_lo=0):
        for lo in range(col_lo, tq, cb):
            cols = slice(lo, lo + cb)
            live = live_rows(mask_off, lo)
            s = s_ref[0:live, cols]
            if mask_off is None or mask_off + tk - 1 <= lo:
                cm = cm_ref[:, cols]
            else:
                krel = lax.broadcasted_iota(jnp.int32, s.shape, 0) + mask_off
                qrel = lax.broadcasted_iota(jnp.int32, s.shape, 1) + lo
                s = jnp.where(krel <= qrel, s, -jnp.inf)
                cm = jnp.max(s, axis=0, keepdims=True)
            m_prev = m_scr[:, cols]
            m_new = jnp.maximum(m_prev, cm)
            alpha = jnp.exp2(m_prev - m_new)
            p = jnp.exp2(s - m_new)
            p_ref[0:live, cols] = p.astype(_BF16)
            al_ref[:, cols] = alpha
            m_scr[:, cols] = m_new

    def v_stage(c, p_ref, al_ref, mask_off=None, col_lo=0):
        for lo in range(col_lo, tq, cb):
            cols = slice(lo, lo + cb)
            live = live_rows(mask_off, lo)
            acc_scr[:, cols] = al_ref[:, cols] * acc_scr[:, cols] + jnp.dot(
                vT_ref[c, :, 0:live], p_ref[0:live, cols], preferred_element_type=_F32)

    s_bufs = (((s00, cm00), (s01, cm01)), ((s10, cm10), (s11, cm11)))

    def start_tile():
        m_scr[...] = jnp.full(m_scr.shape, -jnp.inf, _F32)
        acc_scr[...] = jnp.zeros(acc_scr.shape, _F32)


    def regular_step(qi, a, par, first=False, pre_diag=False):
        (sa, cma), (sb, cmb) = s_bufs[par]
        (na, ncma), (nb, ncmb) = s_bufs[1 - par]
        if first:
            x_stage(sa, cma, p_a, al_a)
            q_stage(qi, a + 2, na, ncma, mask_off=0 if pre_diag else None)
            v_stage(a, p_a, al_a)
            x_stage(sb, cmb, p_b, al_b)
            q_stage(qi, a + 3, nb, ncmb, mask_off=tk if pre_diag else None,
                    col_lo=tk if pre_diag else 0)
            return
        q_stage(qi, a + 2, na, ncma, mask_off=0 if pre_diag else None)
        x_stage(sa, cma, p_a, al_a)
        v_stage(a - 1, p_b, al_b)
        q_stage(qi, a + 3, nb, ncmb, mask_off=tk if pre_diag else None,
                col_lo=tk if pre_diag else 0)
        x_stage(sb, cmb, p_b, al_b)
        v_stage(a, p_a, al_a)

    def diagonal_step(qi, par, first=False, last=False):
        a = 2 * qi
        (sa, cma), (sb, cmb) = s_bufs[par]
        (na, ncma), (nb, ncmb) = s_bufs[1 - par]
        if not last:
            q_stage(qi + 1, 0, na, ncma)
        x_stage(sa, cma, p_a, al_a, mask_off=0)
        if not first:
            v_stage(a - 1, p_b, al_b)
        if not last:
            q_stage(qi + 1, 1, nb, ncmb)
        x_stage(sb, cmb, p_b, al_b, mask_off=tk, col_lo=tk)
        v_stage(a, p_a, al_a, mask_off=0)
        v_stage(a + 1, p_b, al_b, mask_off=tk, col_lo=tk)
        vdim = o_ref.shape[1]
        inv_l = 1.0 / acc_scr[vdim:vdim + 1, :]
        o = (acc_scr[0:vdim, :] * inv_l).T
        o_ref[pl.ds(aligned(qi * tq, tq), tq), :] = o.astype(o_ref.dtype)
        if not last:
            start_tile()

    def by_parity(par, fn):
        if isinstance(par, int):
            fn(par)
            return
        for static_par in range(2):
            @pl.when(par == static_par)
            def _():
                fn(static_par)

    def start_parity(qi):
        return (qi * (qi + 1) // 2) % 2

    def tile(qi, last=False):
        par = start_parity(qi)
        if isinstance(qi, int) and qi == 1:
            by_parity(par, functools.partial(regular_step, qi, 0, first=True, pre_diag=True))
        else:
            def step_body(u, p):
                by_parity(p, functools.partial(regular_step, qi, 2 * u))
                return 1 - p

            by_parity(par, functools.partial(regular_step, qi, 0, first=True))
            lax.fori_loop(1, qi - 1, step_body, jnp.int32(1) - par)
            by_parity((par + qi - 1) % 2,
                      functools.partial(regular_step, qi, 2 * (qi - 1), pre_diag=True))
        by_parity((par + qi) % 2, functools.partial(diagonal_step, qi, last=last))

    def tile_body(qi, carry):
        tile(qi)
        return carry

    start_tile()
    q_stage(0, 0, s00, cm00, mask_off=0)
    q_stage(0, 1, s01, cm01, mask_off=tk, col_lo=tk)
    diagonal_step(0, 0, first=True, last=(n_q == 1))
    if n_q > 2:
        tile(1)
    if n_q > 3:
        lax.fori_loop(2, n_q - 1, tile_body, jnp.int32(0))
    if n_q > 1:
        tile(n_q - 1, last=True)


def _outproj_mlp_kernel(o_ref, x_ref, wo_ref, g1_ref, b1_ref, wup_ref, wdown_ref,
                        g2_ref, b2_ref, out_ref, x1_scr, h_scr, *, alpha):
    def mix_rows(rows):
        return jnp.dot(o_ref[rows, :], wo_ref[...], preferred_element_type=_F32)

    _mix_mlp_ln(mix_rows, x_ref, alpha, g1_ref, b1_ref, wup_ref, wdown_ref, g2_ref, b2_ref,
                out_ref, x1_scr, h_scr)


def _swa_mlp_kernel(x_ref, wqT_ref, wk_ref, wvT_ref, wo_ref, bucketT_ref, relb_ref, sink_ref,
                    g1_ref, b1_ref, wup_ref, wdown_ref, g2_ref, b2_ref, out_ref,
                    k_scr, vTe_scr, qT_scr, biasT_scr, s_scr, p_scr, m_scr, oT_scr, x1_scr, h_scr,
                    *, alpha, q_heads, kv_heads, head_dim, q_scale):
    blk = _SWA_BLOCK
    group = q_heads // kv_heads
    tm = x_ref.shape[0]
    n_blk = tm // blk
    vrows = vTe_scr.shape[0] // kv_heads
    first_tile = pl.program_id(1) == 0
    always = pl.program_id(1) >= 0

    @pl.when((pl.program_id(0) == 0) & first_tile)
    def _():
        bkt = bucketT_ref[...]
        for hq in range(q_heads):
            bias = jnp.full(bkt.shape, -jnp.inf, _F32)
            for bb in range(_REL_BUCKETS):
                bias = jnp.where(bkt == bb, relb_ref[bb, hq] * _LOG2E, bias)
            g, gi = divmod(hq, group)
            biasT_scr[g, :, gi * blk:(gi + 1) * blk] = bias

    @pl.when(first_tile)
    def _():
        k_scr[0:blk, :] = jnp.zeros((blk, k_scr.shape[1]), k_scr.dtype)
        vTe_scr[:, 0:blk] = jnp.zeros((vTe_scr.shape[0], blk), vTe_scr.dtype)

    xb = x_ref[...].astype(_BF16)
    qT_scr[...] = (lax.dot_general(wqT_ref[...], xb, _NT, preferred_element_type=_F32)
                   * q_scale).astype(_BF16)
    k_scr[blk:blk + tm, :] = jnp.dot(xb, wk_ref[...], preferred_element_type=_F32).astype(_BF16)
    vT = lax.dot_general(wvT_ref[...], xb, _NT, preferred_element_type=_F32).astype(_BF16)
    ones_rows = (lax.broadcasted_iota(jnp.int32, (vrows - head_dim, tm), 0) == 0).astype(_BF16)
    for g in range(kv_heads):
        vTe_scr[g * vrows:g * vrows + head_dim, blk:blk + tm] = vT[g * head_dim:(g + 1) * head_dim]
        vTe_scr[g * vrows + head_dim:(g + 1) * vrows, blk:blk + tm] = ones_rows

    first_pen = jnp.where(first_tile, -jnp.inf, 0.0).astype(_F32)
    zeros_q = jnp.zeros((head_dim, group * blk), _BF16)
    heads_per_lane_tile = _LANES // head_dim
    units = [(n, g) for n in range(n_blk) for g in range(kv_heads)]

    def sink_row(g):
        return jnp.concatenate(
            [jnp.full((1, blk), sink_ref[g * group + gi] * _LOG2E, _F32) for gi in range(group)],
            axis=1)

    sinks = [None] * kv_heads

    def scores(u):
        n, g = units[u]
        lt = g // heads_per_lane_tile
        kpair = k_scr[n * blk:(n + 2) * blk, lt * _LANES:(lt + 1) * _LANES]
        qg = jnp.concatenate(
            [qT_scr[(g * group + gi) * head_dim:(g * group + gi + 1) * head_dim,
                    n * blk:(n + 1) * blk] for gi in range(group)], axis=1)
        pieces = [zeros_q] * heads_per_lane_tile
        pieces[g % heads_per_lane_tile] = qg
        qpad = jnp.concatenate(pieces, axis=0)
        s = jnp.dot(kpair, qpad, preferred_element_type=_F32) + biasT_scr[g]
        if n == 0:
            s = jnp.concatenate([s[:blk] + first_pen, s[blk:]], axis=0)
        s_scr[u] = s
        m_scr[u] = jnp.maximum(jnp.max(s, axis=0, keepdims=True), sinks[g])

    def exponentials(u):
        p_scr[u] = jnp.exp2(s_scr[u] - m_scr[u]).astype(_BF16)

    def weighted_values(u):
        n, g = units[u]
        oT = jnp.dot(vTe_scr[g * vrows:(g + 1) * vrows, n * blk:(n + 2) * blk], p_scr[u],
                     preferred_element_type=_F32)
        denom = oT[head_dim:head_dim + 1] + jnp.exp2(sinks[g] - m_scr[u])
        o = (oT[:head_dim] * (1.0 / denom)).astype(_BF16)
        for gi in range(group):
            hq = g * group + gi
            oT_scr[hq * head_dim:(hq + 1) * head_dim, n * blk:(n + 1) * blk] = (
                o[:, gi * blk:(gi + 1) * blk])

    for stage in (scores, exponentials, weighted_values):
        @pl.when(always)
        def _():
            for g in range(kv_heads):
                sinks[g] = sink_row(g)
            for u in range(len(units)):
                stage(u)

    @pl.when(pl.program_id(0) >= 0)
    def _():
        k_scr[0:blk, :] = k_scr[tm:tm + blk, :]
        vTe_scr[:, 0:blk] = vTe_scr[:, tm:tm + blk]

    def mix_rows(rows):
        return lax.dot_general(oT_scr[:, rows], wo_ref[...], _TN, preferred_element_type=_F32)

    _mix_mlp_ln(mix_rows, x_ref, alpha, g1_ref, b1_ref, wup_ref, wdown_ref, g2_ref, b2_ref,
                out_ref, x1_scr, h_scr)


def _t5_bucket_table(blk):
    i = np.arange(blk)[None, :]
    j = np.arange(2 * blk)[:, None]
    dist = i + blk - j
    max_exact = _REL_BUCKETS // 2
    nf = np.maximum(dist, 1).astype(np.float32)
    large = max_exact + (np.log(nf / np.float32(max_exact))
                         / np.float32(math.log(_REL_MAX_DIST / max_exact))
                         * np.float32(_REL_BUCKETS - max_exact)).astype(np.int32)
    large = np.minimum(large, _REL_BUCKETS - 1)
    bucket = np.where(dist < max_exact, np.maximum(dist, 0), large)
    valid = (dist >= 0) & (dist < blk)
    return np.where(valid, bucket, -1).astype(np.int32)


def _rope_tables(seq, rope):
    half = rope // 2
    inv = (np.float32(_ROPE_THETA) ** (-np.arange(half, dtype=np.float32) / np.float32(half)))
    ang = np.arange(seq, dtype=np.float32)[:, None] * inv.astype(np.float32)[None, :]
    cos = np.cos(ang.astype(np.float64)).astype(np.float32)
    sin = np.sin(ang.astype(np.float64)).astype(np.float32)
    cs = np.concatenate([cos, cos, -sin, sin], axis=1)
    return jnp.asarray(cs), jnp.asarray(cos.T.copy()), jnp.asarray(sin.T.copy())


def _row(v):
    return v.reshape(1, -1).astype(_F32)


def kernel(x, mla_w_in, mla_g_q, mla_g_kv, mla_w_uq, mla_w_uk, mla_w_uv, mla_w_o, kv_w_shared,
           swa_w_q, swa_sinks, swa_w_o, rel_bias, mlp_w_up, mlp_w_down, ln_mix_g, ln_mix_b,
           ln_mlp_g, ln_mlp_b):
    B, S, D = x.shape
    depth = mlp_w_up.shape[0]
    assert depth == 2 and mla_w_in.shape[0] == 1 and swa_w_q.shape[0] == 1
    alpha = (2 * depth) ** 0.25
    T = B * S
    tm = _TOKEN_TILE
    ta = _ATTN_TILE
    tk = ta // 2
    bf16_rows = 16
    ta_pad = ta + _LANES
    assert ta % tm == 0 and tm % tk == 0 and S % ta == 0

    q_rank, heads, hd = mla_w_uq.shape[1:]
    kv_rank, _, nope = mla_w_uk.shape[1:]
    vdim = mla_w_uv.shape[3]
    vrows = vdim + bf16_rows
    rope = hd - nope
    half = rope // 2
    assert 2 * rope == _LANES and mla_w_in.shape[2] == q_rank + kv_rank + rope

    w_in = mla_w_in[0]
    r0 = q_rank + kv_rank
    w_in_ext = jnp.concatenate(
        [w_in, w_in[:, r0 + half:r0 + rope], w_in[:, r0:r0 + half]], axis=1).astype(_BF16)
    wqT = mla_w_uq[0].reshape(q_rank, heads * hd).T.astype(_BF16)
    wuk = mla_w_uk[0].reshape(kv_rank, heads * nope).astype(_BF16)
    wvT = mla_w_uv[0].reshape(kv_rank, heads * vdim).T.astype(_BF16)
    cs, cosT, sinT = _rope_tables(S, rope)
    n_t = S // tm
    q_scale = hd ** -0.5 * _LOG2E

    cparams2 = pltpu.CompilerParams(dimension_semantics=("arbitrary", "arbitrary"),
                                    vmem_limit_bytes=_VMEM_LIMIT_BYTES)
    cparams1 = pltpu.CompilerParams(dimension_semantics=("arbitrary",),
                                    vmem_limit_bytes=_VMEM_LIMIT_BYTES)

    qT_all, k_all, vT_all = pl.pallas_call(
        functools.partial(_mla_proj_kernel, q_rank=q_rank, kv_rank=kv_rank, heads=heads,
                          nope=nope, rope=rope, q_scale=q_scale),
        grid=(B, n_t),
        in_specs=[
            pl.BlockSpec((None, tm, D), lambda b, i: (b, i, 0)),
            _const_spec(w_in_ext.shape),
            _const_spec((1, q_rank)),
            _const_spec((1, kv_rank)),
            _const_spec(wqT.shape),
            _const_spec(wuk.shape),
            _const_spec(wvT.shape),
            pl.BlockSpec((tm, 2 * rope), lambda b, i: (i, 0)),
            pl.BlockSpec((half, tm), lambda b, i: (0, i)),
            pl.BlockSpec((half, tm), lambda b, i: (0, i)),
        ],
        out_specs=[
            pl.BlockSpec((None, heads, None, hd, tm), lambda b, i: (b, 0, i, 0, 0)),
            pl.BlockSpec((None, heads, tm, hd), lambda b, i: (b, 0, i, 0)),
            pl.BlockSpec((None, heads, tm // tk, vrows, tk), lambda b, i: (b, 0, i, 0, 0)),
        ],
        out_shape=[
            jax.ShapeDtypeStruct((B, heads, n_t, hd, tm), _BF16),
            jax.ShapeDtypeStruct((B, heads, S, hd), _BF16),
            jax.ShapeDtypeStruct((B, heads, S // tk, vrows, tk), _BF16),
        ],
        compiler_params=cparams2,
        name="mla_proj",
    )(x, w_in_ext, _row(mla_g_q[0]), _row(mla_g_kv[0]), wqT, wuk, wvT, cs, cosT, sinT)

    o = pl.pallas_call(
        _mla_attn_kernel,
        grid=(B, heads),
        in_specs=[
            pl.BlockSpec((None, None, n_t, hd, tm), lambda b, h: (b, h, 0, 0, 0)),
            pl.BlockSpec((None, None, S, hd), lambda b, h: (b, h, 0, 0)),
            pl.BlockSpec((None, None, S // tk, vrows, tk), lambda b, h: (b, h, 0, 0, 0)),
        ],
        out_specs=pl.BlockSpec((None, S, vdim), lambda b, h: (b, 0, h)),
        out_shape=jax.ShapeDtypeStruct((B, S, heads * vdim), _BF16),
        scratch_shapes=[
            pltpu.VMEM((1, ta), _F32),
            pltpu.VMEM((vrows, ta), _F32),
        ] + [pltpu.VMEM((tk, ta_pad), _F32)] * 4
          + [pltpu.VMEM((1, ta), _F32)] * 4
          + [pltpu.VMEM((tk, ta), _BF16)] * 2
          + [pltpu.VMEM((1, ta), _F32)] * 2,
        compiler_params=cparams2,
        name="mla_attn",
    )(qT_all, k_all, vT_all)

    d_ff = mlp_w_up.shape[2]
    w_up_all = mlp_w_up.astype(_BF16)
    w_down_all = mlp_w_down.astype(_BF16)
    mlp_scratch = [
        pltpu.VMEM((tm, D), _F32),
        pltpu.VMEM((tm, d_ff), _BF16),
    ]
    x1 = pl.pallas_call(
        functools.partial(_outproj_mlp_kernel, alpha=alpha),
        grid=(T // tm,),
        in_specs=[
            pl.BlockSpec((tm, heads * vdim), lambda i: (i, 0)),
            pl.BlockSpec((tm, D), lambda i: (i, 0)),
            _const_spec((heads * vdim, D)),
            _const_spec((1, D)),
            _const_spec((1, D)),
            _layer_spec(w_up_all.shape, 0),
            _layer_spec(w_down_all.shape, 0),
            _const_spec((1, D)),
            _const_spec((1, D)),
        ],
        out_specs=pl.BlockSpec((tm, D), lambda i: (i, 0)),
        out_shape=jax.ShapeDtypeStruct((T, D), _F32),
        scratch_shapes=mlp_scratch,
        compiler_params=cparams1,
        name="mla_out_mlp",
    )(o.reshape(T, heads * vdim), x.reshape(T, D), mla_w_o[0].astype(_BF16),
      _row(ln_mix_g[0]), _row(ln_mix_b[0]), w_up_all, w_down_all,
      _row(ln_mlp_g[0]), _row(ln_mlp_b[0]))

    q_heads = swa_sinks.shape[1]
    s_hd = swa_w_q.shape[2] // q_heads
    kv_heads = kv_w_shared.shape[1] // (2 * s_hd)
    group = q_heads // kv_heads
    blk = _SWA_BLOCK
    assert _LANES % s_hd == 0 and tm % blk == 0
    wqT1 = swa_w_q[0].T.astype(_BF16)
    wk1 = kv_w_shared[:, :kv_heads * s_hd].astype(_BF16)
    wvT1 = kv_w_shared[:, kv_heads * s_hd:].T.astype(_BF16)
    bucketT = jnp.asarray(_t5_bucket_table(blk))
    n_units = (tm // blk) * kv_heads

    smem = pl.BlockSpec(memory_space=pltpu.SMEM)
    out = pl.pallas_call(
        functools.partial(_swa_mlp_kernel, alpha=alpha, q_heads=q_heads, kv_heads=kv_heads,
                          head_dim=s_hd, q_scale=s_hd ** -0.5 * _LOG2E),
        grid=(B, n_t),
        in_specs=[
            pl.BlockSpec((None, tm, D), lambda b, i: (b, i, 0)),
            _const_spec(wqT1.shape),
            _const_spec(wk1.shape),
            _const_spec(wvT1.shape),
            _const_spec((q_heads * s_hd, D)),
            _const_spec(bucketT.shape),
            smem,
            smem,
            _const_spec((1, D)),
            _const_spec((1, D)),
            _layer_spec(w_up_all.shape, 1),
            _layer_spec(w_down_all.shape, 1),
            _const_spec((1, D)),
            _const_spec((1, D)),
        ],
        out_specs=pl.BlockSpec((None, tm, D), lambda b, i: (b, i, 0)),
        out_shape=jax.ShapeDtypeStruct((B, S, D), _F32),
        scratch_shapes=[
            pltpu.VMEM((blk + tm, kv_heads * s_hd), _BF16),
            pltpu.VMEM((kv_heads * (s_hd + bf16_rows), blk + tm), _BF16),
            pltpu.VMEM((q_heads * s_hd, tm), _BF16),
            pltpu.VMEM((kv_heads, 2 * blk, group * blk), _F32),
            pltpu.VMEM((n_units, 2 * blk, group * blk), _F32),
            pltpu.VMEM((n_units, 2 * blk, group * blk), _BF16),
            pltpu.VMEM((n_units, 1, group * blk), _F32),
            pltpu.VMEM((q_heads * s_hd, tm), _BF16),
        ] + mlp_scratch,
        compiler_params=cparams2,
        name="swa_mlp",
    )(x1.reshape(B, S, D), wqT1, wk1, wvT1, swa_w_o[0].astype(_BF16), bucketT,
      rel_bias.astype(_F32), swa_sinks[0].astype(_F32),
      _row(ln_mix_g[1]), _row(ln_mix_b[1]), w_up_all, w_down_all,
      _row(ln_mlp_g[1]), _row(ln_mlp_b[1]))
    return out
```

```python
import functools
import math

import numpy as np
import jax
import jax.numpy as jnp
from jax import lax
from jax.experimental import pallas as pl
from jax.experimental.pallas import tpu as pltpu

_F32 = jnp.float32
_BF16 = jnp.bfloat16

_LN_EPS = 1e-5
_RMS_EPS = 1e-6
_ROPE_THETA = 10000.0
_SWA_BLOCK = 128
_REL_BUCKETS = 32
_REL_MAX_DIST = 128
_LOG2E = math.log2(math.e)

_V7X_VMEM_BYTES = 64 * 1024 * 1024
_VMEM_LIMIT_BYTES = _V7X_VMEM_BYTES - 8 * 1024 * 1024
_LANES = 128

_TOKEN_TILE = 512
_ATTN_TILE = 1024
_ATTN_COL_BLOCK = 256
_FF_CHUNK = 1024
_MLP_ROW_BLOCK = 256

_NT = (((1,), (1,)), ((), ()))
_TN = (((0,), (0,)), ((), ()))


def _const_spec(shape):
    nd = len(shape)
    return pl.BlockSpec(shape, lambda *_: (0,) * nd, pipeline_mode=pl.Buffered(1))


def _layer_spec(shape, layer):
    nd = len(shape) - 1
    return pl.BlockSpec((None,) + tuple(shape[1:]), lambda *_: (layer,) + (0,) * nd,
                        pipeline_mode=pl.Buffered(1))


def _layernorm(v, g, b):
    mu = jnp.mean(v, axis=-1, keepdims=True)
    d = v - mu
    var = jnp.mean(d * d, axis=-1, keepdims=True)
    return d * lax.rsqrt(var + _LN_EPS) * g + b


def _rmsnorm(v, g):
    return v * lax.rsqrt(jnp.mean(v * v, axis=-1, keepdims=True) + _RMS_EPS) * g


def _mix_mlp_ln(mix_rows, x_ref, alpha, g1_ref, b1_ref, wup_ref, wdown_ref, g2_ref, b2_ref,
                out_ref, x1_scr, h_scr):
    tm = x_ref.shape[0]
    d_ff = wup_ref.shape[1]
    blocks = [slice(r, r + _MLP_ROW_BLOCK) for r in range(0, tm, _MLP_ROW_BLOCK)]
    for rows in blocks:
        x1_scr[rows, :] = _layernorm(alpha * x_ref[rows, :] + mix_rows(rows), g1_ref[...], b1_ref[...])
    for rows in blocks:
        x1b = x1_scr[rows, :].astype(_BF16)
        for lo in range(0, d_ff, _FF_CHUNK):
            hmid = jnp.dot(x1b, wup_ref[:, lo:lo + _FF_CHUNK], preferred_element_type=_F32)
            hmid = jnp.maximum(hmid, 0.0)
            h_scr[rows, lo:lo + _FF_CHUNK] = (hmid * hmid).astype(_BF16)
    for rows in blocks:
        y = jnp.dot(h_scr[rows, :], wdown_ref[...], preferred_element_type=_F32)
        out_ref[rows, :] = _layernorm(alpha * x1_scr[rows, :] + y, g2_ref[...], b2_ref[...])


def _mla_proj_kernel(x_ref, win_ref, gq_ref, gkv_ref, wqT_ref, wuk_ref, wvT_ref,
                     cs_ref, cosT_ref, sinT_ref, qT_ref, k_ref, vT_ref,
                     *, q_rank, kv_rank, heads, nope, rope, q_scale):
    xb = x_ref[...].astype(_BF16)
    h = jnp.dot(xb, win_ref[...], preferred_element_type=_F32)
    cq = _rmsnorm(h[:, :q_rank], gq_ref[...]).astype(_BF16)
    ckv = _rmsnorm(h[:, q_rank:q_rank + kv_rank], gkv_ref[...]).astype(_BF16)
    t = h[:, q_rank + kv_rank:] * cs_ref[...]
    kr = (t + pltpu.roll(t, rope, 1))[:, :rope].astype(_BF16)

    half = rope // 2
    hd = nope + rope
    qT = lax.dot_general(wqT_ref[...], cq, _NT, preferred_element_type=_F32)
    cosT = cosT_ref[...]
    sinT = sinT_ref[...]
    for hh in range(heads):
        base = hh * hd
        qT_ref[hh, 0:nope, :] = (qT[base:base + nope] * q_scale).astype(_BF16)
        x1 = qT[base + nope:base + nope + half]
        x2 = qT[base + nope + half:base + hd]
        qT_ref[hh, nope:nope + half, :] = ((x1 * cosT - x2 * sinT) * q_scale).astype(_BF16)
        qT_ref[hh, nope + half:hd, :] = ((x1 * sinT + x2 * cosT) * q_scale).astype(_BF16)

    kn = jnp.dot(ckv, wuk_ref[...], preferred_element_type=_F32)
    for hh in range(heads):
        k_ref[hh, :, 0:nope] = kn[:, hh * nope:(hh + 1) * nope].astype(_BF16)
        k_ref[hh, :, nope:hd] = kr

    vdim = wvT_ref.shape[0] // heads
    vT = lax.dot_general(wvT_ref[...], ckv, _NT, preferred_element_type=_F32)
    tk = vT_ref.shape[3]
    pad = vT_ref.shape[2] - vdim
    ones_rows = (lax.broadcasted_iota(jnp.int32, (pad, tk), 0) == 0).astype(_BF16)
    for hh in range(heads):
        for j in range(vT_ref.shape[1]):
            vT_ref[hh, j, 0:vdim, :] = vT[hh * vdim:(hh + 1) * vdim, j * tk:(j + 1) * tk].astype(_BF16)
            vT_ref[hh, j, vdim:vdim + pad, :] = ones_rows


def _mla_attn_kernel(qT_ref, k_ref, vT_ref, o_ref, m_scr, acc_scr,
                     s00, s01, s10, s11, cm00, cm01, cm10, cm11, p_a, p_b, al_a, al_b):
    tk, tq = s00.shape[0], acc_scr.shape[1]
    tsub = qT_ref.shape[2]
    nsub = tq // tsub
    n_q = qT_ref.shape[0] // nsub
    assert tq == 2 * tk and tk == vT_ref.shape[2]

    cb = _ATTN_COL_BLOCK


    def aligned(v, m):
        return v if isinstance(v, int) else pl.multiple_of(v, m)

    def live_rows(mask_off, lo):
        return tk if mask_off is None else min(tk, lo + cb - mask_off)

    def q_stage(qi, c, s_ref, cm_ref, mask_off=None, col_lo=0):
        k = k_ref[pl.ds(aligned(c * tk, tk), tk), :]
        for lo in range(col_lo, tq, cb):
            live = live_rows(mask_off, lo)
            j, off = divmod(lo, tsub)
            qT = qT_ref[qi * nsub + j, :, off:off + cb]
            s = jnp.dot(k[0:live], qT, preferred_element_type=_F32)
            s_ref[0:live, lo:lo + cb] = s
            cm_ref[:, lo:lo + cb] = jnp.max(s, axis=0, keepdims=True)

    def x_stage(s_ref, cm_ref, p_ref, al_ref, mask_off=None, col_lo=0):
        for lo in range(col_lo, tq, cb):
            cols = slice(lo, lo + cb)
            live = live_rows(mask_off, lo)
            s = s_ref[0:live, cols]
            if mask_off is None or mask_off + tk - 1 <= lo:
                cm = cm_ref[:, cols]
            else:
                krel = lax.broadcasted_iota(jnp.int32, s.shape, 0) + mask_off
                qrel = lax.broadcasted_iota(jnp.int32, s.shape, 1) + lo
                s = jnp.where(krel <= qrel, s, -jnp.inf)
                cm = jnp.max(s, axis=0, keepdims=True)
            m_prev = m_scr[:, cols]
            m_new = jnp.maximum(m_prev, cm)
            alpha = jnp.exp2(m_prev - m_new)
            p = jnp.exp2(s - m_new)
            p_ref[0:live, cols] = p.astype(_BF16)
            al_ref[:, cols] = alpha
            m_scr[:, cols] = m_new

    def v_stage(c, p_ref, al_ref, mask_off=None, col_lo=0):
        for lo in range(col_lo, tq, cb):
            cols = slice(lo, lo + cb)
            live = live_rows(mask_off, lo)
            acc_scr[:, cols] = al_ref[:, cols] * acc_scr[:, cols] + jnp.dot(
                vT_ref[c, :, 0:live], p_ref[0:live, cols], preferred_element_type=_F32)

    s_bufs = (((s00, cm00), (s01, cm01)), ((s10, cm10), (s11, cm11)))

    def start_tile():
        m_scr[...] = jnp.full(m_scr.shape, -jnp.inf, _F32)
        acc_scr[...] = jnp.zeros(acc_scr.shape, _F32)


    def regular_step(qi, a, par, first=False, pre_diag=False):
        (sa, cma), (sb, cmb) = s_bufs[par]
        (na, ncma), (nb, ncmb) = s_bufs[1 - par]
        if first:
            x_stage(sa, cma, p_a, al_a)
            q_stage(qi, a + 2, na, ncma, mask_off=0 if pre_diag else None)
            v_stage(a, p_a, al_a)
            x_stage(sb, cmb, p_b, al_b)
            q_stage(qi, a + 3, nb, ncmb, mask_off=tk if pre_diag else None,
                    col_lo=tk if pre_diag else 0)
            return
        q_stage(qi, a + 2, na, ncma, mask_off=0 if pre_diag else None)
        x_stage(sa, cma, p_a, al_a)
        v_stage(a - 1, p_b, al_b)
        q_stage(qi, a + 3, nb, ncmb, mask_off=tk if pre_diag else None,
                col_lo=tk if pre_diag else 0)
        x_stage(sb, cmb, p_b, al_b)
        v_stage(a, p_a, al_a)

    def diagonal_step(qi, par, first=False, last=False):
        a = 2 * qi
        (sa, cma), (sb, cmb) = s_bufs[par]
        (na, ncma), (nb, ncmb) = s_bufs[1 - par]
        if not last:
            q_stage(qi + 1, 0, na, ncma)
        x_stage(sa, cma, p_a, al_a, mask_off=0)
        if not first:
            v_stage(a - 1, p_b, al_b)
        if not last:
            q_stage(qi + 1, 1, nb, ncmb)
        x_stage(sb, cmb, p_b, al_b, mask_off=tk, col_lo=tk)
        v_stage(a, p_a, al_a, mask_off=0)
        v_stage(a + 1, p_b, al_b, mask_off=tk, col_lo=tk)
        vdim = o_ref.shape[0]
        inv_l = 1.0 / acc_scr[vdim:vdim + 1, :]
        o = acc_scr[0:vdim, :] * inv_l
        o_ref[:, pl.ds(aligned(qi * tq, tq), tq)] = o.astype(o_ref.dtype)
        if not last:
            start_tile()

    def by_parity(par, fn):
        if isinstance(par, int):
            fn(par)
            return
        for static_par in range(2):
            @pl.when(par == static_par)
            def _():
                fn(static_par)

    def start_parity(qi):
        return (qi * (qi + 1) // 2) % 2

    def tile(qi, last=False):
        par = start_parity(qi)
        if isinstance(qi, int) and qi == 1:
            by_parity(par, functools.partial(regular_step, qi, 0, first=True, pre_diag=True))
        else:
            def step_body(u, p):
                by_parity(p, functools.partial(regular_step, qi, 2 * u))
                return 1 - p

            by_parity(par, functools.partial(regular_step, qi, 0, first=True))
            lax.fori_loop(1, qi - 1, step_body, jnp.int32(1) - par)
            by_parity((par + qi - 1) % 2,
                      functools.partial(regular_step, qi, 2 * (qi - 1), pre_diag=True))
        by_parity((par + qi) % 2, functools.partial(diagonal_step, qi, last=last))

    def tile_body(qi, carry):
        tile(qi)
        return carry

    start_tile()
    q_stage(0, 0, s00, cm00, mask_off=0)
    q_stage(0, 1, s01, cm01, mask_off=tk, col_lo=tk)
    diagonal_step(0, 0, first=True, last=(n_q == 1))
    if n_q > 2:
        tile(1)
    if n_q > 3:
        lax.fori_loop(2, n_q - 1, tile_body, jnp.int32(0))
    if n_q > 1:
        tile(n_q - 1, last=True)


def _outproj_mlp_kernel(o_ref, x_ref, wo_ref, g1_ref, b1_ref, wup_ref, wdown_ref,
                        g2_ref, b2_ref, out_ref, x1_scr, h_scr, *, alpha):
    def mix_rows(rows):
        return lax.dot_general(o_ref[:, rows], wo_ref[...], _TN, preferred_element_type=_F32)

    _mix_mlp_ln(mix_rows, x_ref, alpha, g1_ref, b1_ref, wup_ref, wdown_ref, g2_ref, b2_ref,
                out_ref, x1_scr, h_scr)


def _swa_mlp_kernel(x_ref, wqT_ref, wk_ref, wvT_ref, wo_ref, bucketT_ref, relb_ref, sink_ref,
                    g1_ref, b1_ref, wup_ref, wdown_ref, g2_ref, b2_ref, out_ref,
                    k_scr, vTe_scr, qT_scr, biasT_scr, s_scr, p_scr, m_scr, oT_scr, x1_scr, h_scr,
                    *, alpha, q_heads, kv_heads, head_dim, q_scale):
    blk = _SWA_BLOCK
    group = q_heads // kv_heads
    tm = x_ref.shape[0]
    n_blk = tm // blk
    vrows = vTe_scr.shape[0] // kv_heads
    first_tile = pl.program_id(1) == 0
    always = pl.program_id(1) >= 0

    @pl.when((pl.program_id(0) == 0) & first_tile)
    def _():
        bkt = bucketT_ref[...]
        for hq in range(q_heads):
            bias = jnp.full(bkt.shape, -jnp.inf, _F32)
            for bb in range(_REL_BUCKETS):
                bias = jnp.where(bkt == bb, relb_ref[bb, hq] * _LOG2E, bias)
            g, gi = divmod(hq, group)
            biasT_scr[g, :, gi * blk:(gi + 1) * blk] = bias

    @pl.when(first_tile)
    def _():
        k_scr[0:blk, :] = jnp.zeros((blk, k_scr.shape[1]), k_scr.dtype)
        vTe_scr[:, 0:blk] = jnp.zeros((vTe_scr.shape[0], blk), vTe_scr.dtype)

    xb = x_ref[...].astype(_BF16)
    qT_scr[...] = (lax.dot_general(wqT_ref[...], xb, _NT, preferred_element_type=_F32)
                   * q_scale).astype(_BF16)
    k_scr[blk:blk + tm, :] = jnp.dot(xb, wk_ref[...], preferred_element_type=_F32).astype(_BF16)
    vT = lax.dot_general(wvT_ref[...], xb, _NT, preferred_element_type=_F32).astype(_BF16)
    ones_rows = (lax.broadcasted_iota(jnp.int32, (vrows - head_dim, tm), 0) == 0).astype(_BF16)
    for g in range(kv_heads):
        vTe_scr[g * vrows:g * vrows + head_dim, blk:blk + tm] = vT[g * head_dim:(g + 1) * head_dim]
        vTe_scr[g * vrows + head_dim:(g + 1) * vrows, blk:blk + tm] = ones_rows

    first_pen = jnp.where(first_tile, -jnp.inf, 0.0).astype(_F32)
    zeros_q = jnp.zeros((head_dim, group * blk), _BF16)
    heads_per_lane_tile = _LANES // head_dim
    units = [(n, g) for n in range(n_blk) for g in range(kv_heads)]

    def sink_row(g):
        return jnp.concatenate(
            [jnp.full((1, blk), sink_ref[g * group + gi] * _LOG2E, _F32) for gi in range(group)],
            axis=1)

    sinks = [None] * kv_heads

    def scores(u):
        n, g = units[u]
        lt = g // heads_per_lane_tile
        kpair = k_scr[n * blk:(n + 2) * blk, lt * _LANES:(lt + 1) * _LANES]
        qg = jnp.concatenate(
            [qT_scr[(g * group + gi) * head_dim:(g * group + gi + 1) * head_dim,
                    n * blk:(n + 1) * blk] for gi in range(group)], axis=1)
        pieces = [zeros_q] * heads_per_lane_tile
        pieces[g % heads_per_lane_tile] = qg
        qpad = jnp.concatenate(pieces, axis=0)
        s = jnp.dot(kpair, qpad, preferred_element_type=_F32) + biasT_scr[g]
        if n == 0:
            s = jnp.concatenate([s[:blk] + first_pen, s[blk:]], axis=0)
        s_scr[u] = s
        m_scr[u] = jnp.maximum(jnp.max(s, axis=0, keepdims=True), sinks[g])

    def exponentials(u):
        p_scr[u] = jnp.exp2(s_scr[u] - m_scr[u]).astype(_BF16)

    def weighted_values(u):
        n, g = units[u]
        oT = jnp.dot(vTe_scr[g * vrows:(g + 1) * vrows, n * blk:(n + 2) * blk], p_scr[u],
                     preferred_element_type=_F32)
        denom = oT[head_dim:head_dim + 1] + jnp.exp2(sinks[g] - m_scr[u])
        o = (oT[:head_dim] * (1.0 / denom)).astype(_BF16)
        for gi in range(group):
            hq = g * group + gi
            oT_scr[hq * head_dim:(hq + 1) * head_dim, n * blk:(n + 1) * blk] = (
                o[:, gi * blk:(gi + 1) * blk])

    for stage in (scores, exponentials, weighted_values):
        @pl.when(always)
        def _():
            for g in range(kv_heads):
                sinks[g] = sink_row(g)
            for u in range(len(units)):
                stage(u)

    @pl.when(pl.program_id(0) >= 0)
    def _():
        k_scr[0:blk, :] = k_scr[tm:tm + blk, :]
        vTe_scr[:, 0:blk] = vTe_scr[:, tm:tm + blk]

    def mix_rows(rows):
        return lax.dot_general(oT_scr[:, rows], wo_ref[...], _TN, preferred_element_type=_F32)

    _mix_mlp_ln(mix_rows, x_ref, alpha, g1_ref, b1_ref, wup_ref, wdown_ref, g2_ref, b2_ref,
                out_ref, x1_scr, h_scr)


def _t5_bucket_table(blk):
    i = np.arange(blk)[None, :]
    j = np.arange(2 * blk)[:, None]
    dist = i + blk - j
    max_exact = _REL_BUCKETS // 2
    nf = np.maximum(dist, 1).astype(np.float32)
    large = max_exact + (np.log(nf / np.float32(max_exact))
                         / np.float32(math.log(_REL_MAX_DIST / max_exact))
                         * np.float32(_REL_BUCKETS - max_exact)).astype(np.int32)
    large = np.minimum(large, _REL_BUCKETS - 1)
    bucket = np.where(dist < max_exact, np.maximum(dist, 0), large)
    valid = (dist >= 0) & (dist < blk)
    return np.where(valid, bucket, -1).astype(np.int32)


def _rope_tables(seq, rope):
    half = rope // 2
    inv = (np.float32(_ROPE_THETA) ** (-np.arange(half, dtype=np.float32) / np.float32(half)))
    ang = np.arange(seq, dtype=np.float32)[:, None] * inv.astype(np.float32)[None, :]
    cos = np.cos(ang.astype(np.float64)).astype(np.float32)
    sin = np.sin(ang.astype(np.float64)).astype(np.float32)
    cs = np.concatenate([cos, cos, -sin, sin], axis=1)
    return jnp.asarray(cs), jnp.asarray(cos.T.copy()), jnp.asarray(sin.T.copy())


def _row(v):
    return v.reshape(1, -1).astype(_F32)


def kernel(x, mla_w_in, mla_g_q, mla_g_kv, mla_w_uq, mla_w_uk, mla_w_uv, mla_w_o, kv_w_shared,
           swa_w_q, swa_sinks, swa_w_o, rel_bias, mlp_w_up, mlp_w_down, ln_mix_g, ln_mix_b,
           ln_mlp_g, ln_mlp_b):
    B, S, D = x.shape
    depth = mlp_w_up.shape[0]
    assert depth == 2 and mla_w_in.shape[0] == 1 and swa_w_q.shape[0] == 1
    alpha = (2 * depth) ** 0.25
    T = B * S
    tm = _TOKEN_TILE
    ta = _ATTN_TILE
    tk = ta // 2
    bf16_rows = 16
    ta_pad = ta + _LANES
    assert ta % tm == 0 and tm % tk == 0 and S % ta == 0

    q_rank, heads, hd = mla_w_uq.shape[1:]
    kv_rank, _, nope = mla_w_uk.shape[1:]
    vdim = mla_w_uv.shape[3]
    vrows = vdim + bf16_rows
    rope = hd - nope
    half = rope // 2
    assert 2 * rope == _LANES and mla_w_in.shape[2] == q_rank + kv_rank + rope

    w_in = mla_w_in[0]
    r0 = q_rank + kv_rank
    w_in_ext = jnp.concatenate(
        [w_in, w_in[:, r0 + half:r0 + rope], w_in[:, r0:r0 + half]], axis=1).astype(_BF16)
    wqT = mla_w_uq[0].reshape(q_rank, heads * hd).T.astype(_BF16)
    wuk = mla_w_uk[0].reshape(kv_rank, heads * nope).astype(_BF16)
    wvT = mla_w_uv[0].reshape(kv_rank, heads * vdim).T.astype(_BF16)
    cs, cosT, sinT = _rope_tables(S, rope)
    n_t = S // tm
    q_scale = hd ** -0.5 * _LOG2E

    cparams2 = pltpu.CompilerParams(dimension_semantics=("arbitrary", "arbitrary"),
                                    vmem_limit_bytes=_VMEM_LIMIT_BYTES)
    cparams1 = pltpu.CompilerParams(dimension_semantics=("arbitrary",),
                                    vmem_limit_bytes=_VMEM_LIMIT_BYTES)

    qT_all, k_all, vT_all = pl.pallas_call(
        functools.partial(_mla_proj_kernel, q_rank=q_rank, kv_rank=kv_rank, heads=heads,
                          nope=nope, rope=rope, q_scale=q_scale),
        grid=(B, n_t),
        in_specs=[
            pl.BlockSpec((None, tm, D), lambda b, i: (b, i, 0)),
            _const_spec(w_in_ext.shape),
            _const_spec((1, q_rank)),
            _const_spec((1, kv_rank)),
            _const_spec(wqT.shape),
            _const_spec(wuk.shape),
            _const_spec(wvT.shape),
            pl.BlockSpec((tm, 2 * rope), lambda b, i: (i, 0)),
            pl.BlockSpec((half, tm), lambda b, i: (0, i)),
            pl.BlockSpec((half, tm), lambda b, i: (0, i)),
        ],
        out_specs=[
            pl.BlockSpec((None, heads, None, hd, tm), lambda b, i: (b, 0, i, 0, 0)),
            pl.BlockSpec((None, heads, tm, hd), lambda b, i: (b, 0, i, 0)),
            pl.BlockSpec((None, heads, tm // tk, vrows, tk), lambda b, i: (b, 0, i, 0, 0)),
        ],
        out_shape=[
            jax.ShapeDtypeStruct((B, heads, n_t, hd, tm), _BF16),
            jax.ShapeDtypeStruct((B, heads, S, hd), _BF16),
            jax.ShapeDtypeStruct((B, heads, S // tk, vrows, tk), _BF16),
        ],
        compiler_params=cparams2,
        name="mla_proj",
    )(x, w_in_ext, _row(mla_g_q[0]), _row(mla_g_kv[0]), wqT, wuk, wvT, cs, cosT, sinT)

    o = pl.pallas_call(
        _mla_attn_kernel,
        grid=(B, heads),
        in_specs=[
            pl.BlockSpec((None, None, n_t, hd, tm), lambda b, h: (b, h, 0, 0, 0)),
            pl.BlockSpec((None, None, S, hd), lambda b, h: (b, h, 0, 0)),
            pl.BlockSpec((None, None, S // tk, vrows, tk), lambda b, h: (b, h, 0, 0, 0)),
        ],
        out_specs=pl.BlockSpec((None, vdim, S), lambda b, h: (b, h, 0)),
        out_shape=jax.ShapeDtypeStruct((B, heads * vdim, S), _BF16),
        scratch_shapes=[
            pltpu.VMEM((1, ta), _F32),
            pltpu.VMEM((vrows, ta), _F32),
        ] + [pltpu.VMEM((tk, ta_pad), _F32)] * 4
          + [pltpu.VMEM((1, ta), _F32)] * 4
          + [pltpu.VMEM((tk, ta), _BF16)] * 2
          + [pltpu.VMEM((1, ta), _F32)] * 2,
        compiler_params=cparams2,
        name="mla_attn",
    )(qT_all, k_all, vT_all)

    d_ff = mlp_w_up.shape[2]
    w_up_all = mlp_w_up.astype(_BF16)
    w_down_all = mlp_w_down.astype(_BF16)
    mlp_scratch = [
        pltpu.VMEM((tm, D), _F32),
        pltpu.VMEM((tm, d_ff), _BF16),
    ]
    x1 = pl.pallas_call(
        functools.partial(_outproj_mlp_kernel, alpha=alpha),
        grid=(T // tm,),
        in_specs=[
            pl.BlockSpec((None, heads * vdim, tm), lambda i: (i // n_t, 0, i % n_t)),
            pl.BlockSpec((tm, D), lambda i: (i, 0)),
            _const_spec((heads * vdim, D)),
            _const_spec((1, D)),
            _const_spec((1, D)),
            _layer_spec(w_up_all.shape, 0),
            _layer_spec(w_down_all.shape, 0),
            _const_spec((1, D)),
            _const_spec((1, D)),
        ],
        out_specs=pl.BlockSpec((tm, D), lambda i: (i, 0)),
        out_shape=jax.ShapeDtypeStruct((T, D), _F32),
        scratch_shapes=mlp_scratch,
        compiler_params=cparams1,
        name="mla_out_mlp",
    )(o, x.reshape(T, D), mla_w_o[0].astype(_BF16),
      _row(ln_mix_g[0]), _row(ln_mix_b[0]), w_up_all, w_down_all,
      _row(ln_mlp_g[0]), _row(ln_mlp_b[0]))

    q_heads = swa_sinks.shape[1]
    s_hd = swa_w_q.shape[2] // q_heads
    kv_heads = kv_w_shared.shape[1] // (2 * s_hd)
    group = q_heads // kv_heads
    blk = _SWA_BLOCK
    assert _LANES % s_hd == 0 and tm % blk == 0
    wqT1 = swa_w_q[0].T.astype(_BF16)
    wk1 = kv_w_shared[:, :kv_heads * s_hd].astype(_BF16)
    wvT1 = kv_w_shared[:, kv_heads * s_hd:].T.astype(_BF16)
    bucketT = jnp.asarray(_t5_bucket_table(blk))
    n_units = (tm // blk) * kv_heads

    smem = pl.BlockSpec(memory_space=pltpu.SMEM)
    out = pl.pallas_call(
        functools.partial(_swa_mlp_kernel, alpha=alpha, q_heads=q_heads, kv_heads=kv_heads,
                          head_dim=s_hd, q_scale=s_hd ** -0.5 * _LOG2E),
        grid=(B, n_t),
        in_specs=[
            pl.BlockSpec((None, tm, D), lambda b, i: (b, i, 0)),
            _const_spec(wqT1.shape),
            _const_spec(wk1.shape),
            _const_spec(wvT1.shape),
            _const_spec((q_heads * s_hd, D)),
            _const_spec(bucketT.shape),
            smem,
            smem,
            _const_spec((1, D)),
            _const_spec((1, D)),
            _layer_spec(w_up_all.shape, 1),
            _layer_spec(w_down_all.shape, 1),
            _const_spec((1, D)),
            _const_spec((1, D)),
        ],
        out_specs=pl.BlockSpec((None, tm, D), lambda b, i: (b, i, 0)),
        out_shape=jax.ShapeDtypeStruct((B, S, D), _F32),
        scratch_shapes=[
            pltpu.VMEM((blk + tm, kv_heads * s_hd), _BF16),
            pltpu.VMEM((kv_heads * (s_hd + bf16_rows), blk + tm), _BF16),
            pltpu.VMEM((q_heads * s_hd, tm), _BF16),
            pltpu.VMEM((kv_heads, 2 * blk, group * blk), _F32),
            pltpu.VMEM((n_units, 2 * blk, group * blk), _F32),
            pltpu.VMEM((n_units, 2 * blk, group * blk), _BF16),
            pltpu.VMEM((n_units, 1, group * blk), _F32),
            pltpu.VMEM((q_heads * s_hd, tm), _BF16),
        ] + mlp_scratch,
        compiler_params=cparams2,
        name="swa_mlp",
    )(x1.reshape(B, S, D), wqT1, wk1, wvT1, swa_w_o[0].astype(_BF16), bucketT,
      rel_bias.astype(_F32), swa_sinks[0].astype(_F32),
      _row(ln_mix_g[1]), _row(ln_mix_b[1]), w_up_all, w_down_all,
      _row(ln_mlp_g[1]), _row(ln_mlp_b[1]))
    return out
```

```python
import functools
import math

import numpy as np
import jax
import jax.numpy as jnp
from jax import lax
from jax.experimental import pallas as pl
from jax.experimental.pallas import tpu as pltpu

_F32 = jnp.float32
_BF16 = jnp.bfloat16

_LN_EPS = 1e-5
_RMS_EPS = 1e-6
_ROPE_THETA = 10000.0
_SWA_BLOCK = 128
_REL_BUCKETS = 32
_REL_MAX_DIST = 128
_LOG2E = math.log2(math.e)

_V7X_VMEM_BYTES = 64 * 1024 * 1024
_VMEM_LIMIT_BYTES = _V7X_VMEM_BYTES - 8 * 1024 * 1024
_LANES = 128

_TOKEN_TILE = 512
_ATTN_TILE = 1024
_ATTN_COL_BLOCK = 256
_FF_CHUNK = 1024
_MLP_ROW_BLOCK = 256

_NT = (((1,), (1,)), ((), ()))
_TN = (((0,), (0,)), ((), ()))


def _const_spec(shape):
    nd = len(shape)
    return pl.BlockSpec(shape, lambda *_: (0,) * nd, pipeline_mode=pl.Buffered(1))


def _layer_spec(shape, layer):
    nd = len(shape) - 1
    return pl.BlockSpec((None,) + tuple(shape[1:]), lambda *_: (layer,) + (0,) * nd,
                        pipeline_mode=pl.Buffered(1))


def _layernorm(v, g, b):
    mu = jnp.mean(v, axis=-1, keepdims=True)
    d = v - mu
    var = jnp.mean(d * d, axis=-1, keepdims=True)
    return d * lax.rsqrt(var + _LN_EPS) * g + b


def _rmsnorm(v, g):
    return v * lax.rsqrt(jnp.mean(v * v, axis=-1, keepdims=True) + _RMS_EPS) * g


def _mix_mlp_ln(mix_rows, x_ref, alpha, g1_ref, b1_ref, wup_ref, wdown_ref, g2_ref, b2_ref,
                out_ref, x1_scr, h_scr):
    tm = x_ref.shape[0]
    d_ff = wup_ref.shape[1]
    blocks = [slice(r, r + _MLP_ROW_BLOCK) for r in range(0, tm, _MLP_ROW_BLOCK)]
    for rows in blocks:
        x1_scr[rows, :] = _layernorm(alpha * x_ref[rows, :] + mix_rows(rows), g1_ref[...], b1_ref[...])
    for rows in blocks:
        x1b = x1_scr[rows, :].astype(_BF16)
        for lo in range(0, d_ff, _FF_CHUNK):
            hmid = jnp.dot(x1b, wup_ref[:, lo:lo + _FF_CHUNK], preferred_element_type=_F32)
            hmid = jnp.maximum(hmid, 0.0)
            h_scr[rows, lo:lo + _FF_CHUNK] = (hmid * hmid).astype(_BF16)
    for rows in blocks:
        y = jnp.dot(h_scr[rows, :], wdown_ref[...], preferred_element_type=_F32)
        out_ref[rows, :] = _layernorm(alpha * x1_scr[rows, :] + y, g2_ref[...], b2_ref[...])


def _mla_proj_kernel(x_ref, win_ref, gq_ref, gkv_ref, wqT_ref, wuk_ref, wvT_ref,
                     cs_ref, cosT_ref, sinT_ref, qT_ref, k_ref, vT_ref,
                     *, q_rank, kv_rank, heads, nope, rope, q_scale):
    xb = x_ref[...].astype(_BF16)
    h = jnp.dot(xb, win_ref[...], preferred_element_type=_F32)
    cq = _rmsnorm(h[:, :q_rank], gq_ref[...]).astype(_BF16)
    ckv = _rmsnorm(h[:, q_rank:q_rank + kv_rank], gkv_ref[...]).astype(_BF16)
    t = h[:, q_rank + kv_rank:] * cs_ref[...]
    kr = (t + pltpu.roll(t, rope, 1))[:, :rope].astype(_BF16)

    half = rope // 2
    hd = nope + rope
    qT = lax.dot_general(wqT_ref[...], cq, _NT, preferred_element_type=_F32)
    cosT = cosT_ref[...]
    sinT = sinT_ref[...]
    for hh in range(heads):
        base = hh * hd
        qT_ref[hh, 0:nope, :] = (qT[base:base + nope] * q_scale).astype(_BF16)
        x1 = qT[base + nope:base + nope + half]
        x2 = qT[base + nope + half:base + hd]
        qT_ref[hh, nope:nope + half, :] = ((x1 * cosT - x2 * sinT) * q_scale).astype(_BF16)
        qT_ref[hh, nope + half:hd, :] = ((x1 * sinT + x2 * cosT) * q_scale).astype(_BF16)

    kn = jnp.dot(ckv, wuk_ref[...], preferred_element_type=_F32)
    for hh in range(heads):
        k_ref[hh, :, 0:nope] = kn[:, hh * nope:(hh + 1) * nope].astype(_BF16)
        k_ref[hh, :, nope:hd] = kr

    vdim = wvT_ref.shape[0] // heads
    vT = lax.dot_general(wvT_ref[...], ckv, _NT, preferred_element_type=_F32)
    tk = vT_ref.shape[3]
    pad = vT_ref.shape[2] - vdim
    ones_rows = (lax.broadcasted_iota(jnp.int32, (pad, tk), 0) == 0).astype(_BF16)
    for hh in range(heads):
        for j in range(vT_ref.shape[1]):
            vT_ref[hh, j, 0:vdim, :] = vT[hh * vdim:(hh + 1) * vdim, j * tk:(j + 1) * tk].astype(_BF16)
            vT_ref[hh, j, vdim:vdim + pad, :] = ones_rows


def _mla_attn_kernel(qT_ref, k_ref, vT_ref, o_ref, m_scr, acc_scr,
                     s00, s01, s10, s11, cm00, cm01, cm10, cm11, p_a, p_b, al_a, al_b):
    tk, tq = s00.shape[0], acc_scr.shape[1]
    tsub = qT_ref.shape[2]
    nsub = tq // tsub
    n_q = qT_ref.shape[0] // nsub
    assert tq == 2 * tk and tk == vT_ref.shape[2]

    cb = _ATTN_COL_BLOCK


    def aligned(v, m):
        return v if isinstance(v, int) else pl.multiple_of(v, m)

    def live_rows(mask_off, lo):
        return tk if mask_off is None else min(tk, lo + cb - mask_off)

    def q_stage(qi, c, s_ref, cm_ref, mask_off=None, col_lo=0):
        k = k_ref[pl.ds(aligned(c * tk, tk), tk), :]
        for lo in range(col_lo, tq, cb):
            live = live_rows(mask_off, lo)
            j, off = divmod(lo, tsub)
            qT = qT_ref[qi * nsub + j, :, off:off + cb]
            s = jnp.dot(k[0:live], qT, preferred_element_type=_F32)
            s_ref[0:live, lo:lo + cb] = s
            cm_ref[:, lo:lo + cb] = jnp.max(s, axis=0, keepdims=True)

    def x_stage(s_ref, cm_ref, p_ref, al_ref, mask_off=None, col_lo=0):
        for lo in range(col_lo, tq, cb):
            cols = slice(lo, lo + cb)
            live = live_rows(mask_off, lo)
            s = s_ref[0:live, cols]
            if mask_off is None or mask_off + tk - 1 <= lo:
                cm = cm_ref[:, cols]
            else:
                full = max(0, lo - mask_off)
                part = s[full:live]
                krel = lax.broadcasted_iota(jnp.int32, part.shape, 0) + (mask_off + full)
                qrel = lax.broadcasted_iota(jnp.int32, part.shape, 1) + lo
                part = jnp.where(krel <= qrel, part, -jnp.inf)
                s = part if full == 0 else jnp.concatenate([s[0:full], part], axis=0)
                cm = jnp.max(s, axis=0, keepdims=True)
            m_prev = m_scr[:, cols]
            m_new = jnp.maximum(m_prev, cm)
            alpha = jnp.exp2(m_prev - m_new)
            p = jnp.exp2(s - m_new)
            p_ref[0:live, cols] = p.astype(_BF16)
            al_ref[:, cols] = alpha
            m_scr[:, cols] = m_new

    def v_stage(c, p_ref, al_ref, mask_off=None, col_lo=0):
        for lo in range(col_lo, tq, cb):
            cols = slice(lo, lo + cb)
            live = live_rows(mask_off, lo)
            acc_scr[:, cols] = al_ref[:, cols] * acc_scr[:, cols] + jnp.dot(
                vT_ref[c, :, 0:live], p_ref[0:live, cols], preferred_element_type=_F32)

    s_bufs = (((s00, cm00), (s01, cm01)), ((s10, cm10), (s11, cm11)))

    def start_tile():
        m_scr[...] = jnp.full(m_scr.shape, -jnp.inf, _F32)
        acc_scr[...] = jnp.zeros(acc_scr.shape, _F32)


    def regular_step(qi, a, par, first=False, pre_diag=False):
        (sa, cma), (sb, cmb) = s_bufs[par]
        (na, ncma), (nb, ncmb) = s_bufs[1 - par]
        if first:
            x_stage(sa, cma, p_a, al_a)
            q_stage(qi, a + 2, na, ncma, mask_off=0 if pre_diag else None)
            v_stage(a, p_a, al_a)
            x_stage(sb, cmb, p_b, al_b)
            q_stage(qi, a + 3, nb, ncmb, mask_off=tk if pre_diag else None,
                    col_lo=tk if pre_diag else 0)
            return
        q_stage(qi, a + 2, na, ncma, mask_off=0 if pre_diag else None)
        x_stage(sa, cma, p_a, al_a)
        v_stage(a - 1, p_b, al_b)
        q_stage(qi, a + 3, nb, ncmb, mask_off=tk if pre_diag else None,
                col_lo=tk if pre_diag else 0)
        x_stage(sb, cmb, p_b, al_b)
        v_stage(a, p_a, al_a)

    def diagonal_step(qi, par, first=False, last=False):
        a = 2 * qi
        (sa, cma), (sb, cmb) = s_bufs[par]
        (na, ncma), (nb, ncmb) = s_bufs[1 - par]
        if not last:
            q_stage(qi + 1, 0, na, ncma)
        x_stage(sa, cma, p_a, al_a, mask_off=0)
        if not first:
            v_stage(a - 1, p_b, al_b)
        if not last:
            q_stage(qi + 1, 1, nb, ncmb)
        x_stage(sb, cmb, p_b, al_b, mask_off=tk, col_lo=tk)
        v_stage(a, p_a, al_a, mask_off=0)
        v_stage(a + 1, p_b, al_b, mask_off=tk, col_lo=tk)
        vdim = o_ref.shape[1]
        inv_l = 1.0 / acc_scr[vdim:vdim + 1, :]
        o = (acc_scr[0:vdim, :] * inv_l).T
        o_ref[pl.ds(aligned(qi * tq, tq), tq), :] = o.astype(o_ref.dtype)
        if not last:
            start_tile()

    def by_parity(par, fn):
        if isinstance(par, int):
            fn(par)
            return
        for static_par in range(2):
            @pl.when(par == static_par)
            def _():
                fn(static_par)

    def start_parity(qi):
        return (qi * (qi + 1) // 2) % 2

    def tile(qi, last=False):
        par = start_parity(qi)
        if isinstance(qi, int) and qi == 1:
            by_parity(par, functools.partial(regular_step, qi, 0, first=True, pre_diag=True))
        else:
            def step_body(u, p):
                by_parity(p, functools.partial(regular_step, qi, 2 * u))
                return 1 - p

            by_parity(par, functools.partial(regular_step, qi, 0, first=True))
            lax.fori_loop(1, qi - 1, step_body, jnp.int32(1) - par)
            by_parity((par + qi - 1) % 2,
                      functools.partial(regular_step, qi, 2 * (qi - 1), pre_diag=True))
        by_parity((par + qi) % 2, functools.partial(diagonal_step, qi, last=last))

    def tile_body(qi, carry):
        tile(qi)
        return carry

    start_tile()
    q_stage(0, 0, s00, cm00, mask_off=0)
    q_stage(0, 1, s01, cm01, mask_off=tk, col_lo=tk)
    diagonal_step(0, 0, first=True, last=(n_q == 1))
    if n_q > 2:
        tile(1)
    if n_q > 3:
        lax.fori_loop(2, n_q - 1, tile_body, jnp.int32(0))
    if n_q > 1:
        tile(n_q - 1, last=True)


def _outproj_mlp_kernel(o_ref, x_ref, wo_ref, g1_ref, b1_ref, wup_ref, wdown_ref,
                        g2_ref, b2_ref, out_ref, x1_scr, h_scr, *, alpha):
    def mix_rows(rows):
        return jnp.dot(o_ref[rows, :], wo_ref[...], preferred_element_type=_F32)

    _mix_mlp_ln(mix_rows, x_ref, alpha, g1_ref, b1_ref, wup_ref, wdown_ref, g2_ref, b2_ref,
                out_ref, x1_scr, h_scr)


def _swa_mlp_kernel(x_ref, wqT_ref, wk_ref, wvT_ref, wo_ref, bucketT_ref, relb_ref, sink_ref,
                    g1_ref, b1_ref, wup_ref, wdown_ref, g2_ref, b2_ref, out_ref,
                    k_scr, vTe_scr, qT_scr, biasT_scr, s_scr, p_scr, m_scr, oT_scr, x1_scr, h_scr,
                    *, alpha, q_heads, kv_heads, head_dim, q_scale):
    blk = _SWA_BLOCK
    group = q_heads // kv_heads
    tm = x_ref.shape[0]
    n_blk = tm // blk
    vrows = vTe_scr.shape[0] // kv_heads
    first_tile = pl.program_id(1) == 0
    always = pl.program_id(1) >= 0

    @pl.when((pl.program_id(0) == 0) & first_tile)
    def _():
        bkt = bucketT_ref[...]
        for hq in range(q_heads):
            bias = jnp.full(bkt.shape, -jnp.inf, _F32)
            for bb in range(_REL_BUCKETS):
                bias = jnp.where(bkt == bb, relb_ref[bb, hq] * _LOG2E, bias)
            g, gi = divmod(hq, group)
            biasT_scr[g, :, gi * blk:(gi + 1) * blk] = bias

    @pl.when(first_tile)
    def _():
        k_scr[0:blk, :] = jnp.zeros((blk, k_scr.shape[1]), k_scr.dtype)
        vTe_scr[:, 0:blk] = jnp.zeros((vTe_scr.shape[0], blk), vTe_scr.dtype)

    xb = x_ref[...].astype(_BF16)
    qT_scr[...] = (lax.dot_general(wqT_ref[...], xb, _NT, preferred_element_type=_F32)
                   * q_scale).astype(_BF16)
    k_scr[blk:blk + tm, :] = jnp.dot(xb, wk_ref[...], preferred_element_type=_F32).astype(_BF16)
    vT = lax.dot_general(wvT_ref[...], xb, _NT, preferred_element_type=_F32).astype(_BF16)
    ones_rows = (lax.broadcasted_iota(jnp.int32, (vrows - head_dim, tm), 0) == 0).astype(_BF16)
    for g in range(kv_heads):
        vTe_scr[g * vrows:g * vrows + head_dim, blk:blk + tm] = vT[g * head_dim:(g + 1) * head_dim]
        vTe_scr[g * vrows + head_dim:(g + 1) * vrows, blk:blk + tm] = ones_rows

    first_pen = jnp.where(first_tile, -jnp.inf, 0.0).astype(_F32)
    zeros_q = jnp.zeros((head_dim, group * blk), _BF16)
    heads_per_lane_tile = _LANES // head_dim
    units = [(n, g) for n in range(n_blk) for g in range(kv_heads)]

    def sink_row(g):
        return jnp.concatenate(
            [jnp.full((1, blk), sink_ref[g * group + gi] * _LOG2E, _F32) for gi in range(group)],
            axis=1)

    sinks = [None] * kv_heads

    def scores(u):
        n, g = units[u]
        lt = g // heads_per_lane_tile
        kpair = k_scr[n * blk:(n + 2) * blk, lt * _LANES:(lt + 1) * _LANES]
        qg = jnp.concatenate(
            [qT_scr[(g * group + gi) * head_dim:(g * group + gi + 1) * head_dim,
                    n * blk:(n + 1) * blk] for gi in range(group)], axis=1)
        pieces = [zeros_q] * heads_per_lane_tile
        pieces[g % heads_per_lane_tile] = qg
        qpad = jnp.concatenate(pieces, axis=0)
        s = jnp.dot(kpair, qpad, preferred_element_type=_F32) + biasT_scr[g]
        if n == 0:
            s = jnp.concatenate([s[:blk] + first_pen, s[blk:]], axis=0)
        s_scr[u] = s
        m_scr[u] = jnp.maximum(jnp.max(s, axis=0, keepdims=True), sinks[g])

    def exponentials(u):
        p_scr[u] = jnp.exp2(s_scr[u] - m_scr[u]).astype(_BF16)

    def weighted_values(u):
        n, g = units[u]
        oT = jnp.dot(vTe_scr[g * vrows:(g + 1) * vrows, n * blk:(n + 2) * blk], p_scr[u],
                     preferred_element_type=_F32)
        denom = oT[head_dim:head_dim + 1] + jnp.exp2(sinks[g] - m_scr[u])
        o = (oT[:head_dim] * (1.0 / denom)).astype(_BF16)
        for gi in range(group):
            hq = g * group + gi
            oT_scr[hq * head_dim:(hq + 1) * head_dim, n * blk:(n + 1) * blk] = (
                o[:, gi * blk:(gi + 1) * blk])

    for stage in (scores, exponentials, weighted_values):
        @pl.when(always)
        def _():
            for g in range(kv_heads):
                sinks[g] = sink_row(g)
            for u in range(len(units)):
                stage(u)

    @pl.when(pl.program_id(0) >= 0)
    def _():
        k_scr[0:blk, :] = k_scr[tm:tm + blk, :]
        vTe_scr[:, 0:blk] = vTe_scr[:, tm:tm + blk]

    def mix_rows(rows):
        return lax.dot_general(oT_scr[:, rows], wo_ref[...], _TN, preferred_element_type=_F32)

    _mix_mlp_ln(mix_rows, x_ref, alpha, g1_ref, b1_ref, wup_ref, wdown_ref, g2_ref, b2_ref,
                out_ref, x1_scr, h_scr)


def _t5_bucket_table(blk):
    i = np.arange(blk)[None, :]
    j = np.arange(2 * blk)[:, None]
    dist = i + blk - j
    max_exact = _REL_BUCKETS // 2
    nf = np.maximum(dist, 1).astype(np.float32)
    large = max_exact + (np.log(nf / np.float32(max_exact))
                         / np.float32(math.log(_REL_MAX_DIST / max_exact))
                         * np.float32(_REL_BUCKETS - max_exact)).astype(np.int32)
    large = np.minimum(large, _REL_BUCKETS - 1)
    bucket = np.where(dist < max_exact, np.maximum(dist, 0), large)
    valid = (dist >= 0) & (dist < blk)
    return np.where(valid, bucket, -1).astype(np.int32)


def _rope_tables(seq, rope):
    half = rope // 2
    inv = (np.float32(_ROPE_THETA) ** (-np.arange(half, dtype=np.float32) / np.float32(half)))
    ang = np.arange(seq, dtype=np.float32)[:, None] * inv.astype(np.float32)[None, :]
    cos = np.cos(ang.astype(np.float64)).astype(np.float32)
    sin = np.sin(ang.astype(np.float64)).astype(np.float32)
    cs = np.concatenate([cos, cos, -sin, sin], axis=1)
    return jnp.asarray(cs), jnp.asarray(cos.T.copy()), jnp.asarray(sin.T.copy())


def _row(v):
    return v.reshape(1, -1).astype(_F32)


def kernel(x, mla_w_in, mla_g_q, mla_g_kv, mla_w_uq, mla_w_uk, mla_w_uv, mla_w_o, kv_w_shared,
           swa_w_q, swa_sinks, swa_w_o, rel_bias, mlp_w_up, mlp_w_down, ln_mix_g, ln_mix_b,
           ln_mlp_g, ln_mlp_b):
    B, S, D = x.shape
    depth = mlp_w_up.shape[0]
    assert depth == 2 and mla_w_in.shape[0] == 1 and swa_w_q.shape[0] == 1
    alpha = (2 * depth) ** 0.25
    T = B * S
    tm = _TOKEN_TILE
    ta = _ATTN_TILE
    tk = ta // 2
    bf16_rows = 16
    ta_pad = ta + _LANES
    assert ta % tm == 0 and tm % tk == 0 and S % ta == 0

    q_rank, heads, hd = mla_w_uq.shape[1:]
    kv_rank, _, nope = mla_w_uk.shape[1:]
    vdim = mla_w_uv.shape[3]
    vrows = vdim + bf16_rows
    rope = hd - nope
    half = rope // 2
    assert 2 * rope == _LANES and mla_w_in.shape[2] == q_rank + kv_rank + rope

    w_in = mla_w_in[0]
    r0 = q_rank + kv_rank
    w_in_ext = jnp.concatenate(
        [w_in, w_in[:, r0 + half:r0 + rope], w_in[:, r0:r0 + half]], axis=1).astype(_BF16)
    wqT = mla_w_uq[0].reshape(q_rank, heads * hd).T.astype(_BF16)
    wuk = mla_w_uk[0].reshape(kv_rank, heads * nope).astype(_BF16)
    wvT = mla_w_uv[0].reshape(kv_rank, heads * vdim).T.astype(_BF16)
    cs, cosT, sinT = _rope_tables(S, rope)
    n_t = S // tm
    q_scale = hd ** -0.5 * _LOG2E

    cparams2 = pltpu.CompilerParams(dimension_semantics=("arbitrary", "arbitrary"),
                                    vmem_limit_bytes=_VMEM_LIMIT_BYTES)
    cparams1 = pltpu.CompilerParams(dimension_semantics=("arbitrary",),
                                    vmem_limit_bytes=_VMEM_LIMIT_BYTES)

    qT_all, k_all, vT_all = pl.pallas_call(
        functools.partial(_mla_proj_kernel, q_rank=q_rank, kv_rank=kv_rank, heads=heads,
                          nope=nope, rope=rope, q_scale=q_scale),
        grid=(B, n_t),
        in_specs=[
            pl.BlockSpec((None, tm, D), lambda b, i: (b, i, 0)),
            _const_spec(w_in_ext.shape),
            _const_spec((1, q_rank)),
            _const_spec((1, kv_rank)),
            _const_spec(wqT.shape),
            _const_spec(wuk.shape),
            _const_spec(wvT.shape),
            pl.BlockSpec((tm, 2 * rope), lambda b, i: (i, 0)),
            pl.BlockSpec((half, tm), lambda b, i: (0, i)),
            pl.BlockSpec((half, tm), lambda b, i: (0, i)),
        ],
        out_specs=[
            pl.BlockSpec((None, heads, None, hd, tm), lambda b, i: (b, 0, i, 0, 0)),
            pl.BlockSpec((None, heads, tm, hd), lambda b, i: (b, 0, i, 0)),
            pl.BlockSpec((None, heads, tm // tk, vrows, tk), lambda b, i: (b, 0, i, 0, 0)),
        ],
        out_shape=[
            jax.ShapeDtypeStruct((B, heads, n_t, hd, tm), _BF16),
            jax.ShapeDtypeStruct((B, heads, S, hd), _BF16),
            jax.ShapeDtypeStruct((B, heads, S // tk, vrows, tk), _BF16),
        ],
        compiler_params=cparams2,
        name="mla_proj",
    )(x, w_in_ext, _row(mla_g_q[0]), _row(mla_g_kv[0]), wqT, wuk, wvT, cs, cosT, sinT)

    o = pl.pallas_call(
        _mla_attn_kernel,
        grid=(B, heads),
        in_specs=[
            pl.BlockSpec((None, None, n_t, hd, tm), lambda b, h: (b, h, 0, 0, 0)),
            pl.BlockSpec((None, None, S, hd), lambda b, h: (b, h, 0, 0)),
            pl.BlockSpec((None, None, S // tk, vrows, tk), lambda b, h: (b, h, 0, 0, 0)),
        ],
        out_specs=pl.BlockSpec((None, S, vdim), lambda b, h: (b, 0, h)),
        out_shape=jax.ShapeDtypeStruct((B, S, heads * vdim), _BF16),
        scratch_shapes=[
            pltpu.VMEM((1, ta), _F32),
            pltpu.VMEM((vrows, ta), _F32),
        ] + [pltpu.VMEM((tk, ta_pad), _F32)] * 4
          + [pltpu.VMEM((1, ta), _F32)] * 4
          + [pltpu.VMEM((tk, ta), _BF16)] * 2
          + [pltpu.VMEM((1, ta), _F32)] * 2,
        compiler_params=cparams2,
        name="mla_attn",
    )(qT_all, k_all, vT_all)

    d_ff = mlp_w_up.shape[2]
    w_up_all = mlp_w_up.astype(_BF16)
    w_down_all = mlp_w_down.astype(_BF16)
    mlp_scratch = [
        pltpu.VMEM((tm, D), _F32),
        pltpu.VMEM((tm, d_ff), _BF16),
    ]
    x1 = pl.pallas_call(
        functools.partial(_outproj_mlp_kernel, alpha=alpha),
        grid=(T // tm,),
        in_specs=[
            pl.BlockSpec((tm, heads * vdim), lambda i: (i, 0)),
            pl.BlockSpec((tm, D), lambda i: (i, 0)),
            _const_spec((heads * vdim, D)),
            _const_spec((1, D)),
            _const_spec((1, D)),
            _layer_spec(w_up_all.shape, 0),
            _layer_spec(w_down_all.shape, 0),
            _const_spec((1, D)),
            _const_spec((1, D)),
        ],
        out_specs=pl.BlockSpec((tm, D), lambda i: (i, 0)),
        out_shape=jax.ShapeDtypeStruct((T, D), _F32),
        scratch_shapes=mlp_scratch,
        compiler_params=cparams1,
        name="mla_out_mlp",
    )(o.reshape(T, heads * vdim), x.reshape(T, D), mla_w_o[0].astype(_BF16),
      _row(ln_mix_g[0]), _row(ln_mix_b[0]), w_up_all, w_down_all,
      _row(ln_mlp_g[0]), _row(ln_mlp_b[0]))

    q_heads = swa_sinks.shape[1]
    s_hd = swa_w_q.shape[2] // q_heads
    kv_heads = kv_w_shared.shape[1] // (2 * s_hd)
    group = q_heads // kv_heads
    blk = _SWA_BLOCK
    assert _LANES % s_hd == 0 and tm % blk == 0
    wqT1 = swa_w_q[0].T.astype(_BF16)
    wk1 = kv_w_shared[:, :kv_heads * s_hd].astype(_BF16)
    wvT1 = kv_w_shared[:, kv_heads * s_hd:].T.astype(_BF16)
    bucketT = jnp.asarray(_t5_bucket_table(blk))
    n_units = (tm // blk) * kv_heads

    smem = pl.BlockSpec(memory_space=pltpu.SMEM)
    out = pl.pallas_call(
        functools.partial(_swa_mlp_kernel, alpha=alpha, q_heads=q_heads, kv_heads=kv_heads,
                          head_dim=s_hd, q_scale=s_hd ** -0.5 * _LOG2E),
        grid=(B, n_t),
        in_specs=[
            pl.BlockSpec((None, tm, D), lambda b, i: (b, i, 0)),
            _const_spec(wqT1.shape),
            _const_spec(wk1.shape),
            _const_spec(wvT1.shape),
            _const_spec((q_heads * s_hd, D)),
            _const_spec(bucketT.shape),
            smem,
            smem,
            _const_spec((1, D)),
            _const_spec((1, D)),
            _layer_spec(w_up_all.shape, 1),
            _layer_spec(w_down_all.shape, 1),
            _const_spec((1, D)),
            _const_spec((1, D)),
        ],
        out_specs=pl.BlockSpec((None, tm, D), lambda b, i: (b, i, 0)),
        out_shape=jax.ShapeDtypeStruct((B, S, D), _F32),
        scratch_shapes=[
            pltpu.VMEM((blk + tm, kv_heads * s_hd), _BF16),
            pltpu.VMEM((kv_heads * (s_hd + bf16_rows), blk + tm), _BF16),
            pltpu.VMEM((q_heads * s_hd, tm), _BF16),
            pltpu.VMEM((kv_heads, 2 * blk, group * blk), _F32),
            pltpu.VMEM((n_units, 2 * blk, group * blk), _F32),
            pltpu.VMEM((n_units, 2 * blk, group * blk), _BF16),
            pltpu.VMEM((n_units, 1, group * blk), _F32),
            pltpu.VMEM((q_heads * s_hd, tm), _BF16),
        ] + mlp_scratch,
        compiler_params=cparams2,
        name="swa_mlp",
    )(x1.reshape(B, S, D), wqT1, wk1, wvT1, swa_w_o[0].astype(_BF16), bucketT,
      rel_bias.astype(_F32), swa_sinks[0].astype(_F32),
      _row(ln_mix_g[1]), _row(ln_mix_b[1]), w_up_all, w_down_all,
      _row(ln_mlp_g[1]), _row(ln_mlp_b[1]))
    return out
```

```python
import functools
import math

import numpy as np
import jax
import jax.numpy as jnp
from jax import lax
from jax.experimental import pallas as pl
from jax.experimental.pallas import tpu as pltpu

_F32 = jnp.float32
_BF16 = jnp.bfloat16

_LN_EPS = 1e-5
_RMS_EPS = 1e-6
_ROPE_THETA = 10000.0
_SWA_BLOCK = 128
_REL_BUCKETS = 32
_REL_MAX_DIST = 128
_LOG2E = math.log2(math.e)

_V7X_VMEM_BYTES = 64 * 1024 * 1024
_VMEM_LIMIT_BYTES = _V7X_VMEM_BYTES - 8 * 1024 * 1024
_LANES = 128

_TOKEN_TILE = 512
_ATTN_TILE = 1024
_ATTN_COL_BLOCK = 256
_FF_CHUNK = 1024
_MLP_ROW_BLOCK = 256

_NT = (((1,), (1,)), ((), ()))
_TN = (((0,), (0,)), ((), ()))


def _const_spec(shape):
    nd = len(shape)
    return pl.BlockSpec(shape, lambda *_: (0,) * nd, pipeline_mode=pl.Buffered(1))


def _layer_spec(shape, layer):
    nd = len(shape) - 1
    return pl.BlockSpec((None,) + tuple(shape[1:]), lambda *_: (layer,) + (0,) * nd,
                        pipeline_mode=pl.Buffered(1))


def _layernorm(v, g, b):
    mu = jnp.mean(v, axis=-1, keepdims=True)
    d = v - mu
    var = jnp.mean(d * d, axis=-1, keepdims=True)
    return d * lax.rsqrt(var + _LN_EPS) * g + b


def _rmsnorm(v, g):
    return v * lax.rsqrt(jnp.mean(v * v, axis=-1, keepdims=True) + _RMS_EPS) * g


def _mix_mlp_ln(mix_rows, x_ref, alpha, g1_ref, b1_ref, wup_ref, wdown_ref, g2_ref, b2_ref,
                out_ref, x1_scr, h_scr):
    tm = x_ref.shape[0]
    d_ff = wup_ref.shape[1]
    blocks = [slice(r, r + _MLP_ROW_BLOCK) for r in range(0, tm, _MLP_ROW_BLOCK)]
    for rows in blocks:
        x1_scr[rows, :] = _layernorm(alpha * x_ref[rows, :] + mix_rows(rows), g1_ref[...], b1_ref[...])
    for rows in blocks:
        x1b = x1_scr[rows, :].astype(_BF16)
        for lo in range(0, d_ff, _FF_CHUNK):
            hmid = jnp.dot(x1b, wup_ref[:, lo:lo + _FF_CHUNK], preferred_element_type=_F32)
            hmid = jnp.maximum(hmid, 0.0)
            h_scr[rows, lo:lo + _FF_CHUNK] = (hmid * hmid).astype(_BF16)
    for rows in blocks:
        y = jnp.dot(h_scr[rows, :], wdown_ref[...], preferred_element_type=_F32)
        out_ref[rows, :] = _layernorm(alpha * x1_scr[rows, :] + y, g2_ref[...], b2_ref[...])


def _mla_proj_kernel(x_ref, win_ref, gq_ref, gkv_ref, wqT_ref, wuk_ref, wvT_ref,
                     cs_ref, cosT_ref, sinT_ref, qT_ref, k_ref, vT_ref,
                     *, q_rank, kv_rank, heads, nope, rope, q_scale):
    xb = x_ref[...].astype(_BF16)
    h = jnp.dot(xb, win_ref[...], preferred_element_type=_F32)
    cq = _rmsnorm(h[:, :q_rank], gq_ref[...]).astype(_BF16)
    ckv = _rmsnorm(h[:, q_rank:q_rank + kv_rank], gkv_ref[...]).astype(_BF16)
    t = h[:, q_rank + kv_rank:] * cs_ref[...]
    kr = (t + pltpu.roll(t, rope, 1))[:, :rope].astype(_BF16)

    half = rope // 2
    hd = nope + rope
    qT = lax.dot_general(wqT_ref[...], cq, _NT, preferred_element_type=_F32)
    cosT = cosT_ref[...]
    sinT = sinT_ref[...]
    for hh in range(heads):
        base = hh * hd
        qT_ref[hh, 0:nope, :] = (qT[base:base + nope] * q_scale).astype(_BF16)
        x1 = qT[base + nope:base + nope + half]
        x2 = qT[base + nope + half:base + hd]
        qT_ref[hh, nope:nope + half, :] = ((x1 * cosT - x2 * sinT) * q_scale).astype(_BF16)
        qT_ref[hh, nope + half:hd, :] = ((x1 * sinT + x2 * cosT) * q_scale).astype(_BF16)

    kn = jnp.dot(ckv, wuk_ref[...], preferred_element_type=_F32)
    for hh in range(heads):
        k_ref[hh, :, 0:nope] = kn[:, hh * nope:(hh + 1) * nope].astype(_BF16)
        k_ref[hh, :, nope:hd] = kr

    vdim = wvT_ref.shape[0] // heads
    vT = lax.dot_general(wvT_ref[...], ckv, _NT, preferred_element_type=_F32)
    tk = vT_ref.shape[3]
    pad = vT_ref.shape[2] - vdim
    ones_rows = (lax.broadcasted_iota(jnp.int32, (pad, tk), 0) == 0).astype(_BF16)
    for hh in range(heads):
        for j in range(vT_ref.shape[1]):
            vT_ref[hh, j, 0:vdim, :] = vT[hh * vdim:(hh + 1) * vdim, j * tk:(j + 1) * tk].astype(_BF16)
            vT_ref[hh, j, vdim:vdim + pad, :] = ones_rows


def _mla_attn_kernel(qT_ref, k_ref, vT_ref, o_ref, m_scr, acc_scr,
                     s00, s01, s10, s11, cm00, cm01, cm10, cm11, p_a, p_b, al_a, al_b):
    tk, tq = s00.shape[0], acc_scr.shape[1]
    tsub = qT_ref.shape[2]
    nsub = tq // tsub
    n_q = qT_ref.shape[0] // nsub
    assert tq == 2 * tk and tk == vT_ref.shape[2]

    cb = _ATTN_COL_BLOCK


    def aligned(v, m):
        return v if isinstance(v, int) else pl.multiple_of(v, m)

    def live_rows(mask_off, lo):
        return tk if mask_off is None else min(tk, lo + cb - mask_off)

    def q_stage(qi, c, s_ref, cm_ref, mask_off=None, col_lo=0):
        k = k_ref[pl.ds(aligned(c * tk, tk), tk), :]
        for lo in range(col_lo, tq, cb):
            live = live_rows(mask_off, lo)
            j, off = divmod(lo, tsub)
            qT = qT_ref[qi * nsub + j, :, off:off + cb]
            s = jnp.dot(k[0:live], qT, preferred_element_type=_F32)
            s_ref[0:live, lo:lo + cb] = s
            cm_ref[:, lo:lo + cb] = jnp.max(s, axis=0, keepdims=True)


    def x_stage(s_ref, cm_ref, p_ref, al_ref, mask_off=None, col_lo=0, fresh=False):
        for lo in range(col_lo, tq, cb):
            cols = slice(lo, lo + cb)
            live = live_rows(mask_off, lo)
            s = s_ref[0:live, cols]
            if mask_off is None or mask_off + tk - 1 <= lo:
                cm = cm_ref[:, cols]
            else:
                full = max(0, lo - mask_off)
                part = s[full:live]
                krel = lax.broadcasted_iota(jnp.int32, part.shape, 0) + (mask_off + full)
                qrel = lax.broadcasted_iota(jnp.int32, part.shape, 1) + lo
                part = jnp.where(krel <= qrel, part, -jnp.inf)
                s = part if full == 0 else jnp.concatenate([s[0:full], part], axis=0)
                cm = jnp.max(s, axis=0, keepdims=True)
            if fresh:
                m_new = cm
            else:
                m_prev = m_scr[:, cols]
                m_new = jnp.maximum(m_prev, cm)
                al_ref[:, cols] = jnp.exp2(m_prev - m_new)
            p = jnp.exp2(s - m_new)
            p_ref[0:live, cols] = p.astype(_BF16)
            m_scr[:, cols] = m_new

    def v_stage(c, p_ref, al_ref, mask_off=None, col_lo=0, fresh=False):
        for lo in range(col_lo, tq, cb):
            cols = slice(lo, lo + cb)
            live = live_rows(mask_off, lo)
            pv = jnp.dot(vT_ref[c, :, 0:live], p_ref[0:live, cols], preferred_element_type=_F32)
            acc_scr[:, cols] = pv if fresh else al_ref[:, cols] * acc_scr[:, cols] + pv

    s_bufs = (((s00, cm00), (s01, cm01)), ((s10, cm10), (s11, cm11)))


    def regular_step(qi, a, par, first=False, pre_diag=False):
        (sa, cma), (sb, cmb) = s_bufs[par]
        (na, ncma), (nb, ncmb) = s_bufs[1 - par]
        if first:
            x_stage(sa, cma, p_a, al_a, fresh=True)
            q_stage(qi, a + 2, na, ncma, mask_off=0 if pre_diag else None)
            v_stage(a, p_a, al_a, fresh=True)
            x_stage(sb, cmb, p_b, al_b)
            q_stage(qi, a + 3, nb, ncmb, mask_off=tk if pre_diag else None,
                    col_lo=tk if pre_diag else 0)
            return
        q_stage(qi, a + 2, na, ncma, mask_off=0 if pre_diag else None)
        x_stage(sa, cma, p_a, al_a)
        v_stage(a - 1, p_b, al_b)
        q_stage(qi, a + 3, nb, ncmb, mask_off=tk if pre_diag else None,
                col_lo=tk if pre_diag else 0)
        x_stage(sb, cmb, p_b, al_b)
        v_stage(a, p_a, al_a)

    def diagonal_step(qi, par, first=False, last=False):
        a = 2 * qi
        (sa, cma), (sb, cmb) = s_bufs[par]
        (na, ncma), (nb, ncmb) = s_bufs[1 - par]
        if not last:
            q_stage(qi + 1, 0, na, ncma)
        x_stage(sa, cma, p_a, al_a, mask_off=0, fresh=first)
        if not first:
            v_stage(a - 1, p_b, al_b)
        if not last:
            q_stage(qi + 1, 1, nb, ncmb)
        x_stage(sb, cmb, p_b, al_b, mask_off=tk, col_lo=tk)
        v_stage(a, p_a, al_a, mask_off=0, fresh=first)
        v_stage(a + 1, p_b, al_b, mask_off=tk, col_lo=tk)
        vdim = o_ref.shape[1]
        inv_l = 1.0 / acc_scr[vdim:vdim + 1, :]
        o = (acc_scr[0:vdim, :] * inv_l).T
        o_ref[pl.ds(aligned(qi * tq, tq), tq), :] = o.astype(o_ref.dtype)

    def by_parity(par, fn):
        if isinstance(par, int):
            fn(par)
            return
        for static_par in range(2):
            @pl.when(par == static_par)
            def _():
                fn(static_par)

    def start_parity(qi):
        return (qi * (qi + 1) // 2) % 2

    def tile(qi, last=False):
        par = start_parity(qi)
        if isinstance(qi, int) and qi == 1:
            by_parity(par, functools.partial(regular_step, qi, 0, first=True, pre_diag=True))
        else:
            def step_body(u, p):
                by_parity(p, functools.partial(regular_step, qi, 2 * u))
                return 1 - p

            by_parity(par, functools.partial(regular_step, qi, 0, first=True))
            lax.fori_loop(1, qi - 1, step_body, jnp.int32(1) - par)
            by_parity((par + qi - 1) % 2,
                      functools.partial(regular_step, qi, 2 * (qi - 1), pre_diag=True))
        by_parity((par + qi) % 2, functools.partial(diagonal_step, qi, last=last))

    def tile_body(qi, carry):
        tile(qi)
        return carry

    q_stage(0, 0, s00, cm00, mask_off=0)
    q_stage(0, 1, s01, cm01, mask_off=tk, col_lo=tk)
    diagonal_step(0, 0, first=True, last=(n_q == 1))
    if n_q > 2:
        tile(1)
    if n_q > 3:
        lax.fori_loop(2, n_q - 1, tile_body, jnp.int32(0))
    if n_q > 1:
        tile(n_q - 1, last=True)


def _outproj_mlp_kernel(o_ref, x_ref, wo_ref, g1_ref, b1_ref, wup_ref, wdown_ref,
                        g2_ref, b2_ref, out_ref, x1_scr, h_scr, *, alpha):
    def mix_rows(rows):
        return jnp.dot(o_ref[rows, :], wo_ref[...], preferred_element_type=_F32)

    _mix_mlp_ln(mix_rows, x_ref, alpha, g1_ref, b1_ref, wup_ref, wdown_ref, g2_ref, b2_ref,
                out_ref, x1_scr, h_scr)


def _swa_mlp_kernel(x_ref, wqT_ref, wk_ref, wvT_ref, wo_ref, bucketT_ref, relb_ref, sink_ref,
                    g1_ref, b1_ref, wup_ref, wdown_ref, g2_ref, b2_ref, out_ref,
                    k_scr, vTe_scr, qT_scr, biasT_scr, s_scr, p_scr, m_scr, oT_scr, x1_scr, h_scr,
                    *, alpha, q_heads, kv_heads, head_dim, q_scale):
    blk = _SWA_BLOCK
    group = q_heads // kv_heads
    tm = x_ref.shape[0]
    n_blk = tm // blk
    vrows = vTe_scr.shape[0] // kv_heads
    first_tile = pl.program_id(1) == 0
    always = pl.program_id(1) >= 0

    @pl.when((pl.program_id(0) == 0) & first_tile)
    def _():
        bkt = bucketT_ref[...]
        for hq in range(q_heads):
            bias = jnp.full(bkt.shape, -jnp.inf, _F32)
            for bb in range(_REL_BUCKETS):
                bias = jnp.where(bkt == bb, relb_ref[bb, hq] * _LOG2E, bias)
            g, gi = divmod(hq, group)
            biasT_scr[g, :, gi * blk:(gi + 1) * blk] = bias

    @pl.when(first_tile)
    def _():
        k_scr[0:blk, :] = jnp.zeros((blk, k_scr.shape[1]), k_scr.dtype)
        vTe_scr[:, 0:blk] = jnp.zeros((vTe_scr.shape[0], blk), vTe_scr.dtype)

    xb = x_ref[...].astype(_BF16)
    qT_scr[...] = (lax.dot_general(wqT_ref[...], xb, _NT, preferred_element_type=_F32)
                   * q_scale).astype(_BF16)
    k_scr[blk:blk + tm, :] = jnp.dot(xb, wk_ref[...], preferred_element_type=_F32).astype(_BF16)
    vT = lax.dot_general(wvT_ref[...], xb, _NT, preferred_element_type=_F32).astype(_BF16)
    ones_rows = (lax.broadcasted_iota(jnp.int32, (vrows - head_dim, tm), 0) == 0).astype(_BF16)
    for g in range(kv_heads):
        vTe_scr[g * vrows:g * vrows + head_dim, blk:blk + tm] = vT[g * head_dim:(g + 1) * head_dim]
        vTe_scr[g * vrows + head_dim:(g + 1) * vrows, blk:blk + tm] = ones_rows

    first_pen = jnp.where(first_tile, -jnp.inf, 0.0).astype(_F32)
    zeros_q = jnp.zeros((head_dim, group * blk), _BF16)
    heads_per_lane_tile = _LANES // head_dim
    units = [(n, g) for n in range(n_blk) for g in range(kv_heads)]

    def sink_row(g):
        return jnp.concatenate(
            [jnp.full((1, blk), sink_ref[g * group + gi] * _LOG2E, _F32) for gi in range(group)],
            axis=1)

    sinks = [None] * kv_heads

    def scores(u):
        n, g = units[u]
        lt = g // heads_per_lane_tile
        kpair = k_scr[n * blk:(n + 2) * blk, lt * _LANES:(lt + 1) * _LANES]
        qg = jnp.concatenate(
            [qT_scr[(g * group + gi) * head_dim:(g * group + gi + 1) * head_dim,
                    n * blk:(n + 1) * blk] for gi in range(group)], axis=1)
        pieces = [zeros_q] * heads_per_lane_tile
        pieces[g % heads_per_lane_tile] = qg
        qpad = jnp.concatenate(pieces, axis=0)
        s = jnp.dot(kpair, qpad, preferred_element_type=_F32) + biasT_scr[g]
        if n == 0:
            s = jnp.concatenate([s[:blk] + first_pen, s[blk:]], axis=0)
        s_scr[u] = s
        m_scr[u] = jnp.maximum(jnp.max(s, axis=0, keepdims=True), sinks[g])

    def exponentials(u):
        p_scr[u] = jnp.exp2(s_scr[u] - m_scr[u]).astype(_BF16)

    def weighted_values(u):
        n, g = units[u]
        oT = jnp.dot(vTe_scr[g * vrows:(g + 1) * vrows, n * blk:(n + 2) * blk], p_scr[u],
                     preferred_element_type=_F32)
        denom = oT[head_dim:head_dim + 1] + jnp.exp2(sinks[g] - m_scr[u])
        o = (oT[:head_dim] * (1.0 / denom)).astype(_BF16)
        for gi in range(group):
            hq = g * group + gi
            oT_scr[hq * head_dim:(hq + 1) * head_dim, n * blk:(n + 1) * blk] = (
                o[:, gi * blk:(gi + 1) * blk])

    for stage in (scores, exponentials, weighted_values):
        @pl.when(always)
        def _():
            for g in range(kv_heads):
                sinks[g] = sink_row(g)
            for u in range(len(units)):
                stage(u)

    @pl.when(pl.program_id(0) >= 0)
    def _():
        k_scr[0:blk, :] = k_scr[tm:tm + blk, :]
        vTe_scr[:, 0:blk] = vTe_scr[:, tm:tm + blk]

    def mix_rows(rows):
        return lax.dot_general(oT_scr[:, rows], wo_ref[...], _TN, preferred_element_type=_F32)

    _mix_mlp_ln(mix_rows, x_ref, alpha, g1_ref, b1_ref, wup_ref, wdown_ref, g2_ref, b2_ref,
                out_ref, x1_scr, h_scr)


def _t5_bucket_table(blk):
    i = np.arange(blk)[None, :]
    j = np.arange(2 * blk)[:, None]
    dist = i + blk - j
    max_exact = _REL_BUCKETS // 2
    nf = np.maximum(dist, 1).astype(np.float32)
    large = max_exact + (np.log(nf / np.float32(max_exact))
                         / np.float32(math.log(_REL_MAX_DIST / max_exact))
                         * np.float32(_REL_BUCKETS - max_exact)).astype(np.int32)
    large = np.minimum(large, _REL_BUCKETS - 1)
    bucket = np.where(dist < max_exact, np.maximum(dist, 0), large)
    valid = (dist >= 0) & (dist < blk)
    return np.where(valid, bucket, -1).astype(np.int32)


def _rope_tables(seq, rope):
    half = rope // 2
    inv = (np.float32(_ROPE_THETA) ** (-np.arange(half, dtype=np.float32) / np.float32(half)))
    ang = np.arange(seq, dtype=np.float32)[:, None] * inv.astype(np.float32)[None, :]
    cos = np.cos(ang.astype(np.float64)).astype(np.float32)
    sin = np.sin(ang.astype(np.float64)).astype(np.float32)
    cs = np.concatenate([cos, cos, -sin, sin], axis=1)
    return jnp.asarray(cs), jnp.asarray(cos.T.copy()), jnp.asarray(sin.T.copy())


def _row(v):
    return v.reshape(1, -1).astype(_F32)


def kernel(x, mla_w_in, mla_g_q, mla_g_kv, mla_w_uq, mla_w_uk, mla_w_uv, mla_w_o, kv_w_shared,
           swa_w_q, swa_sinks, swa_w_o, rel_bias, mlp_w_up, mlp_w_down, ln_mix_g, ln_mix_b,
           ln_mlp_g, ln_mlp_b):
    B, S, D = x.shape
    depth = mlp_w_up.shape[0]
    assert depth == 2 and mla_w_in.shape[0] == 1 and swa_w_q.shape[0] == 1
    alpha = (2 * depth) ** 0.25
    T = B * S
    tm = _TOKEN_TILE
    ta = _ATTN_TILE
    tk = ta // 2
    bf16_rows = 16
    ta_pad = ta + _LANES
    assert ta % tm == 0 and tm % tk == 0 and S % ta == 0

    q_rank, heads, hd = mla_w_uq.shape[1:]
    kv_rank, _, nope = mla_w_uk.shape[1:]
    vdim = mla_w_uv.shape[3]
    vrows = vdim + bf16_rows
    rope = hd - nope
    half = rope // 2
    assert 2 * rope == _LANES and mla_w_in.shape[2] == q_rank + kv_rank + rope

    w_in = mla_w_in[0]
    r0 = q_rank + kv_rank
    w_in_ext = jnp.concatenate(
        [w_in, w_in[:, r0 + half:r0 + rope], w_in[:, r0:r0 + half]], axis=1).astype(_BF16)
    wqT = mla_w_uq[0].reshape(q_rank, heads * hd).T.astype(_BF16)
    wuk = mla_w_uk[0].reshape(kv_rank, heads * nope).astype(_BF16)
    wvT = mla_w_uv[0].reshape(kv_rank, heads * vdim).T.astype(_BF16)
    cs, cosT, sinT = _rope_tables(S, rope)
    n_t = S // tm
    q_scale = hd ** -0.5 * _LOG2E

    cparams2 = pltpu.CompilerParams(dimension_semantics=("arbitrary", "arbitrary"),
                                    vmem_limit_bytes=_VMEM_LIMIT_BYTES)
    cparams1 = pltpu.CompilerParams(dimension_semantics=("arbitrary",),
                                    vmem_limit_bytes=_VMEM_LIMIT_BYTES)

    qT_all, k_all, vT_all = pl.pallas_call(
        functools.partial(_mla_proj_kernel, q_rank=q_rank, kv_rank=kv_rank, heads=heads,
                          nope=nope, rope=rope, q_scale=q_scale),
        grid=(B, n_t),
        in_specs=[
            pl.BlockSpec((None, tm, D), lambda b, i: (b, i, 0)),
            _const_spec(w_in_ext.shape),
            _const_spec((1, q_rank)),
            _const_spec((1, kv_rank)),
            _const_spec(wqT.shape),
            _const_spec(wuk.shape),
            _const_spec(wvT.shape),
            pl.BlockSpec((tm, 2 * rope), lambda b, i: (i, 0)),
            pl.BlockSpec((half, tm), lambda b, i: (0, i)),
            pl.BlockSpec((half, tm), lambda b, i: (0, i)),
        ],
        out_specs=[
            pl.BlockSpec((None, heads, None, hd, tm), lambda b, i: (b, 0, i, 0, 0)),
            pl.BlockSpec((None, heads, tm, hd), lambda b, i: (b, 0, i, 0)),
            pl.BlockSpec((None, heads, tm // tk, vrows, tk), lambda b, i: (b, 0, i, 0, 0)),
        ],
        out_shape=[
            jax.ShapeDtypeStruct((B, heads, n_t, hd, tm), _BF16),
            jax.ShapeDtypeStruct((B, heads, S, hd), _BF16),
            jax.ShapeDtypeStruct((B, heads, S // tk, vrows, tk), _BF16),
        ],
        compiler_params=cparams2,
        name="mla_proj",
    )(x, w_in_ext, _row(mla_g_q[0]), _row(mla_g_kv[0]), wqT, wuk, wvT, cs, cosT, sinT)

    o = pl.pallas_call(
        _mla_attn_kernel,
        grid=(B, heads),
        in_specs=[
            pl.BlockSpec((None, None, n_t, hd, tm), lambda b, h: (b, h, 0, 0, 0)),
            pl.BlockSpec((None, None, S, hd), lambda b, h: (b, h, 0, 0)),
            pl.BlockSpec((None, None, S // tk, vrows, tk), lambda b, h: (b, h, 0, 0, 0)),
        ],
        out_specs=pl.BlockSpec((None, S, vdim), lambda b, h: (b, 0, h)),
        out_shape=jax.ShapeDtypeStruct((B, S, heads * vdim), _BF16),
        scratch_shapes=[
            pltpu.VMEM((1, ta), _F32),
            pltpu.VMEM((vrows, ta), _F32),
        ] + [pltpu.VMEM((tk, ta_pad), _F32)] * 4
          + [pltpu.VMEM((1, ta), _F32)] * 4
          + [pltpu.VMEM((tk, ta), _BF16)] * 2
          + [pltpu.VMEM((1, ta), _F32)] * 2,
        compiler_params=cparams2,
        name="mla_attn",
    )(qT_all, k_all, vT_all)

    d_ff = mlp_w_up.shape[2]
    w_up_all = mlp_w_up.astype(_BF16)
    w_down_all = mlp_w_down.astype(_BF16)
    mlp_scratch = [
        pltpu.VMEM((tm, D), _F32),
        pltpu.VMEM((tm, d_ff), _BF16),
    ]
    x1 = pl.pallas_call(
        functools.partial(_outproj_mlp_kernel, alpha=alpha),
        grid=(T // tm,),
        in_specs=[
            pl.BlockSpec((tm, heads * vdim), lambda i: (i, 0)),
            pl.BlockSpec((tm, D), lambda i: (i, 0)),
            _const_spec((heads * vdim, D)),
            _const_spec((1, D)),
            _const_spec((1, D)),
            _layer_spec(w_up_all.shape, 0),
            _layer_spec(w_down_all.shape, 0),
            _const_spec((1, D)),
            _const_spec((1, D)),
        ],
        out_specs=pl.BlockSpec((tm, D), lambda i: (i, 0)),
        out_shape=jax.ShapeDtypeStruct((T, D), _F32),
        scratch_shapes=mlp_scratch,
        compiler_params=cparams1,
        name="mla_out_mlp",
    )(o.reshape(T, heads * vdim), x.reshape(T, D), mla_w_o[0].astype(_BF16),
      _row(ln_mix_g[0]), _row(ln_mix_b[0]), w_up_all, w_down_all,
      _row(ln_mlp_g[0]), _row(ln_mlp_b[0]))

    q_heads = swa_sinks.shape[1]
    s_hd = swa_w_q.shape[2] // q_heads
    kv_heads = kv_w_shared.shape[1] // (2 * s_hd)
    group = q_heads // kv_heads
    blk = _SWA_BLOCK
    assert _LANES % s_hd == 0 and tm % blk == 0
    wqT1 = swa_w_q[0].T.astype(_BF16)
    wk1 = kv_w_shared[:, :kv_heads * s_hd].astype(_BF16)
    wvT1 = kv_w_shared[:, kv_heads * s_hd:].T.astype(_BF16)
    bucketT = jnp.asarray(_t5_bucket_table(blk))
    n_units = (tm // blk) * kv_heads

    smem = pl.BlockSpec(memory_space=pltpu.SMEM)
    out = pl.pallas_call(
        functools.partial(_swa_mlp_kernel, alpha=alpha, q_heads=q_heads, kv_heads=kv_heads,
                          head_dim=s_hd, q_scale=s_hd ** -0.5 * _LOG2E),
        grid=(B, n_t),
        in_specs=[
            pl.BlockSpec((None, tm, D), lambda b, i: (b, i, 0)),
            _const_spec(wqT1.shape),
            _const_spec(wk1.shape),
            _const_spec(wvT1.shape),
            _const_spec((q_heads * s_hd, D)),
            _const_spec(bucketT.shape),
            smem,
            smem,
            _const_spec((1, D)),
            _const_spec((1, D)),
            _layer_spec(w_up_all.shape, 1),
            _layer_spec(w_down_all.shape, 1),
            _const_spec((1, D)),
            _const_spec((1, D)),
        ],
        out_specs=pl.BlockSpec((None, tm, D), lambda b, i: (b, i, 0)),
        out_shape=jax.ShapeDtypeStruct((B, S, D), _F32),
        scratch_shapes=[
            pltpu.VMEM((blk + tm, kv_heads * s_hd), _BF16),
            pltpu.VMEM((kv_heads * (s_hd + bf16_rows), blk + tm), _BF16),
            pltpu.VMEM((q_heads * s_hd, tm), _BF16),
            pltpu.VMEM((kv_heads, 2 * blk, group * blk), _F32),
            pltpu.VMEM((n_units, 2 * blk, group * blk), _F32),
            pltpu.VMEM((n_units, 2 * blk, group * blk), _BF16),
            pltpu.VMEM((n_units, 1, group * blk), _F32),
            pltpu.VMEM((q_heads * s_hd, tm), _BF16),
        ] + mlp_scratch,
        compiler_params=cparams2,
        name="swa_mlp",
    )(x1.reshape(B, S, D), wqT1, wk1, wvT1, swa_w_o[0].astype(_BF16), bucketT,
      rel_bias.astype(_F32), swa_sinks[0].astype(_F32),
      _row(ln_mix_g[1]), _row(ln_mix_b[1]), w_up_all, w_down_all,
      _row(ln_mlp_g[1]), _row(ln_mlp_b[1]))
    return out
```

```python
import functools
import math

import numpy as np
import jax
import jax.numpy as jnp
from jax import lax
from jax.experimental import pallas as pl
from jax.experimental.pallas import tpu as pltpu

_F32 = jnp.float32
_BF16 = jnp.bfloat16

_LN_EPS = 1e-5
_RMS_EPS = 1e-6
_ROPE_THETA = 10000.0
_SWA_BLOCK = 128
_REL_BUCKETS = 32
_REL_MAX_DIST = 128
_LOG2E = math.log2(math.e)

_V7X_VMEM_BYTES = 64 * 1024 * 1024
_VMEM_LIMIT_BYTES = _V7X_VMEM_BYTES - 8 * 1024 * 1024
_LANES = 128

_TOKEN_TILE = 512
_ATTN_TILE = 1024
_ATTN_COL_BLOCK = 256
_FF_CHUNK = 1024
_MLP_ROW_BLOCK = 256

_NT = (((1,), (1,)), ((), ()))
_TN = (((0,), (0,)), ((), ()))


def _const_spec(shape):
    nd = len(shape)
    return pl.BlockSpec(shape, lambda *_: (0,) * nd, pipeline_mode=pl.Buffered(1))


def _layer_spec(shape, layer):
    nd = len(shape) - 1
    return pl.BlockSpec((None,) + tuple(shape[1:]), lambda *_: (layer,) + (0,) * nd,
                        pipeline_mode=pl.Buffered(1))


def _layernorm(v, g, b):
    mu = jnp.mean(v, axis=-1, keepdims=True)
    d = v - mu
    var = jnp.mean(d * d, axis=-1, keepdims=True)
    return d * lax.rsqrt(var + _LN_EPS) * g + b


def _rmsnorm(v, g):
    return v * lax.rsqrt(jnp.mean(v * v, axis=-1, keepdims=True) + _RMS_EPS) * g


def _mix_mlp_ln(mix_rows, x_ref, alpha, g1_ref, b1_ref, wup_ref, wdown_ref, g2_ref, b2_ref,
                out_ref, x1_scr, h_scr):
    tm = x_ref.shape[0]
    d_ff = wup_ref.shape[1]
    blocks = [slice(r, r + _MLP_ROW_BLOCK) for r in range(0, tm, _MLP_ROW_BLOCK)]
    for rows in blocks:
        x1_scr[rows, :] = _layernorm(alpha * x_ref[rows, :] + mix_rows(rows), g1_ref[...], b1_ref[...])
    for rows in blocks:
        x1b = x1_scr[rows, :].astype(_BF16)
        for lo in range(0, d_ff, _FF_CHUNK):
            hmid = jnp.dot(x1b, wup_ref[:, lo:lo + _FF_CHUNK], preferred_element_type=_F32)
            hmid = jnp.maximum(hmid, 0.0)
            h_scr[rows, lo:lo + _FF_CHUNK] = (hmid * hmid).astype(_BF16)
    for rows in blocks:
        y = jnp.dot(h_scr[rows, :], wdown_ref[...], preferred_element_type=_F32)
        out_ref[rows, :] = _layernorm(alpha * x1_scr[rows, :] + y, g2_ref[...], b2_ref[...])


def _mla_proj_kernel(x_ref, win_ref, gq_ref, gkv_ref, wqT_ref, wuk_ref, wvT_ref,
                     cs_ref, cosT_ref, sinT_ref, qT_ref, k_ref, vT_ref,
                     *, q_rank, kv_rank, heads, nope, rope, q_scale):
    xb = x_ref[...].astype(_BF16)
    h = jnp.dot(xb, win_ref[...], preferred_element_type=_F32)
    cq = _rmsnorm(h[:, :q_rank], gq_ref[...]).astype(_BF16)
    ckv = _rmsnorm(h[:, q_rank:q_rank + kv_rank], gkv_ref[...]).astype(_BF16)
    t = h[:, q_rank + kv_rank:] * cs_ref[...]
    kr = (t + pltpu.roll(t, rope, 1))[:, :rope].astype(_BF16)

    half = rope // 2
    hd = nope + rope
    qT = lax.dot_general(wqT_ref[...], cq, _NT, preferred_element_type=_F32)
    cosT = cosT_ref[...]
    sinT = sinT_ref[...]
    for hh in range(heads):
        base = hh * hd
        qT_ref[hh, 0:nope, :] = (qT[base:base + nope] * q_scale).astype(_BF16)
        x1 = qT[base + nope:base + nope + half]
        x2 = qT[base + nope + half:base + hd]
        qT_ref[hh, nope:nope + half, :] = ((x1 * cosT - x2 * sinT) * q_scale).astype(_BF16)
        qT_ref[hh, nope + half:hd, :] = ((x1 * sinT + x2 * cosT) * q_scale).astype(_BF16)

    kn = jnp.dot(ckv, wuk_ref[...], preferred_element_type=_F32)
    for hh in range(heads):
        k_ref[hh, :, 0:nope] = kn[:, hh * nope:(hh + 1) * nope].astype(_BF16)
        k_ref[hh, :, nope:hd] = kr

    vdim = wvT_ref.shape[0] // heads
    vT = lax.dot_general(wvT_ref[...], ckv, _NT, preferred_element_type=_F32)
    tk = vT_ref.shape[3]
    pad = vT_ref.shape[2] - vdim
    ones_rows = (lax.broadcasted_iota(jnp.int32, (pad, tk), 0) == 0).astype(_BF16)
    for hh in range(heads):
        for j in range(vT_ref.shape[1]):
            vT_ref[hh, j, 0:vdim, :] = vT[hh * vdim:(hh + 1) * vdim, j * tk:(j + 1) * tk].astype(_BF16)
            vT_ref[hh, j, vdim:vdim + pad, :] = ones_rows


def _mla_attn_kernel(qT_ref, k_ref, vT_ref, o_ref, m_scr, acc_scr,
                     s00, s01, s10, s11, cm00, cm01, cm10, cm11, p_a, p_b, al_a, al_b):
    tk, tq = s00.shape[0], acc_scr.shape[1]
    tsub = qT_ref.shape[2]
    nsub = tq // tsub
    n_q = qT_ref.shape[0] // nsub
    assert tq == 2 * tk and tk == vT_ref.shape[2]

    cb = _ATTN_COL_BLOCK


    def aligned(v, m):
        return v if isinstance(v, int) else pl.multiple_of(v, m)

    def live_rows(mask_off, lo):
        return tk if mask_off is None else min(tk, lo + cb - mask_off)

    def q_stage(qi, c, s_ref, cm_ref, mask_off=None, col_lo=0):
        for lo in range(col_lo, tq, cb):
            live = live_rows(mask_off, lo)
            j, off = divmod(lo, tsub)
            qT = qT_ref[qi * nsub + j, :, off:off + cb]
            k = k_ref[pl.ds(aligned(c * tk, tk), live), :]
            s = jnp.dot(k, qT, preferred_element_type=_F32)
            s_ref[0:live, lo:lo + cb] = s
            cm_ref[:, lo:lo + cb] = jnp.max(s, axis=0, keepdims=True)

    def x_stage(s_ref, cm_ref, p_ref, al_ref, mask_off=None, col_lo=0):
        for lo in range(col_lo, tq, cb):
            cols = slice(lo, lo + cb)
            live = live_rows(mask_off, lo)
            s = s_ref[0:live, cols]
            if mask_off is None or mask_off + tk - 1 <= lo:
                cm = cm_ref[:, cols]
            else:
                full = max(0, lo - mask_off)
                part = s[full:live]
                krel = lax.broadcasted_iota(jnp.int32, part.shape, 0) + (mask_off + full)
                qrel = lax.broadcasted_iota(jnp.int32, part.shape, 1) + lo
                part = jnp.where(krel <= qrel, part, -jnp.inf)
                s = part if full == 0 else jnp.concatenate([s[0:full], part], axis=0)
                cm = jnp.max(s, axis=0, keepdims=True)
            m_prev = m_scr[:, cols]
            m_new = jnp.maximum(m_prev, cm)
            alpha = jnp.exp2(m_prev - m_new)
            p = jnp.exp2(s - m_new)
            p_ref[0:live, cols] = p.astype(_BF16)
            al_ref[:, cols] = alpha
            m_scr[:, cols] = m_new

    def v_stage(c, p_ref, al_ref, mask_off=None, col_lo=0):
        for lo in range(col_lo, tq, cb):
            cols = slice(lo, lo + cb)
            live = live_rows(mask_off, lo)
            acc_scr[:, cols] = al_ref[:, cols] * acc_scr[:, cols] + jnp.dot(
                vT_ref[c, :, 0:live], p_ref[0:live, cols], preferred_element_type=_F32)

    s_bufs = (((s00, cm00), (s01, cm01)), ((s10, cm10), (s11, cm11)))

    def start_tile():
        m_scr[...] = jnp.full(m_scr.shape, -jnp.inf, _F32)
        acc_scr[...] = jnp.zeros(acc_scr.shape, _F32)


    def regular_step(qi, a, par, first=False, pre_diag=False):
        (sa, cma), (sb, cmb) = s_bufs[par]
        (na, ncma), (nb, ncmb) = s_bufs[1 - par]
        if first:
            x_stage(sa, cma, p_a, al_a)
            q_stage(qi, a + 2, na, ncma, mask_off=0 if pre_diag else None)
            v_stage(a, p_a, al_a)
            x_stage(sb, cmb, p_b, al_b)
            q_stage(qi, a + 3, nb, ncmb, mask_off=tk if pre_diag else None,
                    col_lo=tk if pre_diag else 0)
            return
        q_stage(qi, a + 2, na, ncma, mask_off=0 if pre_diag else None)
        x_stage(sa, cma, p_a, al_a)
        v_stage(a - 1, p_b, al_b)
        q_stage(qi, a + 3, nb, ncmb, mask_off=tk if pre_diag else None,
                col_lo=tk if pre_diag else 0)
        x_stage(sb, cmb, p_b, al_b)
        v_stage(a, p_a, al_a)

    def diagonal_step(qi, par, first=False, last=False):
        a = 2 * qi
        (sa, cma), (sb, cmb) = s_bufs[par]
        (na, ncma), (nb, ncmb) = s_bufs[1 - par]
        if not last:
            q_stage(qi + 1, 0, na, ncma)
        x_stage(sa, cma, p_a, al_a, mask_off=0)
        if not first:
            v_stage(a - 1, p_b, al_b)
        if not last:
            q_stage(qi + 1, 1, nb, ncmb)
        x_stage(sb, cmb, p_b, al_b, mask_off=tk, col_lo=tk)
        v_stage(a, p_a, al_a, mask_off=0)
        v_stage(a + 1, p_b, al_b, mask_off=tk, col_lo=tk)
        vdim = o_ref.shape[1]
        inv_l = 1.0 / acc_scr[vdim:vdim + 1, :]
        o = (acc_scr[0:vdim, :] * inv_l).T
        o_ref[pl.ds(aligned(qi * tq, tq), tq), :] = o.astype(o_ref.dtype)
        if not last:
            start_tile()

    def by_parity(par, fn):
        if isinstance(par, int):
            fn(par)
            return
        for static_par in range(2):
            @pl.when(par == static_par)
            def _():
                fn(static_par)

    def start_parity(qi):
        return (qi * (qi + 1) // 2) % 2

    def tile(qi, last=False):
        par = start_parity(qi)
        if isinstance(qi, int) and qi == 1:
            by_parity(par, functools.partial(regular_step, qi, 0, first=True, pre_diag=True))
        else:
            def step_body(u, p):
                by_parity(p, functools.partial(regular_step, qi, 2 * u))
                return 1 - p

            by_parity(par, functools.partial(regular_step, qi, 0, first=True))
            lax.fori_loop(1, qi - 1, step_body, jnp.int32(1) - par)
            by_parity((par + qi - 1) % 2,
                      functools.partial(regular_step, qi, 2 * (qi - 1), pre_diag=True))
        by_parity((par + qi) % 2, functools.partial(diagonal_step, qi, last=last))

    def tile_body(qi, carry):
        tile(qi)
        return carry

    start_tile()
    q_stage(0, 0, s00, cm00, mask_off=0)
    q_stage(0, 1, s01, cm01, mask_off=tk, col_lo=tk)
    diagonal_step(0, 0, first=True, last=(n_q == 1))
    if n_q > 2:
        tile(1)
    if n_q > 3:
        lax.fori_loop(2, n_q - 1, tile_body, jnp.int32(0))
    if n_q > 1:
        tile(n_q - 1, last=True)


def _outproj_mlp_kernel(o_ref, x_ref, wo_ref, g1_ref, b1_ref, wup_ref, wdown_ref,
                        g2_ref, b2_ref, out_ref, x1_scr, h_scr, *, alpha):
    def mix_rows(rows):
        return jnp.dot(o_ref[rows, :], wo_ref[...], preferred_element_type=_F32)

    _mix_mlp_ln(mix_rows, x_ref, alpha, g1_ref, b1_ref, wup_ref, wdown_ref, g2_ref, b2_ref,
                out_ref, x1_scr, h_scr)


def _swa_mlp_kernel(x_ref, wqT_ref, wk_ref, wvT_ref, wo_ref, bucketT_ref, relb_ref, sink_ref,
                    g1_ref, b1_ref, wup_ref, wdown_ref, g2_ref, b2_ref, out_ref,
                    k_scr, vTe_scr, qT_scr, biasT_scr, s_scr, p_scr, m_scr, oT_scr, x1_scr, h_scr,
                    *, alpha, q_heads, kv_heads, head_dim, q_scale):
    blk = _SWA_BLOCK
    group = q_heads // kv_heads
    tm = x_ref.shape[0]
    n_blk = tm // blk
    vrows = vTe_scr.shape[0] // kv_heads
    first_tile = pl.program_id(1) == 0
    always = pl.program_id(1) >= 0

    @pl.when((pl.program_id(0) == 0) & first_tile)
    def _():
        bkt = bucketT_ref[...]
        for hq in range(q_heads):
            bias = jnp.full(bkt.shape, -jnp.inf, _F32)
            for bb in range(_REL_BUCKETS):
                bias = jnp.where(bkt == bb, relb_ref[bb, hq] * _LOG2E, bias)
            g, gi = divmod(hq, group)
            biasT_scr[g, :, gi * blk:(gi + 1) * blk] = bias

    @pl.when(first_tile)
    def _():
        k_scr[0:blk, :] = jnp.zeros((blk, k_scr.shape[1]), k_scr.dtype)
        vTe_scr[:, 0:blk] = jnp.zeros((vTe_scr.shape[0], blk), vTe_scr.dtype)

    xb = x_ref[...].astype(_BF16)
    qT_scr[...] = (lax.dot_general(wqT_ref[...], xb, _NT, preferred_element_type=_F32)
                   * q_scale).astype(_BF16)
    k_scr[blk:blk + tm, :] = jnp.dot(xb, wk_ref[...], preferred_element_type=_F32).astype(_BF16)
    vT = lax.dot_general(wvT_ref[...], xb, _NT, preferred_element_type=_F32).astype(_BF16)
    ones_rows = (lax.broadcasted_iota(jnp.int32, (vrows - head_dim, tm), 0) == 0).astype(_BF16)
    for g in range(kv_heads):
        vTe_scr[g * vrows:g * vrows + head_dim, blk:blk + tm] = vT[g * head_dim:(g + 1) * head_dim]
        vTe_scr[g * vrows + head_dim:(g + 1) * vrows, blk:blk + tm] = ones_rows

    first_pen = jnp.where(first_tile, -jnp.inf, 0.0).astype(_F32)
    zeros_q = jnp.zeros((head_dim, group * blk), _BF16)
    heads_per_lane_tile = _LANES // head_dim
    units = [(n, g) for n in range(n_blk) for g in range(kv_heads)]

    def sink_row(g):
        return jnp.concatenate(
            [jnp.full((1, blk), sink_ref[g * group + gi] * _LOG2E, _F32) for gi in range(group)],
            axis=1)

    sinks = [None] * kv_heads

    def scores(u):
        n, g = units[u]
        lt = g // heads_per_lane_tile
        kpair = k_scr[n * blk:(n + 2) * blk, lt * _LANES:(lt + 1) * _LANES]
        qg = jnp.concatenate(
            [qT_scr[(g * group + gi) * head_dim:(g * group + gi + 1) * head_dim,
                    n * blk:(n + 1) * blk] for gi in range(group)], axis=1)
        pieces = [zeros_q] * heads_per_lane_tile
        pieces[g % heads_per_lane_tile] = qg
        qpad = jnp.concatenate(pieces, axis=0)
        s = jnp.dot(kpair, qpad, preferred_element_type=_F32) + biasT_scr[g]
        if n == 0:
            s = jnp.concatenate([s[:blk] + first_pen, s[blk:]], axis=0)
        s_scr[u] = s
        m_scr[u] = jnp.maximum(jnp.max(s, axis=0, keepdims=True), sinks[g])

    def exponentials(u):
        p_scr[u] = jnp.exp2(s_scr[u] - m_scr[u]).astype(_BF16)

    def weighted_values(u):
        n, g = units[u]
        oT = jnp.dot(vTe_scr[g * vrows:(g + 1) * vrows, n * blk:(n + 2) * blk], p_scr[u],
                     preferred_element_type=_F32)
        denom = oT[head_dim:head_dim + 1] + jnp.exp2(sinks[g] - m_scr[u])
        o = (oT[:head_dim] * (1.0 / denom)).astype(_BF16)
        for gi in range(group):
            hq = g * group + gi
            oT_scr[hq * head_dim:(hq + 1) * head_dim, n * blk:(n + 1) * blk] = (
                o[:, gi * blk:(gi + 1) * blk])

    for stage in (scores, exponentials, weighted_values):
        @pl.when(always)
        def _():
            for g in range(kv_heads):
                sinks[g] = sink_row(g)
            for u in range(len(units)):
                stage(u)

    @pl.when(pl.program_id(0) >= 0)
    def _():
        k_scr[0:blk, :] = k_scr[tm:tm + blk, :]
        vTe_scr[:, 0:blk] = vTe_scr[:, tm:tm + blk]

    def mix_rows(rows):
        return lax.dot_general(oT_scr[:, rows], wo_ref[...], _TN, preferred_element_type=_F32)

    _mix_mlp_ln(mix_rows, x_ref, alpha, g1_ref, b1_ref, wup_ref, wdown_ref, g2_ref, b2_ref,
                out_ref, x1_scr, h_scr)


def _t5_bucket_table(blk):
    i = np.arange(blk)[None, :]
    j = np.arange(2 * blk)[:, None]
    dist = i + blk - j
    max_exact = _REL_BUCKETS // 2
    nf = np.maximum(dist, 1).astype(np.float32)
    large = max_exact + (np.log(nf / np.float32(max_exact))
                         / np.float32(math.log(_REL_MAX_DIST / max_exact))
                         * np.float32(_REL_BUCKETS - max_exact)).astype(np.int32)
    large = np.minimum(large, _REL_BUCKETS - 1)
    bucket = np.where(dist < max_exact, np.maximum(dist, 0), large)
    valid = (dist >= 0) & (dist < blk)
    return np.where(valid, bucket, -1).astype(np.int32)


def _rope_tables(seq, rope):
    half = rope // 2
    inv = (np.float32(_ROPE_THETA) ** (-np.arange(half, dtype=np.float32) / np.float32(half)))
    ang = np.arange(seq, dtype=np.float32)[:, None] * inv.astype(np.float32)[None, :]
    cos = np.cos(ang.astype(np.float64)).astype(np.float32)
    sin = np.sin(ang.astype(np.float64)).astype(np.float32)
    cs = np.concatenate([cos, cos, -sin, sin], axis=1)
    return jnp.asarray(cs), jnp.asarray(cos.T.copy()), jnp.asarray(sin.T.copy())


def _row(v):
    return v.reshape(1, -1).astype(_F32)


def kernel(x, mla_w_in, mla_g_q, mla_g_kv, mla_w_uq, mla_w_uk, mla_w_uv, mla_w_o, kv_w_shared,
           swa_w_q, swa_sinks, swa_w_o, rel_bias, mlp_w_up, mlp_w_down, ln_mix_g, ln_mix_b,
           ln_mlp_g, ln_mlp_b):
    B, S, D = x.shape
    depth = mlp_w_up.shape[0]
    assert depth == 2 and mla_w_in.shape[0] == 1 and swa_w_q.shape[0] == 1
    alpha = (2 * depth) ** 0.25
    T = B * S
    tm = _TOKEN_TILE
    ta = _ATTN_TILE
    tk = ta // 2
    bf16_rows = 16
    ta_pad = ta + _LANES
    assert ta % tm == 0 and tm % tk == 0 and S % ta == 0

    q_rank, heads, hd = mla_w_uq.shape[1:]
    kv_rank, _, nope = mla_w_uk.shape[1:]
    vdim = mla_w_uv.shape[3]
    vrows = vdim + bf16_rows
    rope = hd - nope
    half = rope // 2
    assert 2 * rope == _LANES and mla_w_in.shape[2] == q_rank + kv_rank + rope

    w_in = mla_w_in[0]
    r0 = q_rank + kv_rank
    w_in_ext = jnp.concatenate(
        [w_in, w_in[:, r0 + half:r0 + rope], w_in[:, r0:r0 + half]], axis=1).astype(_BF16)
    wqT = mla_w_uq[0].reshape(q_rank, heads * hd).T.astype(_BF16)
    wuk = mla_w_uk[0].reshape(kv_rank, heads * nope).astype(_BF16)
    wvT = mla_w_uv[0].reshape(kv_rank, heads * vdim).T.astype(_BF16)
    cs, cosT, sinT = _rope_tables(S, rope)
    n_t = S // tm
    q_scale = hd ** -0.5 * _LOG2E

    cparams2 = pltpu.CompilerParams(dimension_semantics=("arbitrary", "arbitrary"),
                                    vmem_limit_bytes=_VMEM_LIMIT_BYTES)
    cparams1 = pltpu.CompilerParams(dimension_semantics=("arbitrary",),
                                    vmem_limit_bytes=_VMEM_LIMIT_BYTES)

    qT_all, k_all, vT_all = pl.pallas_call(
        functools.partial(_mla_proj_kernel, q_rank=q_rank, kv_rank=kv_rank, heads=heads,
                          nope=nope, rope=rope, q_scale=q_scale),
        grid=(B, n_t),
        in_specs=[
            pl.BlockSpec((None, tm, D), lambda b, i: (b, i, 0)),
            _const_spec(w_in_ext.shape),
            _const_spec((1, q_rank)),
            _const_spec((1, kv_rank)),
            _const_spec(wqT.shape),
            _const_spec(wuk.shape),
            _const_spec(wvT.shape),
            pl.BlockSpec((tm, 2 * rope), lambda b, i: (i, 0)),
            pl.BlockSpec((half, tm), lambda b, i: (0, i)),
            pl.BlockSpec((half, tm), lambda b, i: (0, i)),
        ],
        out_specs=[
            pl.BlockSpec((None, heads, None, hd, tm), lambda b, i: (b, 0, i, 0, 0)),
            pl.BlockSpec((None, heads, tm, hd), lambda b, i: (b, 0, i, 0)),
            pl.BlockSpec((None, heads, tm // tk, vrows, tk), lambda b, i: (b, 0, i, 0, 0)),
        ],
        out_shape=[
            jax.ShapeDtypeStruct((B, heads, n_t, hd, tm), _BF16),
            jax.ShapeDtypeStruct((B, heads, S, hd), _BF16),
            jax.ShapeDtypeStruct((B, heads, S // tk, vrows, tk), _BF16),
        ],
        compiler_params=cparams2,
        name="mla_proj",
    )(x, w_in_ext, _row(mla_g_q[0]), _row(mla_g_kv[0]), wqT, wuk, wvT, cs, cosT, sinT)

    o = pl.pallas_call(
        _mla_attn_kernel,
        grid=(B, heads),
        in_specs=[
            pl.BlockSpec((None, None, n_t, hd, tm), lambda b, h: (b, h, 0, 0, 0)),
            pl.BlockSpec((None, None, S, hd), lambda b, h: (b, h, 0, 0)),
            pl.BlockSpec((None, None, S // tk, vrows, tk), lambda b, h: (b, h, 0, 0, 0)),
        ],
        out_specs=pl.BlockSpec((None, S, vdim), lambda b, h: (b, 0, h)),
        out_shape=jax.ShapeDtypeStruct((B, S, heads * vdim), _BF16),
        scratch_shapes=[
            pltpu.VMEM((1, ta), _F32),
            pltpu.VMEM((vrows, ta), _F32),
        ] + [pltpu.VMEM((tk, ta_pad), _F32)] * 4
          + [pltpu.VMEM((1, ta), _F32)] * 4
          + [pltpu.VMEM((tk, ta), _BF16)] * 2
          + [pltpu.VMEM((1, ta), _F32)] * 2,
        compiler_params=cparams2,
        name="mla_attn",
    )(qT_all, k_all, vT_all)

    d_ff = mlp_w_up.shape[2]
    w_up_all = mlp_w_up.astype(_BF16)
    w_down_all = mlp_w_down.astype(_BF16)
    mlp_scratch = [
        pltpu.VMEM((tm, D), _F32),
        pltpu.VMEM((tm, d_ff), _BF16),
    ]
    x1 = pl.pallas_call(
        functools.partial(_outproj_mlp_kernel, alpha=alpha),
        grid=(T // tm,),
        in_specs=[
            pl.BlockSpec((tm, heads * vdim), lambda i: (i, 0)),
            pl.BlockSpec((tm, D), lambda i: (i, 0)),
            _const_spec((heads * vdim, D)),
            _const_spec((1, D)),
            _const_spec((1, D)),
            _layer_spec(w_up_all.shape, 0),
            _layer_spec(w_down_all.shape, 0),
            _const_spec((1, D)),
            _const_spec((1, D)),
        ],
        out_specs=pl.BlockSpec((tm, D), lambda i: (i, 0)),
        out_shape=jax.ShapeDtypeStruct((T, D), _F32),
        scratch_shapes=mlp_scratch,
        compiler_params=cparams1,
        name="mla_out_mlp",
    )(o.reshape(T, heads * vdim), x.reshape(T, D), mla_w_o[0].astype(_BF16),
      _row(ln_mix_g[0]), _row(ln_mix_b[0]), w_up_all, w_down_all,
      _row(ln_mlp_g[0]), _row(ln_mlp_b[0]))

    q_heads = swa_sinks.shape[1]
    s_hd = swa_w_q.shape[2] // q_heads
    kv_heads = kv_w_shared.shape[1] // (2 * s_hd)
    group = q_heads // kv_heads
    blk = _SWA_BLOCK
    assert _LANES % s_hd == 0 and tm % blk == 0
    wqT1 = swa_w_q[0].T.astype(_BF16)
    wk1 = kv_w_shared[:, :kv_heads * s_hd].astype(_BF16)
    wvT1 = kv_w_shared[:, kv_heads * s_hd:].T.astype(_BF16)
    bucketT = jnp.asarray(_t5_bucket_table(blk))
    n_units = (tm // blk) * kv_heads

    smem = pl.BlockSpec(memory_space=pltpu.SMEM)
    out = pl.pallas_call(
        functools.partial(_swa_mlp_kernel, alpha=alpha, q_heads=q_heads, kv_heads=kv_heads,
                          head_dim=s_hd, q_scale=s_hd ** -0.5 * _LOG2E),
        grid=(B, n_t),
        in_specs=[
            pl.BlockSpec((None, tm, D), lambda b, i: (b, i, 0)),
            _const_spec(wqT1.shape),
            _const_spec(wk1.shape),
            _const_spec(wvT1.shape),
            _const_spec((q_heads * s_hd, D)),
            _const_spec(bucketT.shape),
            smem,
            smem,
            _const_spec((1, D)),
            _const_spec((1, D)),
            _layer_spec(w_up_all.shape, 1),
            _layer_spec(w_down_all.shape, 1),
            _const_spec((1, D)),
            _const_spec((1, D)),
        ],
        out_specs=pl.BlockSpec((None, tm, D), lambda b, i: (b, i, 0)),
        out_shape=jax.ShapeDtypeStruct((B, S, D), _F32),
        scratch_shapes=[
            pltpu.VMEM((blk + tm, kv_heads * s_hd), _BF16),
            pltpu.VMEM((kv_heads * (s_hd + bf16_rows), blk + tm), _BF16),
            pltpu.VMEM((q_heads * s_hd, tm), _BF16),
            pltpu.VMEM((kv_heads, 2 * blk, group * blk), _F32),
            pltpu.VMEM((n_units, 2 * blk, group * blk), _F32),
            pltpu.VMEM((n_units, 2 * blk, group * blk), _BF16),
            pltpu.VMEM((n_units, 1, group * blk), _F32),
            pltpu.VMEM((q_heads * s_hd, tm), _BF16),
        ] + mlp_scratch,
        compiler_params=cparams2,
        name="swa_mlp",
    )(x1.reshape(B, S, D), wqT1, wk1, wvT1, swa_w_o[0].astype(_BF16), bucketT,
      rel_bias.astype(_F32), swa_sinks[0].astype(_F32),
      _row(ln_mix_g[1]), _row(ln_mix_b[1]), w_up_all, w_down_all,
      _row(ln_mlp_g[1]), _row(ln_mlp_b[1]))
    return out
```
